```python
import functools
import math
import jax
import jax.numpy as jnp
from jax import lax
import numpy as np

D_MODEL = 1024
BATCH = 8
SEQ = 2048
DEPTH = 1
DEC_BATCH = 128
DEC_SEQ = 8
PAST_LEN = 8192
PAGE_SIZE = 128

N_HEADS = 16
N_KV_HEADS = 4
HEAD_DIM = 64
Q_PER_KV = N_HEADS // N_KV_HEADS
ATTN_WIDTH = N_HEADS * HEAD_DIM
KV_WIDTH = N_KV_HEADS * HEAD_DIM
WINDOW = 128
BLOCK = 128
SSM_WIDTH = D_MODEL // 2
SSM_GROUP = 16
SSM_GROUPS = SSM_WIDTH // SSM_GROUP
SSM_STATE = 64
DT_MIN = 1e-3
DT_MAX = 1e-1
D_FF = 2816
IN_WIDTH = ATTN_WIDTH + 2 * KV_WIDTH + SSM_WIDTH + 2 * D_MODEL
SPLIT_POINTS = (
    ATTN_WIDTH,
    ATTN_WIDTH + KV_WIDTH,
    ATTN_WIDTH + 2 * KV_WIDTH,
    ATTN_WIDTH + 2 * KV_WIDTH + SSM_WIDTH,
    ATTN_WIDTH + 2 * KV_WIDTH + SSM_WIDTH + D_MODEL,
)
RMS_EPS = 1e-6
NEG_BIG = -1e30

kernel_name = 'hybrid_swa_sink_s5_macaron_step'


def _rmsnorm(x, g):
    x32 = x.astype(jnp.float32)
    y = x32 * lax.rsqrt(jnp.mean(x32 * x32, axis=-1, keepdims=True) + RMS_EPS)
    return (y * g.astype(jnp.float32)).astype(x.dtype)


def _swiglu(h, w_gate, w_up, w_down):
    return (jax.nn.silu(h @ w_gate) * (h @ w_up)) @ w_down


def _sink_attention(q, k, v, valid, sinks):
    s = jnp.einsum('...qgrd,...kgd->...grqk', q, k).astype(jnp.float32) * (HEAD_DIM ** -0.5)
    s = jnp.where(valid, s, NEG_BIG)
    sink = sinks.astype(jnp.float32)[:, :, None, None]
    m = jnp.maximum(jnp.max(s, axis=-1, keepdims=True), sink)
    p = jnp.exp(s - m)
    denom = jnp.sum(p, axis=-1, keepdims=True) + jnp.exp(sink - m)
    w = (p / denom).astype(v.dtype)
    return jnp.einsum('...grqk,...kgd->...qgrd', w, v)


def _attend_prompt(q, k, v, sinks):
    bt, seq = q.shape[0], q.shape[1]
    nb = seq // BLOCK
    qb = q.reshape(bt, nb, BLOCK, N_KV_HEADS, Q_PER_KV, HEAD_DIM)
    kb = k.reshape(bt, nb, BLOCK, N_KV_HEADS, HEAD_DIM)
    vb = v.reshape(bt, nb, BLOCK, N_KV_HEADS, HEAD_DIM)
    pad = ((0, 0), (1, 0), (0, 0), (0, 0), (0, 0))
    kk = jnp.concatenate([jnp.pad(kb, pad)[:, :-1], kb], axis=2)
    vv = jnp.concatenate([jnp.pad(vb, pad)[:, :-1], vb], axis=2)
    blk = jnp.arange(nb)[:, None, None]
    q_pos = blk * BLOCK + jnp.arange(BLOCK)[None, :, None]
    k_pos = (blk - 1) * BLOCK + jnp.arange(2 * BLOCK)[None, None, :]
    valid = (k_pos >= 0) & (k_pos <= q_pos) & (q_pos - k_pos <= WINDOW)
    out = _sink_attention(qb, kk, vv, valid[:, None, None], sinks)
    w_buf = min(WINDOW, PAST_LEN)
    return out.reshape(bt, seq, ATTN_WIDTH), k[:, -w_buf:], v[:, -w_buf:]


def _attend_sample(q, k, v, cache_k, cache_v, sinks):
    bt, s_len = q.shape[0], q.shape[1]
    w_buf = cache_k.shape[1]
    kk = jnp.concatenate([cache_k.astype(k.dtype), k], axis=1)
    vv = jnp.concatenate([cache_v.astype(v.dtype), v], axis=1)
    q_pos = w_buf + jnp.arange(s_len)[:, None]
    k_pos = jnp.arange(w_buf + s_len)[None, :]
    valid = (k_pos <= q_pos) & (q_pos - k_pos <= WINDOW)
    out = _sink_attention(q, kk, vv, valid, sinks)
    return out.reshape(bt, s_len, ATTN_WIDTH), kk[:, -w_buf:], vv[:, -w_buf:]


def _complex_affine_combine(earlier, later):
    a1r, a1i, b1r, b1i = earlier
    a2r, a2i, b2r, b2i = later
    return (a1r * a2r - a1i * a2i,
            a1r * a2i + a1i * a2r,
            a2r * b1r - a2i * b1i + b2r,
            a2r * b1i + a2i * b1r + b2i)


def _ssm_branch(u, h0_re, h0_im, lam_re, lam_im, log_dt, b_re, b_im, c_re, c_im, d_skip, glu_a, glu_b):
    f32 = jnp.float32
    bt, seq = u.shape[0], u.shape[1]
    lr = lam_re.astype(f32)
    li = lam_im.astype(f32)
    dt = jnp.exp(log_dt.astype(f32))[:, None]
    mag = jnp.exp(lr * dt)
    ang = li * dt
    abar_re = mag * jnp.cos(ang)
    abar_im = mag * jnp.sin(ang)
    den = lr * lr + li * li
    nr = abar_re - 1.0
    fr = (nr * lr + abar_im * li) / den
    fi = (abar_im * lr - nr * li) / den
    br = b_re.astype(f32)
    bi = b_im.astype(f32)
    bb_re = fr[..., None] * br - fi[..., None] * bi
    bb_im = fr[..., None] * bi + fi[..., None] * br
    u32 = u.astype(f32)
    ug = u32.reshape(bt, seq, SSM_GROUPS, SSM_GROUP)
    x_re = jnp.einsum('blgc,gnc->blgn', ug, bb_re)
    x_im = jnp.einsum('blgc,gnc->blgn', ug, bb_im)
    if h0_re is not None:
        h0r = h0_re.astype(f32)
        h0i = h0_im.astype(f32)
        x_re = x_re.at[:, 0].add(abar_re * h0r - abar_im * h0i)
        x_im = x_im.at[:, 0].add(abar_re * h0i + abar_im * h0r)
    a_re = jnp.broadcast_to(abar_re, x_re.shape)
    a_im = jnp.broadcast_to(abar_im, x_im.shape)
    _, _, h_re, h_im = lax.associative_scan(_complex_affine_combine, (a_re, a_im, x_re, x_im), axis=1)
    y = (jnp.einsum('blgn,gcn->blgc', h_re, c_re.astype(f32))
         - jnp.einsum('blgn,gcn->blgc', h_im, c_im.astype(f32))).reshape(bt, seq, SSM_WIDTH)
    y = jax.nn.gelu(y + d_skip.astype(f32) * u32).astype(u.dtype)
    out = (y @ glu_a) * jax.nn.sigmoid(y @ glu_b)
    return out, h_re[:, -1], h_im[:, -1]


def _layer(x, attend, h0_re, h0_im, ffn_a_norm, ffn_a_gate, ffn_a_up, ffn_a_down, mix_norm, w_in,
           lam_re, lam_im, log_dt, b_re, b_im, c_re, c_im, d_skip, glu_a, glu_b, w_out,
           ffn_b_norm, ffn_b_gate, ffn_b_up, ffn_b_down):
    bt, seq = x.shape[0], x.shape[1]
    x = x + 0.5 * _swiglu(_rmsnorm(x, ffn_a_norm), ffn_a_gate, ffn_a_up, ffn_a_down)
    h = _rmsnorm(x, mix_norm)
    q, k, v, u, g_attn, g_ssm = jnp.split(h @ w_in, SPLIT_POINTS, axis=-1)
    q = q.reshape(bt, seq, N_KV_HEADS, Q_PER_KV, HEAD_DIM)
    k = k.reshape(bt, seq, N_KV_HEADS, HEAD_DIM)
    v = v.reshape(bt, seq, N_KV_HEADS, HEAD_DIM)
    attn, k_buf, v_buf = attend(q, k, v)
    ssm, s_re, s_im = _ssm_branch(u, h0_re, h0_im, lam_re, lam_im, log_dt, b_re, b_im,
                                  c_re, c_im, d_skip, glu_a, glu_b)
    merged = jax.nn.sigmoid(g_attn) * attn + jax.nn.sigmoid(g_ssm) * ssm
    x = x + merged @ w_out
    x = x + 0.5 * _swiglu(_rmsnorm(x, ffn_b_norm), ffn_b_gate, ffn_b_up, ffn_b_down)
    return x, k_buf, v_buf, s_re, s_im


def setup_inputs(seed: int = 0) -> dict:
    key = jax.random.key(seed)
    ks = jax.random.split(key, 32)
    w_buf = min(WINDOW, PAST_LEN)

    def nrm(k, shape, scale):
        return jax.random.normal(k, shape, jnp.float32) * scale

    def gain(k, shape):
        return 1.0 + nrm(k, shape, 0.02)

    lam_im_base = jnp.pi * jnp.arange(SSM_STATE, dtype=jnp.float32)
    return {
        'x_prompt': nrm(ks[0], (BATCH, SEQ, D_MODEL), 1.0),
        'x_sample': nrm(ks[1], (DEC_BATCH, DEC_SEQ, D_MODEL), 1.0),
        'cache_k_win': nrm(ks[2], (DEPTH, DEC_BATCH, w_buf, N_KV_HEADS, HEAD_DIM), 1.0),
        'cache_v_win': nrm(ks[3], (DEPTH, DEC_BATCH, w_buf, N_KV_HEADS, HEAD_DIM), 1.0),
        'state_ssm_re': nrm(ks[4], (DEPTH, DEC_BATCH, SSM_GROUPS, SSM_STATE), 0.5),
        'state_ssm_im': nrm(ks[5], (DEPTH, DEC_BATCH, SSM_GROUPS, SSM_STATE), 0.5),
        'ffn_a_norm': gain(ks[6], (DEPTH, D_MODEL)),
        'ffn_a_gate': nrm(ks[7], (DEPTH, D_MODEL, D_FF), D_MODEL ** -0.5),
        'ffn_a_up': nrm(ks[8], (DEPTH, D_MODEL, D_FF), D_MODEL ** -0.5),
        'ffn_a_down': nrm(ks[9], (DEPTH, D_FF, D_MODEL), D_FF ** -0.5),
        'mix_norm': gain(ks[10], (DEPTH, D_MODEL)),
        'w_in': nrm(ks[11], (DEPTH, D_MODEL, IN_WIDTH), D_MODEL ** -0.5),
        'attn_sinks': nrm(ks[12], (DEPTH, N_HEADS), 1.0),
        'ssm_lambda_re': -0.5 * jnp.exp(nrm(ks[13], (DEPTH, SSM_GROUPS, SSM_STATE), 0.05)),
        'ssm_lambda_im': lam_im_base + nrm(ks[14], (DEPTH, SSM_GROUPS, SSM_STATE), 0.01),
        'ssm_log_dt': jax.random.uniform(ks[15], (DEPTH, SSM_GROUPS), jnp.float32,
                                         minval=math.log(DT_MIN), maxval=math.log(DT_MAX)),
        'ssm_b_re': nrm(ks[16], (DEPTH, SSM_GROUPS, SSM_STATE, SSM_GROUP), (2 * SSM_GROUP) ** -0.5),
        'ssm_b_im': nrm(ks[17], (DEPTH, SSM_GROUPS, SSM_STATE, SSM_GROUP), (2 * SSM_GROUP) ** -0.5),
        'ssm_c_re': nrm(ks[18], (DEPTH, SSM_GROUPS, SSM_GROUP, SSM_STATE), (2 * SSM_STATE) ** -0.5),
        'ssm_c_im': nrm(ks[19], (DEPTH, SSM_GROUPS, SSM_GROUP, SSM_STATE), (2 * SSM_STATE) ** -0.5),
        'ssm_d': nrm(ks[20], (DEPTH, SSM_WIDTH), 1.0),
        'glu_a': nrm(ks[21], (DEPTH, SSM_WIDTH, D_MODEL), SSM_WIDTH ** -0.5),
        'glu_b': nrm(ks[22], (DEPTH, SSM_WIDTH, D_MODEL), SSM_WIDTH ** -0.5),
        'w_out': nrm(ks[23], (DEPTH, D_MODEL, D_MODEL), D_MODEL ** -0.5),
        'ffn_b_norm': gain(ks[24], (DEPTH, D_MODEL)),
        'ffn_b_gate': nrm(ks[25], (DEPTH, D_MODEL, D_FF), D_MODEL ** -0.5),
        'ffn_b_up': nrm(ks[26], (DEPTH, D_MODEL, D_FF), D_MODEL ** -0.5),
        'ffn_b_down': nrm(ks[27], (DEPTH, D_FF, D_MODEL), D_FF ** -0.5),
        'final_norm': gain(ks[28], (D_MODEL,)),
    }


def reference(x_prompt, x_sample, cache_k_win, cache_v_win, state_ssm_re, state_ssm_im,
              ffn_a_norm, ffn_a_gate, ffn_a_up, ffn_a_down, mix_norm, w_in, attn_sinks,
              ssm_lambda_re, ssm_lambda_im, ssm_log_dt, ssm_b_re, ssm_b_im, ssm_c_re, ssm_c_im,
              ssm_d, glu_a, glu_b, w_out, ffn_b_norm, ffn_b_gate, ffn_b_up, ffn_b_down, final_norm):
    xp = x_prompt
    xs = x_sample
    kp, vp, ksm, vsm, sp_re, sp_im, ss_re, ss_im = [], [], [], [], [], [], [], []
    for l in range(DEPTH):
        lw = (ffn_a_norm[l], ffn_a_gate[l], ffn_a_up[l], ffn_a_down[l], mix_norm[l], w_in[l],
              ssm_lambda_re[l], ssm_lambda_im[l], ssm_log_dt[l], ssm_b_re[l], ssm_b_im[l],
              ssm_c_re[l], ssm_c_im[l], ssm_d[l], glu_a[l], glu_b[l], w_out[l],
              ffn_b_norm[l], ffn_b_gate[l], ffn_b_up[l], ffn_b_down[l])
        sinks = attn_sinks[l].reshape(N_KV_HEADS, Q_PER_KV)
        xp, k_new, v_new, s_re, s_im = _layer(
            xp, functools.partial(_attend_prompt, sinks=sinks), None, None, *lw)
        kp.append(k_new)
        vp.append(v_new)
        sp_re.append(s_re)
        sp_im.append(s_im)
        xs, k_new, v_new, s_re, s_im = _layer(
            xs, functools.partial(_attend_sample, cache_k=cache_k_win[l], cache_v=cache_v_win[l], sinks=sinks),
            state_ssm_re[l], state_ssm_im[l], *lw)
        ksm.append(k_new)
        vsm.append(v_new)
        ss_re.append(s_re)
        ss_im.append(s_im)
    y_prompt = _rmsnorm(xp, final_norm)
    y_sample = _rmsnorm(xs, final_norm)
    return (y_prompt, y_sample,
            jnp.stack(kp), jnp.stack(vp), jnp.stack(ksm), jnp.stack(vsm),
            jnp.stack(sp_re), jnp.stack(sp_im), jnp.stack(ss_re), jnp.stack(ss_im))
```

```python
import functools
import math

import jax
import jax.numpy as jnp
from jax import lax
from jax.experimental import pallas as pl
from jax.experimental.pallas import tpu as pltpu

F32 = jnp.float32
BF16 = jnp.bfloat16

N_HEADS = 16
N_KV_HEADS = 4
HEAD_DIM = 64
Q_PER_KV = N_HEADS // N_KV_HEADS
WINDOW = 128
SSM_GROUP = 16
SSM_STATE = 64
RMS_EPS = 1e-6
NEG_BIG = -1e30

MXU_COLS = 256
SSM_CHUNKS = 4
VMEM_LIMIT = 60000 * 1024


def _rms(x, g):
    return x * lax.rsqrt(jnp.mean(x * x, axis=-1, keepdims=True) + RMS_EPS) * g


def _swiglu(h_scr, wg_ref, wu_ref, wd_ref, act_scr):
    d_ff = wg_ref.shape[1]
    for c in range(d_ff // MXU_COLS):
        sl = slice(c * MXU_COLS, (c + 1) * MXU_COLS)
        g = jnp.dot(h_scr[...], wg_ref[:, sl], preferred_element_type=F32)
        u = jnp.dot(h_scr[...], wu_ref[:, sl], preferred_element_type=F32)
        act_scr[:, sl] = (jax.nn.silu(g) * u).astype(BF16)
    return jnp.dot(act_scr[...], wd_ref[...], preferred_element_type=F32)


def _ffn_in_kernel(x_ref, na_ref, wg_ref, wu_ref, wd_ref, nm_ref, win_ref,
                   x1_ref, q_ref, k_ref, v_ref, u_ref, ga_ref, gs_ref, h_scr, act_scr):
    x = x_ref[...]
    h_scr[...] = _rms(x, na_ref[...]).astype(BF16)
    x1 = x + 0.5 * _swiglu(h_scr, wg_ref, wu_ref, wd_ref, act_scr)
    x1_ref[...] = x1
    h_scr[...] = _rms(x1, nm_ref[...]).astype(BF16)
    off = 0
    for ref, scale in ((q_ref, HEAD_DIM ** -0.5), (k_ref, None), (v_ref, None),
                       (u_ref, None), (ga_ref, None), (gs_ref, None)):
        width = ref.shape[1]
        for c in range(width // MXU_COLS):
            r = jnp.dot(h_scr[...], win_ref[:, off + c * MXU_COLS: off + (c + 1) * MXU_COLS],
                        preferred_element_type=F32)
            if scale is not None:
                r = r * scale
            ref[:, c * MXU_COLS:(c + 1) * MXU_COLS] = r.astype(ref.dtype)
        off += width


def _const_spec(shape):
    nd = len(shape)
    return pl.BlockSpec(shape, lambda *_: (0,) * nd, pipeline_mode=pl.Buffered(1))


def _ffn_in(x2d, grid, tm, std, umap, u_shape, q_dtype, na, wg, wu, wd, nm, win):
    d = na.shape[1]
    d_ff = wg.shape[1]
    kv = N_KV_HEADS * HEAD_DIM
    ssm_w = win.shape[1] - 3 * d - 2 * kv
    widths = dict(x1=(d, F32), q=(d, q_dtype), k=(kv, F32), v=(kv, F32), ga=(d, F32), gs=(d, F32))
    out_shape, out_specs = [], []
    for name in ("x1", "q", "k", "v"):
        shp, spec = std(widths[name][0])
        out_shape.append(jax.ShapeDtypeStruct(shp, widths[name][1]))
        out_specs.append(spec)
    out_shape.append(jax.ShapeDtypeStruct(u_shape, F32))
    out_specs.append(pl.BlockSpec((tm, ssm_w), umap))
    for name in ("ga", "gs"):
        shp, spec = std(widths[name][0])
        out_shape.append(jax.ShapeDtypeStruct(shp, widths[name][1]))
        out_specs.append(spec)
    return pl.pallas_call(
        _ffn_in_kernel,
        grid=grid,
        in_specs=[std(d)[1], _const_spec(na.shape), _const_spec(wg.shape), _const_spec(wu.shape),
                  _const_spec(wd.shape), _const_spec(nm.shape), _const_spec(win.shape)],
        out_specs=out_specs,
        out_shape=out_shape,
        scratch_shapes=[pltpu.VMEM((tm, d), BF16), pltpu.VMEM((tm, d_ff), BF16)],
        compiler_params=pltpu.CompilerParams(
            dimension_semantics=("parallel", "parallel"), vmem_limit_bytes=VMEM_LIMIT),
        name="ffn_in",
    )(x2d, na, wg, wu, wd, nm, win)


def _attend(q, kctx, knew, vctx, vnew, sink_ref, ctx_ok):
    tn = q.shape[0]
    nk = WINDOW + tn
    rows = Q_PER_KV * tn
    kk = jnp.concatenate([kctx, knew], axis=0).astype(BF16)
    vv = jnp.concatenate([vctx, vnew], axis=0).astype(BF16)
    assert tn & (tn - 1) == 0, "row -> query index uses a power-of-two mask"
    iq = lax.broadcasted_iota(jnp.int32, (rows, nk), 0) & (tn - 1)
    jk = lax.broadcasted_iota(jnp.int32, (rows, nk), 1)
    lo = jnp.maximum(iq, jnp.where(ctx_ok, 0, WINDOW))
    valid = (jk >= lo) & (jk <= iq + WINDOW)
    outs = []
    for g in range(N_KV_HEADS):
        qs = jnp.concatenate(
            [q[:, (g * Q_PER_KV + r) * HEAD_DIM:(g * Q_PER_KV + r + 1) * HEAD_DIM] for r in range(Q_PER_KV)],
            axis=0).astype(BF16)
        kg = kk[:, g * HEAD_DIM:(g + 1) * HEAD_DIM]
        vg = vv[:, g * HEAD_DIM:(g + 1) * HEAD_DIM]
        s = lax.dot_general(qs, kg, (((1,), (1,)), ((), ())), preferred_element_type=F32)
        s = jnp.where(valid, s, NEG_BIG)
        sink = jnp.concatenate(
            [jnp.full((tn, 1), sink_ref[g * Q_PER_KV + r], F32) for r in range(Q_PER_KV)], axis=0)
        m = jnp.maximum(jnp.max(s, axis=-1, keepdims=True), sink)
        p = jnp.exp(s - m)
        denom = jnp.sum(p, axis=-1, keepdims=True) + jnp.exp(sink - m)
        w = (p / denom).astype(BF16)
        o = jnp.dot(w, vg, preferred_element_type=F32)
        outs.extend(o[r * tn:(r + 1) * tn] for r in range(Q_PER_KV))
    return jnp.concatenate(outs, axis=1)


def _attn_prompt_kernel(sink_ref, q_ref, kp_ref, kc_ref, vp_ref, vc_ref, o_ref):
    ctx_ok = pl.program_id(1) > 0
    o_ref[...] = _attend(q_ref[...], kp_ref[...], kc_ref[...], vp_ref[...], vc_ref[...], sink_ref, ctx_ok)


def _attn_prompt(sinks, q, k, v, n_seq, seq):
    nb = seq // WINDOW
    d = q.shape[1]
    kv = k.shape[1]
    cur = lambda b, i: (b * nb + i, 0)
    prev = lambda b, i: (b * nb + jnp.maximum(i - 1, 0), 0)
    return pl.pallas_call(
        _attn_prompt_kernel,
        grid=(n_seq, nb),
        in_specs=[pl.BlockSpec(memory_space=pltpu.SMEM),
                  pl.BlockSpec((WINDOW, d), cur),
                  pl.BlockSpec((WINDOW, kv), prev), pl.BlockSpec((WINDOW, kv), cur),
                  pl.BlockSpec((WINDOW, kv), prev), pl.BlockSpec((WINDOW, kv), cur)],
        out_specs=pl.BlockSpec((WINDOW, d), cur),
        out_shape=jax.ShapeDtypeStruct(q.shape, F32),
        compiler_params=pltpu.CompilerParams(
            dimension_semantics=("parallel", "parallel"), vmem_limit_bytes=VMEM_LIMIT),
        name="attn_prompt",
    )(sinks, q, k, k, v, v)


def _attn_sample_kernel(sink_ref, q_ref, k_ref, v_ref, ck_ref, cv_ref, o_ref, kw_ref, vw_ref, *, n_sub, tn):
    def body(s, carry):
        r0 = pl.multiple_of(s * tn, tn)
        knew = k_ref[pl.ds(r0, tn), :]
        vnew = v_ref[pl.ds(r0, tn), :]
        kctx = ck_ref[s]
        vctx = cv_ref[s]
        o_ref[pl.ds(r0, tn), :] = _attend(q_ref[pl.ds(r0, tn), :], kctx, knew, vctx, vnew, sink_ref, True)
        kw_ref[s, 0:WINDOW - tn, :] = kctx[tn:, :]
        kw_ref[s, WINDOW - tn:WINDOW, :] = knew
        vw_ref[s, 0:WINDOW - tn, :] = vctx[tn:, :]
        vw_ref[s, WINDOW - tn:WINDOW, :] = vnew
        return carry
    lax.fori_loop(0, n_sub, body, 0)


def _attn_sample(sinks, q, k, v, cache_k, cache_v, n_seq, tn, n_sub):
    d = q.shape[1]
    kv = k.shape[1]
    rows = lambda i: (i, 0)
    seqs = lambda i: (i, 0, 0)
    return pl.pallas_call(
        functools.partial(_attn_sample_kernel, n_sub=n_sub, tn=tn),
        grid=(n_seq // n_sub,),
        in_specs=[pl.BlockSpec(memory_space=pltpu.SMEM),
                  pl.BlockSpec((n_sub * tn, d), rows),
                  pl.BlockSpec((n_sub * tn, kv), rows), pl.BlockSpec((n_sub * tn, kv), rows),
                  pl.BlockSpec((n_sub, WINDOW, kv), seqs), pl.BlockSpec((n_sub, WINDOW, kv), seqs)],
        out_specs=[pl.BlockSpec((n_sub * tn, d), rows),
                   pl.BlockSpec((n_sub, WINDOW, kv), seqs), pl.BlockSpec((n_sub, WINDOW, kv), seqs)],
        out_shape=[jax.ShapeDtypeStruct(q.shape, F32),
                   jax.ShapeDtypeStruct(cache_k.shape, F32), jax.ShapeDtypeStruct(cache_v.shape, F32)],
        compiler_params=pltpu.CompilerParams(
            dimension_semantics=("parallel",), vmem_limit_bytes=VMEM_LIMIT),
        name="attn_sample",
    )(sinks, q, k, v, cache_k, cache_v)


def _ssm_prep_kernel(lr_ref, li_ref, ldt_ref, br_ref, bi_ref, ar_ref, ai_ref, bbr_ref, bbi_ref):
    lr = lr_ref[...]
    li = li_ref[...]
    dt = jnp.exp(ldt_ref[...])
    mag = jnp.exp(lr * dt)
    ang = li * dt
    abar_re = mag * jnp.cos(ang)
    abar_im = mag * jnp.sin(ang)
    den = lr * lr + li * li
    nr = abar_re - 1.0
    fr = (nr * lr + abar_im * li) / den
    fi = (abar_im * lr - nr * li) / den
    ar_ref[...] = abar_re
    ai_ref[...] = abar_im
    br = br_ref[...]
    bi = bi_ref[...]
    bbr_ref[...] = fr[:, None, :] * br - fi[:, None, :] * bi
    bbi_ref[...] = fr[:, None, :] * bi + fi[:, None, :] * br


def _ssm_prep(lam_re, lam_im, log_dt, b_re_t, b_im_t):
    g, n = lam_re.shape
    return pl.pallas_call(
        _ssm_prep_kernel,
        out_shape=[jax.ShapeDtypeStruct((g, n), F32), jax.ShapeDtypeStruct((g, n), F32),
                   jax.ShapeDtypeStruct(b_re_t.shape, F32), jax.ShapeDtypeStruct(b_im_t.shape, F32)],
        name="ssm_prep",
    )(lam_re, lam_im, log_dt.reshape(g, 1), b_re_t, b_im_t)


def _ssm_kernel(u_ref, h0_ref, a_ref, d_ref, bd_ref, cd_ref, y_ref, hT_ref, xh_scr, *, n_seq, tt):
    cw = xh_scr.shape[1] // SSM_CHUNKS
    hw = cw // 2
    uc = u_ref.shape[1] // SSM_CHUNKS

    @pl.when(pl.program_id(0) == 0)
    def _():
        hT_ref[...] = h0_ref[...]

    for j in range(SSM_CHUNKS):
        xh_scr[:, j * cw:(j + 1) * cw] = jnp.dot(
            u_ref[:, j * uc:(j + 1) * uc].astype(BF16), bd_ref[j], preferred_element_type=F32)

    def seq_group(s, carry):
        s8 = pl.multiple_of(s * 8, 8)
        for j in range(SSM_CHUNKS):
            re = slice(j * cw, j * cw + hw)
            im = slice(j * cw + hw, (j + 1) * cw)
            ar = a_ref[:, re]
            ai = a_ref[:, im]

            def step(t, h):
                hr, hi = h
                r0 = pl.multiple_of(t * n_seq + s8, 8)
                nr = ar * hr - ai * hi + xh_scr[pl.ds(r0, 8), re]
                ni = ar * hi + ai * hr + xh_scr[pl.ds(r0, 8), im]
                xh_scr[pl.ds(r0, 8), re] = nr
                xh_scr[pl.ds(r0, 8), im] = ni
                return nr, ni

            hr, hi = lax.fori_loop(0, tt, step, (hT_ref[pl.ds(s8, 8), re], hT_ref[pl.ds(s8, 8), im]),
                                   unroll=min(tt, 8))
            hT_ref[pl.ds(s8, 8), re] = hr
            hT_ref[pl.ds(s8, 8), im] = hi
        return carry

    lax.fori_loop(0, n_seq // 8, seq_group, 0)

    for j in range(SSM_CHUNKS):
        y = jnp.dot(xh_scr[:, j * cw:(j + 1) * cw].astype(BF16), cd_ref[j], preferred_element_type=F32)
        cs = slice(j * uc, (j + 1) * uc)
        y_ref[:, cs] = jax.nn.gelu(y + d_ref[:, cs] * u_ref[:, cs]).astype(y_ref.dtype)


def _ssm(u_tb, h0, a8, d_skip, bd, cd, n_seq, tt):
    rows, ssm_w = u_tb.shape
    tile = n_seq * tt
    n_state = h0.shape[1]
    return pl.pallas_call(
        functools.partial(_ssm_kernel, n_seq=n_seq, tt=tt),
        grid=(rows // tile,),
        in_specs=[pl.BlockSpec((tile, ssm_w), lambda i: (i, 0)),
                  _const_spec(h0.shape), _const_spec(a8.shape), _const_spec(d_skip.shape),
                  _const_spec(bd.shape), _const_spec(cd.shape)],
        out_specs=[pl.BlockSpec((tile, ssm_w), lambda i: (i, 0)),
                   pl.BlockSpec(h0.shape, lambda i: (0, 0))],
        out_shape=[jax.ShapeDtypeStruct((rows, ssm_w), BF16), jax.ShapeDtypeStruct(h0.shape, F32)],
        scratch_shapes=[pltpu.VMEM((tile, n_state), F32)],
        compiler_params=pltpu.CompilerParams(
            dimension_semantics=("arbitrary",), vmem_limit_bytes=VMEM_LIMIT),
        name="ssm",
    )(u_tb, h0, a8, d_skip, bd, cd)


def _out_kernel(x1_ref, at_ref, ga_ref, gs_ref, y_ref, glua_ref, glub_ref, wo_ref,
                nb_ref, wg_ref, wu_ref, wd_ref, nf_ref, o_ref, h_scr, act_scr):
    yb = y_ref[...]
    ssm = jnp.dot(yb, glua_ref[...], preferred_element_type=F32) * jax.nn.sigmoid(
        jnp.dot(yb, glub_ref[...], preferred_element_type=F32))
    merged = jax.nn.sigmoid(ga_ref[...]) * at_ref[...] + jax.nn.sigmoid(gs_ref[...]) * ssm
    x2 = x1_ref[...] + jnp.dot(merged.astype(BF16), wo_ref[...], preferred_element_type=F32)
    h_scr[...] = _rms(x2, nb_ref[...]).astype(BF16)
    x3 = x2 + 0.5 * _swiglu(h_scr, wg_ref, wu_ref, wd_ref, act_scr)
    o_ref[...] = _rms(x3, nf_ref[...])


def _out(grid, tm, std, ymap, x1, attn, ga, gs, y_tb, glua, glub, wo, nb, wg, wu, wd, nf):
    d = nb.shape[1]
    d_ff = wg.shape[1]
    shp, spec = std(d)
    return pl.pallas_call(
        _out_kernel,
        grid=grid,
        in_specs=[spec, spec, spec, spec, pl.BlockSpec((tm, glua.shape[0]), ymap),
                  _const_spec(glua.shape), _const_spec(glub.shape), _const_spec(wo.shape),
                  _const_spec(nb.shape), _const_spec(wg.shape), _const_spec(wu.shape),
                  _const_spec(wd.shape), _const_spec(nf.shape)],
        out_specs=spec,
        out_shape=jax.ShapeDtypeStruct(shp, F32),
        scratch_shapes=[pltpu.VMEM((tm, d), BF16), pltpu.VMEM((tm, d_ff), BF16)],
        compiler_params=pltpu.CompilerParams(
            dimension_semantics=("parallel", "parallel"), vmem_limit_bytes=VMEM_LIMIT),
        name="out",
    )(x1, attn, ga, gs, y_tb, glua, glub, wo, nb, wg, wu, wd, nf)


def _block_diag(w, chunks):
    g, a, b = w.shape
    gl = g // chunks
    w = w.reshape(chunks, gl, a, b)
    eye = jnp.eye(gl, dtype=w.dtype)
    return (w[:, :, :, None, :] * eye[None, :, None, :, None]).reshape(chunks, gl * a, gl * b)


def _state_to_lanes(re, im):
    s = re.shape[0]
    re = re.reshape(s, SSM_CHUNKS, -1)
    im = im.reshape(s, SSM_CHUNKS, -1)
    return jnp.stack([re, im], axis=2).reshape(s, -1)


def _lanes_to_state(h, groups):
    s = h.shape[0]
    h = h.reshape(s, SSM_CHUNKS, 2, -1)
    return h[:, :, 0].reshape(s, groups, -1), h[:, :, 1].reshape(s, groups, -1)


def kernel(x_prompt, x_sample, cache_k_win, cache_v_win, state_ssm_re, state_ssm_im, ffn_a_norm, ffn_a_gate, ffn_a_up, ffn_a_down, mix_norm, w_in, attn_sinks, ssm_lambda_re, ssm_lambda_im, ssm_log_dt, ssm_b_re, ssm_b_im, ssm_c_re, ssm_c_im, ssm_d, glu_a, glu_b, w_out, ffn_b_norm, ffn_b_gate, ffn_b_up, ffn_b_down, final_norm):
    depth = ffn_a_norm.shape[0]
    assert depth == 1, "single-layer trunk"
    n_p, seq, d = x_prompt.shape
    n_s, dec, _ = x_sample.shape
    ssm_w = ssm_d.shape[1]
    groups = ssm_lambda_re.shape[1]
    kvw = N_KV_HEADS * HEAD_DIM
    assert cache_k_win.shape[2] == WINDOW and seq % WINDOW == 0 and n_p % 8 == 0 and n_s % 8 == 0

    l = 0
    na, nm, nb = (w[l].reshape(1, d) for w in (ffn_a_norm, mix_norm, ffn_b_norm))
    nf = final_norm.reshape(1, d)
    wga, wua, wda, win = (w[l].astype(BF16) for w in (ffn_a_gate, ffn_a_up, ffn_a_down, w_in))
    wgb, wub, wdb = (w[l].astype(BF16) for w in (ffn_b_gate, ffn_b_up, ffn_b_down))
    glua, glub, wo = (w[l].astype(BF16) for w in (glu_a, glu_b, w_out))
    sinks = attn_sinks[l]
    d_skip = ssm_d[l].reshape(1, ssm_w)

    abar_re, abar_im, bb_re_t, bb_im_t = _ssm_prep(
        ssm_lambda_re[l], ssm_lambda_im[l], ssm_log_dt[l],
        jnp.swapaxes(ssm_b_re[l], 1, 2), jnp.swapaxes(ssm_b_im[l], 1, 2))
    bd = jnp.concatenate([_block_diag(bb_re_t, SSM_CHUNKS), _block_diag(bb_im_t, SSM_CHUNKS)],
                         axis=2).astype(BF16)
    c_re_t = jnp.swapaxes(ssm_c_re[l], 1, 2)
    c_im_t = jnp.swapaxes(ssm_c_im[l], 1, 2)
    cd = jnp.concatenate([_block_diag(c_re_t, SSM_CHUNKS), -_block_diag(c_im_t, SSM_CHUNKS)],
                         axis=1).astype(BF16)
    a8 = jnp.broadcast_to(_state_to_lanes(abar_re[None], abar_im[None]), (8, 2 * groups * SSM_STATE))

    def run_group(x, n_seq, t_len, grid, tm, std, umap, u_shape, q_dtype, attn_fn, h0, tt):
        x1, q, k, v, u_tb, ga, gs = _ffn_in(x, grid, tm, std, umap, u_shape, q_dtype,
                                            na, wga, wua, wda, nm, win)
        attn, extras = attn_fn(q, k, v)
        y_tb, h_t = _ssm(u_tb.reshape(t_len * n_seq, ssm_w), h0, a8, d_skip, bd, cd, n_seq, tt)
        y = _out(grid, tm, std, umap, x1, attn, ga, gs, y_tb.reshape(u_shape),
                 glua, glub, wo, nb, wgb, wub, wdb, nf)
        return y, k, v, h_t, extras

    tm_p = 256
    nt = seq // tm_p

    def std_p(width):
        return (n_p * seq, width), pl.BlockSpec((tm_p, width), lambda b, i: (b * nt + i, 0))

    def attn_p(q, k, v):
        return _attn_prompt(sinks, q, k, v, n_p, seq), None

    h0_p = jnp.zeros((n_p, 2 * groups * SSM_STATE), F32)
    y_p, k_p, v_p, h_p, _ = run_group(
        x_prompt.reshape(n_p * seq, d), n_p, seq, (n_p, nt), tm_p, std_p,
        lambda b, i: (i, b), (seq, n_p * ssm_w), BF16, attn_p, h0_p, 1024 // n_p)

    def std_s(width):
        return (n_s, dec * width), pl.BlockSpec((n_s, width), lambda t, i: (0, t))

    def attn_s(q, k, v):
        o, kw, vw = _attn_sample(
            sinks, q.reshape(n_s * dec, d), k.reshape(n_s * dec, kvw), v.reshape(n_s * dec, kvw),
            cache_k_win[l].reshape(n_s, WINDOW, kvw), cache_v_win[l].reshape(n_s, WINDOW, kvw),
            n_s, dec, 16)
        return o.reshape(n_s, dec * d), (kw, vw)

    h0_s = _state_to_lanes(state_ssm_re[l], state_ssm_im[l])
    y_s, _, _, h_s, (kw_s, vw_s) = run_group(
        x_sample.reshape(n_s, dec * d), n_s, dec, (dec, 1), n_s, std_s,
        lambda t, i: (t, 0), (dec * n_s, ssm_w), F32, attn_s, h0_s, dec)

    win_shape = (depth, -1, WINDOW, N_KV_HEADS, HEAD_DIM)
    k_p = k_p.reshape(n_p, seq, kvw)[:, seq - WINDOW:].reshape((depth, n_p) + win_shape[2:])
    v_p = v_p.reshape(n_p, seq, kvw)[:, seq - WINDOW:].reshape((depth, n_p) + win_shape[2:])
    sp_re, sp_im = _lanes_to_state(h_p, groups)
    ss_re, ss_im = _lanes_to_state(h_s, groups)
    return (y_p.reshape(n_p, seq, d), y_s.reshape(n_s, dec, d),
            k_p, v_p, kw_s.reshape((depth, n_s) + win_shape[2:]), vw_s.reshape((depth, n_s) + win_shape[2:]),
            sp_re[None], sp_im[None], ss_re[None], ss_im[None])
```

```python
import functools
import math

import jax
import jax.numpy as jnp
from jax import lax
from jax.experimental import pallas as pl
from jax.experimental.pallas import tpu as pltpu

F32 = jnp.float32
BF16 = jnp.bfloat16

N_HEADS = 16
N_KV_HEADS = 4
HEAD_DIM = 64
Q_PER_KV = N_HEADS // N_KV_HEADS
WINDOW = 128
SSM_GROUP = 16
SSM_STATE = 64
RMS_EPS = 1e-6
NEG_BIG = -1e30

MXU_COLS = 256
SSM_CHUNKS = 4
VMEM_LIMIT = 60000 * 1024


def _rms(x, g):
    return x * lax.rsqrt(jnp.mean(x * x, axis=-1, keepdims=True) + RMS_EPS) * g


def _swiglu(h_scr, wg_ref, wu_ref, wd_ref, act_scr):
    d_ff = wg_ref.shape[1]
    for c in range(d_ff // MXU_COLS):
        sl = slice(c * MXU_COLS, (c + 1) * MXU_COLS)
        g = jnp.dot(h_scr[...], wg_ref[:, sl], preferred_element_type=F32)
        u = jnp.dot(h_scr[...], wu_ref[:, sl], preferred_element_type=F32)
        act_scr[:, sl] = (jax.nn.silu(g) * u).astype(BF16)
    return jnp.dot(act_scr[...], wd_ref[...], preferred_element_type=F32)


def _ffn_in_kernel(*refs, stacked):
    if stacked:
        (x_ref, na_ref, wg_ref, wu_ref, wd_ref, nm_ref, win_ref, wk2_ref, wvt_ref,
         x1_ref, q_ref, k_ref, v_ref, u_ref, ga_ref, gs_ref, k2_ref, vt_ref, h_scr, act_scr) = refs
    else:
        (x_ref, na_ref, wg_ref, wu_ref, wd_ref, nm_ref, win_ref,
         x1_ref, q_ref, k_ref, v_ref, u_ref, ga_ref, gs_ref, h_scr, act_scr) = refs
    tm, d = x_ref.shape
    x = x_ref[...]
    h_scr[...] = _rms(x, na_ref[...]).astype(BF16)
    x1 = x + 0.5 * _swiglu(h_scr, wg_ref, wu_ref, wd_ref, act_scr)
    x1_ref[...] = x1
    h_scr[...] = _rms(x1, nm_ref[...]).astype(BF16)

    def proj(w_ref, off, c):
        return jnp.dot(h_scr[...], w_ref[:, off + c * MXU_COLS: off + (c + 1) * MXU_COLS],
                       preferred_element_type=F32)

    q_scale = HEAD_DIM ** -0.5
    pair_w = 2 * HEAD_DIM
    if stacked:
        assert Q_PER_KV * HEAD_DIM == MXU_COLS
        low_half = lax.broadcasted_iota(jnp.int32, (tm, MXU_COLS), 1) % pair_w < HEAD_DIM
        for g in range(N_KV_HEADS):
            r = proj(win_ref, 0, g) * q_scale
            halves = (jnp.where(low_half, r, 0.0).astype(BF16), jnp.where(low_half, 0.0, r).astype(BF16))
            for bl in range(tm // WINDOW):
                for rr in range(Q_PER_KV):
                    row = ((bl * N_KV_HEADS + g) * Q_PER_KV + rr) * WINDOW
                    q_ref[row:row + WINDOW, :] = halves[rr % 2][bl * WINDOW:(bl + 1) * WINDOW,
                                                               (rr // 2) * pair_w:(rr // 2 + 1) * pair_w]
    else:
        for c in range(d // MXU_COLS):
            q_ref[:, c * MXU_COLS:(c + 1) * MXU_COLS] = (proj(win_ref, 0, c) * q_scale).astype(q_ref.dtype)
    off = d
    for ref in (k_ref, v_ref, u_ref, ga_ref, gs_ref):
        width = ref.shape[1]
        for c in range(width // MXU_COLS):
            ref[:, c * MXU_COLS:(c + 1) * MXU_COLS] = proj(win_ref, off, c).astype(ref.dtype)
        off += width
    if stacked:
        for c in range(k2_ref.shape[1] // MXU_COLS):
            k2_ref[:, c * MXU_COLS:(c + 1) * MXU_COLS] = proj(wk2_ref, 0, c).astype(k2_ref.dtype)
        vt_ref[...] = lax.dot_general(wvt_ref[...], h_scr[...], (((1,), (1,)), ((), ())),
                                      preferred_element_type=F32).astype(vt_ref.dtype)


def _const_spec(shape):
    nd = len(shape)
    return pl.BlockSpec(shape, lambda *_: (0,) * nd, pipeline_mode=pl.Buffered(1))


def _ffn_in(x2d, grid, tm, std, umap, u_shape, na, wg, wu, wd, nm, win, stacked_w=None):
    d = na.shape[1]
    d_ff = wg.shape[1]
    kv = N_KV_HEADS * HEAD_DIM
    ssm_w = win.shape[1] - 3 * d - 2 * kv
    stacked = stacked_w is not None
    n_tok = x2d.size // d
    out_shape, out_specs = [], []

    def add(shape_spec, dtype):
        out_shape.append(jax.ShapeDtypeStruct(shape_spec[0], dtype))
        out_specs.append(shape_spec[1])

    add(std(d), F32)
    if stacked:
        wk2, wvt, tile_map = stacked_w
        add(((n_tok * N_HEADS, 2 * HEAD_DIM), pl.BlockSpec((tm * N_HEADS, 2 * HEAD_DIM), tile_map)), BF16)
    else:
        add(std(d), F32)
    add(std(kv), F32)
    add(std(kv), F32)
    add((u_shape, pl.BlockSpec((tm, ssm_w), umap)), F32)
    add(std(d), F32)
    add(std(d), F32)
    weights = [na, wg, wu, wd, nm, win]
    if stacked:
        add(std(2 * kv), BF16)
        add(((kv, n_tok), pl.BlockSpec((kv, tm), lambda *g: tile_map(*g)[::-1])), BF16)
        weights += [wk2, wvt]
    return pl.pallas_call(
        functools.partial(_ffn_in_kernel, stacked=stacked),
        grid=grid,
        in_specs=[std(d)[1]] + [_const_spec(w.shape) for w in weights],
        out_specs=out_specs,
        out_shape=out_shape,
        scratch_shapes=[pltpu.VMEM((tm, d), BF16), pltpu.VMEM((tm, d_ff), BF16)],
        compiler_params=pltpu.CompilerParams(
            dimension_semantics=("parallel", "parallel"), vmem_limit_bytes=VMEM_LIMIT),
        name="ffn_in",
    )(x2d, *weights)


def _attend(q, kctx, knew, vctx, vnew, sink_ref, ctx_ok):
    tn = q.shape[0]
    nk = WINDOW + tn
    rows = Q_PER_KV * tn
    kk = jnp.concatenate([kctx, knew], axis=0).astype(BF16)
    vv = jnp.concatenate([vctx, vnew], axis=0).astype(BF16)
    assert tn & (tn - 1) == 0, "row -> query index uses a power-of-two mask"
    iq = lax.broadcasted_iota(jnp.int32, (rows, nk), 0) & (tn - 1)
    jk = lax.broadcasted_iota(jnp.int32, (rows, nk), 1)
    lo = jnp.maximum(iq, jnp.where(ctx_ok, 0, WINDOW))
    valid = (jk >= lo) & (jk <= iq + WINDOW)
    outs = []
    for g in range(N_KV_HEADS):
        qs = jnp.concatenate(
            [q[:, (g * Q_PER_KV + r) * HEAD_DIM:(g * Q_PER_KV + r + 1) * HEAD_DIM] for r in range(Q_PER_KV)],
            axis=0).astype(BF16)
        kg = kk[:, g * HEAD_DIM:(g + 1) * HEAD_DIM]
        vg = vv[:, g * HEAD_DIM:(g + 1) * HEAD_DIM]
        s = lax.dot_general(qs, kg, (((1,), (1,)), ((), ())), preferred_element_type=F32)
        s = jnp.where(valid, s, NEG_BIG)
        sink = jnp.concatenate(
            [jnp.full((tn, 1), sink_ref[g * Q_PER_KV + r], F32) for r in range(Q_PER_KV)], axis=0)
        m = jnp.maximum(jnp.max(s, axis=-1, keepdims=True), sink)
        p = jnp.exp(s - m)
        denom = jnp.sum(p, axis=-1, keepdims=True) + jnp.exp(sink - m)
        w = (p / denom).astype(BF16)
        o = jnp.dot(w, vg, preferred_element_type=F32)
        outs.extend(o[r * tn:(r + 1) * tn] for r in range(Q_PER_KV))
    return jnp.concatenate(outs, axis=1)


def _attn_prompt_kernel(sink_ref, q_ref, kp_ref, kc_ref, vp_ref, vc_ref, o_ref):
    nq = WINDOW
    nk = 2 * WINDOW
    lanes = Q_PER_KV * nq
    pair_w = 2 * HEAD_DIM
    ctx_ok = pl.program_id(1) > 0
    jk = lax.broadcasted_iota(jnp.int32, (nk, lanes), 0)
    iq = lax.broadcasted_iota(jnp.int32, (nk, lanes), 1) & (nq - 1)
    lo = jnp.maximum(iq, jnp.where(ctx_ok, 0, WINDOW))
    valid = (jk >= lo) & (jk <= iq + WINDOW)
    for g in range(N_KV_HEADS):
        kg = jnp.concatenate([kp_ref[:, g * pair_w:(g + 1) * pair_w], kc_ref[:, g * pair_w:(g + 1) * pair_w]], axis=0)
        qs = q_ref[g * lanes:(g + 1) * lanes, :]
        st = lax.dot_general(kg, qs, (((1,), (1,)), ((), ())), preferred_element_type=F32)
        st = jnp.where(valid, st, NEG_BIG)
        sink = jnp.concatenate(
            [jnp.full((1, nq), sink_ref[g * Q_PER_KV + r], F32) for r in range(Q_PER_KV)], axis=1)
        m = jnp.maximum(jnp.max(st, axis=0, keepdims=True), sink)
        p = jnp.exp(st - m)
        denom = jnp.sum(p, axis=0, keepdims=True) + jnp.exp(sink - m)
        vgt = jnp.concatenate([vp_ref[g * HEAD_DIM:(g + 1) * HEAD_DIM, :],
                               vc_ref[g * HEAD_DIM:(g + 1) * HEAD_DIM, :]], axis=1)
        ot = jnp.dot(vgt, p.astype(BF16), preferred_element_type=F32) * (1.0 / denom)
        for pr in range(Q_PER_KV // 2):
            two = jnp.concatenate([ot[:, (2 * pr) * nq:(2 * pr + 1) * nq],
                                   ot[:, (2 * pr + 1) * nq:(2 * pr + 2) * nq]], axis=0)
            col = (g * Q_PER_KV + 2 * pr) * HEAD_DIM
            o_ref[:, col:col + pair_w] = two.T


def _attn_prompt(sinks, qst, k2, vt, n_seq, seq):
    nb = seq // WINDOW
    d = N_HEADS * HEAD_DIM
    kv = N_KV_HEADS * HEAD_DIM
    cur = lambda b, i: (b * nb + i, 0)
    prev = lambda b, i: (b * nb + jnp.maximum(i - 1, 0), 0)
    cur_t = lambda b, i: (0, b * nb + i)
    prev_t = lambda b, i: (0, b * nb + jnp.maximum(i - 1, 0))
    return pl.pallas_call(
        _attn_prompt_kernel,
        grid=(n_seq, nb),
        in_specs=[pl.BlockSpec(memory_space=pltpu.SMEM),
                  pl.BlockSpec((WINDOW * N_HEADS, 2 * HEAD_DIM), cur),
                  pl.BlockSpec((WINDOW, 2 * kv), prev), pl.BlockSpec((WINDOW, 2 * kv), cur),
                  pl.BlockSpec((kv, WINDOW), prev_t), pl.BlockSpec((kv, WINDOW), cur_t)],
        out_specs=pl.BlockSpec((WINDOW, d), cur),
        out_shape=jax.ShapeDtypeStruct((n_seq * seq, d), F32),
        compiler_params=pltpu.CompilerParams(
            dimension_semantics=("parallel", "parallel"), vmem_limit_bytes=VMEM_LIMIT),
        name="attn_prompt",
    )(sinks, qst, k2, k2, vt, vt)


def _attn_sample_kernel(sink_ref, q_ref, k_ref, v_ref, ck_ref, cv_ref, o_ref, kw_ref, vw_ref, *, n_sub, tn):
    def body(s, carry):
        r0 = pl.multiple_of(s * tn, tn)
        knew = k_ref[pl.ds(r0, tn), :]
        vnew = v_ref[pl.ds(r0, tn), :]
        kctx = ck_ref[s]
        vctx = cv_ref[s]
        o_ref[pl.ds(r0, tn), :] = _attend(q_ref[pl.ds(r0, tn), :], kctx, knew, vctx, vnew, sink_ref, True)
        kw_ref[s, 0:WINDOW - tn, :] = kctx[tn:, :]
        kw_ref[s, WINDOW - tn:WINDOW, :] = knew
        vw_ref[s, 0:WINDOW - tn, :] = vctx[tn:, :]
        vw_ref[s, WINDOW - tn:WINDOW, :] = vnew
        return carry
    lax.fori_loop(0, n_sub, body, 0)


def _attn_sample(sinks, q, k, v, cache_k, cache_v, n_seq, tn, n_sub):
    d = q.shape[1]
    kv = k.shape[1]
    rows = lambda i: (i, 0)
    seqs = lambda i: (i, 0, 0)
    return pl.pallas_call(
        functools.partial(_attn_sample_kernel, n_sub=n_sub, tn=tn),
        grid=(n_seq // n_sub,),
        in_specs=[pl.BlockSpec(memory_space=pltpu.SMEM),
                  pl.BlockSpec((n_sub * tn, d), rows),
                  pl.BlockSpec((n_sub * tn, kv), rows), pl.BlockSpec((n_sub * tn, kv), rows),
                  pl.BlockSpec((n_sub, WINDOW, kv), seqs), pl.BlockSpec((n_sub, WINDOW, kv), seqs)],
        out_specs=[pl.BlockSpec((n_sub * tn, d), rows),
                   pl.BlockSpec((n_sub, WINDOW, kv), seqs), pl.BlockSpec((n_sub, WINDOW, kv), seqs)],
        out_shape=[jax.ShapeDtypeStruct(q.shape, F32),
                   jax.ShapeDtypeStruct(cache_k.shape, F32), jax.ShapeDtypeStruct(cache_v.shape, F32)],
        compiler_params=pltpu.CompilerParams(
            dimension_semantics=("parallel",), vmem_limit_bytes=VMEM_LIMIT),
        name="attn_sample",
    )(sinks, q, k, v, cache_k, cache_v)


def _ssm_prep_kernel(lr_ref, li_ref, ldt_ref, br_ref, bi_ref, ar_ref, ai_ref, bbr_ref, bbi_ref):
    lr = lr_ref[...]
    li = li_ref[...]
    dt = jnp.exp(ldt_ref[...])
    mag = jnp.exp(lr * dt)
    ang = li * dt
    abar_re = mag * jnp.cos(ang)
    abar_im = mag * jnp.sin(ang)
    den = lr * lr + li * li
    nr = abar_re - 1.0
    fr = (nr * lr + abar_im * li) / den
    fi = (abar_im * lr - nr * li) / den
    ar_ref[...] = abar_re
    ai_ref[...] = abar_im
    br = br_ref[...]
    bi = bi_ref[...]
    bbr_ref[...] = fr[:, None, :] * br - fi[:, None, :] * bi
    bbi_ref[...] = fr[:, None, :] * bi + fi[:, None, :] * br


def _ssm_prep(lam_re, lam_im, log_dt, b_re_t, b_im_t):
    g, n = lam_re.shape
    return pl.pallas_call(
        _ssm_prep_kernel,
        out_shape=[jax.ShapeDtypeStruct((g, n), F32), jax.ShapeDtypeStruct((g, n), F32),
                   jax.ShapeDtypeStruct(b_re_t.shape, F32), jax.ShapeDtypeStruct(b_im_t.shape, F32)],
        name="ssm_prep",
    )(lam_re, lam_im, log_dt.reshape(g, 1), b_re_t, b_im_t)


def _ssm_kernel(u_ref, h0_ref, a_ref, d_ref, bd_ref, cd_ref, y_ref, hT_ref, xh_scr, *, n_seq, tt):
    cw = xh_scr.shape[1] // SSM_CHUNKS
    hw = cw // 2
    uc = u_ref.shape[1] // SSM_CHUNKS

    @pl.when(pl.program_id(0) == 0)
    def _():
        hT_ref[...] = h0_ref[...]

    for j in range(SSM_CHUNKS):
        xh_scr[:, j * cw:(j + 1) * cw] = jnp.dot(
            u_ref[:, j * uc:(j + 1) * uc].astype(BF16), bd_ref[j], preferred_element_type=F32)

    def seq_group(s, carry):
        s8 = pl.multiple_of(s * 8, 8)
        for j in range(SSM_CHUNKS):
            re = slice(j * cw, j * cw + hw)
            im = slice(j * cw + hw, (j + 1) * cw)
            ar = a_ref[:, re]
            ai = a_ref[:, im]

            def step(t, h):
                hr, hi = h
                r0 = pl.multiple_of(t * n_seq + s8, 8)
                nr = ar * hr - ai * hi + xh_scr[pl.ds(r0, 8), re]
                ni = ar * hi + ai * hr + xh_scr[pl.ds(r0, 8), im]
                xh_scr[pl.ds(r0, 8), re] = nr
                xh_scr[pl.ds(r0, 8), im] = ni
                return nr, ni

            hr, hi = lax.fori_loop(0, tt, step, (hT_ref[pl.ds(s8, 8), re], hT_ref[pl.ds(s8, 8), im]),
                                   unroll=min(tt, 8))
            hT_ref[pl.ds(s8, 8), re] = hr
            hT_ref[pl.ds(s8, 8), im] = hi
        return carry

    lax.fori_loop(0, n_seq // 8, seq_group, 0)

    for j in range(SSM_CHUNKS):
        y = jnp.dot(xh_scr[:, j * cw:(j + 1) * cw].astype(BF16), cd_ref[j], preferred_element_type=F32)
        cs = slice(j * uc, (j + 1) * uc)
        y_ref[:, cs] = jax.nn.gelu(y + d_ref[:, cs] * u_ref[:, cs]).astype(y_ref.dtype)


def _ssm(u_tb, h0, a8, d_skip, bd, cd, n_seq, tt):
    rows, ssm_w = u_tb.shape
    tile = n_seq * tt
    n_state = h0.shape[1]
    return pl.pallas_call(
        functools.partial(_ssm_kernel, n_seq=n_seq, tt=tt),
        grid=(rows // tile,),
        in_specs=[pl.BlockSpec((tile, ssm_w), lambda i: (i, 0)),
                  _const_spec(h0.shape), _const_spec(a8.shape), _const_spec(d_skip.shape),
                  _const_spec(bd.shape), _const_spec(cd.shape)],
        out_specs=[pl.BlockSpec((tile, ssm_w), lambda i: (i, 0)),
                   pl.BlockSpec(h0.shape, lambda i: (0, 0))],
        out_shape=[jax.ShapeDtypeStruct((rows, ssm_w), BF16), jax.ShapeDtypeStruct(h0.shape, F32)],
        scratch_shapes=[pltpu.VMEM((tile, n_state), F32)],
        compiler_params=pltpu.CompilerParams(
            dimension_semantics=("arbitrary",), vmem_limit_bytes=VMEM_LIMIT),
        name="ssm",
    )(u_tb, h0, a8, d_skip, bd, cd)


def _out_kernel(x1_ref, at_ref, ga_ref, gs_ref, y_ref, glua_ref, glub_ref, wo_ref,
                nb_ref, wg_ref, wu_ref, wd_ref, nf_ref, o_ref, h_scr, act_scr):
    yb = y_ref[...]
    ssm = jnp.dot(yb, glua_ref[...], preferred_element_type=F32) * jax.nn.sigmoid(
        jnp.dot(yb, glub_ref[...], preferred_element_type=F32))
    merged = jax.nn.sigmoid(ga_ref[...]) * at_ref[...] + jax.nn.sigmoid(gs_ref[...]) * ssm
    x2 = x1_ref[...] + jnp.dot(merged.astype(BF16), wo_ref[...], preferred_element_type=F32)
    h_scr[...] = _rms(x2, nb_ref[...]).astype(BF16)
    x3 = x2 + 0.5 * _swiglu(h_scr, wg_ref, wu_ref, wd_ref, act_scr)
    o_ref[...] = _rms(x3, nf_ref[...])


def _out(grid, tm, std, ymap, x1, attn, ga, gs, y_tb, glua, glub, wo, nb, wg, wu, wd, nf):
    d = nb.shape[1]
    d_ff = wg.shape[1]
    shp, spec = std(d)
    return pl.pallas_call(
        _out_kernel,
        grid=grid,
        in_specs=[spec, spec, spec, spec, pl.BlockSpec((tm, glua.shape[0]), ymap),
                  _const_spec(glua.shape), _const_spec(glub.shape), _const_spec(wo.shape),
                  _const_spec(nb.shape), _const_spec(wg.shape), _const_spec(wu.shape),
                  _const_spec(wd.shape), _const_spec(nf.shape)],
        out_specs=spec,
        out_shape=jax.ShapeDtypeStruct(shp, F32),
        scratch_shapes=[pltpu.VMEM((tm, d), BF16), pltpu.VMEM((tm, d_ff), BF16)],
        compiler_params=pltpu.CompilerParams(
            dimension_semantics=("parallel", "parallel"), vmem_limit_bytes=VMEM_LIMIT),
        name="out",
    )(x1, attn, ga, gs, y_tb, glua, glub, wo, nb, wg, wu, wd, nf)


def _block_diag(w, chunks):
    g, a, b = w.shape
    gl = g // chunks
    w = w.reshape(chunks, gl, a, b)
    eye = jnp.eye(gl, dtype=w.dtype)
    return (w[:, :, :, None, :] * eye[None, :, None, :, None]).reshape(chunks, gl * a, gl * b)


def _state_to_lanes(re, im):
    s = re.shape[0]
    re = re.reshape(s, SSM_CHUNKS, -1)
    im = im.reshape(s, SSM_CHUNKS, -1)
    return jnp.stack([re, im], axis=2).reshape(s, -1)


def _lanes_to_state(h, groups):
    s = h.shape[0]
    h = h.reshape(s, SSM_CHUNKS, 2, -1)
    return h[:, :, 0].reshape(s, groups, -1), h[:, :, 1].reshape(s, groups, -1)


def kernel(x_prompt, x_sample, cache_k_win, cache_v_win, state_ssm_re, state_ssm_im, ffn_a_norm, ffn_a_gate, ffn_a_up, ffn_a_down, mix_norm, w_in, attn_sinks, ssm_lambda_re, ssm_lambda_im, ssm_log_dt, ssm_b_re, ssm_b_im, ssm_c_re, ssm_c_im, ssm_d, glu_a, glu_b, w_out, ffn_b_norm, ffn_b_gate, ffn_b_up, ffn_b_down, final_norm):
    depth = ffn_a_norm.shape[0]
    assert depth == 1, "single-layer trunk"
    n_p, seq, d = x_prompt.shape
    n_s, dec, _ = x_sample.shape
    ssm_w = ssm_d.shape[1]
    groups = ssm_lambda_re.shape[1]
    kvw = N_KV_HEADS * HEAD_DIM
    assert cache_k_win.shape[2] == WINDOW and seq % WINDOW == 0 and n_p % 8 == 0 and n_s % 8 == 0

    l = 0
    na, nm, nb = (w[l].reshape(1, d) for w in (ffn_a_norm, mix_norm, ffn_b_norm))
    nf = final_norm.reshape(1, d)
    wga, wua, wda, win = (w[l].astype(BF16) for w in (ffn_a_gate, ffn_a_up, ffn_a_down, w_in))
    wgb, wub, wdb = (w[l].astype(BF16) for w in (ffn_b_gate, ffn_b_up, ffn_b_down))
    glua, glub, wo = (w[l].astype(BF16) for w in (glu_a, glu_b, w_out))
    sinks = attn_sinks[l]
    d_skip = ssm_d[l].reshape(1, ssm_w)

    abar_re, abar_im, bb_re_t, bb_im_t = _ssm_prep(
        ssm_lambda_re[l], ssm_lambda_im[l], ssm_log_dt[l],
        jnp.swapaxes(ssm_b_re[l], 1, 2), jnp.swapaxes(ssm_b_im[l], 1, 2))
    bd = jnp.concatenate([_block_diag(bb_re_t, SSM_CHUNKS), _block_diag(bb_im_t, SSM_CHUNKS)],
                         axis=2).astype(BF16)
    c_re_t = jnp.swapaxes(ssm_c_re[l], 1, 2)
    c_im_t = jnp.swapaxes(ssm_c_im[l], 1, 2)
    cd = jnp.concatenate([_block_diag(c_re_t, SSM_CHUNKS), -_block_diag(c_im_t, SSM_CHUNKS)],
                         axis=1).astype(BF16)
    a8 = jnp.broadcast_to(_state_to_lanes(abar_re[None], abar_im[None]), (8, 2 * groups * SSM_STATE))

    def run_group(x, n_seq, t_len, grid, tm, std, umap, u_shape, stacked_w, attn_fn, h0, tt):
        x1, q, k, v, u_tb, ga, gs, *stacked = _ffn_in(x, grid, tm, std, umap, u_shape,
                                                       na, wga, wua, wda, nm, win, stacked_w)
        attn, extras = attn_fn(q, k, v, *stacked)
        y_tb, h_t = _ssm(u_tb.reshape(t_len * n_seq, ssm_w), h0, a8, d_skip, bd, cd, n_seq, tt)
        y = _out(grid, tm, std, umap, x1, attn, ga, gs, y_tb.reshape(u_shape),
                 glua, glub, wo, nb, wgb, wub, wdb, nf)
        return y, k, v, h_t, extras

    tm_p = 256
    nt = seq // tm_p

    def std_p(width):
        return (n_p * seq, width), pl.BlockSpec((tm_p, width), lambda b, i: (b * nt + i, 0))

    def attn_p(qst, k, v, k2, vt):
        return _attn_prompt(sinks, qst, k2, vt, n_p, seq), None

    w_k = w_in[l][:, d:d + kvw].reshape(d, N_KV_HEADS, 1, HEAD_DIM)
    wk2 = jnp.broadcast_to(w_k, (d, N_KV_HEADS, 2, HEAD_DIM)).reshape(d, 2 * kvw).astype(BF16)
    wvt = w_in[l][:, d + kvw:d + 2 * kvw].T.astype(BF16)

    h0_p = jnp.zeros((n_p, 2 * groups * SSM_STATE), F32)
    y_p, k_p, v_p, h_p, _ = run_group(
        x_prompt.reshape(n_p * seq, d), n_p, seq, (n_p, nt), tm_p, std_p,
        lambda b, i: (i, b), (seq, n_p * ssm_w), (wk2, wvt, lambda b, i: (b * nt + i, 0)),
        attn_p, h0_p, 1024 // n_p)

    def std_s(width):
        return (n_s, dec * width), pl.BlockSpec((n_s, width), lambda t, i: (0, t))

    def attn_s(q, k, v):
        o, kw, vw = _attn_sample(
            sinks, q.reshape(n_s * dec, d), k.reshape(n_s * dec, kvw), v.reshape(n_s * dec, kvw),
            cache_k_win[l].reshape(n_s, WINDOW, kvw), cache_v_win[l].reshape(n_s, WINDOW, kvw),
            n_s, dec, 16)
        return o.reshape(n_s, dec * d), (kw, vw)

    h0_s = _state_to_lanes(state_ssm_re[l], state_ssm_im[l])
    y_s, _, _, h_s, (kw_s, vw_s) = run_group(
        x_sample.reshape(n_s, dec * d), n_s, dec, (dec, 1), n_s, std_s,
        lambda t, i: (t, 0), (dec * n_s, ssm_w), None, attn_s, h0_s, dec)

    win_shape = (depth, -1, WINDOW, N_KV_HEADS, HEAD_DIM)
    k_p = k_p.reshape(n_p, seq, kvw)[:, seq - WINDOW:].reshape((depth, n_p) + win_shape[2:])
    v_p = v_p.reshape(n_p, seq, kvw)[:, seq - WINDOW:].reshape((depth, n_p) + win_shape[2:])
    sp_re, sp_im = _lanes_to_state(h_p, groups)
    ss_re, ss_im = _lanes_to_state(h_s, groups)
    return (y_p.reshape(n_p, seq, d), y_s.reshape(n_s, dec, d),
            k_p, v_p, kw_s.reshape((depth, n_s) + win_shape[2:]), vw_s.reshape((depth, n_s) + win_shape[2:]),
            sp_re[None], sp_im[None], ss_re[None], ss_im[None])
```

```python
import functools
import math

import jax
import jax.numpy as jnp
from jax import lax
from jax.experimental import pallas as pl
from jax.experimental.pallas import tpu as pltpu

F32 = jnp.float32
BF16 = jnp.bfloat16

N_HEADS = 16
N_KV_HEADS = 4
HEAD_DIM = 64
Q_PER_KV = N_HEADS // N_KV_HEADS
WINDOW = 128
SSM_GROUP = 16
SSM_STATE = 64
RMS_EPS = 1e-6
NEG_BIG = -1e30

MXU_COLS = 256
SSM_CHUNKS = 4
VMEM_LIMIT = 60000 * 1024


def _rms(x, g):
    return x * lax.rsqrt(jnp.mean(x * x, axis=-1, keepdims=True) + RMS_EPS) * g


def _swiglu(h_scr, wg_ref, wu_ref, wd_ref, act_scr):
    d_ff = wg_ref.shape[1]
    for c in range(d_ff // MXU_COLS):
        sl = slice(c * MXU_COLS, (c + 1) * MXU_COLS)
        g = jnp.dot(h_scr[...], wg_ref[:, sl], preferred_element_type=F32)
        u = jnp.dot(h_scr[...], wu_ref[:, sl], preferred_element_type=F32)
        act_scr[:, sl] = (jax.nn.silu(g) * u).astype(BF16)
    return jnp.dot(act_scr[...], wd_ref[...], preferred_element_type=F32)


def _ffn_in_kernel(*refs, stacked):
    if stacked:
        (x_ref, na_ref, wg_ref, wu_ref, wd_ref, nm_ref, win_ref, wk2_ref, wvt_ref,
         x1_ref, q_ref, k_ref, v_ref, u_ref, ga_ref, gs_ref, k2_ref, vt_ref, h_scr, act_scr) = refs
    else:
        (x_ref, na_ref, wg_ref, wu_ref, wd_ref, nm_ref, win_ref, wv2_ref,
         x1_ref, q_ref, k_ref, v_ref, u_ref, ga_ref, gs_ref, v2_ref, h_scr, act_scr) = refs
    tm, d = x_ref.shape
    x = x_ref[...]
    h_scr[...] = _rms(x, na_ref[...]).astype(BF16)
    x1 = x + 0.5 * _swiglu(h_scr, wg_ref, wu_ref, wd_ref, act_scr)
    x1_ref[...] = x1
    h_scr[...] = _rms(x1, nm_ref[...]).astype(BF16)

    def proj(w_ref, off, c):
        return jnp.dot(h_scr[...], w_ref[:, off + c * MXU_COLS: off + (c + 1) * MXU_COLS],
                       preferred_element_type=F32)

    q_scale = HEAD_DIM ** -0.5
    pair_w = 2 * HEAD_DIM
    if stacked:
        assert Q_PER_KV * HEAD_DIM == MXU_COLS
        low_half = lax.broadcasted_iota(jnp.int32, (tm, MXU_COLS), 1) % pair_w < HEAD_DIM
        for g in range(N_KV_HEADS):
            r = proj(win_ref, 0, g) * q_scale
            halves = (jnp.where(low_half, r, 0.0).astype(BF16), jnp.where(low_half, 0.0, r).astype(BF16))
            for bl in range(tm // WINDOW):
                for rr in range(Q_PER_KV):
                    row = ((bl * N_KV_HEADS + g) * Q_PER_KV + rr) * WINDOW
                    q_ref[row:row + WINDOW, :] = halves[rr % 2][bl * WINDOW:(bl + 1) * WINDOW,
                                                               (rr // 2) * pair_w:(rr // 2 + 1) * pair_w]
    else:
        for c in range(d // MXU_COLS):
            q_ref[:, c * MXU_COLS:(c + 1) * MXU_COLS] = (proj(win_ref, 0, c) * q_scale).astype(q_ref.dtype)
    off = d
    for ref in (k_ref, v_ref, u_ref, ga_ref, gs_ref):
        width = ref.shape[1]
        for c in range(width // MXU_COLS):
            ref[:, c * MXU_COLS:(c + 1) * MXU_COLS] = proj(win_ref, off, c).astype(ref.dtype)
        off += width
    if stacked:
        for c in range(k2_ref.shape[1] // MXU_COLS):
            k2_ref[:, c * MXU_COLS:(c + 1) * MXU_COLS] = proj(wk2_ref, 0, c).astype(k2_ref.dtype)
        vt_ref[...] = lax.dot_general(wvt_ref[...], h_scr[...], (((1,), (1,)), ((), ())),
                                      preferred_element_type=F32).astype(vt_ref.dtype)
    else:
        for c in range(v2_ref.shape[1] // MXU_COLS):
            v2_ref[:, c * MXU_COLS:(c + 1) * MXU_COLS] = proj(wv2_ref, 0, c).astype(v2_ref.dtype)


def _const_spec(shape):
    nd = len(shape)
    return pl.BlockSpec(shape, lambda *_: (0,) * nd, pipeline_mode=pl.Buffered(1))


def _ffn_in(x2d, grid, tm, std, umap, u_shape, na, wg, wu, wd, nm, win, extra_w):
    d = na.shape[1]
    d_ff = wg.shape[1]
    kv = N_KV_HEADS * HEAD_DIM
    ssm_w = win.shape[1] - 3 * d - 2 * kv
    stacked = len(extra_w) == 3
    n_tok = x2d.size // d
    out_shape, out_specs = [], []

    def add(shape_spec, dtype):
        out_shape.append(jax.ShapeDtypeStruct(shape_spec[0], dtype))
        out_specs.append(shape_spec[1])

    add(std(d), F32)
    if stacked:
        wk2, wvt, tile_map = extra_w
        add(((n_tok * N_HEADS, 2 * HEAD_DIM), pl.BlockSpec((tm * N_HEADS, 2 * HEAD_DIM), tile_map)), BF16)
    else:
        add(std(d), F32)
    add(std(kv), F32)
    add(std(kv), F32)
    add((u_shape, pl.BlockSpec((tm, ssm_w), umap)), F32)
    add(std(d), F32)
    add(std(d), F32)
    weights = [na, wg, wu, wd, nm, win]
    if stacked:
        add(std(2 * kv), BF16)
        add(((kv, n_tok), pl.BlockSpec((kv, tm), lambda *g: tile_map(*g)[::-1])), BF16)
        weights += [wk2, wvt]
    else:
        add(std(2 * kv), F32)
        weights += [extra_w[0]]
    return pl.pallas_call(
        functools.partial(_ffn_in_kernel, stacked=stacked),
        grid=grid,
        in_specs=[std(d)[1]] + [_const_spec(w.shape) for w in weights],
        out_specs=out_specs,
        out_shape=out_shape,
        scratch_shapes=[pltpu.VMEM((tm, d), BF16), pltpu.VMEM((tm, d_ff), BF16)],
        compiler_params=pltpu.CompilerParams(
            dimension_semantics=("parallel", "parallel"), vmem_limit_bytes=VMEM_LIMIT),
        name="ffn_in",
    )(x2d, *weights)


def _attn_prompt_kernel(sink_ref, q_ref, kp_ref, kc_ref, vp_ref, vc_ref, o_ref):
    nq = WINDOW
    nk = 2 * WINDOW
    lanes = Q_PER_KV * nq
    pair_w = 2 * HEAD_DIM
    ctx_ok = pl.program_id(1) > 0
    jk = lax.broadcasted_iota(jnp.int32, (nk, lanes), 0)
    iq = lax.broadcasted_iota(jnp.int32, (nk, lanes), 1) & (nq - 1)
    lo = jnp.maximum(iq, jnp.where(ctx_ok, 0, WINDOW))
    valid = (jk >= lo) & (jk <= iq + WINDOW)
    for g in range(N_KV_HEADS):
        kg = jnp.concatenate([kp_ref[:, g * pair_w:(g + 1) * pair_w], kc_ref[:, g * pair_w:(g + 1) * pair_w]], axis=0)
        qs = q_ref[g * lanes:(g + 1) * lanes, :]
        st = lax.dot_general(kg, qs, (((1,), (1,)), ((), ())), preferred_element_type=F32)
        st = jnp.where(valid, st, NEG_BIG)
        sink = jnp.concatenate(
            [jnp.full((1, nq), sink_ref[g * Q_PER_KV + r], F32) for r in range(Q_PER_KV)], axis=1)
        m = jnp.maximum(jnp.max(st, axis=0, keepdims=True), sink)
        p = jnp.exp(st - m)
        denom = jnp.sum(p, axis=0, keepdims=True) + jnp.exp(sink - m)
        vgt = jnp.concatenate([vp_ref[g * HEAD_DIM:(g + 1) * HEAD_DIM, :],
                               vc_ref[g * HEAD_DIM:(g + 1) * HEAD_DIM, :]], axis=1)
        ot = jnp.dot(vgt, p.astype(BF16), preferred_element_type=F32) * (1.0 / denom)
        for pr in range(Q_PER_KV // 2):
            two = jnp.concatenate([ot[:, (2 * pr) * nq:(2 * pr + 1) * nq],
                                   ot[:, (2 * pr + 1) * nq:(2 * pr + 2) * nq]], axis=0)
            col = (g * Q_PER_KV + 2 * pr) * HEAD_DIM
            o_ref[:, col:col + pair_w] = two.T


def _attn_prompt(sinks, qst, k2, vt, n_seq, seq):
    nb = seq // WINDOW
    d = N_HEADS * HEAD_DIM
    kv = N_KV_HEADS * HEAD_DIM
    cur = lambda b, i: (b * nb + i, 0)
    prev = lambda b, i: (b * nb + jnp.maximum(i - 1, 0), 0)
    cur_t = lambda b, i: (0, b * nb + i)
    prev_t = lambda b, i: (0, b * nb + jnp.maximum(i - 1, 0))
    return pl.pallas_call(
        _attn_prompt_kernel,
        grid=(n_seq, nb),
        in_specs=[pl.BlockSpec(memory_space=pltpu.SMEM),
                  pl.BlockSpec((WINDOW * N_HEADS, 2 * HEAD_DIM), cur),
                  pl.BlockSpec((WINDOW, 2 * kv), prev), pl.BlockSpec((WINDOW, 2 * kv), cur),
                  pl.BlockSpec((kv, WINDOW), prev_t), pl.BlockSpec((kv, WINDOW), cur_t)],
        out_specs=pl.BlockSpec((WINDOW, d), cur),
        out_shape=jax.ShapeDtypeStruct((n_seq * seq, d), F32),
        compiler_params=pltpu.CompilerParams(
            dimension_semantics=("parallel", "parallel"), vmem_limit_bytes=VMEM_LIMIT),
        name="attn_prompt",
    )(sinks, qst, k2, k2, vt, vt)


def _attn_sample_kernel(sink_ref, q_ref, k_ref, v2_ref, ck_ref, cv2_ref, o_ref, kw_ref, vw_ref, *, n_sub, tn, unroll):
    nk = WINDOW + tn
    pair_w = 2 * HEAD_DIM
    kv = N_KV_HEADS * HEAD_DIM
    n_pairs = N_HEADS // 2
    rows = n_pairs * tn
    assert tn & (tn - 1) == 0, "row -> token index uses a power-of-two mask"
    low = lax.broadcasted_iota(jnp.int32, (tn, pair_w), 1) < HEAD_DIM
    low_ctx = lax.broadcasted_iota(jnp.int32, (WINDOW, pair_w), 1) < HEAD_DIM
    tq = lax.broadcasted_iota(jnp.int32, (rows, nk), 0) & (tn - 1)
    jk = lax.broadcasted_iota(jnp.int32, (rows, nk), 1)
    valid = (jk >= tq) & (jk <= tq + WINDOW)
    zeros = jnp.zeros((tn, pair_w), F32)

    def one_sequence(s, carry):
        r0 = pl.multiple_of(s * tn, tn)
        q = q_ref[pl.ds(r0, tn), :]
        knew = k_ref[pl.ds(r0, tn), :]
        kctx = ck_ref[s]
        kk = jnp.concatenate([kctx, knew], axis=0).astype(BF16)
        v2new = v2_ref[pl.ds(r0, tn), :]
        v2ctx = cv2_ref[s]
        vv = jnp.concatenate([v2ctx, v2new], axis=0).astype(BF16)
        outs = []
        for half in range(2):
            blocks = []
            for g in range(N_KV_HEADS):
                for pp in range(Q_PER_KV // 2):
                    piece = q[:, (2 * g + pp) * pair_w:(2 * g + pp + 1) * pair_w]
                    if half != g % 2:
                        piece = pltpu.roll(piece, HEAD_DIM, axis=1)
                    piece = jnp.where(low if g % 2 == 0 else ~low, piece, 0.0)
                    blocks.append(jnp.concatenate([piece, zeros] if g // 2 == 0 else [zeros, piece], axis=1))
            qh = jnp.concatenate(blocks, axis=0).astype(BF16)
            sc = lax.dot_general(qh, kk, (((1,), (1,)), ((), ())), preferred_element_type=F32)
            sc = jnp.where(valid, sc, NEG_BIG)
            sink = jnp.concatenate(
                [jnp.full((tn, 1), sink_ref[2 * pr + half], F32) for pr in range(n_pairs)], axis=0)
            m = jnp.maximum(jnp.max(sc, axis=-1, keepdims=True), sink)
            p = jnp.exp(sc - m)
            denom = jnp.sum(p, axis=-1, keepdims=True) + jnp.exp(sink - m)
            w = (p / denom).astype(BF16)
            outs.append(jnp.dot(w, vv, preferred_element_type=F32))
        for pr in range(n_pairs):
            g = pr // (Q_PER_KV // 2)
            sel = [o[pr * tn:(pr + 1) * tn, g * pair_w:(g + 1) * pair_w] for o in outs]
            o_ref[pl.ds(r0, tn), pr * pair_w:(pr + 1) * pair_w] = jnp.where(low, sel[0], sel[1])
        kw_ref[s, 0:WINDOW - tn, :] = kctx[tn:, :]
        kw_ref[s, WINDOW - tn:WINDOW, :] = knew
        for c in range(kv // pair_w):
            ctx = jnp.where(low_ctx, v2ctx[:, (2 * c) * pair_w:(2 * c + 1) * pair_w],
                            v2ctx[:, (2 * c + 1) * pair_w:(2 * c + 2) * pair_w])
            new = jnp.where(low, v2new[:, (2 * c) * pair_w:(2 * c + 1) * pair_w],
                            v2new[:, (2 * c + 1) * pair_w:(2 * c + 2) * pair_w])
            vw_ref[s, 0:WINDOW - tn, c * pair_w:(c + 1) * pair_w] = ctx[tn:, :]
            vw_ref[s, WINDOW - tn:WINDOW, c * pair_w:(c + 1) * pair_w] = new
        return carry

    lax.fori_loop(0, n_sub, one_sequence, 0, unroll=unroll)


def _attn_sample(sinks, q, k, v2, cache_k, cache_v2, n_seq, tn, n_sub, unroll):
    d = q.shape[1]
    kv = k.shape[1]
    rows = lambda i: (i, 0)
    seqs = lambda i: (i, 0, 0)
    return pl.pallas_call(
        functools.partial(_attn_sample_kernel, n_sub=n_sub, tn=tn, unroll=unroll),
        grid=(n_seq // n_sub,),
        in_specs=[pl.BlockSpec(memory_space=pltpu.SMEM),
                  pl.BlockSpec((n_sub * tn, d), rows),
                  pl.BlockSpec((n_sub * tn, kv), rows), pl.BlockSpec((n_sub * tn, 2 * kv), rows),
                  pl.BlockSpec((n_sub, WINDOW, kv), seqs), pl.BlockSpec((n_sub, WINDOW, 2 * kv), seqs)],
        out_specs=[pl.BlockSpec((n_sub * tn, d), rows),
                   pl.BlockSpec((n_sub, WINDOW, kv), seqs), pl.BlockSpec((n_sub, WINDOW, kv), seqs)],
        out_shape=[jax.ShapeDtypeStruct(q.shape, F32),
                   jax.ShapeDtypeStruct(cache_k.shape, F32), jax.ShapeDtypeStruct(cache_k.shape, F32)],
        compiler_params=pltpu.CompilerParams(
            dimension_semantics=("parallel",), vmem_limit_bytes=VMEM_LIMIT),
        name="attn_sample",
    )(sinks, q, k, v2, cache_k, cache_v2)


def _ssm_prep_kernel(lr_ref, li_ref, ldt_ref, br_ref, bi_ref, ar_ref, ai_ref, bbr_ref, bbi_ref):
    lr = lr_ref[...]
    li = li_ref[...]
    dt = jnp.exp(ldt_ref[...])
    mag = jnp.exp(lr * dt)
    ang = li * dt
    abar_re = mag * jnp.cos(ang)
    abar_im = mag * jnp.sin(ang)
    den = lr * lr + li * li
    nr = abar_re - 1.0
    fr = (nr * lr + abar_im * li) / den
    fi = (abar_im * lr - nr * li) / den
    ar_ref[...] = abar_re
    ai_ref[...] = abar_im
    br = br_ref[...]
    bi = bi_ref[...]
    bbr_ref[...] = fr[:, None, :] * br - fi[:, None, :] * bi
    bbi_ref[...] = fr[:, None, :] * bi + fi[:, None, :] * br


def _ssm_prep(lam_re, lam_im, log_dt, b_re_t, b_im_t):
    g, n = lam_re.shape
    return pl.pallas_call(
        _ssm_prep_kernel,
        out_shape=[jax.ShapeDtypeStruct((g, n), F32), jax.ShapeDtypeStruct((g, n), F32),
                   jax.ShapeDtypeStruct(b_re_t.shape, F32), jax.ShapeDtypeStruct(b_im_t.shape, F32)],
        name="ssm_prep",
    )(lam_re, lam_im, log_dt.reshape(g, 1), b_re_t, b_im_t)


def _ssm_kernel(u_ref, h0_ref, a_ref, d_ref, bd_ref, cd_ref, y_ref, hT_ref, xh_scr, *, n_seq, tt):
    cw = xh_scr.shape[1] // SSM_CHUNKS
    hw = cw // 2
    uc = u_ref.shape[1] // SSM_CHUNKS

    @pl.when(pl.program_id(0) == 0)
    def _():
        hT_ref[...] = h0_ref[...]

    for j in range(SSM_CHUNKS):
        xh_scr[:, j * cw:(j + 1) * cw] = jnp.dot(
            u_ref[:, j * uc:(j + 1) * uc].astype(BF16), bd_ref[j], preferred_element_type=F32)

    def seq_group(s, carry):
        s8 = pl.multiple_of(s * 8, 8)
        for j in range(SSM_CHUNKS):
            re = slice(j * cw, j * cw + hw)
            im = slice(j * cw + hw, (j + 1) * cw)
            ar = a_ref[:, re]
            ai = a_ref[:, im]

            def step(t, h):
                hr, hi = h
                r0 = pl.multiple_of(t * n_seq + s8, 8)
                nr = ar * hr - ai * hi + xh_scr[pl.ds(r0, 8), re]
                ni = ar * hi + ai * hr + xh_scr[pl.ds(r0, 8), im]
                xh_scr[pl.ds(r0, 8), re] = nr
                xh_scr[pl.ds(r0, 8), im] = ni
                return nr, ni

            hr, hi = lax.fori_loop(0, tt, step, (hT_ref[pl.ds(s8, 8), re], hT_ref[pl.ds(s8, 8), im]),
                                   unroll=min(tt, 8))
            hT_ref[pl.ds(s8, 8), re] = hr
            hT_ref[pl.ds(s8, 8), im] = hi
        return carry

    lax.fori_loop(0, n_seq // 8, seq_group, 0)

    for j in range(SSM_CHUNKS):
        y = jnp.dot(xh_scr[:, j * cw:(j + 1) * cw].astype(BF16), cd_ref[j], preferred_element_type=F32)
        cs = slice(j * uc, (j + 1) * uc)
        y_ref[:, cs] = jax.nn.gelu(y + d_ref[:, cs] * u_ref[:, cs]).astype(y_ref.dtype)


def _ssm(u_tb, h0, a8, d_skip, bd, cd, n_seq, tt):
    rows, ssm_w = u_tb.shape
    tile = n_seq * tt
    n_state = h0.shape[1]
    return pl.pallas_call(
        functools.partial(_ssm_kernel, n_seq=n_seq, tt=tt),
        grid=(rows // tile,),
        in_specs=[pl.BlockSpec((tile, ssm_w), lambda i: (i, 0)),
                  _const_spec(h0.shape), _const_spec(a8.shape), _const_spec(d_skip.shape),
                  _const_spec(bd.shape), _const_spec(cd.shape)],
        out_specs=[pl.BlockSpec((tile, ssm_w), lambda i: (i, 0)),
                   pl.BlockSpec(h0.shape, lambda i: (0, 0))],
        out_shape=[jax.ShapeDtypeStruct((rows, ssm_w), BF16), jax.ShapeDtypeStruct(h0.shape, F32)],
        scratch_shapes=[pltpu.VMEM((tile, n_state), F32)],
        compiler_params=pltpu.CompilerParams(
            dimension_semantics=("arbitrary",), vmem_limit_bytes=VMEM_LIMIT),
        name="ssm",
    )(u_tb, h0, a8, d_skip, bd, cd)


def _out_kernel(x1_ref, at_ref, ga_ref, gs_ref, y_ref, glua_ref, glub_ref, wo_ref,
                nb_ref, wg_ref, wu_ref, wd_ref, nf_ref, o_ref, h_scr, act_scr):
    yb = y_ref[...]
    ssm = jnp.dot(yb, glua_ref[...], preferred_element_type=F32) * jax.nn.sigmoid(
        jnp.dot(yb, glub_ref[...], preferred_element_type=F32))
    merged = jax.nn.sigmoid(ga_ref[...]) * at_ref[...] + jax.nn.sigmoid(gs_ref[...]) * ssm
    x2 = x1_ref[...] + jnp.dot(merged.astype(BF16), wo_ref[...], preferred_element_type=F32)
    h_scr[...] = _rms(x2, nb_ref[...]).astype(BF16)
    x3 = x2 + 0.5 * _swiglu(h_scr, wg_ref, wu_ref, wd_ref, act_scr)
    o_ref[...] = _rms(x3, nf_ref[...])


def _out(grid, tm, std, ymap, x1, attn, ga, gs, y_tb, glua, glub, wo, nb, wg, wu, wd, nf):
    d = nb.shape[1]
    d_ff = wg.shape[1]
    shp, spec = std(d)
    return pl.pallas_call(
        _out_kernel,
        grid=grid,
        in_specs=[spec, spec, spec, spec, pl.BlockSpec((tm, glua.shape[0]), ymap),
                  _const_spec(glua.shape), _const_spec(glub.shape), _const_spec(wo.shape),
                  _const_spec(nb.shape), _const_spec(wg.shape), _const_spec(wu.shape),
                  _const_spec(wd.shape), _const_spec(nf.shape)],
        out_specs=spec,
        out_shape=jax.ShapeDtypeStruct(shp, F32),
        scratch_shapes=[pltpu.VMEM((tm, d), BF16), pltpu.VMEM((tm, d_ff), BF16)],
        compiler_params=pltpu.CompilerParams(
            dimension_semantics=("parallel", "parallel"), vmem_limit_bytes=VMEM_LIMIT),
        name="out",
    )(x1, attn, ga, gs, y_tb, glua, glub, wo, nb, wg, wu, wd, nf)


def _block_diag(w, chunks):
    g, a, b = w.shape
    gl = g // chunks
    w = w.reshape(chunks, gl, a, b)
    eye = jnp.eye(gl, dtype=w.dtype)
    return (w[:, :, :, None, :] * eye[None, :, None, :, None]).reshape(chunks, gl * a, gl * b)


def _state_to_lanes(re, im):
    s = re.shape[0]
    re = re.reshape(s, SSM_CHUNKS, -1)
    im = im.reshape(s, SSM_CHUNKS, -1)
    return jnp.stack([re, im], axis=2).reshape(s, -1)


def _lanes_to_state(h, groups):
    s = h.shape[0]
    h = h.reshape(s, SSM_CHUNKS, 2, -1)
    return h[:, :, 0].reshape(s, groups, -1), h[:, :, 1].reshape(s, groups, -1)


def kernel(x_prompt, x_sample, cache_k_win, cache_v_win, state_ssm_re, state_ssm_im, ffn_a_norm, ffn_a_gate, ffn_a_up, ffn_a_down, mix_norm, w_in, attn_sinks, ssm_lambda_re, ssm_lambda_im, ssm_log_dt, ssm_b_re, ssm_b_im, ssm_c_re, ssm_c_im, ssm_d, glu_a, glu_b, w_out, ffn_b_norm, ffn_b_gate, ffn_b_up, ffn_b_down, final_norm):
    depth = ffn_a_norm.shape[0]
    assert depth == 1, "single-layer trunk"
    n_p, seq, d = x_prompt.shape
    n_s, dec, _ = x_sample.shape
    ssm_w = ssm_d.shape[1]
    groups = ssm_lambda_re.shape[1]
    kvw = N_KV_HEADS * HEAD_DIM
    assert cache_k_win.shape[2] == WINDOW and seq % WINDOW == 0 and n_p % 8 == 0 and n_s % 8 == 0

    l = 0
    na, nm, nb = (w[l].reshape(1, d) for w in (ffn_a_norm, mix_norm, ffn_b_norm))
    nf = final_norm.reshape(1, d)
    wga, wua, wda, win = (w[l].astype(BF16) for w in (ffn_a_gate, ffn_a_up, ffn_a_down, w_in))
    wgb, wub, wdb = (w[l].astype(BF16) for w in (ffn_b_gate, ffn_b_up, ffn_b_down))
    glua, glub, wo = (w[l].astype(BF16) for w in (glu_a, glu_b, w_out))
    sinks = attn_sinks[l]
    d_skip = ssm_d[l].reshape(1, ssm_w)

    abar_re, abar_im, bb_re_t, bb_im_t = _ssm_prep(
        ssm_lambda_re[l], ssm_lambda_im[l], ssm_log_dt[l],
        jnp.swapaxes(ssm_b_re[l], 1, 2), jnp.swapaxes(ssm_b_im[l], 1, 2))
    bd = jnp.concatenate([_block_diag(bb_re_t, SSM_CHUNKS), _block_diag(bb_im_t, SSM_CHUNKS)],
                         axis=2).astype(BF16)
    c_re_t = jnp.swapaxes(ssm_c_re[l], 1, 2)
    c_im_t = jnp.swapaxes(ssm_c_im[l], 1, 2)
    cd = jnp.concatenate([_block_diag(c_re_t, SSM_CHUNKS), -_block_diag(c_im_t, SSM_CHUNKS)],
                         axis=1).astype(BF16)
    a8 = jnp.broadcast_to(_state_to_lanes(abar_re[None], abar_im[None]), (8, 2 * groups * SSM_STATE))

    def run_group(x, n_seq, t_len, grid, tm, std, umap, u_shape, extra_w, attn_fn, h0, tt):
        x1, q, k, v, u_tb, ga, gs, *extra = _ffn_in(x, grid, tm, std, umap, u_shape,
                                                     na, wga, wua, wda, nm, win, extra_w)
        attn, extras = attn_fn(q, k, v, *extra)
        y_tb, h_t = _ssm(u_tb.reshape(t_len * n_seq, ssm_w), h0, a8, d_skip, bd, cd, n_seq, tt)
        y = _out(grid, tm, std, umap, x1, attn, ga, gs, y_tb.reshape(u_shape),
                 glua, glub, wo, nb, wgb, wub, wdb, nf)
        return y, k, v, h_t, extras

    tm_p = 256
    nt = seq // tm_p

    def std_p(width):
        return (n_p * seq, width), pl.BlockSpec((tm_p, width), lambda b, i: (b * nt + i, 0))

    def attn_p(qst, k, v, k2, vt):
        return _attn_prompt(sinks, qst, k2, vt, n_p, seq), None

    def twice(w):
        lead = w.shape[:-1]
        w = w.reshape(lead + (N_KV_HEADS, 1, HEAD_DIM))
        return jnp.broadcast_to(w, lead + (N_KV_HEADS, 2, HEAD_DIM)).reshape(lead + (2 * kvw,))

    wk2 = twice(w_in[l][:, d:d + kvw]).astype(BF16)
    wv2 = twice(w_in[l][:, d + kvw:d + 2 * kvw]).astype(BF16)
    wvt = w_in[l][:, d + kvw:d + 2 * kvw].T.astype(BF16)

    h0_p = jnp.zeros((n_p, 2 * groups * SSM_STATE), F32)
    y_p, k_p, v_p, h_p, _ = run_group(
        x_prompt.reshape(n_p * seq, d), n_p, seq, (n_p, nt), tm_p, std_p,
        lambda b, i: (i, b), (seq, n_p * ssm_w), (wk2, wvt, lambda b, i: (b * nt + i, 0)),
        attn_p, h0_p, 1024 // n_p)

    def std_s(width):
        return (n_s, dec * width), pl.BlockSpec((n_s, width), lambda t, i: (0, t))

    def attn_s(q, k, v, v2):
        o, kw, vw = _attn_sample(
            sinks, q.reshape(n_s * dec, d), k.reshape(n_s * dec, kvw), v2.reshape(n_s * dec, 2 * kvw),
            cache_k_win[l].reshape(n_s, WINDOW, kvw), twice(cache_v_win[l].reshape(n_s, WINDOW, kvw)),
            n_s, dec, 16, 4)
        return o.reshape(n_s, dec * d), (kw, vw)

    h0_s = _state_to_lanes(state_ssm_re[l], state_ssm_im[l])
    y_s, _, _, h_s, (kw_s, vw_s) = run_group(
        x_sample.reshape(n_s, dec * d), n_s, dec, (dec, 1), n_s, std_s,
        lambda t, i: (t, 0), (dec * n_s, ssm_w), (wv2,), attn_s, h0_s, dec)

    win_shape = (depth, -1, WINDOW, N_KV_HEADS, HEAD_DIM)
    k_p = k_p.reshape(n_p, seq, kvw)[:, seq - WINDOW:].reshape((depth, n_p) + win_shape[2:])
    v_p = v_p.reshape(n_p, seq, kvw)[:, seq - WINDOW:].reshape((depth, n_p) + win_shape[2:])
    sp_re, sp_im = _lanes_to_state(h_p, groups)
    ss_re, ss_im = _lanes_to_state(h_s, groups)
    return (y_p.reshape(n_p, seq, d), y_s.reshape(n_s, dec, d),
            k_p, v_p, kw_s.reshape((depth, n_s) + win_shape[2:]), vw_s.reshape((depth, n_s) + win_shape[2:]),
            sp_re[None], sp_im[None], ss_re[None], ss_im[None])
```

```python
import functools

import jax
import jax.numpy as jnp
from jax import lax
from jax.experimental import pallas as pl
from jax.experimental.pallas import tpu as pltpu

F32 = jnp.float32
BF16 = jnp.bfloat16

N_HEADS = 16
N_KV_HEADS = 4
HEAD_DIM = 64
Q_PER_KV = N_HEADS // N_KV_HEADS
WINDOW = 128
SSM_STATE = 64
RMS_EPS = 1e-6
NEG_BIG = -1e30

MXU_COLS = 256
SSM_CHUNKS = 4
VMEM_LIMIT = 60000 * 1024


def _rms(x, g):
    return x * lax.rsqrt(jnp.mean(x * x, axis=-1, keepdims=True) + RMS_EPS) * g


def _swiglu(h_scr, wg_ref, wu_ref, wd_ref, act_scr):
    d_ff = wg_ref.shape[1]
    for c in range(d_ff // MXU_COLS):
        sl = slice(c * MXU_COLS, (c + 1) * MXU_COLS)
        g = jnp.dot(h_scr[...], wg_ref[:, sl], preferred_element_type=F32)
        u = jnp.dot(h_scr[...], wu_ref[:, sl], preferred_element_type=F32)
        act_scr[:, sl] = (jax.nn.silu(g) * u).astype(BF16)
    return jnp.dot(act_scr[...], wd_ref[...], preferred_element_type=F32)


def _ffn_in_kernel(*refs, stacked):
    if stacked:
        (x_ref, na_ref, wg_ref, wu_ref, wd_ref, nm_ref, win_ref, wk2_ref, wvt_ref,
         x1_ref, q_ref, k_ref, v_ref, u_ref, ga_ref, gs_ref, k2_ref, vt_ref, h_scr, act_scr) = refs
    else:
        (x_ref, na_ref, wg_ref, wu_ref, wd_ref, nm_ref, win_ref,
         x1_ref, q_ref, k_ref, v_ref, u_ref, ga_ref, gs_ref, h_scr, act_scr) = refs
    tm, d = x_ref.shape
    x = x_ref[...]
    h_scr[...] = _rms(x, na_ref[...]).astype(BF16)
    x1 = x + 0.5 * _swiglu(h_scr, wg_ref, wu_ref, wd_ref, act_scr)
    x1_ref[...] = x1
    h_scr[...] = _rms(x1, nm_ref[...]).astype(BF16)

    def proj(w_ref, off, c):
        return jnp.dot(h_scr[...], w_ref[:, off + c * MXU_COLS: off + (c + 1) * MXU_COLS],
                       preferred_element_type=F32)

    q_scale = HEAD_DIM ** -0.5
    pair_w = 2 * HEAD_DIM
    if stacked:
        assert Q_PER_KV * HEAD_DIM == MXU_COLS
        low_half = lax.broadcasted_iota(jnp.int32, (tm, MXU_COLS), 1) % pair_w < HEAD_DIM
        for g in range(N_KV_HEADS):
            r = proj(win_ref, 0, g) * q_scale
            halves = (jnp.where(low_half, r, 0.0).astype(BF16), jnp.where(low_half, 0.0, r).astype(BF16))
            for bl in range(tm // WINDOW):
                for rr in range(Q_PER_KV):
                    row = ((bl * N_KV_HEADS + g) * Q_PER_KV + rr) * WINDOW
                    q_ref[row:row + WINDOW, :] = halves[rr % 2][bl * WINDOW:(bl + 1) * WINDOW,
                                                               (rr // 2) * pair_w:(rr // 2 + 1) * pair_w]
    else:
        for c in range(d // MXU_COLS):
            q_ref[:, c * MXU_COLS:(c + 1) * MXU_COLS] = (proj(win_ref, 0, c) * q_scale).astype(q_ref.dtype)
    off = d
    for ref in (k_ref, v_ref, u_ref, ga_ref, gs_ref):
        width = ref.shape[1]
        for c in range(width // MXU_COLS):
            ref[:, c * MXU_COLS:(c + 1) * MXU_COLS] = proj(win_ref, off, c).astype(ref.dtype)
        off += width
    if stacked:
        for c in range(k2_ref.shape[1] // MXU_COLS):
            k2_ref[:, c * MXU_COLS:(c + 1) * MXU_COLS] = proj(wk2_ref, 0, c).astype(k2_ref.dtype)
        vt_ref[...] = lax.dot_general(wvt_ref[...], h_scr[...], (((1,), (1,)), ((), ())),
                                      preferred_element_type=F32).astype(vt_ref.dtype)


def _const_spec(shape):
    nd = len(shape)
    return pl.BlockSpec(shape, lambda *_: (0,) * nd, pipeline_mode=pl.Buffered(1))


def _ffn_in(x2d, grid, tm, std, umap, u_shape, na, wg, wu, wd, nm, win, stacked_w=None):
    d = na.shape[1]
    d_ff = wg.shape[1]
    kv = N_KV_HEADS * HEAD_DIM
    ssm_w = win.shape[1] - 3 * d - 2 * kv
    stacked = stacked_w is not None
    n_tok = x2d.size // d
    out_shape, out_specs = [], []

    def add(shape_spec, dtype):
        out_shape.append(jax.ShapeDtypeStruct(shape_spec[0], dtype))
        out_specs.append(shape_spec[1])

    add(std(d), F32)
    if stacked:
        wk2, wvt, tile_map = stacked_w
        add(((n_tok * N_HEADS, 2 * HEAD_DIM), pl.BlockSpec((tm * N_HEADS, 2 * HEAD_DIM), tile_map)), BF16)
    else:
        add(std(d), F32)
    add(std(kv), F32)
    add(std(kv), F32)
    add((u_shape, pl.BlockSpec((tm, ssm_w), umap)), F32)
    add(std(d), F32)
    add(std(d), F32)
    weights = [na, wg, wu, wd, nm, win]
    if stacked:
        add(std(2 * kv), BF16)
        add(((kv, n_tok), pl.BlockSpec((kv, tm), lambda *g: tile_map(*g)[::-1])), BF16)
        weights += [wk2, wvt]
    return pl.pallas_call(
        functools.partial(_ffn_in_kernel, stacked=stacked),
        grid=grid,
        in_specs=[std(d)[1]] + [_const_spec(w.shape) for w in weights],
        out_specs=out_specs,
        out_shape=out_shape,
        scratch_shapes=[pltpu.VMEM((tm, d), BF16), pltpu.VMEM((tm, d_ff), BF16)],
        compiler_params=pltpu.CompilerParams(
            dimension_semantics=("parallel", "parallel"), vmem_limit_bytes=VMEM_LIMIT),
        name="ffn_in",
    )(x2d, *weights)


def _attn_prompt_kernel(sink_ref, q_ref, kp_ref, kc_ref, vp_ref, vc_ref, o_ref):
    nq = WINDOW
    nk = 2 * WINDOW
    lanes = Q_PER_KV * nq
    pair_w = 2 * HEAD_DIM
    ctx_ok = pl.program_id(1) > 0
    jk = lax.broadcasted_iota(jnp.int32, (nk, lanes), 0)
    iq = lax.broadcasted_iota(jnp.int32, (nk, lanes), 1) & (nq - 1)
    lo = jnp.maximum(iq, jnp.where(ctx_ok, 0, WINDOW))
    valid = (jk >= lo) & (jk <= iq + WINDOW)
    for g in range(N_KV_HEADS):
        kg = jnp.concatenate([kp_ref[:, g * pair_w:(g + 1) * pair_w], kc_ref[:, g * pair_w:(g + 1) * pair_w]], axis=0)
        qs = q_ref[g * lanes:(g + 1) * lanes, :]
        st = lax.dot_general(kg, qs, (((1,), (1,)), ((), ())), preferred_element_type=F32)
        st = jnp.where(valid, st, NEG_BIG)
        sink = jnp.concatenate(
            [jnp.full((1, nq), sink_ref[g * Q_PER_KV + r], F32) for r in range(Q_PER_KV)], axis=1)
        m = jnp.maximum(jnp.max(st, axis=0, keepdims=True), sink)
        p = jnp.exp(st - m)
        denom = jnp.sum(p, axis=0, keepdims=True) + jnp.exp(sink - m)
        vgt = jnp.concatenate([vp_ref[g * HEAD_DIM:(g + 1) * HEAD_DIM, :],
                               vc_ref[g * HEAD_DIM:(g + 1) * HEAD_DIM, :]], axis=1)
        ot = jnp.dot(vgt, p.astype(BF16), preferred_element_type=F32) * (1.0 / denom)
        for pr in range(Q_PER_KV // 2):
            two = jnp.concatenate([ot[:, (2 * pr) * nq:(2 * pr + 1) * nq],
                                   ot[:, (2 * pr + 1) * nq:(2 * pr + 2) * nq]], axis=0)
            col = (g * Q_PER_KV + 2 * pr) * HEAD_DIM
            o_ref[:, col:col + pair_w] = two.T


def _attn_prompt(sinks, qst, k2, vt, n_seq, seq):
    nb = seq // WINDOW
    d = N_HEADS * HEAD_DIM
    kv = N_KV_HEADS * HEAD_DIM
    cur = lambda b, i: (b * nb + i, 0)
    prev = lambda b, i: (b * nb + jnp.maximum(i - 1, 0), 0)
    cur_t = lambda b, i: (0, b * nb + i)
    prev_t = lambda b, i: (0, b * nb + jnp.maximum(i - 1, 0))
    return pl.pallas_call(
        _attn_prompt_kernel,
        grid=(n_seq, nb),
        in_specs=[pl.BlockSpec(memory_space=pltpu.SMEM),
                  pl.BlockSpec((WINDOW * N_HEADS, 2 * HEAD_DIM), cur),
                  pl.BlockSpec((WINDOW, 2 * kv), prev), pl.BlockSpec((WINDOW, 2 * kv), cur),
                  pl.BlockSpec((kv, WINDOW), prev_t), pl.BlockSpec((kv, WINDOW), cur_t)],
        out_specs=pl.BlockSpec((WINDOW, d), cur),
        out_shape=jax.ShapeDtypeStruct((n_seq * seq, d), F32),
        compiler_params=pltpu.CompilerParams(
            dimension_semantics=("parallel", "parallel"), vmem_limit_bytes=VMEM_LIMIT),
        name="attn_prompt",
    )(sinks, qst, k2, k2, vt, vt)


def _attn_sample_kernel(sink_ref, q_ref, k_ref, v_ref, ckt_ref, cvt_ref, o_ref, *, n_sub, tn, unroll):
    pair_w = 2 * HEAD_DIM
    n_pairs = N_HEADS // 2
    rows = n_pairs * tn
    assert tn & (tn - 1) == 0, "row -> token index uses a power-of-two mask"
    low = lax.broadcasted_iota(jnp.int32, (tn, pair_w), 1) < HEAD_DIM
    tq = lax.broadcasted_iota(jnp.int32, (rows, WINDOW), 0) & (tn - 1)
    valid_c = lax.broadcasted_iota(jnp.int32, (rows, WINDOW), 1) >= tq
    valid_n = lax.broadcasted_iota(jnp.int32, (rows, tn), 1) <= (
        lax.broadcasted_iota(jnp.int32, (rows, tn), 0) & (tn - 1))
    zeros = jnp.zeros((tn, pair_w), F32)
    nt = (((1,), (1,)), ((), ()))

    def one_sequence(s, carry):
        r0 = pl.multiple_of(s * tn, tn)
        q = q_ref[pl.ds(r0, tn), :]
        knew = k_ref[pl.ds(r0, tn), :].astype(BF16)
        vnew = v_ref[pl.ds(r0, tn), :].astype(BF16)
        kt = ckt_ref[s].astype(BF16)
        vt = cvt_ref[s].astype(BF16)
        outs = []
        for half in range(2):
            blocks = []
            for g in range(N_KV_HEADS):
                for pp in range(Q_PER_KV // 2):
                    piece = q[:, (2 * g + pp) * pair_w:(2 * g + pp + 1) * pair_w]
                    if half != g % 2:
                        piece = pltpu.roll(piece, HEAD_DIM, axis=1)
                    piece = jnp.where(low if g % 2 == 0 else ~low, piece, 0.0)
                    blocks.append(jnp.concatenate([piece, zeros] if g // 2 == 0 else [zeros, piece], axis=1))
            qh = jnp.concatenate(blocks, axis=0).astype(BF16)
            sc = jnp.where(valid_c, jnp.dot(qh, kt, preferred_element_type=F32), NEG_BIG)
            sn = jnp.where(valid_n, lax.dot_general(qh, knew, nt, preferred_element_type=F32), NEG_BIG)
            sink = jnp.concatenate(
                [jnp.full((tn, 1), sink_ref[2 * pr + half], F32) for pr in range(n_pairs)], axis=0)
            m = jnp.maximum(jnp.maximum(jnp.max(sc, axis=-1, keepdims=True),
                                        jnp.max(sn, axis=-1, keepdims=True)), sink)
            pc = jnp.exp(sc - m)
            pn = jnp.exp(sn - m)
            denom = (jnp.sum(pc, axis=-1, keepdims=True) + jnp.sum(pn, axis=-1, keepdims=True)
                     + jnp.exp(sink - m))
            rden = 1.0 / denom
            o = (lax.dot_general((pc * rden).astype(BF16), vt, nt, preferred_element_type=F32)
                 + jnp.dot((pn * rden).astype(BF16), vnew, preferred_element_type=F32))
            outs.append(o)
        for pr in range(n_pairs):
            g = pr // (Q_PER_KV // 2)
            sel = []
            for half in range(2):
                blk = outs[half][pr * tn:(pr + 1) * tn, (g // 2) * pair_w:(g // 2 + 1) * pair_w]
                sel.append(blk if half == g % 2 else pltpu.roll(blk, HEAD_DIM, axis=1))
            o_ref[pl.ds(r0, tn), pr * pair_w:(pr + 1) * pair_w] = jnp.where(low, sel[0], sel[1])
        return carry

    lax.fori_loop(0, n_sub, one_sequence, 0, unroll=unroll)


def _attn_sample(sinks, q, k, v, cache_kt, cache_vt, n_seq, tn, n_sub, unroll):
    d = q.shape[1]
    kv = k.shape[1]
    rows = lambda i: (i, 0)
    seqs = lambda i: (i, 0, 0)
    return pl.pallas_call(
        functools.partial(_attn_sample_kernel, n_sub=n_sub, tn=tn, unroll=unroll),
        grid=(n_seq // n_sub,),
        in_specs=[pl.BlockSpec(memory_space=pltpu.SMEM),
                  pl.BlockSpec((n_sub * tn, d), rows),
                  pl.BlockSpec((n_sub * tn, kv), rows), pl.BlockSpec((n_sub * tn, kv), rows),
                  pl.BlockSpec((n_sub, kv, WINDOW), seqs), pl.BlockSpec((n_sub, kv, WINDOW), seqs)],
        out_specs=pl.BlockSpec((n_sub * tn, d), rows),
        out_shape=jax.ShapeDtypeStruct(q.shape, F32),
        compiler_params=pltpu.CompilerParams(
            dimension_semantics=("parallel",), vmem_limit_bytes=VMEM_LIMIT),
        name="attn_sample",
    )(sinks, q, k, v, cache_kt, cache_vt)


def _ssm_prep_kernel(lr_ref, li_ref, ldt_ref, br_ref, bi_ref, ar_ref, ai_ref, bbr_ref, bbi_ref):
    lr = lr_ref[...]
    li = li_ref[...]
    dt = jnp.exp(ldt_ref[...])
    mag = jnp.exp(lr * dt)
    ang = li * dt
    abar_re = mag * jnp.cos(ang)
    abar_im = mag * jnp.sin(ang)
    den = lr * lr + li * li
    nr = abar_re - 1.0
    fr = (nr * lr + abar_im * li) / den
    fi = (abar_im * lr - nr * li) / den
    ar_ref[...] = abar_re
    ai_ref[...] = abar_im
    br = br_ref[...]
    bi = bi_ref[...]
    bbr_ref[...] = fr[:, None, :] * br - fi[:, None, :] * bi
    bbi_ref[...] = fr[:, None, :] * bi + fi[:, None, :] * br


def _ssm_prep(lam_re, lam_im, log_dt, b_re_t, b_im_t):
    g, n = lam_re.shape
    return pl.pallas_call(
        _ssm_prep_kernel,
        out_shape=[jax.ShapeDtypeStruct((g, n), F32), jax.ShapeDtypeStruct((g, n), F32),
                   jax.ShapeDtypeStruct(b_re_t.shape, F32), jax.ShapeDtypeStruct(b_im_t.shape, F32)],
        name="ssm_prep",
    )(lam_re, lam_im, log_dt.reshape(g, 1), b_re_t, b_im_t)


def _ssm_kernel(u_ref, h0_ref, a_ref, d_ref, bd_ref, cd_ref, y_ref, hT_ref, xh_scr, u_scr, y_scr,
                *, n_seq, tt, seq_lanes, scan_unroll):
    cw = xh_scr.shape[1] // SSM_CHUNKS
    hw = cw // 2
    uc = u_scr.shape[2]
    ssm_w = SSM_CHUNKS * uc

    @pl.when(pl.program_id(0) == 0)
    def _():
        hT_ref[...] = h0_ref[...]

    for j in range(SSM_CHUNKS):
        if seq_lanes:
            for b in range(n_seq):
                u_scr[j, pl.ds(b, tt, stride=n_seq), :] = u_ref[:, b * ssm_w + j * uc:b * ssm_w + (j + 1) * uc]
        else:
            u_scr[j] = u_ref[:, j * uc:(j + 1) * uc]

    for j in range(SSM_CHUNKS):
        xh_scr[:, j * cw:(j + 1) * cw] = jnp.dot(u_scr[j].astype(BF16), bd_ref[j], preferred_element_type=F32)

    def seq_group(s, carry):
        s8 = pl.multiple_of(s * 8, 8)
        for j in range(SSM_CHUNKS):
            re = slice(j * cw, j * cw + hw)
            im = slice(j * cw + hw, (j + 1) * cw)
            ar = a_ref[:, re]
            ai = a_ref[:, im]

            def step(t, h):
                hr, hi = h
                r0 = pl.multiple_of(t * n_seq + s8, 8)
                nr = ar * hr - ai * hi + xh_scr[pl.ds(r0, 8), re]
                ni = ar * hi + ai * hr + xh_scr[pl.ds(r0, 8), im]
                xh_scr[pl.ds(r0, 8), re] = nr
                xh_scr[pl.ds(r0, 8), im] = ni
                return nr, ni

            hr, hi = lax.fori_loop(0, tt, step, (hT_ref[pl.ds(s8, 8), re], hT_ref[pl.ds(s8, 8), im]),
                                   unroll=scan_unroll)
            hT_ref[pl.ds(s8, 8), re] = hr
            hT_ref[pl.ds(s8, 8), im] = hi
        return carry

    lax.fori_loop(0, n_seq // 8, seq_group, 0)

    for j in range(SSM_CHUNKS):
        y = jnp.dot(xh_scr[:, j * cw:(j + 1) * cw].astype(BF16), cd_ref[j], preferred_element_type=F32)
        cs = slice(j * uc, (j + 1) * uc)
        y = jax.nn.gelu(y + d_ref[:, cs] * u_scr[j])
        if seq_lanes:
            y_scr[j] = y
            for b in range(n_seq):
                y_ref[:, b * ssm_w + j * uc:b * ssm_w + (j + 1) * uc] = (
                    y_scr[j, pl.ds(b, tt, stride=n_seq), :].astype(y_ref.dtype))
        else:
            y_ref[:, cs] = y.astype(y_ref.dtype)


def _ssm(u, h0, a8, d_skip, bd, cd, n_seq, tt, seq_lanes, scan_unroll):
    ssm_w = d_skip.shape[1]
    tile = n_seq * tt
    n_state = h0.shape[1]
    block = (tt, n_seq * ssm_w) if seq_lanes else (tile, ssm_w)
    slab = pltpu.VMEM((SSM_CHUNKS, tile, ssm_w // SSM_CHUNKS), F32)
    return pl.pallas_call(
        functools.partial(_ssm_kernel, n_seq=n_seq, tt=tt, seq_lanes=seq_lanes, scan_unroll=scan_unroll),
        grid=(u.shape[0] // block[0],),
        in_specs=[pl.BlockSpec(block, lambda i: (i, 0)),
                  _const_spec(h0.shape), _const_spec(a8.shape), _const_spec(d_skip.shape),
                  _const_spec(bd.shape), _const_spec(cd.shape)],
        out_specs=[pl.BlockSpec(block, lambda i: (i, 0)),
                   pl.BlockSpec(h0.shape, lambda i: (0, 0))],
        out_shape=[jax.ShapeDtypeStruct(u.shape, BF16), jax.ShapeDtypeStruct(h0.shape, F32)],
        scratch_shapes=[pltpu.VMEM((tile, n_state), F32), slab, slab],
        compiler_params=pltpu.CompilerParams(
            dimension_semantics=("arbitrary",), vmem_limit_bytes=VMEM_LIMIT),
        name="ssm",
    )(u, h0, a8, d_skip, bd, cd)


def _out_kernel(x1_ref, at_ref, ga_ref, gs_ref, y_ref, glua_ref, glub_ref, wo_ref,
                nb_ref, wg_ref, wu_ref, wd_ref, nf_ref, o_ref, h_scr, act_scr):
    yb = y_ref[...]
    ssm = jnp.dot(yb, glua_ref[...], preferred_element_type=F32) * jax.nn.sigmoid(
        jnp.dot(yb, glub_ref[...], preferred_element_type=F32))
    merged = jax.nn.sigmoid(ga_ref[...]) * at_ref[...] + jax.nn.sigmoid(gs_ref[...]) * ssm
    x2 = x1_ref[...] + jnp.dot(merged.astype(BF16), wo_ref[...], preferred_element_type=F32)
    h_scr[...] = _rms(x2, nb_ref[...]).astype(BF16)
    x3 = x2 + 0.5 * _swiglu(h_scr, wg_ref, wu_ref, wd_ref, act_scr)
    o_ref[...] = _rms(x3, nf_ref[...])


def _out(grid, tm, std, ymap, x1, attn, ga, gs, y_tb, glua, glub, wo, nb, wg, wu, wd, nf):
    d = nb.shape[1]
    d_ff = wg.shape[1]
    shp, spec = std(d)
    return pl.pallas_call(
        _out_kernel,
        grid=grid,
        in_specs=[spec, spec, spec, spec, pl.BlockSpec((tm, glua.shape[0]), ymap),
                  _const_spec(glua.shape), _const_spec(glub.shape), _const_spec(wo.shape),
                  _const_spec(nb.shape), _const_spec(wg.shape), _const_spec(wu.shape),
                  _const_spec(wd.shape), _const_spec(nf.shape)],
        out_specs=spec,
        out_shape=jax.ShapeDtypeStruct(shp, F32),
        scratch_shapes=[pltpu.VMEM((tm, d), BF16), pltpu.VMEM((tm, d_ff), BF16)],
        compiler_params=pltpu.CompilerParams(
            dimension_semantics=("parallel", "parallel"), vmem_limit_bytes=VMEM_LIMIT),
        name="out",
    )(x1, attn, ga, gs, y_tb, glua, glub, wo, nb, wg, wu, wd, nf)


def _block_diag(w, chunks):
    g, a, b = w.shape
    gl = g // chunks
    w = w.reshape(chunks, gl, a, b)
    eye = jnp.eye(gl, dtype=w.dtype)
    return (w[:, :, :, None, :] * eye[None, :, None, :, None]).reshape(chunks, gl * a, gl * b)


def _state_to_lanes(re, im):
    s = re.shape[0]
    re = re.reshape(s, SSM_CHUNKS, -1)
    im = im.reshape(s, SSM_CHUNKS, -1)
    return jnp.stack([re, im], axis=2).reshape(s, -1)


def _lanes_to_state(h, groups):
    s = h.shape[0]
    h = h.reshape(s, SSM_CHUNKS, 2, -1)
    return h[:, :, 0].reshape(s, groups, -1), h[:, :, 1].reshape(s, groups, -1)


def kernel(x_prompt, x_sample, cache_k_win, cache_v_win, state_ssm_re, state_ssm_im, ffn_a_norm, ffn_a_gate, ffn_a_up, ffn_a_down, mix_norm, w_in, attn_sinks, ssm_lambda_re, ssm_lambda_im, ssm_log_dt, ssm_b_re, ssm_b_im, ssm_c_re, ssm_c_im, ssm_d, glu_a, glu_b, w_out, ffn_b_norm, ffn_b_gate, ffn_b_up, ffn_b_down, final_norm):
    depth = ffn_a_norm.shape[0]
    assert depth == 1, "single-layer trunk"
    n_p, seq, d = x_prompt.shape
    n_s, dec, _ = x_sample.shape
    ssm_w = ssm_d.shape[1]
    groups = ssm_lambda_re.shape[1]
    kvw = N_KV_HEADS * HEAD_DIM
    assert cache_k_win.shape[2] == WINDOW and seq % WINDOW == 0 and n_p % 8 == 0 and n_s % 8 == 0

    l = 0
    na, nm, nb = (w[l].reshape(1, d) for w in (ffn_a_norm, mix_norm, ffn_b_norm))
    nf = final_norm.reshape(1, d)
    wga, wua, wda, win = (w[l].astype(BF16) for w in (ffn_a_gate, ffn_a_up, ffn_a_down, w_in))
    wgb, wub, wdb = (w[l].astype(BF16) for w in (ffn_b_gate, ffn_b_up, ffn_b_down))
    glua, glub, wo = (w[l].astype(BF16) for w in (glu_a, glu_b, w_out))
    sinks = attn_sinks[l]
    d_skip = ssm_d[l].reshape(1, ssm_w)

    abar_re, abar_im, bb_re_t, bb_im_t = _ssm_prep(
        ssm_lambda_re[l], ssm_lambda_im[l], ssm_log_dt[l],
        jnp.swapaxes(ssm_b_re[l], 1, 2), jnp.swapaxes(ssm_b_im[l], 1, 2))
    bd = jnp.concatenate([_block_diag(bb_re_t, SSM_CHUNKS), _block_diag(bb_im_t, SSM_CHUNKS)],
                         axis=2).astype(BF16)
    c_re_t = jnp.swapaxes(ssm_c_re[l], 1, 2)
    c_im_t = jnp.swapaxes(ssm_c_im[l], 1, 2)
    cd = jnp.concatenate([_block_diag(c_re_t, SSM_CHUNKS), -_block_diag(c_im_t, SSM_CHUNKS)],
                         axis=1).astype(BF16)
    a8 = jnp.broadcast_to(_state_to_lanes(abar_re[None], abar_im[None]), (8, 2 * groups * SSM_STATE))

    def run_group(x, n_seq, t_len, grid, tm, std, umap, u_shape, stacked_w, attn_fn, h0, tt):
        x1, q, k, v, u_tb, ga, gs, *stacked = _ffn_in(x, grid, tm, std, umap, u_shape,
                                                       na, wga, wua, wda, nm, win, stacked_w)
        attn = attn_fn(q, k, v, *stacked)
        y_tb, h_t = _ssm(u_tb, h0, a8, d_skip, bd, cd, n_seq, tt,
                         seq_lanes=u_shape[1] != ssm_w, scan_unroll=tt)
        y = _out(grid, tm, std, umap, x1, attn, ga, gs, y_tb, glua, glub, wo, nb, wgb, wub, wdb, nf)
        return y, k, v, h_t

    tm_p = 256
    nt = seq // tm_p

    def std_p(width):
        return (n_p * seq, width), pl.BlockSpec((tm_p, width), lambda b, i: (b * nt + i, 0))

    def attn_p(qst, k, v, k2, vt):
        return _attn_prompt(sinks, qst, k2, vt, n_p, seq)

    w_k = w_in[l][:, d:d + kvw].reshape(d, N_KV_HEADS, 1, HEAD_DIM)
    wk2 = jnp.broadcast_to(w_k, (d, N_KV_HEADS, 2, HEAD_DIM)).reshape(d, 2 * kvw).astype(BF16)
    wvt = w_in[l][:, d + kvw:d + 2 * kvw].T.astype(BF16)

    h0_p = jnp.zeros((n_p, 2 * groups * SSM_STATE), F32)
    y_p, k_p, v_p, h_p = run_group(
        x_prompt.reshape(n_p * seq, d), n_p, seq, (n_p, nt), tm_p, std_p,
        lambda b, i: (i, b), (seq, n_p * ssm_w), (wk2, wvt, lambda b, i: (b * nt + i, 0)),
        attn_p, h0_p, 1024 // n_p)

    def std_s(width):
        return (n_s, dec * width), pl.BlockSpec((n_s, width), lambda t, i: (0, t))

    def cache_t(c):
        return jnp.transpose(c, (0, 2, 3, 1)).reshape(n_s, kvw, WINDOW)

    def attn_s(q, k, v):
        o = _attn_sample(sinks, q.reshape(n_s * dec, d), k.reshape(n_s * dec, kvw), v.reshape(n_s * dec, kvw),
                         cache_t(cache_k_win[l]), cache_t(cache_v_win[l]), n_s, dec, 16, 4)
        return o.reshape(n_s, dec * d)

    h0_s = _state_to_lanes(state_ssm_re[l], state_ssm_im[l])
    y_s, k_s, v_s, h_s = run_group(
        x_sample.reshape(n_s, dec * d), n_s, dec, (dec, 1), n_s, std_s,
        lambda t, i: (t, 0), (dec * n_s, ssm_w), None, attn_s, h0_s, dec)

    def prompt_window(a):
        return a.reshape(n_p, seq, N_KV_HEADS, HEAD_DIM)[None, :, seq - WINDOW:]

    def sample_window(cache, new):
        new = new.reshape(depth, n_s, dec, N_KV_HEADS, HEAD_DIM)
        return jnp.concatenate([cache[:, :, dec:], new], axis=2)

    sp_re, sp_im = _lanes_to_state(h_p, groups)
    ss_re, ss_im = _lanes_to_state(h_s, groups)
    return (y_p.reshape(n_p, seq, d), y_s.reshape(n_s, dec, d),
            prompt_window(k_p), prompt_window(v_p),
            sample_window(cache_k_win, k_s), sample_window(cache_v_win, v_s),
            sp_re[None], sp_im[None], ss_re[None], ss_im[None])
```

```python
import functools

import jax
import jax.numpy as jnp
from jax import lax
from jax.experimental import pallas as pl
from jax.experimental.pallas import tpu as pltpu

F32 = jnp.float32
BF16 = jnp.bfloat16

N_HEADS = 16
N_KV_HEADS = 4
HEAD_DIM = 64
Q_PER_KV = N_HEADS // N_KV_HEADS
WINDOW = 128
SSM_STATE = 64
RMS_EPS = 1e-6
NEG_BIG = -1e30

MXU_COLS = 256
WINDOW_COLS = N_KV_HEADS * HEAD_DIM
SSM_CHUNKS = 4
VMEM_LIMIT = 60000 * 1024


def _rms(x, g):
    return x * lax.rsqrt(jnp.mean(x * x, axis=-1, keepdims=True) + RMS_EPS) * g


def _swiglu(h_scr, wg_ref, wu_ref, wd_ref, act_scr):
    d_ff = wg_ref.shape[1]
    for c in range(d_ff // MXU_COLS):
        sl = slice(c * MXU_COLS, (c + 1) * MXU_COLS)
        g = jnp.dot(h_scr[...], wg_ref[:, sl], preferred_element_type=F32)
        u = jnp.dot(h_scr[...], wu_ref[:, sl], preferred_element_type=F32)
        act_scr[:, sl] = (jax.nn.silu(g) * u).astype(BF16)
    return jnp.dot(act_scr[...], wd_ref[...], preferred_element_type=F32)


def _ffn_in_kernel(*refs, stacked):
    if stacked:
        (x_ref, na_ref, wg_ref, wu_ref, wd_ref, nm_ref, win_ref, wk2_ref, wvt_ref,
         x1_ref, q_ref, k_ref, v_ref, u_ref, ga_ref, gs_ref, k2_ref, vt_ref, h_scr, act_scr) = refs
        assert WINDOW_COLS == k_ref.shape[0]
    else:
        (x_ref, na_ref, wg_ref, wu_ref, wd_ref, nm_ref, win_ref,
         x1_ref, q_ref, k_ref, v_ref, u_ref, ga_ref, gs_ref, h_scr, act_scr) = refs
    tm, d = x_ref.shape
    x = x_ref[...]
    h_scr[...] = _rms(x, na_ref[...]).astype(BF16)
    x1 = x + 0.5 * _swiglu(h_scr, wg_ref, wu_ref, wd_ref, act_scr)
    x1_ref[...] = x1
    h_scr[...] = _rms(x1, nm_ref[...]).astype(BF16)

    def proj(w_ref, off, c):
        return jnp.dot(h_scr[...], w_ref[:, off + c * MXU_COLS: off + (c + 1) * MXU_COLS],
                       preferred_element_type=F32)

    q_scale = HEAD_DIM ** -0.5
    pair_w = 2 * HEAD_DIM
    if stacked:
        assert Q_PER_KV * HEAD_DIM == MXU_COLS
        low_half = lax.broadcasted_iota(jnp.int32, (tm, MXU_COLS), 1) % pair_w < HEAD_DIM
        for g in range(N_KV_HEADS):
            r = proj(win_ref, 0, g) * q_scale
            halves = (jnp.where(low_half, r, 0.0).astype(BF16), jnp.where(low_half, 0.0, r).astype(BF16))
            for bl in range(tm // WINDOW):
                for rr in range(Q_PER_KV):
                    row = ((bl * N_KV_HEADS + g) * Q_PER_KV + rr) * WINDOW
                    q_ref[row:row + WINDOW, :] = halves[rr % 2][bl * WINDOW:(bl + 1) * WINDOW,
                                                               (rr // 2) * pair_w:(rr // 2 + 1) * pair_w]
    else:
        for c in range(d // MXU_COLS):
            q_ref[:, c * MXU_COLS:(c + 1) * MXU_COLS] = (proj(win_ref, 0, c) * q_scale).astype(q_ref.dtype)
    off = d
    for ref in (k_ref, v_ref):
        if stacked:
            ref[...] = proj(win_ref, off, 0)[tm - WINDOW:, :].T
        else:
            ref[...] = proj(win_ref, off, 0)
        off += WINDOW_COLS
    for ref in (u_ref, ga_ref, gs_ref):
        width = ref.shape[1]
        for c in range(width // MXU_COLS):
            ref[:, c * MXU_COLS:(c + 1) * MXU_COLS] = proj(win_ref, off, c).astype(ref.dtype)
        off += width
    if stacked:
        for c in range(k2_ref.shape[1] // MXU_COLS):
            k2_ref[:, c * MXU_COLS:(c + 1) * MXU_COLS] = proj(wk2_ref, 0, c).astype(k2_ref.dtype)
        vt_ref[...] = lax.dot_general(wvt_ref[...], h_scr[...], (((1,), (1,)), ((), ())),
                                      preferred_element_type=F32).astype(vt_ref.dtype)


def _const_spec(shape):
    nd = len(shape)
    return pl.BlockSpec(shape, lambda *_: (0,) * nd, pipeline_mode=pl.Buffered(1))


def _ffn_in(x2d, grid, tm, std, umap, u_shape, na, wg, wu, wd, nm, win, stacked_w=None):
    d = na.shape[1]
    d_ff = wg.shape[1]
    kv = N_KV_HEADS * HEAD_DIM
    ssm_w = win.shape[1] - 3 * d - 2 * kv
    stacked = stacked_w is not None
    n_tok = x2d.size // d
    out_shape, out_specs = [], []

    def add(shape_spec, dtype):
        out_shape.append(jax.ShapeDtypeStruct(shape_spec[0], dtype))
        out_specs.append(shape_spec[1])

    add(std(d), F32)
    if stacked:
        wk2, wvt, tile_map = stacked_w
        add(((n_tok * N_HEADS, 2 * HEAD_DIM), pl.BlockSpec((tm * N_HEADS, 2 * HEAD_DIM), tile_map)), BF16)
    else:
        add(std(d), F32)
    if stacked:
        for _ in range(2):
            add(((grid[0], kv, WINDOW), pl.BlockSpec((None, kv, WINDOW), lambda b, i: (b, 0, 0))), F32)
    else:
        add(std(kv), F32)
        add(std(kv), F32)
    add((u_shape, pl.BlockSpec((tm, ssm_w), umap)), F32)
    add(std(d), F32)
    add(std(d), F32)
    weights = [na, wg, wu, wd, nm, win]
    if stacked:
        add(std(2 * kv), BF16)
        add(((kv, n_tok), pl.BlockSpec((kv, tm), lambda *g: tile_map(*g)[::-1])), BF16)
        weights += [wk2, wvt]
    return pl.pallas_call(
        functools.partial(_ffn_in_kernel, stacked=stacked),
        grid=grid,
        in_specs=[std(d)[1]] + [_const_spec(w.shape) for w in weights],
        out_specs=out_specs,
        out_shape=out_shape,
        scratch_shapes=[pltpu.VMEM((tm, d), BF16), pltpu.VMEM((tm, d_ff), BF16)],
        compiler_params=pltpu.CompilerParams(
            dimension_semantics=("parallel", "arbitrary"), vmem_limit_bytes=VMEM_LIMIT),
        name="ffn_in",
    )(x2d, *weights)


def _attn_prompt_kernel(sink_ref, q_ref, kp_ref, kc_ref, vp_ref, vc_ref, o_ref):
    nq = WINDOW
    nk = 2 * WINDOW
    lanes = Q_PER_KV * nq
    pair_w = 2 * HEAD_DIM
    ctx_ok = pl.program_id(1) > 0
    jk = lax.broadcasted_iota(jnp.int32, (nk, lanes), 0)
    iq = lax.broadcasted_iota(jnp.int32, (nk, lanes), 1) & (nq - 1)
    lo = jnp.maximum(iq, jnp.where(ctx_ok, 0, WINDOW))
    valid = (jk >= lo) & (jk <= iq + WINDOW)
    for g in range(N_KV_HEADS):
        kg = jnp.concatenate([kp_ref[:, g * pair_w:(g + 1) * pair_w], kc_ref[:, g * pair_w:(g + 1) * pair_w]], axis=0)
        qs = q_ref[g * lanes:(g + 1) * lanes, :]
        st = lax.dot_general(kg, qs, (((1,), (1,)), ((), ())), preferred_element_type=F32)
        st = jnp.where(valid, st, NEG_BIG)
        sink = jnp.concatenate(
            [jnp.full((1, nq), sink_ref[g * Q_PER_KV + r], F32) for r in range(Q_PER_KV)], axis=1)
        m = jnp.maximum(jnp.max(st, axis=0, keepdims=True), sink)
        p = jnp.exp(st - m)
        denom = jnp.sum(p, axis=0, keepdims=True) + jnp.exp(sink - m)
        vgt = jnp.concatenate([vp_ref[g * HEAD_DIM:(g + 1) * HEAD_DIM, :],
                               vc_ref[g * HEAD_DIM:(g + 1) * HEAD_DIM, :]], axis=1)
        ot = jnp.dot(vgt, p.astype(BF16), preferred_element_type=F32) * (1.0 / denom)
        for pr in range(Q_PER_KV // 2):
            two = jnp.concatenate([ot[:, (2 * pr) * nq:(2 * pr + 1) * nq],
                                   ot[:, (2 * pr + 1) * nq:(2 * pr + 2) * nq]], axis=0)
            col = (g * Q_PER_KV + 2 * pr) * HEAD_DIM
            o_ref[:, col:col + pair_w] = two.T


def _attn_prompt(sinks, qst, k2, vt, n_seq, seq):
    nb = seq // WINDOW
    d = N_HEADS * HEAD_DIM
    kv = N_KV_HEADS * HEAD_DIM
    cur = lambda b, i: (b * nb + i, 0)
    prev = lambda b, i: (b * nb + jnp.maximum(i - 1, 0), 0)
    cur_t = lambda b, i: (0, b * nb + i)
    prev_t = lambda b, i: (0, b * nb + jnp.maximum(i - 1, 0))
    return pl.pallas_call(
        _attn_prompt_kernel,
        grid=(n_seq, nb),
        in_specs=[pl.BlockSpec(memory_space=pltpu.SMEM),
                  pl.BlockSpec((WINDOW * N_HEADS, 2 * HEAD_DIM), cur),
                  pl.BlockSpec((WINDOW, 2 * kv), prev), pl.BlockSpec((WINDOW, 2 * kv), cur),
                  pl.BlockSpec((kv, WINDOW), prev_t), pl.BlockSpec((kv, WINDOW), cur_t)],
        out_specs=pl.BlockSpec((WINDOW, d), cur),
        out_shape=jax.ShapeDtypeStruct((n_seq * seq, d), F32),
        compiler_params=pltpu.CompilerParams(
            dimension_semantics=("parallel", "parallel"), vmem_limit_bytes=VMEM_LIMIT),
        name="attn_prompt",
    )(sinks, qst, k2, k2, vt, vt)


def _attn_sample_kernel(sink_ref, q_ref, k_ref, v_ref, ckt_ref, cvt_ref, o_ref, kw_ref, vw_ref, *, n_sub, tn, unroll):
    pair_w = 2 * HEAD_DIM
    n_pairs = N_HEADS // 2
    rows = n_pairs * tn
    kv = N_KV_HEADS * HEAD_DIM
    old = WINDOW - tn
    assert tn & (tn - 1) == 0, "row -> token index uses a power-of-two mask"
    low = lax.broadcasted_iota(jnp.int32, (tn, pair_w), 1) < HEAD_DIM
    tq = lax.broadcasted_iota(jnp.int32, (rows, WINDOW), 0) & (tn - 1)
    col = lax.broadcasted_iota(jnp.int32, (rows, WINDOW), 1)
    valid_c = col >= tq
    valid_n = (col >= old) & (col - old <= tq)
    is_new = lax.broadcasted_iota(jnp.int32, (kv, WINDOW), 1) >= old
    zeros = jnp.zeros((tn, pair_w), F32)
    pad = jnp.zeros((old, kv), F32)
    nt = (((1,), (1,)), ((), ()))

    def one_sequence(s, carry):
        r0 = pl.multiple_of(s * tn, tn)
        q = q_ref[pl.ds(r0, tn), :]
        kt = ckt_ref[s]
        vt = cvt_ref[s]
        knt = jnp.concatenate([pad, k_ref[pl.ds(r0, tn), :]], axis=0).T
        vnt = jnp.concatenate([pad, v_ref[pl.ds(r0, tn), :]], axis=0).T
        kw_ref[s] = jnp.where(is_new, knt, pltpu.roll(kt, old, axis=1))
        vw_ref[s] = jnp.where(is_new, vnt, pltpu.roll(vt, old, axis=1))
        kt, vt, knt, vnt = (a.astype(BF16) for a in (kt, vt, knt, vnt))
        outs = []
        for half in range(2):
            blocks = []
            for g in range(N_KV_HEADS):
                for pp in range(Q_PER_KV // 2):
                    piece = q[:, (2 * g + pp) * pair_w:(2 * g + pp + 1) * pair_w]
                    if half != g % 2:
                        piece = pltpu.roll(piece, HEAD_DIM, axis=1)
                    piece = jnp.where(low if g % 2 == 0 else ~low, piece, 0.0)
                    blocks.append(jnp.concatenate([piece, zeros] if g // 2 == 0 else [zeros, piece], axis=1))
            qh = jnp.concatenate(blocks, axis=0).astype(BF16)
            sc = jnp.where(valid_c, jnp.dot(qh, kt, preferred_element_type=F32), NEG_BIG)
            sn = jnp.where(valid_n, jnp.dot(qh, knt, preferred_element_type=F32), NEG_BIG)
            sink = jnp.concatenate(
                [jnp.full((tn, 1), sink_ref[2 * pr + half], F32) for pr in range(n_pairs)], axis=0)
            m = jnp.maximum(jnp.max(jnp.maximum(sc, sn), axis=-1, keepdims=True), sink)
            pc = jnp.exp(sc - m)
            pn = jnp.exp(sn - m)
            rden = 1.0 / (jnp.sum(pc + pn, axis=-1, keepdims=True) + jnp.exp(sink - m))
            outs.append(lax.dot_general((pc * rden).astype(BF16), vt, nt, preferred_element_type=F32)
                        + lax.dot_general((pn * rden).astype(BF16), vnt, nt, preferred_element_type=F32))
        for pr in range(n_pairs):
            g = pr // (Q_PER_KV // 2)
            sel = []
            for half in range(2):
                blk = outs[half][pr * tn:(pr + 1) * tn, (g // 2) * pair_w:(g // 2 + 1) * pair_w]
                sel.append(blk if half == g % 2 else pltpu.roll(blk, HEAD_DIM, axis=1))
            o_ref[pl.ds(r0, tn), pr * pair_w:(pr + 1) * pair_w] = jnp.where(low, sel[0], sel[1])
        return carry

    lax.fori_loop(0, n_sub, one_sequence, 0, unroll=unroll)


def _attn_sample(sinks, q, k, v, cache_kt, cache_vt, n_seq, tn, n_sub, unroll):
    d = q.shape[1]
    kv = k.shape[1]
    rows = lambda i: (i, 0)
    seqs = lambda i: (i, 0, 0)
    win = pl.BlockSpec((n_sub, kv, WINDOW), seqs)
    return pl.pallas_call(
        functools.partial(_attn_sample_kernel, n_sub=n_sub, tn=tn, unroll=unroll),
        grid=(n_seq // n_sub,),
        in_specs=[pl.BlockSpec(memory_space=pltpu.SMEM),
                  pl.BlockSpec((n_sub * tn, d), rows),
                  pl.BlockSpec((n_sub * tn, kv), rows), pl.BlockSpec((n_sub * tn, kv), rows), win, win],
        out_specs=[pl.BlockSpec((n_sub * tn, d), rows), win, win],
        out_shape=[jax.ShapeDtypeStruct(q.shape, F32),
                   jax.ShapeDtypeStruct(cache_kt.shape, F32), jax.ShapeDtypeStruct(cache_vt.shape, F32)],
        compiler_params=pltpu.CompilerParams(
            dimension_semantics=("parallel",), vmem_limit_bytes=VMEM_LIMIT),
        name="attn_sample",
    )(sinks, q, k, v, cache_kt, cache_vt)


def _ssm_prep_kernel(lr_ref, li_ref, ldt_ref, br_ref, bi_ref, ar_ref, ai_ref, bbr_ref, bbi_ref):
    lr = lr_ref[...]
    li = li_ref[...]
    dt = jnp.exp(ldt_ref[...])
    mag = jnp.exp(lr * dt)
    ang = li * dt
    abar_re = mag * jnp.cos(ang)
    abar_im = mag * jnp.sin(ang)
    den = lr * lr + li * li
    nr = abar_re - 1.0
    fr = (nr * lr + abar_im * li) / den
    fi = (abar_im * lr - nr * li) / den
    ar_ref[...] = abar_re
    ai_ref[...] = abar_im
    br = br_ref[...]
    bi = bi_ref[...]
    bbr_ref[...] = fr[:, None, :] * br - fi[:, None, :] * bi
    bbi_ref[...] = fr[:, None, :] * bi + fi[:, None, :] * br


def _ssm_prep(lam_re, lam_im, log_dt, b_re_t, b_im_t):
    g, n = lam_re.shape
    return pl.pallas_call(
        _ssm_prep_kernel,
        out_shape=[jax.ShapeDtypeStruct((g, n), F32), jax.ShapeDtypeStruct((g, n), F32),
                   jax.ShapeDtypeStruct(b_re_t.shape, F32), jax.ShapeDtypeStruct(b_im_t.shape, F32)],
        name="ssm_prep",
    )(lam_re, lam_im, log_dt.reshape(g, 1), b_re_t, b_im_t)


def _ssm_kernel(u_ref, h0_ref, a_ref, d_ref, bd_ref, cd_ref, y_ref, hT_ref, xh_scr, u_scr, y_scr,
                *, n_seq, tt, seq_lanes, scan_unroll):
    cw = xh_scr.shape[1] // SSM_CHUNKS
    hw = cw // 2
    uc = u_scr.shape[2]
    ssm_w = SSM_CHUNKS * uc

    @pl.when(pl.program_id(0) == 0)
    def _():
        hT_ref[...] = h0_ref[...]

    for j in range(SSM_CHUNKS):
        for b in range(n_seq):
            if seq_lanes:
                u_b = u_ref[:, b * ssm_w + j * uc:b * ssm_w + (j + 1) * uc]
            else:
                u_b = u_ref[b * tt:(b + 1) * tt, j * uc:(j + 1) * uc]
            u_scr[j, pl.ds(b, tt, stride=n_seq), :] = u_b

    for j in range(SSM_CHUNKS):
        xh_scr[:, j * cw:(j + 1) * cw] = jnp.dot(u_scr[j].astype(BF16), bd_ref[j], preferred_element_type=F32)

    def seq_group(s, carry):
        s8 = pl.multiple_of(s * 8, 8)
        for j in range(SSM_CHUNKS):
            re = slice(j * cw, j * cw + hw)
            im = slice(j * cw + hw, (j + 1) * cw)
            ar = a_ref[:, re]
            ai = a_ref[:, im]

            def step(t, h):
                hr, hi = h
                r0 = pl.multiple_of(t * n_seq + s8, 8)
                nr = ar * hr - ai * hi + xh_scr[pl.ds(r0, 8), re]
                ni = ar * hi + ai * hr + xh_scr[pl.ds(r0, 8), im]
                xh_scr[pl.ds(r0, 8), re] = nr
                xh_scr[pl.ds(r0, 8), im] = ni
                return nr, ni

            hr, hi = lax.fori_loop(0, tt, step, (hT_ref[pl.ds(s8, 8), re], hT_ref[pl.ds(s8, 8), im]),
                                   unroll=scan_unroll)
            hT_ref[pl.ds(s8, 8), re] = hr
            hT_ref[pl.ds(s8, 8), im] = hi
        return carry

    lax.fori_loop(0, n_seq // 8, seq_group, 0)

    for j in range(SSM_CHUNKS):
        y = jnp.dot(xh_scr[:, j * cw:(j + 1) * cw].astype(BF16), cd_ref[j], preferred_element_type=F32)
        cs = slice(j * uc, (j + 1) * uc)
        y_scr[j] = jax.nn.gelu(y + d_ref[:, cs] * u_scr[j])
        for b in range(n_seq):
            y_b = y_scr[j, pl.ds(b, tt, stride=n_seq), :].astype(y_ref.dtype)
            if seq_lanes:
                y_ref[:, b * ssm_w + j * uc:b * ssm_w + (j + 1) * uc] = y_b
            else:
                y_ref[b * tt:(b + 1) * tt, cs] = y_b


def _ssm(u, h0, a8, d_skip, bd, cd, n_seq, tt, seq_lanes, scan_unroll):
    ssm_w = d_skip.shape[1]
    tile = n_seq * tt
    n_state = h0.shape[1]
    block = (tt, n_seq * ssm_w) if seq_lanes else (tile, ssm_w)
    assert seq_lanes or u.shape[0] == tile, "row order (seq, t) cannot be tiled over time"
    slab = pltpu.VMEM((SSM_CHUNKS, tile, ssm_w // SSM_CHUNKS), F32)
    return pl.pallas_call(
        functools.partial(_ssm_kernel, n_seq=n_seq, tt=tt, seq_lanes=seq_lanes, scan_unroll=scan_unroll),
        grid=(u.shape[0] // block[0],),
        in_specs=[pl.BlockSpec(block, lambda i: (i, 0)),
                  _const_spec(h0.shape), _const_spec(a8.shape), _const_spec(d_skip.shape),
                  _const_spec(bd.shape), _const_spec(cd.shape)],
        out_specs=[pl.BlockSpec(block, lambda i: (i, 0)),
                   pl.BlockSpec(h0.shape, lambda i: (0, 0))],
        out_shape=[jax.ShapeDtypeStruct(u.shape, BF16), jax.ShapeDtypeStruct(h0.shape, F32)],
        scratch_shapes=[pltpu.VMEM((tile, n_state), F32), slab, slab],
        compiler_params=pltpu.CompilerParams(
            dimension_semantics=("arbitrary",), vmem_limit_bytes=VMEM_LIMIT),
        name="ssm",
    )(u, h0, a8, d_skip, bd, cd)


def _out_kernel(x1_ref, at_ref, ga_ref, gs_ref, y_ref, glua_ref, glub_ref, wo_ref,
                nb_ref, wg_ref, wu_ref, wd_ref, nf_ref, o_ref, h_scr, act_scr):
    yb = y_ref[...]
    ssm = jnp.dot(yb, glua_ref[...], preferred_element_type=F32) * jax.nn.sigmoid(
        jnp.dot(yb, glub_ref[...], preferred_element_type=F32))
    merged = jax.nn.sigmoid(ga_ref[...]) * at_ref[...] + jax.nn.sigmoid(gs_ref[...]) * ssm
    x2 = x1_ref[...] + jnp.dot(merged.astype(BF16), wo_ref[...], preferred_element_type=F32)
    h_scr[...] = _rms(x2, nb_ref[...]).astype(BF16)
    x3 = x2 + 0.5 * _swiglu(h_scr, wg_ref, wu_ref, wd_ref, act_scr)
    o_ref[...] = _rms(x3, nf_ref[...])


def _out(grid, tm, std, ymap, x1, attn, ga, gs, y_tb, glua, glub, wo, nb, wg, wu, wd, nf):
    d = nb.shape[1]
    d_ff = wg.shape[1]
    shp, spec = std(d)
    return pl.pallas_call(
        _out_kernel,
        grid=grid,
        in_specs=[spec, spec, spec, spec, pl.BlockSpec((tm, glua.shape[0]), ymap),
                  _const_spec(glua.shape), _const_spec(glub.shape), _const_spec(wo.shape),
                  _const_spec(nb.shape), _const_spec(wg.shape), _const_spec(wu.shape),
                  _const_spec(wd.shape), _const_spec(nf.shape)],
        out_specs=spec,
        out_shape=jax.ShapeDtypeStruct(shp, F32),
        scratch_shapes=[pltpu.VMEM((tm, d), BF16), pltpu.VMEM((tm, d_ff), BF16)],
        compiler_params=pltpu.CompilerParams(
            dimension_semantics=("parallel", "parallel"), vmem_limit_bytes=VMEM_LIMIT),
        name="out",
    )(x1, attn, ga, gs, y_tb, glua, glub, wo, nb, wg, wu, wd, nf)


def _block_diag(w, chunks):
    g, a, b = w.shape
    gl = g // chunks
    w = w.reshape(chunks, gl, a, b)
    eye = jnp.eye(gl, dtype=w.dtype)
    return (w[:, :, :, None, :] * eye[None, :, None, :, None]).reshape(chunks, gl * a, gl * b)


def _state_to_lanes(re, im):
    s = re.shape[0]
    re = re.reshape(s, SSM_CHUNKS, -1)
    im = im.reshape(s, SSM_CHUNKS, -1)
    return jnp.stack([re, im], axis=2).reshape(s, -1)


def _lanes_to_state(h, groups):
    s = h.shape[0]
    h = h.reshape(s, SSM_CHUNKS, 2, -1)
    return h[:, :, 0].reshape(s, groups, -1), h[:, :, 1].reshape(s, groups, -1)


def kernel(x_prompt, x_sample, cache_k_win, cache_v_win, state_ssm_re, state_ssm_im, ffn_a_norm, ffn_a_gate, ffn_a_up, ffn_a_down, mix_norm, w_in, attn_sinks, ssm_lambda_re, ssm_lambda_im, ssm_log_dt, ssm_b_re, ssm_b_im, ssm_c_re, ssm_c_im, ssm_d, glu_a, glu_b, w_out, ffn_b_norm, ffn_b_gate, ffn_b_up, ffn_b_down, final_norm):
    depth = ffn_a_norm.shape[0]
    assert depth == 1, "single-layer trunk"
    n_p, seq, d = x_prompt.shape
    n_s, dec, _ = x_sample.shape
    ssm_w = ssm_d.shape[1]
    groups = ssm_lambda_re.shape[1]
    kvw = N_KV_HEADS * HEAD_DIM
    assert cache_k_win.shape[2] == WINDOW and seq % WINDOW == 0 and n_p % 8 == 0 and n_s % 8 == 0

    l = 0
    na, nm, nb = (w[l].reshape(1, d) for w in (ffn_a_norm, mix_norm, ffn_b_norm))
    nf = final_norm.reshape(1, d)
    wga, wua, wda, win = (w[l].astype(BF16) for w in (ffn_a_gate, ffn_a_up, ffn_a_down, w_in))
    wgb, wub, wdb = (w[l].astype(BF16) for w in (ffn_b_gate, ffn_b_up, ffn_b_down))
    glua, glub, wo = (w[l].astype(BF16) for w in (glu_a, glu_b, w_out))
    sinks = attn_sinks[l]
    d_skip = ssm_d[l].reshape(1, ssm_w)

    abar_re, abar_im, bb_re_t, bb_im_t = _ssm_prep(
        ssm_lambda_re[l], ssm_lambda_im[l], ssm_log_dt[l],
        jnp.swapaxes(ssm_b_re[l], 1, 2), jnp.swapaxes(ssm_b_im[l], 1, 2))
    bd = jnp.concatenate([_block_diag(bb_re_t, SSM_CHUNKS), _block_diag(bb_im_t, SSM_CHUNKS)],
                         axis=2).astype(BF16)
    c_re_t = jnp.swapaxes(ssm_c_re[l], 1, 2)
    c_im_t = jnp.swapaxes(ssm_c_im[l], 1, 2)
    cd = jnp.concatenate([_block_diag(c_re_t, SSM_CHUNKS), -_block_diag(c_im_t, SSM_CHUNKS)],
                         axis=1).astype(BF16)
    a8 = jnp.broadcast_to(_state_to_lanes(abar_re[None], abar_im[None]), (8, 2 * groups * SSM_STATE))

    def run_group(x, n_seq, t_len, grid, tm, std, umap, u_shape, stacked_w, attn_fn, h0, tt):
        x1, q, k, v, u_tb, ga, gs, *stacked = _ffn_in(x, grid, tm, std, umap, u_shape,
                                                       na, wga, wua, wda, nm, win, stacked_w)
        attn = attn_fn(q, k, v, *stacked)
        y_tb, h_t = _ssm(u_tb, h0, a8, d_skip, bd, cd, n_seq, tt,
                         seq_lanes=u_shape[1] != ssm_w, scan_unroll=tt)
        y = _out(grid, tm, std, umap, x1, attn, ga, gs, y_tb, glua, glub, wo, nb, wgb, wub, wdb, nf)
        return y, k, v, h_t

    tm_p = 256
    nt = seq // tm_p

    def std_p(width):
        return (n_p * seq, width), pl.BlockSpec((tm_p, width), lambda b, i: (b * nt + i, 0))

    def attn_p(qst, k, v, k2, vt):
        return _attn_prompt(sinks, qst, k2, vt, n_p, seq)

    w_k = w_in[l][:, d:d + kvw].reshape(d, N_KV_HEADS, 1, HEAD_DIM)
    wk2 = jnp.broadcast_to(w_k, (d, N_KV_HEADS, 2, HEAD_DIM)).reshape(d, 2 * kvw).astype(BF16)
    wvt = w_in[l][:, d + kvw:d + 2 * kvw].T.astype(BF16)

    h0_p = jnp.zeros((n_p, 2 * groups * SSM_STATE), F32)
    y_p, k_p, v_p, h_p = run_group(
        x_prompt.reshape(n_p * seq, d), n_p, seq, (n_p, nt), tm_p, std_p,
        lambda b, i: (i, b), (seq, n_p * ssm_w), (wk2, wvt, lambda b, i: (b * nt + i, 0)),
        attn_p, h0_p, 1024 // n_p)

    tm_s = 128

    def std_s(width):
        return (n_s * dec, width), pl.BlockSpec((tm_s, width), lambda i, j: (i, 0))

    def window_t(c):
        return jnp.transpose(c, (0, 2, 3, 1)).reshape(c.shape[0], kvw, WINDOW)

    def window(ct):
        return jnp.transpose(ct.reshape(ct.shape[0], N_KV_HEADS, HEAD_DIM, WINDOW), (0, 3, 1, 2))[None]

    new_windows = []

    def attn_s(q, k, v):
        o, kw, vw = _attn_sample(sinks, q, k, v, window_t(cache_k_win[l]), window_t(cache_v_win[l]),
                                 n_s, dec, 16, 4)
        new_windows.extend([kw, vw])
        return o

    h0_s = _state_to_lanes(state_ssm_re[l], state_ssm_im[l])
    y_s, _, _, h_s = run_group(
        x_sample.reshape(n_s * dec, d), n_s, dec, (n_s * dec // tm_s, 1), tm_s, std_s,
        lambda i, j: (i, 0), (n_s * dec, ssm_w), None, attn_s, h0_s, dec)

    sp_re, sp_im = _lanes_to_state(h_p, groups)
    ss_re, ss_im = _lanes_to_state(h_s, groups)
    return (y_p.reshape(n_p, seq, d), y_s.reshape(n_s, dec, d),
            window(k_p), window(v_p), window(new_windows[0]), window(new_windows[1]),
            sp_re[None], sp_im[None], ss_re[None], ss_im[None])
```

```python
import functools

import jax
import jax.numpy as jnp
from jax import lax
from jax.experimental import pallas as pl
from jax.experimental.pallas import tpu as pltpu

F32 = jnp.float32
BF16 = jnp.bfloat16

N_HEADS = 16
N_KV_HEADS = 4
HEAD_DIM = 64
Q_PER_KV = N_HEADS // N_KV_HEADS
WINDOW = 128
SSM_STATE = 64
RMS_EPS = 1e-6
NEG_BIG = -1e30
LOG2_E = 1.4426950408889634

MXU_COLS = 256
WINDOW_COLS = N_KV_HEADS * HEAD_DIM
SSM_CHUNKS = 4
VMEM_LIMIT = 60000 * 1024


def _rms(x, g):
    return x * lax.rsqrt(jnp.mean(x * x, axis=-1, keepdims=True) + RMS_EPS) * g


def _swiglu(h_scr, wg_ref, wu_ref, wd_ref, act_scr):
    d_ff = wg_ref.shape[1]
    for c in range(d_ff // MXU_COLS):
        sl = slice(c * MXU_COLS, (c + 1) * MXU_COLS)
        g = jnp.dot(h_scr[...], wg_ref[:, sl], preferred_element_type=F32)
        u = jnp.dot(h_scr[...], wu_ref[:, sl], preferred_element_type=F32)
        act_scr[:, sl] = (jax.nn.silu(g) * u).astype(BF16)
    return jnp.dot(act_scr[...], wd_ref[...], preferred_element_type=F32)


def _ffn_in_kernel(*refs, stacked):
    if stacked:
        (x_ref, na_ref, wg_ref, wu_ref, wd_ref, nm_ref, win_ref, wk2_ref, wvt_ref,
         x1_ref, q_ref, k_ref, v_ref, u_ref, ga_ref, gs_ref, k2_ref, vt_ref, h_scr, act_scr) = refs
        assert WINDOW_COLS == k_ref.shape[0]
    else:
        (x_ref, na_ref, wg_ref, wu_ref, wd_ref, nm_ref, win_ref,
         x1_ref, q_ref, k_ref, v_ref, u_ref, ga_ref, gs_ref, h_scr, act_scr) = refs
    tm, d = x_ref.shape
    x = x_ref[...]
    h_scr[...] = _rms(x, na_ref[...]).astype(BF16)
    x1 = x + 0.5 * _swiglu(h_scr, wg_ref, wu_ref, wd_ref, act_scr)
    x1_ref[...] = x1
    h_scr[...] = _rms(x1, nm_ref[...]).astype(BF16)

    def proj(w_ref, off, c):
        return jnp.dot(h_scr[...], w_ref[:, off + c * MXU_COLS: off + (c + 1) * MXU_COLS],
                       preferred_element_type=F32)

    q_scale = HEAD_DIM ** -0.5 * (LOG2_E if stacked else 1.0)
    pair_w = 2 * HEAD_DIM
    if stacked:
        assert Q_PER_KV * HEAD_DIM == MXU_COLS
        low_half = lax.broadcasted_iota(jnp.int32, (tm, MXU_COLS), 1) % pair_w < HEAD_DIM
        for g in range(N_KV_HEADS):
            r = proj(win_ref, 0, g) * q_scale
            halves = (jnp.where(low_half, r, 0.0).astype(BF16), jnp.where(low_half, 0.0, r).astype(BF16))
            for bl in range(tm // WINDOW):
                for rr in range(Q_PER_KV):
                    row = ((bl * N_KV_HEADS + g) * Q_PER_KV + rr) * WINDOW
                    q_ref[row:row + WINDOW, :] = halves[rr % 2][bl * WINDOW:(bl + 1) * WINDOW,
                                                               (rr // 2) * pair_w:(rr // 2 + 1) * pair_w]
    else:
        for c in range(d // MXU_COLS):
            q_ref[:, c * MXU_COLS:(c + 1) * MXU_COLS] = (proj(win_ref, 0, c) * q_scale).astype(q_ref.dtype)
    off = d
    for ref in (k_ref, v_ref):
        if stacked:
            ref[...] = proj(win_ref, off, 0)[tm - WINDOW:, :].T
        else:
            ref[...] = proj(win_ref, off, 0)
        off += WINDOW_COLS
    for ref in (u_ref, ga_ref, gs_ref):
        width = ref.shape[1]
        for c in range(width // MXU_COLS):
            ref[:, c * MXU_COLS:(c + 1) * MXU_COLS] = proj(win_ref, off, c).astype(ref.dtype)
        off += width
    if stacked:
        for c in range(k2_ref.shape[1] // MXU_COLS):
            k2_ref[:, c * MXU_COLS:(c + 1) * MXU_COLS] = proj(wk2_ref, 0, c).astype(k2_ref.dtype)
        vt_ref[...] = lax.dot_general(wvt_ref[...], h_scr[...], (((1,), (1,)), ((), ())),
                                      preferred_element_type=F32).astype(vt_ref.dtype)


def _const_spec(shape):
    nd = len(shape)
    return pl.BlockSpec(shape, lambda *_: (0,) * nd, pipeline_mode=pl.Buffered(1))


def _ffn_in(x2d, grid, tm, std, umap, u_shape, na, wg, wu, wd, nm, win, stacked_w=None):
    d = na.shape[1]
    d_ff = wg.shape[1]
    kv = N_KV_HEADS * HEAD_DIM
    ssm_w = win.shape[1] - 3 * d - 2 * kv
    stacked = stacked_w is not None
    n_tok = x2d.size // d
    out_shape, out_specs = [], []

    def add(shape_spec, dtype):
        out_shape.append(jax.ShapeDtypeStruct(shape_spec[0], dtype))
        out_specs.append(shape_spec[1])

    add(std(d), F32)
    if stacked:
        wk2, wvt, tile_map = stacked_w
        add(((n_tok * N_HEADS, 2 * HEAD_DIM), pl.BlockSpec((tm * N_HEADS, 2 * HEAD_DIM), tile_map)), BF16)
    else:
        add(std(d), F32)
    if stacked:
        for _ in range(2):
            add(((grid[0], kv, WINDOW), pl.BlockSpec((None, kv, WINDOW), lambda b, i: (b, 0, 0))), F32)
    else:
        add(std(kv), F32)
        add(std(kv), F32)
    add((u_shape, pl.BlockSpec((tm, ssm_w), umap)), F32)
    add(std(d), F32)
    add(std(d), F32)
    weights = [na, wg, wu, wd, nm, win]
    if stacked:
        add(std(2 * kv), BF16)
        add(((kv, n_tok), pl.BlockSpec((kv, tm), lambda *g: tile_map(*g)[::-1])), BF16)
        weights += [wk2, wvt]
    return pl.pallas_call(
        functools.partial(_ffn_in_kernel, stacked=stacked),
        grid=grid,
        in_specs=[std(d)[1]] + [_const_spec(w.shape) for w in weights],
        out_specs=out_specs,
        out_shape=out_shape,
        scratch_shapes=[pltpu.VMEM((tm, d), BF16), pltpu.VMEM((tm, d_ff), BF16)],
        compiler_params=pltpu.CompilerParams(
            dimension_semantics=("parallel", "arbitrary"), vmem_limit_bytes=VMEM_LIMIT),
        name="ffn_in",
    )(x2d, *weights)


def _score_block(q_ref, keys, st_scr, bl, ctx_ok):
    nk = 2 * WINDOW
    lanes = Q_PER_KV * WINDOW
    jk = lax.broadcasted_iota(jnp.int32, (nk, lanes), 0)
    iq = lax.broadcasted_iota(jnp.int32, (nk, lanes), 1) & (WINDOW - 1)
    lo = iq if ctx_ok is True else jnp.maximum(iq, jnp.where(ctx_ok, 0, WINDOW))
    valid = (jk >= lo) & (jk <= iq + WINDOW)
    for g in range(N_KV_HEADS):
        row = (bl * N_KV_HEADS + g) * lanes
        st = lax.dot_general(keys(g), q_ref[row:row + lanes, :], (((1,), (1,)), ((), ())),
                             preferred_element_type=F32)
        st_scr[bl * N_KV_HEADS + g] = jnp.where(valid, st, NEG_BIG)


def _attend_block(sink_ref, st_scr, vals_t, o_ref, bl):
    nq = WINDOW
    pair_w = 2 * HEAD_DIM
    for g in range(N_KV_HEADS):
        st = st_scr[bl * N_KV_HEADS + g]
        sink = jnp.concatenate(
            [jnp.full((1, nq), sink_ref[g * Q_PER_KV + r] * LOG2_E, F32) for r in range(Q_PER_KV)], axis=1)
        m = jnp.maximum(jnp.max(st, axis=0, keepdims=True), sink)
        p = jnp.exp2(st - m)
        denom = jnp.sum(p, axis=0, keepdims=True) + jnp.exp2(sink - m)
        ot = jnp.dot(vals_t(g), p.astype(BF16), preferred_element_type=F32) * (1.0 / denom)
        for pr in range(Q_PER_KV // 2):
            two = jnp.concatenate([ot[:, (2 * pr) * nq:(2 * pr + 1) * nq],
                                   ot[:, (2 * pr + 1) * nq:(2 * pr + 2) * nq]], axis=0)
            col = (g * Q_PER_KV + 2 * pr) * HEAD_DIM
            o_ref[bl * nq:(bl + 1) * nq, col:col + pair_w] = two.T


def _attn_prompt_kernel(sink_ref, q_ref, kp_ref, kc_ref, vp_ref, vc_ref, o_ref, st_scr, *, n_blk):
    pair_w = 2 * HEAD_DIM

    def keys(bl):
        if bl == 0:
            return lambda g: jnp.concatenate([kp_ref[:, g * pair_w:(g + 1) * pair_w],
                                              kc_ref[0:WINDOW, g * pair_w:(g + 1) * pair_w]], axis=0)
        return lambda g: kc_ref[(bl - 1) * WINDOW:(bl + 1) * WINDOW, g * pair_w:(g + 1) * pair_w]

    def vals_t(bl):
        if bl == 0:
            return lambda g: jnp.concatenate([vp_ref[g * HEAD_DIM:(g + 1) * HEAD_DIM, :],
                                              vc_ref[g * HEAD_DIM:(g + 1) * HEAD_DIM, 0:WINDOW]], axis=1)
        return lambda g: vc_ref[g * HEAD_DIM:(g + 1) * HEAD_DIM, (bl - 1) * WINDOW:(bl + 1) * WINDOW]

    for bl in range(n_blk):
        _score_block(q_ref, keys(bl), st_scr, bl, pl.program_id(1) > 0 if bl == 0 else True)
    for bl in range(n_blk):
        _attend_block(sink_ref, st_scr, vals_t(bl), o_ref, bl)


def _attn_prompt(sinks, qst, k2, vt, n_seq, seq, n_blk):
    nb = seq // WINDOW
    nt = nb // n_blk
    d = N_HEADS * HEAD_DIM
    kv = N_KV_HEADS * HEAD_DIM
    cur = lambda b, i: (b * nt + i, 0)
    prev = lambda b, i: (b * nb + jnp.maximum(i * n_blk - 1, 0), 0)
    cur_t = lambda b, i: (0, b * nt + i)
    prev_t = lambda b, i: (0, b * nb + jnp.maximum(i * n_blk - 1, 0))
    return pl.pallas_call(
        functools.partial(_attn_prompt_kernel, n_blk=n_blk),
        grid=(n_seq, nt),
        in_specs=[pl.BlockSpec(memory_space=pltpu.SMEM),
                  pl.BlockSpec((n_blk * WINDOW * N_HEADS, 2 * HEAD_DIM), cur),
                  pl.BlockSpec((WINDOW, 2 * kv), prev), pl.BlockSpec((n_blk * WINDOW, 2 * kv), cur),
                  pl.BlockSpec((kv, WINDOW), prev_t), pl.BlockSpec((kv, n_blk * WINDOW), cur_t)],
        out_specs=pl.BlockSpec((n_blk * WINDOW, d), cur),
        out_shape=jax.ShapeDtypeStruct((n_seq * seq, d), F32),
        scratch_shapes=[pltpu.VMEM((n_blk * N_KV_HEADS, 2 * WINDOW, Q_PER_KV * WINDOW), F32)],
        compiler_params=pltpu.CompilerParams(
            dimension_semantics=("parallel", "parallel"), vmem_limit_bytes=VMEM_LIMIT),
        name="attn_prompt",
    )(sinks, qst, k2, k2, vt, vt)


def _attn_sample_kernel(sink_ref, q_ref, k_ref, v_ref, ckt_ref, cvt_ref, o_ref, kw_ref, vw_ref, *, n_sub, tn, unroll):
    pair_w = 2 * HEAD_DIM
    n_pairs = N_HEADS // 2
    rows = n_pairs * tn
    kv = N_KV_HEADS * HEAD_DIM
    old = WINDOW - tn
    assert tn & (tn - 1) == 0, "row -> token index uses a power-of-two mask"
    low = lax.broadcasted_iota(jnp.int32, (tn, pair_w), 1) < HEAD_DIM
    tq = lax.broadcasted_iota(jnp.int32, (rows, WINDOW), 0) & (tn - 1)
    col = lax.broadcasted_iota(jnp.int32, (rows, WINDOW), 1)
    valid_c = col >= tq
    valid_n = (col >= old) & (col - old <= tq)
    is_new = lax.broadcasted_iota(jnp.int32, (kv, WINDOW), 1) >= old
    zeros = jnp.zeros((tn, pair_w), F32)
    pad = jnp.zeros((old, kv), F32)
    nt = (((1,), (1,)), ((), ()))

    sinks = [jnp.concatenate([jnp.full((tn, 1), sink_ref[2 * pr + half], F32) for pr in range(n_pairs)], axis=0)
             for half in range(2)]

    def scores(s):
        r0 = pl.multiple_of(s * tn, tn)
        q = q_ref[pl.ds(r0, tn), :]
        kt = ckt_ref[s]
        vt = cvt_ref[s]
        knt = jnp.concatenate([pad, k_ref[pl.ds(r0, tn), :]], axis=0).T
        vnt = jnp.concatenate([pad, v_ref[pl.ds(r0, tn), :]], axis=0).T
        kw_ref[s] = jnp.where(is_new, knt, pltpu.roll(kt, old, axis=1))
        vw_ref[s] = jnp.where(is_new, vnt, pltpu.roll(vt, old, axis=1))
        kt, vt, knt, vnt = (a.astype(BF16) for a in (kt, vt, knt, vnt))
        sc_sn = []
        for half in range(2):
            blocks = []
            for g in range(N_KV_HEADS):
                for pp in range(Q_PER_KV // 2):
                    piece = q[:, (2 * g + pp) * pair_w:(2 * g + pp + 1) * pair_w]
                    if half != g % 2:
                        piece = pltpu.roll(piece, HEAD_DIM, axis=1)
                    piece = jnp.where(low if g % 2 == 0 else ~low, piece, 0.0)
                    blocks.append(jnp.concatenate([piece, zeros] if g // 2 == 0 else [zeros, piece], axis=1))
            qh = jnp.concatenate(blocks, axis=0).astype(BF16)
            sc_sn.append((jnp.where(valid_c, jnp.dot(qh, kt, preferred_element_type=F32), NEG_BIG),
                          jnp.where(valid_n, jnp.dot(qh, knt, preferred_element_type=F32), NEG_BIG)))
        return r0, vt, vnt, sc_sn

    def attend(r0, vt, vnt, sc_sn):
        outs = []
        for half, (sc, sn) in enumerate(sc_sn):
            sink = sinks[half]
            m = jnp.maximum(jnp.max(jnp.maximum(sc, sn), axis=-1, keepdims=True), sink)
            pc = jnp.exp(sc - m)
            pn = jnp.exp(sn - m)
            rden = 1.0 / (jnp.sum(pc + pn, axis=-1, keepdims=True) + jnp.exp(sink - m))
            outs.append(lax.dot_general((pc * rden).astype(BF16), vt, nt, preferred_element_type=F32)
                        + lax.dot_general((pn * rden).astype(BF16), vnt, nt, preferred_element_type=F32))
        for pr in range(n_pairs):
            g = pr // (Q_PER_KV // 2)
            sel = []
            for half in range(2):
                blk = outs[half][pr * tn:(pr + 1) * tn, (g // 2) * pair_w:(g // 2 + 1) * pair_w]
                sel.append(blk if half == g % 2 else pltpu.roll(blk, HEAD_DIM, axis=1))
            o_ref[pl.ds(r0, tn), pr * pair_w:(pr + 1) * pair_w] = jnp.where(low, sel[0], sel[1])

    def some_sequences(i, carry):
        staged = [scores(i * unroll + u) for u in range(unroll)]
        for st in staged:
            attend(*st)
        return carry

    lax.fori_loop(0, n_sub // unroll, some_sequences, 0)


def _attn_sample(sinks, q, k, v, cache_kt, cache_vt, n_seq, tn, n_sub, unroll):
    d = q.shape[1]
    kv = k.shape[1]
    rows = lambda i: (i, 0)
    seqs = lambda i: (i, 0, 0)
    win = pl.BlockSpec((n_sub, kv, WINDOW), seqs)
    return pl.pallas_call(
        functools.partial(_attn_sample_kernel, n_sub=n_sub, tn=tn, unroll=unroll),
        grid=(n_seq // n_sub,),
        in_specs=[pl.BlockSpec(memory_space=pltpu.SMEM),
                  pl.BlockSpec((n_sub * tn, d), rows),
                  pl.BlockSpec((n_sub * tn, kv), rows), pl.BlockSpec((n_sub * tn, kv), rows), win, win],
        out_specs=[pl.BlockSpec((n_sub * tn, d), rows), win, win],
        out_shape=[jax.ShapeDtypeStruct(q.shape, F32),
                   jax.ShapeDtypeStruct(cache_kt.shape, F32), jax.ShapeDtypeStruct(cache_vt.shape, F32)],
        compiler_params=pltpu.CompilerParams(
            dimension_semantics=("parallel",), vmem_limit_bytes=VMEM_LIMIT),
        name="attn_sample",
    )(sinks, q, k, v, cache_kt, cache_vt)


def _ssm_prep_kernel(lr_ref, li_ref, ldt_ref, br_ref, bi_ref, ar_ref, ai_ref, bbr_ref, bbi_ref):
    lr = lr_ref[...]
    li = li_ref[...]
    dt = jnp.exp(ldt_ref[...])
    mag = jnp.exp(lr * dt)
    ang = li * dt
    abar_re = mag * jnp.cos(ang)
    abar_im = mag * jnp.sin(ang)
    den = lr * lr + li * li
    nr = abar_re - 1.0
    fr = (nr * lr + abar_im * li) / den
    fi = (abar_im * lr - nr * li) / den
    ar_ref[...] = abar_re
    ai_ref[...] = abar_im
    br = br_ref[...]
    bi = bi_ref[...]
    bbr_ref[...] = fr[:, None, :] * br - fi[:, None, :] * bi
    bbi_ref[...] = fr[:, None, :] * bi + fi[:, None, :] * br


def _ssm_prep(lam_re, lam_im, log_dt, b_re_t, b_im_t):
    g, n = lam_re.shape
    return pl.pallas_call(
        _ssm_prep_kernel,
        out_shape=[jax.ShapeDtypeStruct((g, n), F32), jax.ShapeDtypeStruct((g, n), F32),
                   jax.ShapeDtypeStruct(b_re_t.shape, F32), jax.ShapeDtypeStruct(b_im_t.shape, F32)],
        name="ssm_prep",
    )(lam_re, lam_im, log_dt.reshape(g, 1), b_re_t, b_im_t)


def _ssm_kernel(u_ref, h0_ref, a_ref, d_ref, bd_ref, cd_ref, y_ref, hT_ref, xh_scr, u_scr, y_scr,
                *, n_seq, tt, seq_lanes, scan_unroll):
    cw = xh_scr.shape[1] // SSM_CHUNKS
    hw = cw // 2
    uc = u_scr.shape[2]
    ssm_w = SSM_CHUNKS * uc

    @pl.when(pl.program_id(0) == 0)
    def _():
        hT_ref[...] = h0_ref[...]

    for j in range(SSM_CHUNKS):
        for b in range(n_seq):
            if seq_lanes:
                u_b = u_ref[:, b * ssm_w + j * uc:b * ssm_w + (j + 1) * uc]
            else:
                u_b = u_ref[b * tt:(b + 1) * tt, j * uc:(j + 1) * uc]
            u_scr[j, pl.ds(b, tt, stride=n_seq), :] = u_b

    for j in range(SSM_CHUNKS):
        xh_scr[:, j * cw:(j + 1) * cw] = jnp.dot(u_scr[j].astype(BF16), bd_ref[j], preferred_element_type=F32)

    def seq_group(s, carry):
        s8 = pl.multiple_of(s * 8, 8)
        for j in range(SSM_CHUNKS):
            re = slice(j * cw, j * cw + hw)
            im = slice(j * cw + hw, (j + 1) * cw)
            ar = a_ref[:, re]
            ai = a_ref[:, im]

            def step(t, h):
                hr, hi = h
                r0 = pl.multiple_of(t * n_seq + s8, 8)
                nr = ar * hr - ai * hi + xh_scr[pl.ds(r0, 8), re]
                ni = ar * hi + ai * hr + xh_scr[pl.ds(r0, 8), im]
                xh_scr[pl.ds(r0, 8), re] = nr
                xh_scr[pl.ds(r0, 8), im] = ni
                return nr, ni

            hr, hi = lax.fori_loop(0, tt, step, (hT_ref[pl.ds(s8, 8), re], hT_ref[pl.ds(s8, 8), im]),
                                   unroll=scan_unroll)
            hT_ref[pl.ds(s8, 8), re] = hr
            hT_ref[pl.ds(s8, 8), im] = hi
        return carry

    lax.fori_loop(0, n_seq // 8, seq_group, 0)

    for j in range(SSM_CHUNKS):
        y = jnp.dot(xh_scr[:, j * cw:(j + 1) * cw].astype(BF16), cd_ref[j], preferred_element_type=F32)
        cs = slice(j * uc, (j + 1) * uc)
        y_scr[j] = jax.nn.gelu(y + d_ref[:, cs] * u_scr[j])
        for b in range(n_seq):
            y_b = y_scr[j, pl.ds(b, tt, stride=n_seq), :].astype(y_ref.dtype)
            if seq_lanes:
                y_ref[:, b * ssm_w + j * uc:b * ssm_w + (j + 1) * uc] = y_b
            else:
                y_ref[b * tt:(b + 1) * tt, cs] = y_b


def _ssm(u, h0, a8, d_skip, bd, cd, n_seq, tt, seq_lanes, scan_unroll):
    ssm_w = d_skip.shape[1]
    tile = n_seq * tt
    n_state = h0.shape[1]
    block = (tt, n_seq * ssm_w) if seq_lanes else (tile, ssm_w)
    assert seq_lanes or u.shape[0] == tile, "row order (seq, t) cannot be tiled over time"
    slab = pltpu.VMEM((SSM_CHUNKS, tile, ssm_w // SSM_CHUNKS), F32)
    return pl.pallas_call(
        functools.partial(_ssm_kernel, n_seq=n_seq, tt=tt, seq_lanes=seq_lanes, scan_unroll=scan_unroll),
        grid=(u.shape[0] // block[0],),
        in_specs=[pl.BlockSpec(block, lambda i: (i, 0)),
                  _const_spec(h0.shape), _const_spec(a8.shape), _const_spec(d_skip.shape),
                  _const_spec(bd.shape), _const_spec(cd.shape)],
        out_specs=[pl.BlockSpec(block, lambda i: (i, 0)),
                   pl.BlockSpec(h0.shape, lambda i: (0, 0))],
        out_shape=[jax.ShapeDtypeStruct(u.shape, BF16), jax.ShapeDtypeStruct(h0.shape, F32)],
        scratch_shapes=[pltpu.VMEM((tile, n_state), F32), slab, slab],
        compiler_params=pltpu.CompilerParams(
            dimension_semantics=("arbitrary",), vmem_limit_bytes=VMEM_LIMIT),
        name="ssm",
    )(u, h0, a8, d_skip, bd, cd)


def _out_kernel(x1_ref, at_ref, ga_ref, gs_ref, y_ref, glua_ref, glub_ref, wo_ref,
                nb_ref, wg_ref, wu_ref, wd_ref, nf_ref, o_ref, h_scr, act_scr):
    yb = y_ref[...]
    ssm = jnp.dot(yb, glua_ref[...], preferred_element_type=F32) * jax.nn.sigmoid(
        jnp.dot(yb, glub_ref[...], preferred_element_type=F32))
    merged = jax.nn.sigmoid(ga_ref[...]) * at_ref[...] + jax.nn.sigmoid(gs_ref[...]) * ssm
    x2 = x1_ref[...] + jnp.dot(merged.astype(BF16), wo_ref[...], preferred_element_type=F32)
    h_scr[...] = _rms(x2, nb_ref[...]).astype(BF16)
    x3 = x2 + 0.5 * _swiglu(h_scr, wg_ref, wu_ref, wd_ref, act_scr)
    o_ref[...] = _rms(x3, nf_ref[...])


def _out(grid, tm, std, ymap, x1, attn, ga, gs, y_tb, glua, glub, wo, nb, wg, wu, wd, nf):
    d = nb.shape[1]
    d_ff = wg.shape[1]
    shp, spec = std(d)
    return pl.pallas_call(
        _out_kernel,
        grid=grid,
        in_specs=[spec, spec, spec, spec, pl.BlockSpec((tm, glua.shape[0]), ymap),
                  _const_spec(glua.shape), _const_spec(glub.shape), _const_spec(wo.shape),
                  _const_spec(nb.shape), _const_spec(wg.shape), _const_spec(wu.shape),
                  _const_spec(wd.shape), _const_spec(nf.shape)],
        out_specs=spec,
        out_shape=jax.ShapeDtypeStruct(shp, F32),
        scratch_shapes=[pltpu.VMEM((tm, d), BF16), pltpu.VMEM((tm, d_ff), BF16)],
        compiler_params=pltpu.CompilerParams(
            dimension_semantics=("parallel", "parallel"), vmem_limit_bytes=VMEM_LIMIT),
        name="out",
    )(x1, attn, ga, gs, y_tb, glua, glub, wo, nb, wg, wu, wd, nf)


def _block_diag(w, chunks):
    g, a, b = w.shape
    gl = g // chunks
    w = w.reshape(chunks, gl, a, b)
    eye = jnp.eye(gl, dtype=w.dtype)
    return (w[:, :, :, None, :] * eye[None, :, None, :, None]).reshape(chunks, gl * a, gl * b)


def _state_to_lanes(re, im):
    s = re.shape[0]
    re = re.reshape(s, SSM_CHUNKS, -1)
    im = im.reshape(s, SSM_CHUNKS, -1)
    return jnp.stack([re, im], axis=2).reshape(s, -1)


def _lanes_to_state(h, groups):
    s = h.shape[0]
    h = h.reshape(s, SSM_CHUNKS, 2, -1)
    return h[:, :, 0].reshape(s, groups, -1), h[:, :, 1].reshape(s, groups, -1)


def kernel(x_prompt, x_sample, cache_k_win, cache_v_win, state_ssm_re, state_ssm_im, ffn_a_norm, ffn_a_gate, ffn_a_up, ffn_a_down, mix_norm, w_in, attn_sinks, ssm_lambda_re, ssm_lambda_im, ssm_log_dt, ssm_b_re, ssm_b_im, ssm_c_re, ssm_c_im, ssm_d, glu_a, glu_b, w_out, ffn_b_norm, ffn_b_gate, ffn_b_up, ffn_b_down, final_norm):
    depth = ffn_a_norm.shape[0]
    assert depth == 1, "single-layer trunk"
    n_p, seq, d = x_prompt.shape
    n_s, dec, _ = x_sample.shape
    ssm_w = ssm_d.shape[1]
    groups = ssm_lambda_re.shape[1]
    kvw = N_KV_HEADS * HEAD_DIM
    assert cache_k_win.shape[2] == WINDOW and seq % WINDOW == 0 and n_p % 8 == 0 and n_s % 8 == 0

    l = 0
    na, nm, nb = (w[l].reshape(1, d) for w in (ffn_a_norm, mix_norm, ffn_b_norm))
    nf = final_norm.reshape(1, d)
    wga, wua, wda, win = (w[l].astype(BF16) for w in (ffn_a_gate, ffn_a_up, ffn_a_down, w_in))
    wgb, wub, wdb = (w[l].astype(BF16) for w in (ffn_b_gate, ffn_b_up, ffn_b_down))
    glua, glub, wo = (w[l].astype(BF16) for w in (glu_a, glu_b, w_out))
    sinks = attn_sinks[l]
    d_skip = ssm_d[l].reshape(1, ssm_w)

    abar_re, abar_im, bb_re_t, bb_im_t = _ssm_prep(
        ssm_lambda_re[l], ssm_lambda_im[l], ssm_log_dt[l],
        jnp.swapaxes(ssm_b_re[l], 1, 2), jnp.swapaxes(ssm_b_im[l], 1, 2))
    bd = jnp.concatenate([_block_diag(bb_re_t, SSM_CHUNKS), _block_diag(bb_im_t, SSM_CHUNKS)],
                         axis=2).astype(BF16)
    c_re_t = jnp.swapaxes(ssm_c_re[l], 1, 2)
    c_im_t = jnp.swapaxes(ssm_c_im[l], 1, 2)
    cd = jnp.concatenate([_block_diag(c_re_t, SSM_CHUNKS), -_block_diag(c_im_t, SSM_CHUNKS)],
                         axis=1).astype(BF16)
    a8 = jnp.broadcast_to(_state_to_lanes(abar_re[None], abar_im[None]), (8, 2 * groups * SSM_STATE))

    def run_group(x, n_seq, t_len, grid, tm, std, umap, u_shape, stacked_w, attn_fn, h0, tt):
        x1, q, k, v, u_tb, ga, gs, *stacked = _ffn_in(x, grid, tm, std, umap, u_shape,
                                                       na, wga, wua, wda, nm, win, stacked_w)
        attn = attn_fn(q, k, v, *stacked)
        y_tb, h_t = _ssm(u_tb, h0, a8, d_skip, bd, cd, n_seq, tt,
                         seq_lanes=u_shape[1] != ssm_w, scan_unroll=tt)
        y = _out(grid, tm, std, umap, x1, attn, ga, gs, y_tb, glua, glub, wo, nb, wgb, wub, wdb, nf)
        return y, k, v, h_t

    tm_p = 256
    nt = seq // tm_p

    def std_p(width):
        return (n_p * seq, width), pl.BlockSpec((tm_p, width), lambda b, i: (b * nt + i, 0))

    def attn_p(qst, k, v, k2, vt):
        return _attn_prompt(sinks, qst, k2, vt, n_p, seq, 2)

    w_k = w_in[l][:, d:d + kvw].reshape(d, N_KV_HEADS, 1, HEAD_DIM)
    wk2 = jnp.broadcast_to(w_k, (d, N_KV_HEADS, 2, HEAD_DIM)).reshape(d, 2 * kvw).astype(BF16)
    wvt = w_in[l][:, d + kvw:d + 2 * kvw].T.astype(BF16)

    h0_p = jnp.zeros((n_p, 2 * groups * SSM_STATE), F32)
    y_p, k_p, v_p, h_p = run_group(
        x_prompt.reshape(n_p * seq, d), n_p, seq, (n_p, nt), tm_p, std_p,
        lambda b, i: (i, b), (seq, n_p * ssm_w), (wk2, wvt, lambda b, i: (b * nt + i, 0)),
        attn_p, h0_p, 1024 // n_p)

    tm_s = 128

    def std_s(width):
        return (n_s * dec, width), pl.BlockSpec((tm_s, width), lambda i, j: (i, 0))

    def window_t(c):
        return jnp.transpose(c, (0, 2, 3, 1)).reshape(c.shape[0], kvw, WINDOW)

    def window(ct):
        return jnp.transpose(ct.reshape(ct.shape[0], N_KV_HEADS, HEAD_DIM, WINDOW), (0, 3, 1, 2))[None]

    new_windows = []

    def attn_s(q, k, v):
        o, kw, vw = _attn_sample(sinks, q, k, v, window_t(cache_k_win[l]), window_t(cache_v_win[l]),
                                 n_s, dec, 16, 4)
        new_windows.extend([kw, vw])
        return o

    h0_s = _state_to_lanes(state_ssm_re[l], state_ssm_im[l])
    y_s, _, _, h_s = run_group(
        x_sample.reshape(n_s * dec, d), n_s, dec, (n_s * dec // tm_s, 1), tm_s, std_s,
        lambda i, j: (i, 0), (n_s * dec, ssm_w), None, attn_s, h0_s, dec)

    sp_re, sp_im = _lanes_to_state(h_p, groups)
    ss_re, ss_im = _lanes_to_state(h_s, groups)
    return (y_p.reshape(n_p, seq, d), y_s.reshape(n_s, dec, d),
            window(k_p), window(v_p), window(new_windows[0]), window(new_windows[1]),
            sp_re[None], sp_im[None], ss_re[None], ss_im[None])
```

```python
import functools

import jax
import jax.numpy as jnp
from jax import lax
from jax.experimental import pallas as pl
from jax.experimental.pallas import tpu as pltpu

F32 = jnp.float32
BF16 = jnp.bfloat16

N_HEADS = 16
N_KV_HEADS = 4
HEAD_DIM = 64
Q_PER_KV = N_HEADS // N_KV_HEADS
WINDOW = 128
SSM_STATE = 64
RMS_EPS = 1e-6
NEG_BIG = -1e30
LOG2_E = 1.4426950408889634

MXU_COLS = 256
WINDOW_COLS = N_KV_HEADS * HEAD_DIM
SSM_CHUNKS = 4
VMEM_LIMIT = 60000 * 1024


def _rms(x, g):
    return x * lax.rsqrt(jnp.mean(x * x, axis=-1, keepdims=True) + RMS_EPS) * g


def _row_slices(tm, n_sub):
    rs = tm // n_sub
    return [slice(r * rs, (r + 1) * rs) for r in range(n_sub)]


def _swiglu(h_scr, wg_ref, wu_ref, wd_ref, act_scr, rows=(slice(None),)):
    d_ff = wg_ref.shape[1]
    for c in range(d_ff // MXU_COLS):
        sl = slice(c * MXU_COLS, (c + 1) * MXU_COLS)
        for r in rows:
            g = jnp.dot(h_scr[r, :], wg_ref[:, sl], preferred_element_type=F32)
            u = jnp.dot(h_scr[r, :], wu_ref[:, sl], preferred_element_type=F32)
            act_scr[r, sl] = (jax.nn.silu(g) * u).astype(BF16)
    return [jnp.dot(act_scr[r, :], wd_ref[...], preferred_element_type=F32) for r in rows]


def _ffn_in_kernel(*refs, stacked, n_sub):
    if stacked:
        (x_ref, na_ref, wg_ref, wu_ref, wd_ref, nm_ref, win_ref, wk2_ref, wvt_ref,
         x1_ref, q_ref, k_ref, v_ref, u_ref, ga_ref, gs_ref, k2_ref, vt_ref, h_scr, act_scr) = refs
        assert WINDOW_COLS == k_ref.shape[0]
    else:
        (x_ref, na_ref, wg_ref, wu_ref, wd_ref, nm_ref, win_ref,
         x1_ref, q_ref, k_ref, v_ref, u_ref, ga_ref, gs_ref, h_scr, act_scr) = refs
    tm, d = x_ref.shape
    rows = _row_slices(tm, n_sub)
    rs = tm // n_sub
    for r in rows:
        h_scr[r, :] = _rms(x_ref[r, :], na_ref[...]).astype(BF16)
    x1 = [x_ref[r, :] + 0.5 * f for r, f in zip(rows, _swiglu(h_scr, wg_ref, wu_ref, wd_ref, act_scr, rows))]
    for r, x in zip(rows, x1):
        x1_ref[r, :] = x
        h_scr[r, :] = _rms(x, nm_ref[...]).astype(BF16)

    def proj(w_ref, off, c, r):
        return jnp.dot(h_scr[r, :], w_ref[:, off + c * MXU_COLS: off + (c + 1) * MXU_COLS],
                       preferred_element_type=F32)

    q_scale = HEAD_DIM ** -0.5 * (LOG2_E if stacked else 1.0)
    pair_w = 2 * HEAD_DIM
    if stacked:
        assert Q_PER_KV * HEAD_DIM == MXU_COLS
        low_half = lax.broadcasted_iota(jnp.int32, (rs, MXU_COLS), 1) % pair_w < HEAD_DIM
        for g in range(N_KV_HEADS):
            for ri, r in enumerate(rows):
                res = proj(win_ref, 0, g, r) * q_scale
                halves = (jnp.where(low_half, res, 0.0).astype(BF16), jnp.where(low_half, 0.0, res).astype(BF16))
                for bl in range(rs // WINDOW):
                    for rr in range(Q_PER_KV):
                        row = (((ri * (rs // WINDOW) + bl) * N_KV_HEADS + g) * Q_PER_KV + rr) * WINDOW
                        q_ref[row:row + WINDOW, :] = halves[rr % 2][bl * WINDOW:(bl + 1) * WINDOW,
                                                                   (rr // 2) * pair_w:(rr // 2 + 1) * pair_w]
    else:
        for c in range(d // MXU_COLS):
            for r in rows:
                q_ref[r, c * MXU_COLS:(c + 1) * MXU_COLS] = (proj(win_ref, 0, c, r) * q_scale).astype(q_ref.dtype)
        for ref, off in ((k_ref, d), (v_ref, d + WINDOW_COLS)):
            for r in rows:
                ref[r, :] = proj(win_ref, off, 0, r)
    off = d + 2 * WINDOW_COLS
    for ref in (u_ref, ga_ref, gs_ref):
        width = ref.shape[1]
        for c in range(width // MXU_COLS):
            for r in rows:
                ref[r, c * MXU_COLS:(c + 1) * MXU_COLS] = proj(win_ref, off, c, r).astype(ref.dtype)
        off += width
    if stacked:
        for c in range(k2_ref.shape[1] // MXU_COLS):
            for r in rows:
                k2_ref[r, c * MXU_COLS:(c + 1) * MXU_COLS] = proj(wk2_ref, 0, c, r).astype(k2_ref.dtype)
        for r in rows:
            vt_ref[:, r] = lax.dot_general(wvt_ref[...], h_scr[r, :], (((1,), (1,)), ((), ())),
                                           preferred_element_type=F32).astype(vt_ref.dtype)

        @pl.when(pl.program_id(1) == pl.num_programs(1) - 1)
        def _():
            tail = h_scr[tm - WINDOW:, :]
            for ref, off in ((k_ref, d), (v_ref, d + WINDOW_COLS)):
                ref[...] = jnp.dot(tail, win_ref[:, off:off + WINDOW_COLS], preferred_element_type=F32).T


def _const_spec(shape):
    nd = len(shape)
    return pl.BlockSpec(shape, lambda *_: (0,) * nd, pipeline_mode=pl.Buffered(1))


def _ffn_in(x2d, grid, tm, n_sub, std, umap, u_shape, na, wg, wu, wd, nm, win, stacked_w=None):
    d = na.shape[1]
    d_ff = wg.shape[1]
    kv = N_KV_HEADS * HEAD_DIM
    ssm_w = win.shape[1] - 3 * d - 2 * kv
    stacked = stacked_w is not None
    n_tok = x2d.size // d
    out_shape, out_specs = [], []

    def add(shape_spec, dtype):
        out_shape.append(jax.ShapeDtypeStruct(shape_spec[0], dtype))
        out_specs.append(shape_spec[1])

    add(std(d), F32)
    if stacked:
        wk2, wvt, tile_map = stacked_w
        add(((n_tok * N_HEADS, 2 * HEAD_DIM), pl.BlockSpec((tm * N_HEADS, 2 * HEAD_DIM), tile_map)), BF16)
    else:
        add(std(d), F32)
    if stacked:
        for _ in range(2):
            add(((grid[0], kv, WINDOW), pl.BlockSpec((None, kv, WINDOW), lambda b, i: (b, 0, 0))), F32)
    else:
        add(std(kv), F32)
        add(std(kv), F32)
    add((u_shape, pl.BlockSpec((tm, ssm_w), umap)), F32)
    add(std(d), F32)
    add(std(d), F32)
    weights = [na, wg, wu, wd, nm, win]
    if stacked:
        add(std(2 * kv), BF16)
        add(((kv, n_tok), pl.BlockSpec((kv, tm), lambda *g: tile_map(*g)[::-1])), BF16)
        weights += [wk2, wvt]
    return pl.pallas_call(
        functools.partial(_ffn_in_kernel, stacked=stacked, n_sub=n_sub),
        grid=grid,
        in_specs=[std(d)[1]] + [_const_spec(w.shape) for w in weights],
        out_specs=out_specs,
        out_shape=out_shape,
        scratch_shapes=[pltpu.VMEM((tm, d), BF16), pltpu.VMEM((tm, d_ff), BF16)],
        compiler_params=pltpu.CompilerParams(
            dimension_semantics=("parallel", "arbitrary"), vmem_limit_bytes=VMEM_LIMIT),
        name="ffn_in",
    )(x2d, *weights)


def _score_block(q_ref, keys, st_scr, bl, ctx_ok):
    nk = 2 * WINDOW
    lanes = Q_PER_KV * WINDOW
    jk = lax.broadcasted_iota(jnp.int32, (nk, lanes), 0)
    iq = lax.broadcasted_iota(jnp.int32, (nk, lanes), 1) & (WINDOW - 1)
    lo = iq if ctx_ok is True else jnp.maximum(iq, jnp.where(ctx_ok, 0, WINDOW))
    valid = (jk >= lo) & (jk <= iq + WINDOW)
    for g in range(N_KV_HEADS):
        row = (bl * N_KV_HEADS + g) * lanes
        st = lax.dot_general(keys(g), q_ref[row:row + lanes, :], (((1,), (1,)), ((), ())),
                             preferred_element_type=F32)
        st_scr[bl * N_KV_HEADS + g] = jnp.where(valid, st, NEG_BIG)


def _attend_block(sink_ref, st_scr, vals_t, o_ref, bl):
    nq = WINDOW
    pair_w = 2 * HEAD_DIM
    for g in range(N_KV_HEADS):
        st = st_scr[bl * N_KV_HEADS + g]
        sink = jnp.concatenate(
            [jnp.full((1, nq), sink_ref[g * Q_PER_KV + r] * LOG2_E, F32) for r in range(Q_PER_KV)], axis=1)
        m = jnp.maximum(jnp.max(st, axis=0, keepdims=True), sink)
        p = jnp.exp2(st - m)
        denom = jnp.sum(p, axis=0, keepdims=True) + jnp.exp2(sink - m)
        ot = jnp.dot(vals_t(g), p.astype(BF16), preferred_element_type=F32) * (1.0 / denom)
        for pr in range(Q_PER_KV // 2):
            two = jnp.concatenate([ot[:, (2 * pr) * nq:(2 * pr + 1) * nq],
                                   ot[:, (2 * pr + 1) * nq:(2 * pr + 2) * nq]], axis=0)
            col = (g * Q_PER_KV + 2 * pr) * HEAD_DIM
            o_ref[bl * nq:(bl + 1) * nq, col:col + pair_w] = two.T


def _attn_prompt_kernel(sink_ref, q_ref, kp_ref, kc_ref, vp_ref, vc_ref, o_ref, st_scr, *, n_blk):
    pair_w = 2 * HEAD_DIM

    def keys(bl):
        if bl == 0:
            return lambda g: jnp.concatenate([kp_ref[:, g * pair_w:(g + 1) * pair_w],
                                              kc_ref[0:WINDOW, g * pair_w:(g + 1) * pair_w]], axis=0)
        return lambda g: kc_ref[(bl - 1) * WINDOW:(bl + 1) * WINDOW, g * pair_w:(g + 1) * pair_w]

    def vals_t(bl):
        if bl == 0:
            return lambda g: jnp.concatenate([vp_ref[g * HEAD_DIM:(g + 1) * HEAD_DIM, :],
                                              vc_ref[g * HEAD_DIM:(g + 1) * HEAD_DIM, 0:WINDOW]], axis=1)
        return lambda g: vc_ref[g * HEAD_DIM:(g + 1) * HEAD_DIM, (bl - 1) * WINDOW:(bl + 1) * WINDOW]

    for bl in range(n_blk):
        _score_block(q_ref, keys(bl), st_scr, bl, pl.program_id(1) > 0 if bl == 0 else True)
    for bl in range(n_blk):
        _attend_block(sink_ref, st_scr, vals_t(bl), o_ref, bl)


def _attn_prompt(sinks, qst, k2, vt, n_seq, seq, n_blk):
    nb = seq // WINDOW
    nt = nb // n_blk
    d = N_HEADS * HEAD_DIM
    kv = N_KV_HEADS * HEAD_DIM
    cur = lambda b, i: (b * nt + i, 0)
    prev = lambda b, i: (b * nb + jnp.maximum(i * n_blk - 1, 0), 0)
    cur_t = lambda b, i: (0, b * nt + i)
    prev_t = lambda b, i: (0, b * nb + jnp.maximum(i * n_blk - 1, 0))
    return pl.pallas_call(
        functools.partial(_attn_prompt_kernel, n_blk=n_blk),
        grid=(n_seq, nt),
        in_specs=[pl.BlockSpec(memory_space=pltpu.SMEM),
                  pl.BlockSpec((n_blk * WINDOW * N_HEADS, 2 * HEAD_DIM), cur),
                  pl.BlockSpec((WINDOW, 2 * kv), prev), pl.BlockSpec((n_blk * WINDOW, 2 * kv), cur),
                  pl.BlockSpec((kv, WINDOW), prev_t), pl.BlockSpec((kv, n_blk * WINDOW), cur_t)],
        out_specs=pl.BlockSpec((n_blk * WINDOW, d), cur),
        out_shape=jax.ShapeDtypeStruct((n_seq * seq, d), F32),
        scratch_shapes=[pltpu.VMEM((n_blk * N_KV_HEADS, 2 * WINDOW, Q_PER_KV * WINDOW), F32)],
        compiler_params=pltpu.CompilerParams(
            dimension_semantics=("parallel", "parallel"), vmem_limit_bytes=VMEM_LIMIT),
        name="attn_prompt",
    )(sinks, qst, k2, k2, vt, vt)


def _attn_sample_kernel(sink_ref, q_ref, k_ref, v_ref, ckt_ref, cvt_ref, o_ref, kw_ref, vw_ref, *, n_sub, tn, unroll):
    pair_w = 2 * HEAD_DIM
    n_pairs = N_HEADS // 2
    rows = n_pairs * tn
    kv = N_KV_HEADS * HEAD_DIM
    old = WINDOW - tn
    assert tn & (tn - 1) == 0, "row -> token index uses a power-of-two mask"
    low = lax.broadcasted_iota(jnp.int32, (tn, pair_w), 1) < HEAD_DIM
    tq = lax.broadcasted_iota(jnp.int32, (rows, WINDOW), 0) & (tn - 1)
    col = lax.broadcasted_iota(jnp.int32, (rows, WINDOW), 1)
    valid_c = col >= tq
    valid_n = (col >= old) & (col - old <= tq)
    is_new = lax.broadcasted_iota(jnp.int32, (kv, WINDOW), 1) >= old
    zeros = jnp.zeros((tn, pair_w), F32)
    pad = jnp.zeros((old, kv), F32)
    nt = (((1,), (1,)), ((), ()))

    sinks = [jnp.concatenate([jnp.full((tn, 1), sink_ref[2 * pr + half], F32) for pr in range(n_pairs)], axis=0)
             for half in range(2)]

    def scores(s):
        r0 = pl.multiple_of(s * tn, tn)
        q = q_ref[pl.ds(r0, tn), :]
        kt = ckt_ref[s]
        vt = cvt_ref[s]
        knt = jnp.concatenate([pad, k_ref[pl.ds(r0, tn), :]], axis=0).T
        vnt = jnp.concatenate([pad, v_ref[pl.ds(r0, tn), :]], axis=0).T
        kw_ref[s] = jnp.where(is_new, knt, pltpu.roll(kt, old, axis=1))
        vw_ref[s] = jnp.where(is_new, vnt, pltpu.roll(vt, old, axis=1))
        kt, vt, knt, vnt = (a.astype(BF16) for a in (kt, vt, knt, vnt))
        sc_sn = []
        for half in range(2):
            blocks = []
            for g in range(N_KV_HEADS):
                for pp in range(Q_PER_KV // 2):
                    piece = q[:, (2 * g + pp) * pair_w:(2 * g + pp + 1) * pair_w]
                    if half != g % 2:
                        piece = pltpu.roll(piece, HEAD_DIM, axis=1)
                    piece = jnp.where(low if g % 2 == 0 else ~low, piece, 0.0)
                    blocks.append(jnp.concatenate([piece, zeros] if g // 2 == 0 else [zeros, piece], axis=1))
            qh = jnp.concatenate(blocks, axis=0).astype(BF16)
            sc_sn.append((jnp.where(valid_c, jnp.dot(qh, kt, preferred_element_type=F32), NEG_BIG),
                          jnp.where(valid_n, jnp.dot(qh, knt, preferred_element_type=F32), NEG_BIG)))
        return r0, vt, vnt, sc_sn

    def attend(r0, vt, vnt, sc_sn):
        outs = []
        for half, (sc, sn) in enumerate(sc_sn):
            sink = sinks[half]
            m = jnp.maximum(jnp.max(jnp.maximum(sc, sn), axis=-1, keepdims=True), sink)
            pc = jnp.exp(sc - m)
            pn = jnp.exp(sn - m)
            rden = 1.0 / (jnp.sum(pc + pn, axis=-1, keepdims=True) + jnp.exp(sink - m))
            outs.append(lax.dot_general((pc * rden).astype(BF16), vt, nt, preferred_element_type=F32)
                        + lax.dot_general((pn * rden).astype(BF16), vnt, nt, preferred_element_type=F32))
        for pr in range(n_pairs):
            g = pr // (Q_PER_KV // 2)
            sel = []
            for half in range(2):
                blk = outs[half][pr * tn:(pr + 1) * tn, (g // 2) * pair_w:(g // 2 + 1) * pair_w]
                sel.append(blk if half == g % 2 else pltpu.roll(blk, HEAD_DIM, axis=1))
            o_ref[pl.ds(r0, tn), pr * pair_w:(pr + 1) * pair_w] = jnp.where(low, sel[0], sel[1])

    def some_sequences(i, carry):
        staged = [scores(i * unroll + u) for u in range(unroll)]
        for st in staged:
            attend(*st)
        return carry

    lax.fori_loop(0, n_sub // unroll, some_sequences, 0)


def _attn_sample(sinks, q, k, v, cache_kt, cache_vt, n_seq, tn, n_sub, unroll):
    d = q.shape[1]
    kv = k.shape[1]
    rows = lambda i: (i, 0)
    seqs = lambda i: (i, 0, 0)
    win = pl.BlockSpec((n_sub, kv, WINDOW), seqs)
    return pl.pallas_call(
        functools.partial(_attn_sample_kernel, n_sub=n_sub, tn=tn, unroll=unroll),
        grid=(n_seq // n_sub,),
        in_specs=[pl.BlockSpec(memory_space=pltpu.SMEM),
                  pl.BlockSpec((n_sub * tn, d), rows),
                  pl.BlockSpec((n_sub * tn, kv), rows), pl.BlockSpec((n_sub * tn, kv), rows), win, win],
        out_specs=[pl.BlockSpec((n_sub * tn, d), rows), win, win],
        out_shape=[jax.ShapeDtypeStruct(q.shape, F32),
                   jax.ShapeDtypeStruct(cache_kt.shape, F32), jax.ShapeDtypeStruct(cache_vt.shape, F32)],
        compiler_params=pltpu.CompilerParams(
            dimension_semantics=("parallel",), vmem_limit_bytes=VMEM_LIMIT),
        name="attn_sample",
    )(sinks, q, k, v, cache_kt, cache_vt)


def _ssm_prep_kernel(lr_ref, li_ref, ldt_ref, br_ref, bi_ref, ar_ref, ai_ref, bbr_ref, bbi_ref):
    lr = lr_ref[...]
    li = li_ref[...]
    dt = jnp.exp(ldt_ref[...])
    mag = jnp.exp(lr * dt)
    ang = li * dt
    abar_re = mag * jnp.cos(ang)
    abar_im = mag * jnp.sin(ang)
    den = lr * lr + li * li
    nr = abar_re - 1.0
    fr = (nr * lr + abar_im * li) / den
    fi = (abar_im * lr - nr * li) / den
    ar_ref[...] = abar_re
    ai_ref[...] = abar_im
    br = br_ref[...]
    bi = bi_ref[...]
    bbr_ref[...] = fr[:, None, :] * br - fi[:, None, :] * bi
    bbi_ref[...] = fr[:, None, :] * bi + fi[:, None, :] * br


def _ssm_prep(lam_re, lam_im, log_dt, b_re_t, b_im_t):
    g, n = lam_re.shape
    return pl.pallas_call(
        _ssm_prep_kernel,
        out_shape=[jax.ShapeDtypeStruct((g, n), F32), jax.ShapeDtypeStruct((g, n), F32),
                   jax.ShapeDtypeStruct(b_re_t.shape, F32), jax.ShapeDtypeStruct(b_im_t.shape, F32)],
        name="ssm_prep",
    )(lam_re, lam_im, log_dt.reshape(g, 1), b_re_t, b_im_t)


def _ssm_kernel(u_ref, h0_ref, a_ref, d_ref, bd_ref, cd_ref, y_ref, hT_ref, xh_scr, u_scr, y_scr,
                *, n_seq, tt, seq_lanes, scan_unroll):
    cw = xh_scr.shape[1] // SSM_CHUNKS
    hw = cw // 2
    uc = u_scr.shape[2]
    ssm_w = SSM_CHUNKS * uc

    @pl.when(pl.program_id(0) == 0)
    def _():
        hT_ref[...] = h0_ref[...]

    for j in range(SSM_CHUNKS):
        for b in range(n_seq):
            if seq_lanes:
                u_b = u_ref[:, b * ssm_w + j * uc:b * ssm_w + (j + 1) * uc]
            else:
                u_b = u_ref[b * tt:(b + 1) * tt, j * uc:(j + 1) * uc]
            u_scr[j, pl.ds(b, tt, stride=n_seq), :] = u_b

    def project_in(j):
        xh_scr[:, j * cw:(j + 1) * cw] = jnp.dot(u_scr[j].astype(BF16), bd_ref[j], preferred_element_type=F32)

    def scan(j):
        re = slice(j * cw, j * cw + hw)
        im = slice(j * cw + hw, (j + 1) * cw)
        ar = a_ref[:, re]
        ai = a_ref[:, im]

        def seq_group(s, carry):
            s8 = pl.multiple_of(s * 8, 8)

            def step(t, h):
                hr, hi = h
                r0 = pl.multiple_of(t * n_seq + s8, 8)
                nr = ar * hr - ai * hi + xh_scr[pl.ds(r0, 8), re]
                ni = ar * hi + ai * hr + xh_scr[pl.ds(r0, 8), im]
                xh_scr[pl.ds(r0, 8), re] = nr
                xh_scr[pl.ds(r0, 8), im] = ni
                return nr, ni

            hr, hi = lax.fori_loop(0, tt, step, (hT_ref[pl.ds(s8, 8), re], hT_ref[pl.ds(s8, 8), im]),
                                   unroll=scan_unroll)
            hT_ref[pl.ds(s8, 8), re] = hr
            hT_ref[pl.ds(s8, 8), im] = hi
            return carry

        lax.fori_loop(0, n_seq // 8, seq_group, 0)

    def project_out(j):
        y = jnp.dot(xh_scr[:, j * cw:(j + 1) * cw].astype(BF16), cd_ref[j], preferred_element_type=F32)
        cs = slice(j * uc, (j + 1) * uc)
        y_scr[j] = jax.nn.gelu(y + d_ref[:, cs] * u_scr[j])
        for b in range(n_seq):
            y_b = y_scr[j, pl.ds(b, tt, stride=n_seq), :].astype(y_ref.dtype)
            if seq_lanes:
                y_ref[:, b * ssm_w + j * uc:b * ssm_w + (j + 1) * uc] = y_b
            else:
                y_ref[b * tt:(b + 1) * tt, cs] = y_b

    for stage in range(SSM_CHUNKS + 2):
        if stage < SSM_CHUNKS:
            project_in(stage)
        if 1 <= stage <= SSM_CHUNKS:
            scan(stage - 1)
        if stage >= 2:
            project_out(stage - 2)


def _ssm(u, h0, a8, d_skip, bd, cd, n_seq, tt, seq_lanes, scan_unroll):
    ssm_w = d_skip.shape[1]
    tile = n_seq * tt
    n_state = h0.shape[1]
    block = (tt, n_seq * ssm_w) if seq_lanes else (tile, ssm_w)
    assert seq_lanes or u.shape[0] == tile, "row order (seq, t) cannot be tiled over time"
    slab = pltpu.VMEM((SSM_CHUNKS, tile, ssm_w // SSM_CHUNKS), F32)
    return pl.pallas_call(
        functools.partial(_ssm_kernel, n_seq=n_seq, tt=tt, seq_lanes=seq_lanes, scan_unroll=scan_unroll),
        grid=(u.shape[0] // block[0],),
        in_specs=[pl.BlockSpec(block, lambda i: (i, 0)),
                  _const_spec(h0.shape), _const_spec(a8.shape), _const_spec(d_skip.shape),
                  _const_spec(bd.shape), _const_spec(cd.shape)],
        out_specs=[pl.BlockSpec(block, lambda i: (i, 0)),
                   pl.BlockSpec(h0.shape, lambda i: (0, 0))],
        out_shape=[jax.ShapeDtypeStruct(u.shape, BF16), jax.ShapeDtypeStruct(h0.shape, F32)],
        scratch_shapes=[pltpu.VMEM((tile, n_state), F32), slab, slab],
        compiler_params=pltpu.CompilerParams(
            dimension_semantics=("arbitrary",), vmem_limit_bytes=VMEM_LIMIT),
        name="ssm",
    )(u, h0, a8, d_skip, bd, cd)


def _out_kernel(x1_ref, at_ref, ga_ref, gs_ref, y_ref, glua_ref, glub_ref, wo_ref,
                nb_ref, wg_ref, wu_ref, wd_ref, nf_ref, o_ref, h_scr, act_scr, *, n_sub):
    rows = _row_slices(x1_ref.shape[0], n_sub)
    ssm = [jnp.dot(y_ref[r, :], glua_ref[...], preferred_element_type=F32) * jax.nn.sigmoid(
        jnp.dot(y_ref[r, :], glub_ref[...], preferred_element_type=F32)) for r in rows]
    merged = [(jax.nn.sigmoid(ga_ref[r, :]) * at_ref[r, :] + jax.nn.sigmoid(gs_ref[r, :]) * s).astype(BF16)
              for r, s in zip(rows, ssm)]
    x2 = [x1_ref[r, :] + jnp.dot(m, wo_ref[...], preferred_element_type=F32) for r, m in zip(rows, merged)]
    for r, x in zip(rows, x2):
        h_scr[r, :] = _rms(x, nb_ref[...]).astype(BF16)
    for r, x, f in zip(rows, x2, _swiglu(h_scr, wg_ref, wu_ref, wd_ref, act_scr, rows)):
        o_ref[r, :] = _rms(x + 0.5 * f, nf_ref[...])


def _out(grid, tm, n_sub, std, ymap, x1, attn, ga, gs, y_tb, glua, glub, wo, nb, wg, wu, wd, nf):
    d = nb.shape[1]
    d_ff = wg.shape[1]
    shp, spec = std(d)
    return pl.pallas_call(
        functools.partial(_out_kernel, n_sub=n_sub),
        grid=grid,
        in_specs=[spec, spec, spec, spec, pl.BlockSpec((tm, glua.shape[0]), ymap),
                  _const_spec(glua.shape), _const_spec(glub.shape), _const_spec(wo.shape),
                  _const_spec(nb.shape), _const_spec(wg.shape), _const_spec(wu.shape),
                  _const_spec(wd.shape), _const_spec(nf.shape)],
        out_specs=spec,
        out_shape=jax.ShapeDtypeStruct(shp, F32),
        scratch_shapes=[pltpu.VMEM((tm, d), BF16), pltpu.VMEM((tm, d_ff), BF16)],
        compiler_params=pltpu.CompilerParams(
            dimension_semantics=("parallel", "parallel"), vmem_limit_bytes=VMEM_LIMIT),
        name="out",
    )(x1, attn, ga, gs, y_tb, glua, glub, wo, nb, wg, wu, wd, nf)


def _block_diag(w, chunks):
    g, a, b = w.shape
    gl = g // chunks
    w = w.reshape(chunks, gl, a, b)
    eye = jnp.eye(gl, dtype=w.dtype)
    return (w[:, :, :, None, :] * eye[None, :, None, :, None]).reshape(chunks, gl * a, gl * b)


def _state_to_lanes(re, im):
    s = re.shape[0]
    re = re.reshape(s, SSM_CHUNKS, -1)
    im = im.reshape(s, SSM_CHUNKS, -1)
    return jnp.stack([re, im], axis=2).reshape(s, -1)


def _lanes_to_state(h, groups):
    s = h.shape[0]
    h = h.reshape(s, SSM_CHUNKS, 2, -1)
    return h[:, :, 0].reshape(s, groups, -1), h[:, :, 1].reshape(s, groups, -1)


def kernel(x_prompt, x_sample, cache_k_win, cache_v_win, state_ssm_re, state_ssm_im, ffn_a_norm, ffn_a_gate, ffn_a_up, ffn_a_down, mix_norm, w_in, attn_sinks, ssm_lambda_re, ssm_lambda_im, ssm_log_dt, ssm_b_re, ssm_b_im, ssm_c_re, ssm_c_im, ssm_d, glu_a, glu_b, w_out, ffn_b_norm, ffn_b_gate, ffn_b_up, ffn_b_down, final_norm):
    depth = ffn_a_norm.shape[0]
    assert depth == 1, "single-layer trunk"
    n_p, seq, d = x_prompt.shape
    n_s, dec, _ = x_sample.shape
    ssm_w = ssm_d.shape[1]
    groups = ssm_lambda_re.shape[1]
    kvw = N_KV_HEADS * HEAD_DIM
    assert cache_k_win.shape[2] == WINDOW and seq % WINDOW == 0 and n_p % 8 == 0 and n_s % 8 == 0

    l = 0
    na, nm, nb = (w[l].reshape(1, d) for w in (ffn_a_norm, mix_norm, ffn_b_norm))
    nf = final_norm.reshape(1, d)
    wga, wua, wda, win = (w[l].astype(BF16) for w in (ffn_a_gate, ffn_a_up, ffn_a_down, w_in))
    wgb, wub, wdb = (w[l].astype(BF16) for w in (ffn_b_gate, ffn_b_up, ffn_b_down))
    glua, glub, wo = (w[l].astype(BF16) for w in (glu_a, glu_b, w_out))
    sinks = attn_sinks[l]
    d_skip = ssm_d[l].reshape(1, ssm_w)

    abar_re, abar_im, bb_re_t, bb_im_t = _ssm_prep(
        ssm_lambda_re[l], ssm_lambda_im[l], ssm_log_dt[l],
        jnp.swapaxes(ssm_b_re[l], 1, 2), jnp.swapaxes(ssm_b_im[l], 1, 2))
    bd = jnp.concatenate([_block_diag(bb_re_t, SSM_CHUNKS), _block_diag(bb_im_t, SSM_CHUNKS)],
                         axis=2).astype(BF16)
    c_re_t = jnp.swapaxes(ssm_c_re[l], 1, 2)
    c_im_t = jnp.swapaxes(ssm_c_im[l], 1, 2)
    cd = jnp.concatenate([_block_diag(c_re_t, SSM_CHUNKS), -_block_diag(c_im_t, SSM_CHUNKS)],
                         axis=1).astype(BF16)
    a8 = jnp.broadcast_to(_state_to_lanes(abar_re[None], abar_im[None]), (8, 2 * groups * SSM_STATE))

    def run_group(x, n_seq, tiling, out_tiling, u_shape, stacked_w, attn_fn, h0, tt):
        grid, tm, std, umap, n_sub = tiling
        x1, q, k, v, u_tb, ga, gs, *stacked = _ffn_in(x, grid, tm, n_sub, std, umap, u_shape,
                                                       na, wga, wua, wda, nm, win, stacked_w)
        attn = attn_fn(q, k, v, *stacked)
        y_tb, h_t = _ssm(u_tb, h0, a8, d_skip, bd, cd, n_seq, tt,
                         seq_lanes=u_shape[1] != ssm_w, scan_unroll=tt)
        grid, tm, std, umap, n_sub = out_tiling
        y = _out(grid, tm, n_sub, std, umap, x1, attn, ga, gs, y_tb, glua, glub, wo, nb, wgb, wub, wdb, nf)
        return y, k, v, h_t

    def tiling_p(tm):
        nt = seq // tm

        def std(width):
            return (n_p * seq, width), pl.BlockSpec((tm, width), lambda b, i: (b * nt + i, 0))
        return (n_p, nt), tm, std, lambda b, i: (i, b), tm // 256

    tm_p = 512
    nt = seq // tm_p

    def attn_p(qst, k, v, k2, vt):
        return _attn_prompt(sinks, qst, k2, vt, n_p, seq, 2)

    w_k = w_in[l][:, d:d + kvw].reshape(d, N_KV_HEADS, 1, HEAD_DIM)
    wk2 = jnp.broadcast_to(w_k, (d, N_KV_HEADS, 2, HEAD_DIM)).reshape(d, 2 * kvw).astype(BF16)
    wvt = w_in[l][:, d + kvw:d + 2 * kvw].T.astype(BF16)

    h0_p = jnp.zeros((n_p, 2 * groups * SSM_STATE), F32)
    y_p, k_p, v_p, h_p = run_group(
        x_prompt.reshape(n_p * seq, d), n_p, tiling_p(tm_p), tiling_p(tm_p),
        (seq, n_p * ssm_w), (wk2, wvt, lambda b, i: (b * nt + i, 0)), attn_p, h0_p, 1024 // n_p)

    tm_s = 128

    def std_s(width):
        return (n_s * dec, width), pl.BlockSpec((tm_s, width), lambda i, j: (i, 0))

    def window_t(c):
        return jnp.transpose(c, (0, 2, 3, 1)).reshape(c.shape[0], kvw, WINDOW)

    def window(ct):
        return jnp.transpose(ct.reshape(ct.shape[0], N_KV_HEADS, HEAD_DIM, WINDOW), (0, 3, 1, 2))[None]

    new_windows = []

    def attn_s(q, k, v):
        o, kw, vw = _attn_sample(sinks, q, k, v, window_t(cache_k_win[l]), window_t(cache_v_win[l]),
                                 n_s, dec, 16, 4)
        new_windows.extend([kw, vw])
        return o

    h0_s = _state_to_lanes(state_ssm_re[l], state_ssm_im[l])
    tiling_s = ((n_s * dec // tm_s, 1), tm_s, std_s, lambda i, j: (i, 0), 1)
    y_s, _, _, h_s = run_group(
        x_sample.reshape(n_s * dec, d), n_s, tiling_s, tiling_s,
        (n_s * dec, ssm_w), None, attn_s, h0_s, dec)

    sp_re, sp_im = _lanes_to_state(h_p, groups)
    ss_re, ss_im = _lanes_to_state(h_s, groups)
    return (y_p.reshape(n_p, seq, d), y_s.reshape(n_s, dec, d),
            window(k_p), window(v_p), window(new_windows[0]), window(new_windows[1]),
            sp_re[None], sp_im[None], ss_re[None], ss_im[None])
```

```python
import functools

import jax
import jax.numpy as jnp
from jax import lax
from jax.experimental import pallas as pl
from jax.experimental.pallas import tpu as pltpu

F32 = jnp.float32
BF16 = jnp.bfloat16

N_HEADS = 16
N_KV_HEADS = 4
HEAD_DIM = 64
Q_PER_KV = N_HEADS // N_KV_HEADS
WINDOW = 128
SSM_STATE = 64
RMS_EPS = 1e-6
NEG_BIG = -1e30
LOG2_E = 1.4426950408889634

MXU_COLS = 256
WINDOW_COLS = N_KV_HEADS * HEAD_DIM
ATTN_AHEAD = 2
ATTN_SLOTS = 4
SSM_CHUNKS = 4
VMEM_LIMIT = 62 * 1024 * 1024


def _rms(x, g):
    return x * lax.rsqrt(jnp.mean(x * x, axis=-1, keepdims=True) + RMS_EPS) * g


def _row_slices(tm, n_sub):
    rs = tm // n_sub
    return [slice(r * rs, (r + 1) * rs) for r in range(n_sub)]


def _swiglu(h_scr, wg_ref, wu_ref, wd_ref, act_scr, rows=(slice(None),), fillers=()):
    d_ff = wg_ref.shape[1]
    fillers = list(fillers)
    for c in range(d_ff // MXU_COLS):
        sl = slice(c * MXU_COLS, (c + 1) * MXU_COLS)
        for r in rows:
            g = jnp.dot(h_scr[r, :], wg_ref[:, sl], preferred_element_type=F32)
            u = jnp.dot(h_scr[r, :], wu_ref[:, sl], preferred_element_type=F32)
            act_scr[r, sl] = (jax.nn.silu(g) * u).astype(BF16)
            if fillers:
                fillers.pop(0)()
    for f in fillers:
        f()
    return [jnp.dot(act_scr[r, :], wd_ref[...], preferred_element_type=F32) for r in rows]


def _ffn_in_kernel(*refs, stacked, n_sub):
    if stacked:
        (x_ref, na_ref, wg_ref, wu_ref, wd_ref, nm_ref, win_ref, wk2_ref, wvt_ref,
         x1_ref, q_ref, k_ref, v_ref, u_ref, ga_ref, gs_ref, k2_ref, vt_ref, h_scr, act_scr) = refs
        assert WINDOW_COLS == k_ref.shape[0]
    else:
        (x_ref, na_ref, wg_ref, wu_ref, wd_ref, nm_ref, win_ref,
         x1_ref, q_ref, k_ref, v_ref, u_ref, ga_ref, gs_ref, h_scr, act_scr) = refs
    tm, d = x_ref.shape
    rows = _row_slices(tm, n_sub)
    rs = tm // n_sub
    for r in rows:
        h_scr[r, :] = _rms(x_ref[r, :], na_ref[...]).astype(BF16)
    x1 = [x_ref[r, :] + 0.5 * f for r, f in zip(rows, _swiglu(h_scr, wg_ref, wu_ref, wd_ref, act_scr, rows))]
    for r, x in zip(rows, x1):
        x1_ref[r, :] = x
        h_scr[r, :] = _rms(x, nm_ref[...]).astype(BF16)

    def proj(w_ref, off, c, r):
        return jnp.dot(h_scr[r, :], w_ref[:, off + c * MXU_COLS: off + (c + 1) * MXU_COLS],
                       preferred_element_type=F32)

    q_scale = HEAD_DIM ** -0.5 * (LOG2_E if stacked else 1.0)
    pair_w = 2 * HEAD_DIM
    if stacked:
        assert Q_PER_KV * HEAD_DIM == MXU_COLS
        low_half = lax.broadcasted_iota(jnp.int32, (rs, MXU_COLS), 1) % pair_w < HEAD_DIM
        for g in range(N_KV_HEADS):
            for ri, r in enumerate(rows):
                res = proj(win_ref, 0, g, r) * q_scale
                halves = (jnp.where(low_half, res, 0.0).astype(BF16), jnp.where(low_half, 0.0, res).astype(BF16))
                for bl in range(rs // WINDOW):
                    for rr in range(Q_PER_KV):
                        row = (((ri * (rs // WINDOW) + bl) * N_KV_HEADS + g) * Q_PER_KV + rr) * WINDOW
                        q_ref[row:row + WINDOW, :] = halves[rr % 2][bl * WINDOW:(bl + 1) * WINDOW,
                                                                   (rr // 2) * pair_w:(rr // 2 + 1) * pair_w]
    else:
        for c in range(d // MXU_COLS):
            for r in rows:
                q_ref[r, c * MXU_COLS:(c + 1) * MXU_COLS] = (proj(win_ref, 0, c, r) * q_scale).astype(q_ref.dtype)
        for ref, off in ((k_ref, d), (v_ref, d + WINDOW_COLS)):
            for r in rows:
                ref[r, :] = proj(win_ref, off, 0, r)
    off = d + 2 * WINDOW_COLS
    for ref in (u_ref, ga_ref, gs_ref):
        width = ref.shape[1]
        for c in range(width // MXU_COLS):
            for r in rows:
                ref[r, c * MXU_COLS:(c + 1) * MXU_COLS] = proj(win_ref, off, c, r).astype(ref.dtype)
        off += width
    if stacked:
        for c in range(k2_ref.shape[1] // MXU_COLS):
            for r in rows:
                k2_ref[r, c * MXU_COLS:(c + 1) * MXU_COLS] = proj(wk2_ref, 0, c, r).astype(k2_ref.dtype)
        for r in rows:
            vt_ref[:, r] = lax.dot_general(wvt_ref[...], h_scr[r, :], (((1,), (1,)), ((), ())),
                                           preferred_element_type=F32).astype(vt_ref.dtype)

        @pl.when(pl.program_id(1) == pl.num_programs(1) - 1)
        def _():
            tail = h_scr[tm - WINDOW:, :]
            for ref, off in ((k_ref, d), (v_ref, d + WINDOW_COLS)):
                ref[...] = jnp.dot(tail, win_ref[:, off:off + WINDOW_COLS], preferred_element_type=F32).T


def _const_spec(shape):
    nd = len(shape)
    return pl.BlockSpec(shape, lambda *_: (0,) * nd, pipeline_mode=pl.Buffered(1))


def _ffn_in(x2d, grid, tm, n_sub, std, umap, u_shape, na, wg, wu, wd, nm, win, stacked_w=None):
    d = na.shape[1]
    d_ff = wg.shape[1]
    kv = N_KV_HEADS * HEAD_DIM
    ssm_w = win.shape[1] - 3 * d - 2 * kv
    stacked = stacked_w is not None
    n_tok = x2d.size // d
    out_shape, out_specs = [], []

    def add(shape_spec, dtype):
        out_shape.append(jax.ShapeDtypeStruct(shape_spec[0], dtype))
        out_specs.append(shape_spec[1])

    add(std(d), F32)
    if stacked:
        wk2, wvt, tile_map = stacked_w
        add(((n_tok * N_HEADS, 2 * HEAD_DIM), pl.BlockSpec((tm * N_HEADS, 2 * HEAD_DIM), tile_map)), BF16)
    else:
        add(std(d), F32)
    if stacked:
        for _ in range(2):
            add(((grid[0], kv, WINDOW), pl.BlockSpec((None, kv, WINDOW), lambda b, i: (b, 0, 0))), F32)
    else:
        add(std(kv), F32)
        add(std(kv), F32)
    add((u_shape, pl.BlockSpec((tm, ssm_w), umap)), F32)
    add(std(d), F32)
    add(std(d), F32)
    weights = [na, wg, wu, wd, nm, win]
    if stacked:
        add(std(2 * kv), BF16)
        add(((kv, n_tok), pl.BlockSpec((kv, tm), lambda *g: tile_map(*g)[::-1])), BF16)
        weights += [wk2, wvt]
    return pl.pallas_call(
        functools.partial(_ffn_in_kernel, stacked=stacked, n_sub=n_sub),
        grid=grid,
        in_specs=[std(d)[1]] + [_const_spec(w.shape) for w in weights],
        out_specs=out_specs,
        out_shape=out_shape,
        scratch_shapes=[pltpu.VMEM((tm, d), BF16), pltpu.VMEM((tm, d_ff), BF16)],
        compiler_params=pltpu.CompilerParams(
            dimension_semantics=("parallel", "arbitrary"), vmem_limit_bytes=VMEM_LIMIT),
        name="ffn_in",
    )(x2d, *weights)


def _attention_pieces(sink_ref, q_ref, kp_ref, kc_ref, vp_ref, vc_ref, st_scr, o_ref, ctx_ok):
    nq = WINDOW
    nk = 2 * WINDOW
    lanes = Q_PER_KV * nq
    pair_w = 2 * HEAD_DIM
    n_blk = kc_ref.shape[0] // WINDOW
    n_slots = st_scr.shape[0]
    masks = {}

    def valid(bl):
        if bl not in masks:
            jk = lax.broadcasted_iota(jnp.int32, (nk, lanes), 0)
            iq = lax.broadcasted_iota(jnp.int32, (nk, lanes), 1) & (nq - 1)
            ok = True if bl > 0 else ctx_ok
            lo = iq if ok is True else jnp.maximum(iq, jnp.where(ok, 0, WINDOW))
            masks[bl] = (jk >= lo) & (jk <= iq + WINDOW)
        return masks[bl]

    def keys(bl, g):
        cols = slice(g * pair_w, (g + 1) * pair_w)
        if bl == 0:
            return jnp.concatenate([kp_ref[:, cols], kc_ref[0:WINDOW, cols]], axis=0)
        return kc_ref[(bl - 1) * WINDOW:(bl + 1) * WINDOW, cols]

    def vals_t(bl, g):
        rows = slice(g * HEAD_DIM, (g + 1) * HEAD_DIM)
        if bl == 0:
            return jnp.concatenate([vp_ref[rows, :], vc_ref[rows, 0:WINDOW]], axis=1)
        return vc_ref[rows, (bl - 1) * WINDOW:(bl + 1) * WINDOW]

    def score(bl, g):
        k = bl * N_KV_HEADS + g
        st = lax.dot_general(keys(bl, g), q_ref[k * lanes:(k + 1) * lanes, :], (((1,), (1,)), ((), ())),
                             preferred_element_type=F32)
        st_scr[k % n_slots] = jnp.where(valid(bl), st, NEG_BIG)

    def attend(bl, g):
        st = st_scr[(bl * N_KV_HEADS + g) % n_slots]
        sink = jnp.concatenate(
            [jnp.full((1, nq), sink_ref[g * Q_PER_KV + r] * LOG2_E, F32) for r in range(Q_PER_KV)], axis=1)
        m = jnp.maximum(jnp.max(st, axis=0, keepdims=True), sink)
        p = jnp.exp2(st - m)
        denom = jnp.sum(p, axis=0, keepdims=True) + jnp.exp2(sink - m)
        ot = jnp.dot(vals_t(bl, g), p.astype(BF16), preferred_element_type=F32) * (1.0 / denom)
        for pr in range(Q_PER_KV // 2):
            two = jnp.concatenate([ot[:, (2 * pr) * nq:(2 * pr + 1) * nq],
                                   ot[:, (2 * pr + 1) * nq:(2 * pr + 2) * nq]], axis=0)
            col = (g * Q_PER_KV + 2 * pr) * HEAD_DIM
            o_ref[bl * nq:(bl + 1) * nq, col:col + pair_w] = two.T

    order = [(bl, g) for bl in range(n_blk) for g in range(N_KV_HEADS)]
    return ([functools.partial(score, bl, g) for bl, g in order],
            [functools.partial(attend, bl, g) for bl, g in order])


def _staggered(score, attend, ahead):
    def piece(k):
        def run():
            if k + ahead < len(score):
                score[k + ahead]()
            attend[k]()
        return run

    def first():
        for f in score[:ahead]:
            f()
    return first, [piece(k) for k in range(len(attend))]


def _attn_sample_kernel(sink_ref, q_ref, k_ref, v_ref, ckt_ref, cvt_ref, o_ref, kw_ref, vw_ref, *, n_sub, tn, unroll):
    pair_w = 2 * HEAD_DIM
    n_pairs = N_HEADS // 2
    rows = n_pairs * tn
    kv = N_KV_HEADS * HEAD_DIM
    old = WINDOW - tn
    assert tn & (tn - 1) == 0, "row -> token index uses a power-of-two mask"
    low = lax.broadcasted_iota(jnp.int32, (tn, pair_w), 1) < HEAD_DIM
    tq = lax.broadcasted_iota(jnp.int32, (rows, WINDOW), 0) & (tn - 1)
    col = lax.broadcasted_iota(jnp.int32, (rows, WINDOW), 1)
    valid_c = col >= tq
    valid_n = (col >= old) & (col - old <= tq)
    is_new = lax.broadcasted_iota(jnp.int32, (kv, WINDOW), 1) >= old
    zeros = jnp.zeros((tn, pair_w), F32)
    pad = jnp.zeros((old, kv), F32)
    nt = (((1,), (1,)), ((), ()))

    sinks = [jnp.concatenate([jnp.full((tn, 1), sink_ref[2 * pr + half], F32) for pr in range(n_pairs)], axis=0)
             for half in range(2)]

    def scores(s):
        r0 = pl.multiple_of(s * tn, tn)
        q = q_ref[pl.ds(r0, tn), :]
        kt = ckt_ref[s]
        vt = cvt_ref[s]
        knt = jnp.concatenate([pad, k_ref[pl.ds(r0, tn), :]], axis=0).T
        vnt = jnp.concatenate([pad, v_ref[pl.ds(r0, tn), :]], axis=0).T
        kw_ref[s] = jnp.where(is_new, knt, pltpu.roll(kt, old, axis=1))
        vw_ref[s] = jnp.where(is_new, vnt, pltpu.roll(vt, old, axis=1))
        kt, vt, knt, vnt = (a.astype(BF16) for a in (kt, vt, knt, vnt))
        sc_sn = []
        for half in range(2):
            blocks = []
            for g in range(N_KV_HEADS):
                for pp in range(Q_PER_KV // 2):
                    piece = q[:, (2 * g + pp) * pair_w:(2 * g + pp + 1) * pair_w]
                    if half != g % 2:
                        piece = pltpu.roll(piece, HEAD_DIM, axis=1)
                    piece = jnp.where(low if g % 2 == 0 else ~low, piece, 0.0)
                    blocks.append(jnp.concatenate([piece, zeros] if g // 2 == 0 else [zeros, piece], axis=1))
            qh = jnp.concatenate(blocks, axis=0).astype(BF16)
            sc_sn.append((jnp.where(valid_c, jnp.dot(qh, kt, preferred_element_type=F32), NEG_BIG),
                          jnp.where(valid_n, jnp.dot(qh, knt, preferred_element_type=F32), NEG_BIG)))
        return r0, vt, vnt, sc_sn

    def attend(r0, vt, vnt, sc_sn):
        outs = []
        for half, (sc, sn) in enumerate(sc_sn):
            sink = sinks[half]
            m = jnp.maximum(jnp.max(jnp.maximum(sc, sn), axis=-1, keepdims=True), sink)
            pc = jnp.exp(sc - m)
            pn = jnp.exp(sn - m)
            rden = 1.0 / (jnp.sum(pc + pn, axis=-1, keepdims=True) + jnp.exp(sink - m))
            outs.append(lax.dot_general((pc * rden).astype(BF16), vt, nt, preferred_element_type=F32)
                        + lax.dot_general((pn * rden).astype(BF16), vnt, nt, preferred_element_type=F32))
        for pr in range(n_pairs):
            g = pr // (Q_PER_KV // 2)
            sel = []
            for half in range(2):
                blk = outs[half][pr * tn:(pr + 1) * tn, (g // 2) * pair_w:(g // 2 + 1) * pair_w]
                sel.append(blk if half == g % 2 else pltpu.roll(blk, HEAD_DIM, axis=1))
            o_ref[pl.ds(r0, tn), pr * pair_w:(pr + 1) * pair_w] = jnp.where(low, sel[0], sel[1])

    def some_sequences(i, carry):
        staged = [scores(i * unroll + u) for u in range(unroll)]
        for st in staged:
            attend(*st)
        return carry

    lax.fori_loop(0, n_sub // unroll, some_sequences, 0)


def _attn_sample(sinks, q, k, v, cache_kt, cache_vt, n_seq, tn, n_sub, unroll):
    d = q.shape[1]
    kv = k.shape[1]
    rows = lambda i: (i, 0)
    seqs = lambda i: (i, 0, 0)
    win = pl.BlockSpec((n_sub, kv, WINDOW), seqs)
    return pl.pallas_call(
        functools.partial(_attn_sample_kernel, n_sub=n_sub, tn=tn, unroll=unroll),
        grid=(n_seq // n_sub,),
        in_specs=[pl.BlockSpec(memory_space=pltpu.SMEM),
                  pl.BlockSpec((n_sub * tn, d), rows),
                  pl.BlockSpec((n_sub * tn, kv), rows), pl.BlockSpec((n_sub * tn, kv), rows), win, win],
        out_specs=[pl.BlockSpec((n_sub * tn, d), rows), win, win],
        out_shape=[jax.ShapeDtypeStruct(q.shape, F32),
                   jax.ShapeDtypeStruct(cache_kt.shape, F32), jax.ShapeDtypeStruct(cache_vt.shape, F32)],
        compiler_params=pltpu.CompilerParams(
            dimension_semantics=("parallel",), vmem_limit_bytes=VMEM_LIMIT),
        name="attn_sample",
    )(sinks, q, k, v, cache_kt, cache_vt)


def _ssm_prep_kernel(lr_ref, li_ref, ldt_ref, br_ref, bi_ref, ar_ref, ai_ref, bbr_ref, bbi_ref):
    lr = lr_ref[...]
    li = li_ref[...]
    dt = jnp.exp(ldt_ref[...])
    mag = jnp.exp(lr * dt)
    ang = li * dt
    abar_re = mag * jnp.cos(ang)
    abar_im = mag * jnp.sin(ang)
    den = lr * lr + li * li
    nr = abar_re - 1.0
    fr = (nr * lr + abar_im * li) / den
    fi = (abar_im * lr - nr * li) / den
    ar_ref[...] = abar_re
    ai_ref[...] = abar_im
    br = br_ref[...]
    bi = bi_ref[...]
    bbr_ref[...] = fr[:, None, :] * br - fi[:, None, :] * bi
    bbi_ref[...] = fr[:, None, :] * bi + fi[:, None, :] * br


def _ssm_prep(lam_re, lam_im, log_dt, b_re_t, b_im_t):
    g, n = lam_re.shape
    return pl.pallas_call(
        _ssm_prep_kernel,
        out_shape=[jax.ShapeDtypeStruct((g, n), F32), jax.ShapeDtypeStruct((g, n), F32),
                   jax.ShapeDtypeStruct(b_re_t.shape, F32), jax.ShapeDtypeStruct(b_im_t.shape, F32)],
        name="ssm_prep",
    )(lam_re, lam_im, log_dt.reshape(g, 1), b_re_t, b_im_t)


def _ssm_kernel(u_ref, h0_ref, a_ref, d_ref, bd_ref, cd_ref, y_ref, hT_ref, xh_scr, u_scr, y_scr,
                *, n_seq, tt, seq_lanes, scan_unroll):
    cw = xh_scr.shape[1] // SSM_CHUNKS
    hw = cw // 2
    uc = u_scr.shape[2]
    ssm_w = SSM_CHUNKS * uc

    @pl.when(pl.program_id(0) == 0)
    def _():
        hT_ref[...] = h0_ref[...]

    for j in range(SSM_CHUNKS):
        for b in range(n_seq):
            if seq_lanes:
                u_b = u_ref[:, b * ssm_w + j * uc:b * ssm_w + (j + 1) * uc]
            else:
                u_b = u_ref[b * tt:(b + 1) * tt, j * uc:(j + 1) * uc]
            u_scr[j, pl.ds(b, tt, stride=n_seq), :] = u_b

    def project_in(j):
        xh_scr[:, j * cw:(j + 1) * cw] = jnp.dot(u_scr[j].astype(BF16), bd_ref[j], preferred_element_type=F32)

    def scan(j):
        re = slice(j * cw, j * cw + hw)
        im = slice(j * cw + hw, (j + 1) * cw)
        ar = a_ref[:, re]
        ai = a_ref[:, im]

        def seq_group(s, carry):
            s8 = pl.multiple_of(s * 8, 8)

            def step(t, h):
                hr, hi = h
                r0 = pl.multiple_of(t * n_seq + s8, 8)
                nr = ar * hr - ai * hi + xh_scr[pl.ds(r0, 8), re]
                ni = ar * hi + ai * hr + xh_scr[pl.ds(r0, 8), im]
                xh_scr[pl.ds(r0, 8), re] = nr
                xh_scr[pl.ds(r0, 8), im] = ni
                return nr, ni

            hr, hi = lax.fori_loop(0, tt, step, (hT_ref[pl.ds(s8, 8), re], hT_ref[pl.ds(s8, 8), im]),
                                   unroll=scan_unroll)
            hT_ref[pl.ds(s8, 8), re] = hr
            hT_ref[pl.ds(s8, 8), im] = hi
            return carry

        lax.fori_loop(0, n_seq // 8, seq_group, 0)

    def project_out(j):
        y = jnp.dot(xh_scr[:, j * cw:(j + 1) * cw].astype(BF16), cd_ref[j], preferred_element_type=F32)
        cs = slice(j * uc, (j + 1) * uc)
        y_scr[j] = jax.nn.gelu(y + d_ref[:, cs] * u_scr[j])
        for b in range(n_seq):
            y_b = y_scr[j, pl.ds(b, tt, stride=n_seq), :].astype(y_ref.dtype)
            if seq_lanes:
                y_ref[:, b * ssm_w + j * uc:b * ssm_w + (j + 1) * uc] = y_b
            else:
                y_ref[b * tt:(b + 1) * tt, cs] = y_b

    for stage in range(SSM_CHUNKS + 2):
        if stage < SSM_CHUNKS:
            project_in(stage)
        if 1 <= stage <= SSM_CHUNKS:
            scan(stage - 1)
        if stage >= 2:
            project_out(stage - 2)


def _ssm(u, h0, a8, d_skip, bd, cd, n_seq, tt, seq_lanes, scan_unroll):
    ssm_w = d_skip.shape[1]
    tile = n_seq * tt
    n_state = h0.shape[1]
    block = (tt, n_seq * ssm_w) if seq_lanes else (tile, ssm_w)
    assert seq_lanes or u.shape[0] == tile, "row order (seq, t) cannot be tiled over time"
    slab = pltpu.VMEM((SSM_CHUNKS, tile, ssm_w // SSM_CHUNKS), F32)
    return pl.pallas_call(
        functools.partial(_ssm_kernel, n_seq=n_seq, tt=tt, seq_lanes=seq_lanes, scan_unroll=scan_unroll),
        grid=(u.shape[0] // block[0],),
        in_specs=[pl.BlockSpec(block, lambda i: (i, 0)),
                  _const_spec(h0.shape), _const_spec(a8.shape), _const_spec(d_skip.shape),
                  _const_spec(bd.shape), _const_spec(cd.shape)],
        out_specs=[pl.BlockSpec(block, lambda i: (i, 0)),
                   pl.BlockSpec(h0.shape, lambda i: (0, 0))],
        out_shape=[jax.ShapeDtypeStruct(u.shape, BF16), jax.ShapeDtypeStruct(h0.shape, F32)],
        scratch_shapes=[pltpu.VMEM((tile, n_state), F32), slab, slab],
        compiler_params=pltpu.CompilerParams(
            dimension_semantics=("arbitrary",), vmem_limit_bytes=VMEM_LIMIT),
        name="ssm",
    )(u, h0, a8, d_skip, bd, cd)


def _out_kernel(*refs, n_sub, tiles_per_seq):
    if tiles_per_seq:
        (sink_ref, x1_ref, ga_ref, gs_ref, y_ref, q0_ref, k0_ref, v0_ref, qn_ref, kp_ref, kn_ref, vp_ref, vn_ref,
         glua_ref, glub_ref, wo_ref, nb_ref, wg_ref, wu_ref, wd_ref, nf_ref, o_ref,
         h_scr, act_scr, at_ref, st_scr) = refs
        step = pl.program_id(0)

        @pl.when(step == 0)
        def _():
            score, attend = _attention_pieces(sink_ref, q0_ref, k0_ref.at[pl.ds(0, WINDOW)], k0_ref,
                                              v0_ref.at[:, pl.ds(0, WINDOW)], v0_ref, st_scr, at_ref, False)
            first, rest = _staggered(score, attend, ATTN_AHEAD)
            first()
            for f in rest:
                f()

        nxt = jnp.minimum(step + 1, pl.num_programs(0) - 1)
        score, attend = _attention_pieces(sink_ref, qn_ref, kp_ref, kn_ref, vp_ref, vn_ref, st_scr, at_ref,
                                          nxt % tiles_per_seq != 0)
        first, fillers = _staggered(score, attend, ATTN_AHEAD)
    else:
        (x1_ref, at_ref, ga_ref, gs_ref, y_ref, glua_ref, glub_ref, wo_ref,
         nb_ref, wg_ref, wu_ref, wd_ref, nf_ref, o_ref, h_scr, act_scr) = refs
        first, fillers = (lambda: None), ()
    rows = _row_slices(x1_ref.shape[0], n_sub)
    ssm = [jnp.dot(y_ref[r, :], glua_ref[...], preferred_element_type=F32) * jax.nn.sigmoid(
        jnp.dot(y_ref[r, :], glub_ref[...], preferred_element_type=F32)) for r in rows]
    merged = [(jax.nn.sigmoid(ga_ref[r, :]) * at_ref[r, :] + jax.nn.sigmoid(gs_ref[r, :]) * s).astype(BF16)
              for r, s in zip(rows, ssm)]
    x2 = [x1_ref[r, :] + jnp.dot(m, wo_ref[...], preferred_element_type=F32) for r, m in zip(rows, merged)]
    first()
    for r, x in zip(rows, x2):
        h_scr[r, :] = _rms(x, nb_ref[...]).astype(BF16)
    for r, x, f in zip(rows, x2, _swiglu(h_scr, wg_ref, wu_ref, wd_ref, act_scr, rows, fillers)):
        o_ref[r, :] = _rms(x + 0.5 * f, nf_ref[...])


def _out(grid, tm, n_sub, std, ymap, x1, attn, ga, gs, y_tb, glua, glub, wo, nb, wg, wu, wd, nf):
    d = nb.shape[1]
    d_ff = wg.shape[1]
    kv = N_KV_HEADS * HEAD_DIM
    shp, spec = std(d)
    weights = [glua, glub, wo, nb, wg, wu, wd, nf]
    scratch = [pltpu.VMEM((tm, d), BF16), pltpu.VMEM((tm, d_ff), BF16)]
    fused = isinstance(attn, tuple)
    if fused:
        sinks, qst, k2, vt, tiles_per_seq = attn
        (n_tiles,) = grid
        blk = tm // WINDOW
        nxt = lambda s: jnp.minimum(s + 1, n_tiles - 1)
        before = lambda s: jnp.maximum(nxt(s) * blk - 1, 0)
        once = dict(pipeline_mode=pl.Buffered(1))
        acts = [sinks, x1, ga, gs, y_tb, qst, k2, vt, qst, k2, k2, vt, vt]
        act_specs = [pl.BlockSpec(memory_space=pltpu.SMEM), spec, spec, spec, pl.BlockSpec((tm, glua.shape[0]), ymap),
                     pl.BlockSpec((tm * N_HEADS, 2 * HEAD_DIM), lambda s: (0, 0), **once),
                     pl.BlockSpec((tm, 2 * kv), lambda s: (0, 0), **once),
                     pl.BlockSpec((kv, tm), lambda s: (0, 0), **once),
                     pl.BlockSpec((tm * N_HEADS, 2 * HEAD_DIM), lambda s: (nxt(s), 0)),
                     pl.BlockSpec((WINDOW, 2 * kv), lambda s: (before(s), 0)),
                     pl.BlockSpec((tm, 2 * kv), lambda s: (nxt(s), 0)),
                     pl.BlockSpec((kv, WINDOW), lambda s: (0, before(s))),
                     pl.BlockSpec((kv, tm), lambda s: (0, nxt(s)))]
        scratch += [pltpu.VMEM((tm, d), F32), pltpu.VMEM((ATTN_SLOTS, 2 * WINDOW, Q_PER_KV * WINDOW), F32)]
    else:
        tiles_per_seq = 0
        acts = [x1, attn, ga, gs, y_tb]
        act_specs = [spec, spec, spec, spec, pl.BlockSpec((tm, glua.shape[0]), ymap)]
    return pl.pallas_call(
        functools.partial(_out_kernel, n_sub=n_sub, tiles_per_seq=tiles_per_seq),
        grid=grid,
        in_specs=act_specs + [_const_spec(w.shape) for w in weights],
        out_specs=spec,
        out_shape=jax.ShapeDtypeStruct(shp, F32),
        scratch_shapes=scratch,
        compiler_params=pltpu.CompilerParams(
            dimension_semantics=("arbitrary",) * len(grid), vmem_limit_bytes=VMEM_LIMIT),
        name="out",
    )(*acts, *weights)


def _block_diag(w, chunks):
    g, a, b = w.shape
    gl = g // chunks
    w = w.reshape(chunks, gl, a, b)
    eye = jnp.eye(gl, dtype=w.dtype)
    return (w[:, :, :, None, :] * eye[None, :, None, :, None]).reshape(chunks, gl * a, gl * b)


def _state_to_lanes(re, im):
    s = re.shape[0]
    re = re.reshape(s, SSM_CHUNKS, -1)
    im = im.reshape(s, SSM_CHUNKS, -1)
    return jnp.stack([re, im], axis=2).reshape(s, -1)


def _lanes_to_state(h, groups):
    s = h.shape[0]
    h = h.reshape(s, SSM_CHUNKS, 2, -1)
    return h[:, :, 0].reshape(s, groups, -1), h[:, :, 1].reshape(s, groups, -1)


def kernel(x_prompt, x_sample, cache_k_win, cache_v_win, state_ssm_re, state_ssm_im, ffn_a_norm, ffn_a_gate, ffn_a_up, ffn_a_down, mix_norm, w_in, attn_sinks, ssm_lambda_re, ssm_lambda_im, ssm_log_dt, ssm_b_re, ssm_b_im, ssm_c_re, ssm_c_im, ssm_d, glu_a, glu_b, w_out, ffn_b_norm, ffn_b_gate, ffn_b_up, ffn_b_down, final_norm):
    depth = ffn_a_norm.shape[0]
    assert depth == 1, "single-layer trunk"
    n_p, seq, d = x_prompt.shape
    n_s, dec, _ = x_sample.shape
    ssm_w = ssm_d.shape[1]
    groups = ssm_lambda_re.shape[1]
    kvw = N_KV_HEADS * HEAD_DIM
    assert cache_k_win.shape[2] == WINDOW and seq % WINDOW == 0 and n_p % 8 == 0 and n_s % 8 == 0

    l = 0
    na, nm, nb = (w[l].reshape(1, d) for w in (ffn_a_norm, mix_norm, ffn_b_norm))
    nf = final_norm.reshape(1, d)
    wga, wua, wda, win = (w[l].astype(BF16) for w in (ffn_a_gate, ffn_a_up, ffn_a_down, w_in))
    wgb, wub, wdb = (w[l].astype(BF16) for w in (ffn_b_gate, ffn_b_up, ffn_b_down))
    glua, glub, wo = (w[l].astype(BF16) for w in (glu_a, glu_b, w_out))
    sinks = attn_sinks[l]
    d_skip = ssm_d[l].reshape(1, ssm_w)

    abar_re, abar_im, bb_re_t, bb_im_t = _ssm_prep(
        ssm_lambda_re[l], ssm_lambda_im[l], ssm_log_dt[l],
        jnp.swapaxes(ssm_b_re[l], 1, 2), jnp.swapaxes(ssm_b_im[l], 1, 2))
    bd = jnp.concatenate([_block_diag(bb_re_t, SSM_CHUNKS), _block_diag(bb_im_t, SSM_CHUNKS)],
                         axis=2).astype(BF16)
    c_re_t = jnp.swapaxes(ssm_c_re[l], 1, 2)
    c_im_t = jnp.swapaxes(ssm_c_im[l], 1, 2)
    cd = jnp.concatenate([_block_diag(c_re_t, SSM_CHUNKS), -_block_diag(c_im_t, SSM_CHUNKS)],
                         axis=1).astype(BF16)
    a8 = jnp.broadcast_to(_state_to_lanes(abar_re[None], abar_im[None]), (8, 2 * groups * SSM_STATE))

    def run_group(x, n_seq, tiling, out_tiling, u_shape, stacked_w, attn_fn, h0, tt):
        grid, tm, std, umap, n_sub = tiling
        x1, q, k, v, u_tb, ga, gs, *stacked = _ffn_in(x, grid, tm, n_sub, std, umap, u_shape,
                                                       na, wga, wua, wda, nm, win, stacked_w)
        attn = attn_fn(q, k, v, *stacked)
        y_tb, h_t = _ssm(u_tb, h0, a8, d_skip, bd, cd, n_seq, tt,
                         seq_lanes=u_shape[1] != ssm_w, scan_unroll=tt)
        grid, tm, std, umap, n_sub = out_tiling
        y = _out(grid, tm, n_sub, std, umap, x1, attn, ga, gs, y_tb, glua, glub, wo, nb, wgb, wub, wdb, nf)
        return y, k, v, h_t

    def tiling_p(tm):
        nt = seq // tm

        def std(width):
            return (n_p * seq, width), pl.BlockSpec((tm, width), lambda b, i: (b * nt + i, 0))
        return (n_p, nt), tm, std, lambda b, i: (i, b), tm // 256

    tm_p = 512
    nt = seq // tm_p

    def attn_p(qst, k, v, k2, vt):
        return sinks, qst, k2, vt, nt

    w_k = w_in[l][:, d:d + kvw].reshape(d, N_KV_HEADS, 1, HEAD_DIM)
    wk2 = jnp.broadcast_to(w_k, (d, N_KV_HEADS, 2, HEAD_DIM)).reshape(d, 2 * kvw).astype(BF16)
    wvt = w_in[l][:, d + kvw:d + 2 * kvw].T.astype(BF16)

    h0_p = jnp.zeros((n_p, 2 * groups * SSM_STATE), F32)
    def std_flat(width):
        return (n_p * seq, width), pl.BlockSpec((tm_p, width), lambda s: (s, 0))

    out_tiling_p = ((n_p * nt,), tm_p, std_flat, lambda s: (s % nt, s // nt), tm_p // 256)
    y_p, k_p, v_p, h_p = run_group(
        x_prompt.reshape(n_p * seq, d), n_p, tiling_p(tm_p), out_tiling_p,
        (seq, n_p * ssm_w), (wk2, wvt, lambda b, i: (b * nt + i, 0)), attn_p, h0_p, 1024 // n_p)

    tm_s = 128

    def std_s(width):
        return (n_s * dec, width), pl.BlockSpec((tm_s, width), lambda i, j: (i, 0))

    def window_t(c):
        return jnp.transpose(c, (0, 2, 3, 1)).reshape(c.shape[0], kvw, WINDOW)

    def window(ct):
        return jnp.transpose(ct.reshape(ct.shape[0], N_KV_HEADS, HEAD_DIM, WINDOW), (0, 3, 1, 2))[None]

    new_windows = []

    def attn_s(q, k, v):
        o, kw, vw = _attn_sample(sinks, q, k, v, window_t(cache_k_win[l]), window_t(cache_v_win[l]),
                                 n_s, dec, 16, 4)
        new_windows.extend([kw, vw])
        return o

    h0_s = _state_to_lanes(state_ssm_re[l], state_ssm_im[l])
    tiling_s = ((n_s * dec // tm_s, 1), tm_s, std_s, lambda i, j: (i, 0), 1)
    y_s, _, _, h_s = run_group(
        x_sample.reshape(n_s * dec, d), n_s, tiling_s, tiling_s,
        (n_s * dec, ssm_w), None, attn_s, h0_s, dec)

    sp_re, sp_im = _lanes_to_state(h_p, groups)
    ss_re, ss_im = _lanes_to_state(h_s, groups)
    return (y_p.reshape(n_p, seq, d), y_s.reshape(n_s, dec, d),
            window(k_p), window(v_p), window(new_windows[0]), window(new_windows[1]),
            sp_re[None], sp_im[None], ss_re[None], ss_im[None])
```

```python
import functools

import jax
import jax.numpy as jnp
from jax import lax
from jax.experimental import pallas as pl
from jax.experimental.pallas import tpu as pltpu

F32 = jnp.float32
BF16 = jnp.bfloat16

N_HEADS = 16
N_KV_HEADS = 4
HEAD_DIM = 64
Q_PER_KV = N_HEADS // N_KV_HEADS
WINDOW = 128
SSM_STATE = 64
RMS_EPS = 1e-6
NEG_BIG = -1e30
LOG2_E = 1.4426950408889634

MXU_COLS = 256
WINDOW_COLS = N_KV_HEADS * HEAD_DIM
ATTN_AHEAD = 2
ATTN_SLOTS = 4
SSM_CHUNKS = 4
VMEM_LIMIT = 62 * 1024 * 1024


def _rms(x, g):
    return x * lax.rsqrt(jnp.mean(x * x, axis=-1, keepdims=True) + RMS_EPS) * g


def _row_slices(tm, n_sub):
    rs = tm // n_sub
    return [slice(r * rs, (r + 1) * rs) for r in range(n_sub)]


def _swiglu(h_scr, wg_ref, wu_ref, wd_ref, act_scr, rows=(slice(None),), fillers=()):
    d_ff = wg_ref.shape[1]
    fillers = list(fillers)
    for c in range(d_ff // MXU_COLS):
        sl = slice(c * MXU_COLS, (c + 1) * MXU_COLS)
        for r in rows:
            g = jnp.dot(h_scr[r, :], wg_ref[:, sl], preferred_element_type=F32)
            u = jnp.dot(h_scr[r, :], wu_ref[:, sl], preferred_element_type=F32)
            act_scr[r, sl] = (jax.nn.silu(g) * u).astype(BF16)
            if fillers:
                fillers.pop(0)()
    for f in fillers:
        f()
    return [jnp.dot(act_scr[r, :], wd_ref[...], preferred_element_type=F32) for r in rows]


def _ffn_in_kernel(*refs, stacked, n_sub):
    if stacked:
        (x_ref, na_ref, wg_ref, wu_ref, wd_ref, nm_ref, win_ref, wk2_ref, wvt_ref,
         x1_ref, q_ref, k_ref, v_ref, u_ref, ga_ref, gs_ref, k2_ref, vt_ref, h_scr, act_scr) = refs
        assert WINDOW_COLS == k_ref.shape[0]
    else:
        (x_ref, na_ref, wg_ref, wu_ref, wd_ref, nm_ref, win_ref, wkt_ref, wvt_ref,
         x1_ref, q_ref, k_ref, v_ref, u_ref, ga_ref, gs_ref, h_scr, act_scr) = refs
    tm, d = x_ref.shape
    rows = _row_slices(tm, n_sub)
    rs = tm // n_sub
    for r in rows:
        h_scr[r, :] = _rms(x_ref[r, :], na_ref[...]).astype(BF16)
    x1 = [x_ref[r, :] + 0.5 * f for r, f in zip(rows, _swiglu(h_scr, wg_ref, wu_ref, wd_ref, act_scr, rows))]
    for r, x in zip(rows, x1):
        x1_ref[r, :] = x
        h_scr[r, :] = _rms(x, nm_ref[...]).astype(BF16)

    def proj(w_ref, off, c, r):
        return jnp.dot(h_scr[r, :], w_ref[:, off + c * MXU_COLS: off + (c + 1) * MXU_COLS],
                       preferred_element_type=F32)

    q_scale = HEAD_DIM ** -0.5 * (LOG2_E if stacked else 1.0)
    pair_w = 2 * HEAD_DIM
    if stacked:
        assert Q_PER_KV * HEAD_DIM == MXU_COLS
        low_half = lax.broadcasted_iota(jnp.int32, (rs, MXU_COLS), 1) % pair_w < HEAD_DIM
        for g in range(N_KV_HEADS):
            for ri, r in enumerate(rows):
                res = proj(win_ref, 0, g, r) * q_scale
                halves = (jnp.where(low_half, res, 0.0).astype(BF16), jnp.where(low_half, 0.0, res).astype(BF16))
                for bl in range(rs // WINDOW):
                    for rr in range(Q_PER_KV):
                        row = (((ri * (rs // WINDOW) + bl) * N_KV_HEADS + g) * Q_PER_KV + rr) * WINDOW
                        q_ref[row:row + WINDOW, :] = halves[rr % 2][bl * WINDOW:(bl + 1) * WINDOW,
                                                                   (rr // 2) * pair_w:(rr // 2 + 1) * pair_w]
    else:
        for c in range(d // MXU_COLS):
            for r in rows:
                q_ref[r, c * MXU_COLS:(c + 1) * MXU_COLS] = (proj(win_ref, 0, c, r) * q_scale).astype(q_ref.dtype)
        for ref, wt_ref in ((k_ref, wkt_ref), (v_ref, wvt_ref)):
            for r in rows:
                ref[:, r] = lax.dot_general(wt_ref[...], h_scr[r, :], (((1,), (1,)), ((), ())),
                                            preferred_element_type=F32)
    off = d + 2 * WINDOW_COLS
    for ref in (u_ref, ga_ref, gs_ref):
        width = ref.shape[1]
        for c in range(width // MXU_COLS):
            for r in rows:
                ref[r, c * MXU_COLS:(c + 1) * MXU_COLS] = proj(win_ref, off, c, r).astype(ref.dtype)
        off += width
    if stacked:
        for c in range(k2_ref.shape[1] // MXU_COLS):
            for r in rows:
                k2_ref[r, c * MXU_COLS:(c + 1) * MXU_COLS] = proj(wk2_ref, 0, c, r).astype(k2_ref.dtype)
        for r in rows:
            vt_ref[:, r] = lax.dot_general(wvt_ref[...], h_scr[r, :], (((1,), (1,)), ((), ())),
                                           preferred_element_type=F32).astype(vt_ref.dtype)

        @pl.when(pl.program_id(1) == pl.num_programs(1) - 1)
        def _():
            tail = h_scr[tm - WINDOW:, :]
            for ref, off in ((k_ref, d), (v_ref, d + WINDOW_COLS)):
                ref[...] = jnp.dot(tail, win_ref[:, off:off + WINDOW_COLS], preferred_element_type=F32).T


def _const_spec(shape):
    nd = len(shape)
    return pl.BlockSpec(shape, lambda *_: (0,) * nd, pipeline_mode=pl.Buffered(1))


def _ffn_in(x2d, grid, tm, n_sub, std, umap, u_shape, na, wg, wu, wd, nm, win, attn_w, stacked):
    d = na.shape[1]
    d_ff = wg.shape[1]
    kv = N_KV_HEADS * HEAD_DIM
    ssm_w = win.shape[1] - 3 * d - 2 * kv
    n_tok = x2d.size // d
    w_a, w_b, tile_map = attn_w
    t_spec = pl.BlockSpec((kv, tm), lambda *g: tile_map(*g)[::-1])
    out_shape, out_specs = [], []

    def add(shape_spec, dtype):
        out_shape.append(jax.ShapeDtypeStruct(shape_spec[0], dtype))
        out_specs.append(shape_spec[1])

    add(std(d), F32)
    if stacked:
        add(((n_tok * N_HEADS, 2 * HEAD_DIM), pl.BlockSpec((tm * N_HEADS, 2 * HEAD_DIM), tile_map)), BF16)
        for _ in range(2):
            add(((grid[0], kv, WINDOW), pl.BlockSpec((None, kv, WINDOW), lambda b, i: (b, 0, 0))), F32)
    else:
        add(std(d), F32)
        add(((kv, n_tok), t_spec), F32)
        add(((kv, n_tok), t_spec), F32)
    add((u_shape, pl.BlockSpec((tm, ssm_w), umap)), F32)
    add(std(d), F32)
    add(std(d), F32)
    weights = [na, wg, wu, wd, nm, win, w_a, w_b]
    if stacked:
        add(std(2 * kv), BF16)
        add(((kv, n_tok), t_spec), BF16)
    return pl.pallas_call(
        functools.partial(_ffn_in_kernel, stacked=stacked, n_sub=n_sub),
        grid=grid,
        in_specs=[std(d)[1]] + [_const_spec(w.shape) for w in weights],
        out_specs=out_specs,
        out_shape=out_shape,
        scratch_shapes=[pltpu.VMEM((tm, d), BF16), pltpu.VMEM((tm, d_ff), BF16)],
        compiler_params=pltpu.CompilerParams(
            dimension_semantics=("parallel", "arbitrary"), vmem_limit_bytes=VMEM_LIMIT),
        name="ffn_in",
    )(x2d, *weights)


def _attention_pieces(sink_ref, q_ref, kp_ref, kc_ref, vp_ref, vc_ref, st_scr, o_ref, ctx_ok):
    nq = WINDOW
    nk = 2 * WINDOW
    lanes = Q_PER_KV * nq
    pair_w = 2 * HEAD_DIM
    n_blk = kc_ref.shape[0] // WINDOW
    n_slots = st_scr.shape[0]
    masks = {}

    def valid(bl):
        if bl not in masks:
            jk = lax.broadcasted_iota(jnp.int32, (nk, lanes), 0)
            iq = lax.broadcasted_iota(jnp.int32, (nk, lanes), 1) & (nq - 1)
            ok = True if bl > 0 else ctx_ok
            lo = iq if ok is True else jnp.maximum(iq, jnp.where(ok, 0, WINDOW))
            masks[bl] = (jk >= lo) & (jk <= iq + WINDOW)
        return masks[bl]

    def keys(bl, g):
        cols = slice(g * pair_w, (g + 1) * pair_w)
        if bl == 0:
            return jnp.concatenate([kp_ref[:, cols], kc_ref[0:WINDOW, cols]], axis=0)
        return kc_ref[(bl - 1) * WINDOW:(bl + 1) * WINDOW, cols]

    def vals_t(bl, g):
        rows = slice(g * HEAD_DIM, (g + 1) * HEAD_DIM)
        if bl == 0:
            return jnp.concatenate([vp_ref[rows, :], vc_ref[rows, 0:WINDOW]], axis=1)
        return vc_ref[rows, (bl - 1) * WINDOW:(bl + 1) * WINDOW]

    def score(bl, g):
        k = bl * N_KV_HEADS + g
        st = lax.dot_general(keys(bl, g), q_ref[k * lanes:(k + 1) * lanes, :], (((1,), (1,)), ((), ())),
                             preferred_element_type=F32)
        st_scr[k % n_slots] = jnp.where(valid(bl), st, NEG_BIG)

    def attend(bl, g):
        st = st_scr[(bl * N_KV_HEADS + g) % n_slots]
        sink = jnp.concatenate(
            [jnp.full((1, nq), sink_ref[g * Q_PER_KV + r] * LOG2_E, F32) for r in range(Q_PER_KV)], axis=1)
        m = jnp.maximum(jnp.max(st, axis=0, keepdims=True), sink)
        p = jnp.exp2(st - m)
        denom = jnp.sum(p, axis=0, keepdims=True) + jnp.exp2(sink - m)
        ot = jnp.dot(vals_t(bl, g), p.astype(BF16), preferred_element_type=F32) * (1.0 / denom)
        for pr in range(Q_PER_KV // 2):
            two = jnp.concatenate([ot[:, (2 * pr) * nq:(2 * pr + 1) * nq],
                                   ot[:, (2 * pr + 1) * nq:(2 * pr + 2) * nq]], axis=0)
            col = (g * Q_PER_KV + 2 * pr) * HEAD_DIM
            o_ref[bl * nq:(bl + 1) * nq, col:col + pair_w] = two.T

    order = [(bl, g) for bl in range(n_blk) for g in range(N_KV_HEADS)]
    return ([functools.partial(score, bl, g) for bl, g in order],
            [functools.partial(attend, bl, g) for bl, g in order])


def _staggered(score, attend, ahead):
    def piece(k):
        def run():
            if k + ahead < len(score):
                score[k + ahead]()
            attend[k]()
        return run

    def first():
        for f in score[:ahead]:
            f()
    return first, [piece(k) for k in range(len(attend))]


def _attn_sample_kernel(sink_ref, q_ref, knt_ref, vnt_ref, ckt_ref, cvt_ref, o_ref, kw_ref, vw_ref, *, n_sub, tn, unroll):
    pair_w = 2 * HEAD_DIM
    n_pairs = N_HEADS // 2
    rows = n_pairs * tn
    kv = N_KV_HEADS * HEAD_DIM
    old = WINDOW - tn
    assert tn & (tn - 1) == 0, "row -> token index uses a power-of-two mask"
    low = lax.broadcasted_iota(jnp.int32, (tn, pair_w), 1) < HEAD_DIM
    tq = lax.broadcasted_iota(jnp.int32, (2 * rows, 2 * WINDOW), 0) & (tn - 1)
    col = lax.broadcasted_iota(jnp.int32, (2 * rows, 2 * WINDOW), 1)
    valid = ((col < WINDOW) & (col >= tq)) | ((col >= WINDOW + old) & (col - WINDOW - old <= tq))
    is_new = lax.broadcasted_iota(jnp.int32, (kv, WINDOW), 1) >= old
    zeros = jnp.zeros((tn, pair_w), F32)
    nt = (((1,), (1,)), ((), ()))
    assert n_sub * tn == WINDOW

    sink = jnp.concatenate([jnp.full((tn, 1), sink_ref[2 * pr + half], F32)
                            for half in range(2) for pr in range(n_pairs)], axis=0)

    def scores(s):
        r0 = pl.multiple_of(s * tn, tn)
        q = q_ref[pl.ds(r0, tn), :]
        kt = ckt_ref[s]
        vt = cvt_ref[s]
        shift = (old - r0) & (WINDOW - 1)
        knt = jnp.where(is_new, pltpu.roll(knt_ref[...], shift, axis=1), 0.0)
        vnt = jnp.where(is_new, pltpu.roll(vnt_ref[...], shift, axis=1), 0.0)
        kw_ref[s] = jnp.where(is_new, knt, pltpu.roll(kt, old, axis=1))
        vw_ref[s] = jnp.where(is_new, vnt, pltpu.roll(vt, old, axis=1))
        k_all = jnp.concatenate([kt, knt], axis=1).astype(BF16)
        v_all = jnp.concatenate([vt, vnt], axis=1).astype(BF16)
        blocks = []
        for half in range(2):
            for g in range(N_KV_HEADS):
                for pp in range(Q_PER_KV // 2):
                    piece = q[:, (2 * g + pp) * pair_w:(2 * g + pp + 1) * pair_w]
                    if half != g % 2:
                        piece = pltpu.roll(piece, HEAD_DIM, axis=1)
                    piece = jnp.where(low if g % 2 == 0 else ~low, piece, 0.0)
                    blocks.append(jnp.concatenate([piece, zeros] if g // 2 == 0 else [zeros, piece], axis=1))
        qh = jnp.concatenate(blocks, axis=0).astype(BF16)
        return r0, v_all, jnp.where(valid, jnp.dot(qh, k_all, preferred_element_type=F32), NEG_BIG)

    def attend(r0, v_all, sc):
        m = jnp.maximum(jnp.max(sc, axis=-1, keepdims=True), sink)
        p = jnp.exp(sc - m)
        rden = 1.0 / (jnp.sum(p, axis=-1, keepdims=True) + jnp.exp(sink - m))
        o = lax.dot_general((p * rden).astype(BF16), v_all, nt, preferred_element_type=F32)
        for pr in range(n_pairs):
            g = pr // (Q_PER_KV // 2)
            sel = []
            for half in range(2):
                blk = o[half * rows + pr * tn:half * rows + (pr + 1) * tn, (g // 2) * pair_w:(g // 2 + 1) * pair_w]
                sel.append(blk if half == g % 2 else pltpu.roll(blk, HEAD_DIM, axis=1))
            o_ref[pl.ds(r0, tn), pr * pair_w:(pr + 1) * pair_w] = jnp.where(low, sel[0], sel[1])

    def some_sequences(i, carry):
        staged = [scores(i * unroll + u) for u in range(unroll)]
        for st in staged:
            attend(*st)
        return carry

    lax.fori_loop(0, n_sub // unroll, some_sequences, 0)


def _attn_sample(sinks, q, knt, vnt, cache_kt, cache_vt, n_seq, tn, n_sub, unroll):
    d = q.shape[1]
    kv = knt.shape[0]
    rows = lambda i: (i, 0)
    seqs = lambda i: (i, 0, 0)
    win = pl.BlockSpec((n_sub, kv, WINDOW), seqs)
    new = pl.BlockSpec((kv, n_sub * tn), lambda i: (0, i))
    return pl.pallas_call(
        functools.partial(_attn_sample_kernel, n_sub=n_sub, tn=tn, unroll=unroll),
        grid=(n_seq // n_sub,),
        in_specs=[pl.BlockSpec(memory_space=pltpu.SMEM),
                  pl.BlockSpec((n_sub * tn, d), rows),
                  new, new, win, win],
        out_specs=[pl.BlockSpec((n_sub * tn, d), rows), win, win],
        out_shape=[jax.ShapeDtypeStruct(q.shape, F32),
                   jax.ShapeDtypeStruct(cache_kt.shape, F32), jax.ShapeDtypeStruct(cache_vt.shape, F32)],
        compiler_params=pltpu.CompilerParams(
            dimension_semantics=("parallel",), vmem_limit_bytes=VMEM_LIMIT),
        name="attn_sample",
    )(sinks, q, knt, vnt, cache_kt, cache_vt)


def _ssm_prep_kernel(lr_ref, li_ref, ldt_ref, br_ref, bi_ref, ar_ref, ai_ref, bbr_ref, bbi_ref):
    lr = lr_ref[...]
    li = li_ref[...]
    dt = jnp.exp(ldt_ref[...])
    mag = jnp.exp(lr * dt)
    ang = li * dt
    abar_re = mag * jnp.cos(ang)
    abar_im = mag * jnp.sin(ang)
    den = lr * lr + li * li
    nr = abar_re - 1.0
    fr = (nr * lr + abar_im * li) / den
    fi = (abar_im * lr - nr * li) / den
    ar_ref[...] = abar_re
    ai_ref[...] = abar_im
    br = br_ref[...]
    bi = bi_ref[...]
    bbr_ref[...] = fr[:, None, :] * br - fi[:, None, :] * bi
    bbi_ref[...] = fr[:, None, :] * bi + fi[:, None, :] * br


def _ssm_prep(lam_re, lam_im, log_dt, b_re_t, b_im_t):
    g, n = lam_re.shape
    return pl.pallas_call(
        _ssm_prep_kernel,
        out_shape=[jax.ShapeDtypeStruct((g, n), F32), jax.ShapeDtypeStruct((g, n), F32),
                   jax.ShapeDtypeStruct(b_re_t.shape, F32), jax.ShapeDtypeStruct(b_im_t.shape, F32)],
        name="ssm_prep",
    )(lam_re, lam_im, log_dt.reshape(g, 1), b_re_t, b_im_t)


def _ssm_kernel(u_ref, h0_ref, a_ref, d_ref, bd_ref, cd_ref, y_ref, hT_ref, xh_scr, u_scr, y_scr,
                *, n_seq, tt, seq_lanes, scan_unroll):
    cw = xh_scr.shape[1] // SSM_CHUNKS
    hw = cw // 2
    uc = u_scr.shape[2]
    ssm_w = SSM_CHUNKS * uc

    @pl.when(pl.program_id(0) == 0)
    def _():
        hT_ref[...] = h0_ref[...]

    for j in range(SSM_CHUNKS):
        for b in range(n_seq):
            if seq_lanes:
                u_b = u_ref[:, b * ssm_w + j * uc:b * ssm_w + (j + 1) * uc]
            else:
                u_b = u_ref[b * tt:(b + 1) * tt, j * uc:(j + 1) * uc]
            u_scr[j, pl.ds(b, tt, stride=n_seq), :] = u_b

    def project_in(j):
        xh_scr[:, j * cw:(j + 1) * cw] = jnp.dot(u_scr[j].astype(BF16), bd_ref[j], preferred_element_type=F32)

    def scan(j):
        re = slice(j * cw, j * cw + hw)
        im = slice(j * cw + hw, (j + 1) * cw)
        ar = a_ref[:, re]
        ai = a_ref[:, im]

        def seq_group(s, carry):
            s8 = pl.multiple_of(s * 8, 8)

            def step(t, h):
                hr, hi = h
                r0 = pl.multiple_of(t * n_seq + s8, 8)
                nr = ar * hr - ai * hi + xh_scr[pl.ds(r0, 8), re]
                ni = ar * hi + ai * hr + xh_scr[pl.ds(r0, 8), im]
                xh_scr[pl.ds(r0, 8), re] = nr
                xh_scr[pl.ds(r0, 8), im] = ni
                return nr, ni

            hr, hi = lax.fori_loop(0, tt, step, (hT_ref[pl.ds(s8, 8), re], hT_ref[pl.ds(s8, 8), im]),
                                   unroll=scan_unroll)
            hT_ref[pl.ds(s8, 8), re] = hr
            hT_ref[pl.ds(s8, 8), im] = hi
            return carry

        lax.fori_loop(0, n_seq // 8, seq_group, 0)

    def project_out(j):
        y = jnp.dot(xh_scr[:, j * cw:(j + 1) * cw].astype(BF16), cd_ref[j], preferred_element_type=F32)
        cs = slice(j * uc, (j + 1) * uc)
        y_scr[j] = jax.nn.gelu(y + d_ref[:, cs] * u_scr[j])
        for b in range(n_seq):
            y_b = y_scr[j, pl.ds(b, tt, stride=n_seq), :].astype(y_ref.dtype)
            if seq_lanes:
                y_ref[:, b * ssm_w + j * uc:b * ssm_w + (j + 1) * uc] = y_b
            else:
                y_ref[b * tt:(b + 1) * tt, cs] = y_b

    for stage in range(SSM_CHUNKS + 2):
        if stage < SSM_CHUNKS:
            project_in(stage)
        if 1 <= stage <= SSM_CHUNKS:
            scan(stage - 1)
        if stage >= 2:
            project_out(stage - 2)


def _ssm(u, h0, a8, d_skip, bd, cd, n_seq, tt, seq_lanes, scan_unroll):
    ssm_w = d_skip.shape[1]
    tile = n_seq * tt
    n_state = h0.shape[1]
    block = (tt, n_seq * ssm_w) if seq_lanes else (tile, ssm_w)
    assert seq_lanes or u.shape[0] == tile, "row order (seq, t) cannot be tiled over time"
    slab = pltpu.VMEM((SSM_CHUNKS, tile, ssm_w // SSM_CHUNKS), F32)
    return pl.pallas_call(
        functools.partial(_ssm_kernel, n_seq=n_seq, tt=tt, seq_lanes=seq_lanes, scan_unroll=scan_unroll),
        grid=(u.shape[0] // block[0],),
        in_specs=[pl.BlockSpec(block, lambda i: (i, 0)),
                  _const_spec(h0.shape), _const_spec(a8.shape), _const_spec(d_skip.shape),
                  _const_spec(bd.shape), _const_spec(cd.shape)],
        out_specs=[pl.BlockSpec(block, lambda i: (i, 0)),
                   pl.BlockSpec(h0.shape, lambda i: (0, 0))],
        out_shape=[jax.ShapeDtypeStruct(u.shape, BF16), jax.ShapeDtypeStruct(h0.shape, F32)],
        scratch_shapes=[pltpu.VMEM((tile, n_state), F32), slab, slab],
        compiler_params=pltpu.CompilerParams(
            dimension_semantics=("arbitrary",), vmem_limit_bytes=VMEM_LIMIT),
        name="ssm",
    )(u, h0, a8, d_skip, bd, cd)


def _out_kernel(*refs, n_sub, tiles_per_seq):
    if tiles_per_seq:
        (sink_ref, x1_ref, ga_ref, gs_ref, y_ref, q0_ref, k0_ref, v0_ref, qn_ref, kp_ref, kn_ref, vp_ref, vn_ref,
         glua_ref, glub_ref, wo_ref, nb_ref, wg_ref, wu_ref, wd_ref, nf_ref, o_ref,
         h_scr, act_scr, at_ref, st_scr) = refs
        step = pl.program_id(0)

        @pl.when(step == 0)
        def _():
            score, attend = _attention_pieces(sink_ref, q0_ref, k0_ref.at[pl.ds(0, WINDOW)], k0_ref,
                                              v0_ref.at[:, pl.ds(0, WINDOW)], v0_ref, st_scr, at_ref, False)
            first, rest = _staggered(score, attend, ATTN_AHEAD)
            first()
            for f in rest:
                f()

        nxt = jnp.minimum(step + 1, pl.num_programs(0) - 1)
        score, attend = _attention_pieces(sink_ref, qn_ref, kp_ref, kn_ref, vp_ref, vn_ref, st_scr, at_ref,
                                          nxt % tiles_per_seq != 0)
        first, fillers = _staggered(score, attend, ATTN_AHEAD)
    else:
        (x1_ref, at_ref, ga_ref, gs_ref, y_ref, glua_ref, glub_ref, wo_ref,
         nb_ref, wg_ref, wu_ref, wd_ref, nf_ref, o_ref, h_scr, act_scr) = refs
        first, fillers = (lambda: None), ()
    rows = _row_slices(x1_ref.shape[0], n_sub)
    ssm = [jnp.dot(y_ref[r, :], glua_ref[...], preferred_element_type=F32) * jax.nn.sigmoid(
        jnp.dot(y_ref[r, :], glub_ref[...], preferred_element_type=F32)) for r in rows]
    merged = [(jax.nn.sigmoid(ga_ref[r, :]) * at_ref[r, :] + jax.nn.sigmoid(gs_ref[r, :]) * s).astype(BF16)
              for r, s in zip(rows, ssm)]
    x2 = [x1_ref[r, :] + jnp.dot(m, wo_ref[...], preferred_element_type=F32) for r, m in zip(rows, merged)]
    first()
    for r, x in zip(rows, x2):
        h_scr[r, :] = _rms(x, nb_ref[...]).astype(BF16)
    for r, x, f in zip(rows, x2, _swiglu(h_scr, wg_ref, wu_ref, wd_ref, act_scr, rows, fillers)):
        o_ref[r, :] = _rms(x + 0.5 * f, nf_ref[...])


def _out(grid, tm, n_sub, std, ymap, x1, attn, ga, gs, y_tb, glua, glub, wo, nb, wg, wu, wd, nf):
    d = nb.shape[1]
    d_ff = wg.shape[1]
    kv = N_KV_HEADS * HEAD_DIM
    shp, spec = std(d)
    weights = [glua, glub, wo, nb, wg, wu, wd, nf]
    scratch = [pltpu.VMEM((tm, d), BF16), pltpu.VMEM((tm, d_ff), BF16)]
    fused = isinstance(attn, tuple)
    if fused:
        sinks, qst, k2, vt, tiles_per_seq = attn
        (n_tiles,) = grid
        blk = tm // WINDOW
        nxt = lambda s: jnp.minimum(s + 1, n_tiles - 1)
        before = lambda s: jnp.maximum(nxt(s) * blk - 1, 0)
        once = dict(pipeline_mode=pl.Buffered(1))
        acts = [sinks, x1, ga, gs, y_tb, qst, k2, vt, qst, k2, k2, vt, vt]
        act_specs = [pl.BlockSpec(memory_space=pltpu.SMEM), spec, spec, spec, pl.BlockSpec((tm, glua.shape[0]), ymap),
                     pl.BlockSpec((tm * N_HEADS, 2 * HEAD_DIM), lambda s: (0, 0), **once),
                     pl.BlockSpec((tm, 2 * kv), lambda s: (0, 0), **once),
                     pl.BlockSpec((kv, tm), lambda s: (0, 0), **once),
                     pl.BlockSpec((tm * N_HEADS, 2 * HEAD_DIM), lambda s: (nxt(s), 0)),
                     pl.BlockSpec((WINDOW, 2 * kv), lambda s: (before(s), 0)),
                     pl.BlockSpec((tm, 2 * kv), lambda s: (nxt(s), 0)),
                     pl.BlockSpec((kv, WINDOW), lambda s: (0, before(s))),
                     pl.BlockSpec((kv, tm), lambda s: (0, nxt(s)))]
        scratch += [pltpu.VMEM((tm, d), F32), pltpu.VMEM((ATTN_SLOTS, 2 * WINDOW, Q_PER_KV * WINDOW), F32)]
    else:
        tiles_per_seq = 0
        acts = [x1, attn, ga, gs, y_tb]
        act_specs = [spec, spec, spec, spec, pl.BlockSpec((tm, glua.shape[0]), ymap)]
    return pl.pallas_call(
        functools.partial(_out_kernel, n_sub=n_sub, tiles_per_seq=tiles_per_seq),
        grid=grid,
        in_specs=act_specs + [_const_spec(w.shape) for w in weights],
        out_specs=spec,
        out_shape=jax.ShapeDtypeStruct(shp, F32),
        scratch_shapes=scratch,
        compiler_params=pltpu.CompilerParams(
            dimension_semantics=("arbitrary",) * len(grid), vmem_limit_bytes=VMEM_LIMIT),
        name="out",
    )(*acts, *weights)


def _block_diag(w, chunks):
    g, a, b = w.shape
    gl = g // chunks
    w = w.reshape(chunks, gl, a, b)
    eye = jnp.eye(gl, dtype=w.dtype)
    return (w[:, :, :, None, :] * eye[None, :, None, :, None]).reshape(chunks, gl * a, gl * b)


def _state_to_lanes(re, im):
    s = re.shape[0]
    re = re.reshape(s, SSM_CHUNKS, -1)
    im = im.reshape(s, SSM_CHUNKS, -1)
    return jnp.stack([re, im], axis=2).reshape(s, -1)


def _lanes_to_state(h, groups):
    s = h.shape[0]
    h = h.reshape(s, SSM_CHUNKS, 2, -1)
    return h[:, :, 0].reshape(s, groups, -1), h[:, :, 1].reshape(s, groups, -1)


def kernel(x_prompt, x_sample, cache_k_win, cache_v_win, state_ssm_re, state_ssm_im, ffn_a_norm, ffn_a_gate, ffn_a_up, ffn_a_down, mix_norm, w_in, attn_sinks, ssm_lambda_re, ssm_lambda_im, ssm_log_dt, ssm_b_re, ssm_b_im, ssm_c_re, ssm_c_im, ssm_d, glu_a, glu_b, w_out, ffn_b_norm, ffn_b_gate, ffn_b_up, ffn_b_down, final_norm):
    depth = ffn_a_norm.shape[0]
    assert depth == 1, "single-layer trunk"
    n_p, seq, d = x_prompt.shape
    n_s, dec, _ = x_sample.shape
    ssm_w = ssm_d.shape[1]
    groups = ssm_lambda_re.shape[1]
    kvw = N_KV_HEADS * HEAD_DIM
    assert cache_k_win.shape[2] == WINDOW and seq % WINDOW == 0 and n_p % 8 == 0 and n_s % 8 == 0

    l = 0
    na, nm, nb = (w[l].reshape(1, d) for w in (ffn_a_norm, mix_norm, ffn_b_norm))
    nf = final_norm.reshape(1, d)
    wga, wua, wda, win = (w[l].astype(BF16) for w in (ffn_a_gate, ffn_a_up, ffn_a_down, w_in))
    wgb, wub, wdb = (w[l].astype(BF16) for w in (ffn_b_gate, ffn_b_up, ffn_b_down))
    glua, glub, wo = (w[l].astype(BF16) for w in (glu_a, glu_b, w_out))
    sinks = attn_sinks[l]
    d_skip = ssm_d[l].reshape(1, ssm_w)

    abar_re, abar_im, bb_re_t, bb_im_t = _ssm_prep(
        ssm_lambda_re[l], ssm_lambda_im[l], ssm_log_dt[l],
        jnp.swapaxes(ssm_b_re[l], 1, 2), jnp.swapaxes(ssm_b_im[l], 1, 2))
    bd = jnp.concatenate([_block_diag(bb_re_t, SSM_CHUNKS), _block_diag(bb_im_t, SSM_CHUNKS)],
                         axis=2).astype(BF16)
    c_re_t = jnp.swapaxes(ssm_c_re[l], 1, 2)
    c_im_t = jnp.swapaxes(ssm_c_im[l], 1, 2)
    cd = jnp.concatenate([_block_diag(c_re_t, SSM_CHUNKS), -_block_diag(c_im_t, SSM_CHUNKS)],
                         axis=1).astype(BF16)
    a8 = jnp.broadcast_to(_state_to_lanes(abar_re[None], abar_im[None]), (8, 2 * groups * SSM_STATE))

    def run_group(x, n_seq, tiling, out_tiling, u_shape, attn_w, stacked, attn_fn, h0, tt):
        grid, tm, std, umap, n_sub = tiling
        x1, q, k, v, u_tb, ga, gs, *more = _ffn_in(x, grid, tm, n_sub, std, umap, u_shape,
                                                    na, wga, wua, wda, nm, win, attn_w, stacked)
        attn = attn_fn(q, k, v, *more)
        y_tb, h_t = _ssm(u_tb, h0, a8, d_skip, bd, cd, n_seq, tt,
                         seq_lanes=u_shape[1] != ssm_w, scan_unroll=tt)
        grid, tm, std, umap, n_sub = out_tiling
        y = _out(grid, tm, n_sub, std, umap, x1, attn, ga, gs, y_tb, glua, glub, wo, nb, wgb, wub, wdb, nf)
        return y, k, v, h_t

    def tiling_p(tm):
        nt = seq // tm

        def std(width):
            return (n_p * seq, width), pl.BlockSpec((tm, width), lambda b, i: (b * nt + i, 0))
        return (n_p, nt), tm, std, lambda b, i: (i, b), tm // 256

    tm_p = 512
    nt = seq // tm_p

    def attn_p(qst, k, v, k2, vt):
        return sinks, qst, k2, vt, nt

    w_k = w_in[l][:, d:d + kvw]
    wk2 = jnp.broadcast_to(w_k.reshape(d, N_KV_HEADS, 1, HEAD_DIM),
                           (d, N_KV_HEADS, 2, HEAD_DIM)).reshape(d, 2 * kvw).astype(BF16)
    wkt = w_k.T.astype(BF16)
    wvt = w_in[l][:, d + kvw:d + 2 * kvw].T.astype(BF16)

    h0_p = jnp.zeros((n_p, 2 * groups * SSM_STATE), F32)
    def std_flat(width):
        return (n_p * seq, width), pl.BlockSpec((tm_p, width), lambda s: (s, 0))

    out_tiling_p = ((n_p * nt,), tm_p, std_flat, lambda s: (s % nt, s // nt), tm_p // 256)
    y_p, k_p, v_p, h_p = run_group(
        x_prompt.reshape(n_p * seq, d), n_p, tiling_p(tm_p), out_tiling_p,
        (seq, n_p * ssm_w), (wk2, wvt, lambda b, i: (b * nt + i, 0)), True, attn_p, h0_p, 1024 // n_p)

    tm_s = 512

    def std_s(width):
        return (n_s * dec, width), pl.BlockSpec((tm_s, width), lambda i, j: (i, 0))

    def window_t(c):
        return jnp.transpose(c, (0, 2, 3, 1)).reshape(c.shape[0], kvw, WINDOW)

    def window(ct):
        return jnp.transpose(ct.reshape(ct.shape[0], N_KV_HEADS, HEAD_DIM, WINDOW), (0, 3, 1, 2))[None]

    new_windows = []

    def attn_s(q, kt, vt):
        o, kw, vw = _attn_sample(sinks, q, kt, vt, window_t(cache_k_win[l]), window_t(cache_v_win[l]),
                                 n_s, dec, WINDOW // dec, 4)
        new_windows.extend([kw, vw])
        return o

    h0_s = _state_to_lanes(state_ssm_re[l], state_ssm_im[l])
    tiling_s = ((n_s * dec // tm_s, 1), tm_s, std_s, lambda i, j: (i, 0), tm_s // 256)
    y_s, _, _, h_s = run_group(
        x_sample.reshape(n_s * dec, d), n_s, tiling_s, tiling_s,
        (n_s * dec, ssm_w), (wkt, wvt, lambda i, j: (i, 0)), False, attn_s, h0_s, dec)

    sp_re, sp_im = _lanes_to_state(h_p, groups)
    ss_re, ss_im = _lanes_to_state(h_s, groups)
    return (y_p.reshape(n_p, seq, d), y_s.reshape(n_s, dec, d),
            window(k_p), window(v_p), window(new_windows[0]), window(new_windows[1]),
            sp_re[None], sp_im[None], ss_re[None], ss_im[None])
```

```python
import functools

import jax
import jax.numpy as jnp
from jax import lax
from jax.experimental import pallas as pl
from jax.experimental.pallas import tpu as pltpu

F32 = jnp.float32
BF16 = jnp.bfloat16

N_HEADS = 16
N_KV_HEADS = 4
HEAD_DIM = 64
Q_PER_KV = N_HEADS // N_KV_HEADS
WINDOW = 128
SSM_STATE = 64
RMS_EPS = 1e-6
NEG_BIG = -1e30
LOG2_E = 1.4426950408889634

MXU_COLS = 256
WINDOW_COLS = N_KV_HEADS * HEAD_DIM
ATTN_AHEAD = 2
ATTN_SLOTS = 4
SSM_CHUNKS = 4
VMEM_LIMIT = 62 * 1024 * 1024


def _rms(x, g):
    return x * lax.rsqrt(jnp.mean(x * x, axis=-1, keepdims=True) + RMS_EPS) * g


def _row_slices(tm, n_sub):
    rs = tm // n_sub
    return [slice(r * rs, (r + 1) * rs) for r in range(n_sub)]


def _swiglu(h_scr, wg_ref, wu_ref, wd_ref, act_scr, rows=(slice(None),), fillers=()):
    d_ff = wg_ref.shape[1]
    fillers = list(fillers)
    for c in range(d_ff // MXU_COLS):
        sl = slice(c * MXU_COLS, (c + 1) * MXU_COLS)
        for r in rows:
            g = jnp.dot(h_scr[r, :], wg_ref[:, sl], preferred_element_type=F32)
            u = jnp.dot(h_scr[r, :], wu_ref[:, sl], preferred_element_type=F32)
            act_scr[r, sl] = (jax.nn.silu(g) * u).astype(BF16)
            if fillers:
                fillers.pop(0)()
    for f in fillers:
        f()
    return [jnp.dot(act_scr[r, :], wd_ref[...], preferred_element_type=F32) for r in rows]


def _ffn_in_kernel(*refs, stacked, n_sub):
    if stacked:
        (x_ref, na_ref, wg_ref, wu_ref, wd_ref, nm_ref, win_ref, wk2_ref, wvt_ref,
         x1_ref, q_ref, k_ref, v_ref, u_ref, ga_ref, gs_ref, k2_ref, vt_ref, h_scr, act_scr) = refs
        assert WINDOW_COLS == k_ref.shape[0]
    else:
        (x_ref, na_ref, wg_ref, wu_ref, wd_ref, nm_ref, win_ref, wkt_ref, wvt_ref,
         x1_ref, q_ref, k_ref, v_ref, u_ref, ga_ref, gs_ref, h_scr, act_scr) = refs
    tm, d = x_ref.shape
    rows = _row_slices(tm, n_sub)
    rs = tm // n_sub
    for r in rows:
        h_scr[r, :] = _rms(x_ref[r, :], na_ref[...]).astype(BF16)
    x1 = [x_ref[r, :] + 0.5 * f for r, f in zip(rows, _swiglu(h_scr, wg_ref, wu_ref, wd_ref, act_scr, rows))]
    for r, x in zip(rows, x1):
        x1_ref[r, :] = x
        h_scr[r, :] = _rms(x, nm_ref[...]).astype(BF16)

    def proj(w_ref, off, c, r):
        return jnp.dot(h_scr[r, :], w_ref[:, off + c * MXU_COLS: off + (c + 1) * MXU_COLS],
                       preferred_element_type=F32)

    q_scale = HEAD_DIM ** -0.5 * (LOG2_E if stacked else 1.0)
    pair_w = 2 * HEAD_DIM
    if stacked:
        assert Q_PER_KV * HEAD_DIM == MXU_COLS
        low_half = lax.broadcasted_iota(jnp.int32, (rs, MXU_COLS), 1) % pair_w < HEAD_DIM
        for g in range(N_KV_HEADS):
            for ri, r in enumerate(rows):
                res = proj(win_ref, 0, g, r) * q_scale
                halves = (jnp.where(low_half, res, 0.0).astype(BF16), jnp.where(low_half, 0.0, res).astype(BF16))
                for bl in range(rs // WINDOW):
                    for rr in range(Q_PER_KV):
                        row = (((ri * (rs // WINDOW) + bl) * N_KV_HEADS + g) * Q_PER_KV + rr) * WINDOW
                        q_ref[row:row + WINDOW, :] = halves[rr % 2][bl * WINDOW:(bl + 1) * WINDOW,
                                                                   (rr // 2) * pair_w:(rr // 2 + 1) * pair_w]
    else:
        for c in range(d // MXU_COLS):
            for r in rows:
                q_ref[r, c * MXU_COLS:(c + 1) * MXU_COLS] = (proj(win_ref, 0, c, r) * q_scale).astype(q_ref.dtype)
        for ref, wt_ref in ((k_ref, wkt_ref), (v_ref, wvt_ref)):
            for r in rows:
                ref[:, r] = lax.dot_general(wt_ref[...], h_scr[r, :], (((1,), (1,)), ((), ())),
                                            preferred_element_type=F32)
    off = d + 2 * WINDOW_COLS
    for ref in (u_ref, ga_ref, gs_ref):
        width = ref.shape[1]
        for c in range(width // MXU_COLS):
            for r in rows:
                ref[r, c * MXU_COLS:(c + 1) * MXU_COLS] = proj(win_ref, off, c, r).astype(ref.dtype)
        off += width
    if stacked:
        for c in range(k2_ref.shape[1] // MXU_COLS):
            for r in rows:
                k2_ref[r, c * MXU_COLS:(c + 1) * MXU_COLS] = proj(wk2_ref, 0, c, r).astype(k2_ref.dtype)
        for r in rows:
            vt_ref[:, r] = lax.dot_general(wvt_ref[...], h_scr[r, :], (((1,), (1,)), ((), ())),
                                           preferred_element_type=F32).astype(vt_ref.dtype)

        @pl.when(pl.program_id(1) == pl.num_programs(1) - 1)
        def _():
            tail = h_scr[tm - WINDOW:, :]
            for ref, off in ((k_ref, d), (v_ref, d + WINDOW_COLS)):
                ref[...] = jnp.dot(tail, win_ref[:, off:off + WINDOW_COLS], preferred_element_type=F32).T


def _const_spec(shape):
    nd = len(shape)
    return pl.BlockSpec(shape, lambda *_: (0,) * nd, pipeline_mode=pl.Buffered(1))


def _ffn_in(x2d, grid, tm, n_sub, std, umap, u_shape, na, wg, wu, wd, nm, win, attn_w, stacked):
    d = na.shape[1]
    d_ff = wg.shape[1]
    kv = N_KV_HEADS * HEAD_DIM
    ssm_w = win.shape[1] - 3 * d - 2 * kv
    n_tok = x2d.size // d
    w_a, w_b, tile_map = attn_w
    t_spec = pl.BlockSpec((kv, tm), lambda *g: tile_map(*g)[::-1])
    out_shape, out_specs = [], []

    def add(shape_spec, dtype):
        out_shape.append(jax.ShapeDtypeStruct(shape_spec[0], dtype))
        out_specs.append(shape_spec[1])

    add(std(d), F32)
    if stacked:
        add(((n_tok * N_HEADS, 2 * HEAD_DIM), pl.BlockSpec((tm * N_HEADS, 2 * HEAD_DIM), tile_map)), BF16)
        for _ in range(2):
            add(((grid[0], kv, WINDOW), pl.BlockSpec((None, kv, WINDOW), lambda b, i: (b, 0, 0))), F32)
    else:
        add(std(d), F32)
        add(((kv, n_tok), t_spec), F32)
        add(((kv, n_tok), t_spec), F32)
    add((u_shape, pl.BlockSpec((tm, ssm_w), umap)), F32)
    add(std(d), F32)
    add(std(d), F32)
    weights = [na, wg, wu, wd, nm, win, w_a, w_b]
    if stacked:
        add(std(2 * kv), BF16)
        add(((kv, n_tok), t_spec), BF16)
    return pl.pallas_call(
        functools.partial(_ffn_in_kernel, stacked=stacked, n_sub=n_sub),
        grid=grid,
        in_specs=[std(d)[1]] + [_const_spec(w.shape) for w in weights],
        out_specs=out_specs,
        out_shape=out_shape,
        scratch_shapes=[pltpu.VMEM((tm, d), BF16), pltpu.VMEM((tm, d_ff), BF16)],
        compiler_params=pltpu.CompilerParams(
            dimension_semantics=("parallel", "arbitrary"), vmem_limit_bytes=VMEM_LIMIT),
        name="ffn_in",
    )(x2d, *weights)


def _attention_pieces(sink_ref, q_ref, kp_ref, kc_ref, vp_ref, vc_ref, st_scr, o_ref, ctx_ok):
    nq = WINDOW
    nk = 2 * WINDOW
    lanes = Q_PER_KV * nq
    pair_w = 2 * HEAD_DIM
    n_blk = kc_ref.shape[0] // WINDOW
    n_slots = st_scr.shape[0]
    masks = {}

    def valid(bl):
        if bl not in masks:
            jk = lax.broadcasted_iota(jnp.int32, (nk, lanes), 0)
            iq = lax.broadcasted_iota(jnp.int32, (nk, lanes), 1) & (nq - 1)
            ok = True if bl > 0 else ctx_ok
            lo = iq if ok is True else jnp.maximum(iq, jnp.where(ok, 0, WINDOW))
            masks[bl] = (jk >= lo) & (jk <= iq + WINDOW)
        return masks[bl]

    def keys(bl, g):
        cols = slice(g * pair_w, (g + 1) * pair_w)
        if bl == 0:
            return jnp.concatenate([kp_ref[:, cols], kc_ref[0:WINDOW, cols]], axis=0)
        return kc_ref[(bl - 1) * WINDOW:(bl + 1) * WINDOW, cols]

    def vals_t(bl, g):
        rows = slice(g * HEAD_DIM, (g + 1) * HEAD_DIM)
        if bl == 0:
            return jnp.concatenate([vp_ref[rows, :], vc_ref[rows, 0:WINDOW]], axis=1)
        return vc_ref[rows, (bl - 1) * WINDOW:(bl + 1) * WINDOW]

    def score(bl, g):
        k = bl * N_KV_HEADS + g
        st = lax.dot_general(keys(bl, g), q_ref[k * lanes:(k + 1) * lanes, :], (((1,), (1,)), ((), ())),
                             preferred_element_type=F32)
        st_scr[k % n_slots] = jnp.where(valid(bl), st, NEG_BIG)

    def attend(bl, g):
        st = st_scr[(bl * N_KV_HEADS + g) % n_slots]
        sink = jnp.concatenate(
            [jnp.full((1, nq), sink_ref[g * Q_PER_KV + r] * LOG2_E, F32) for r in range(Q_PER_KV)], axis=1)
        m = jnp.maximum(jnp.max(st, axis=0, keepdims=True), sink)
        p = jnp.exp2(st - m)
        denom = jnp.sum(p, axis=0, keepdims=True) + jnp.exp2(sink - m)
        ot = jnp.dot(vals_t(bl, g), p.astype(BF16), preferred_element_type=F32) * (1.0 / denom)
        for pr in range(Q_PER_KV // 2):
            two = jnp.concatenate([ot[:, (2 * pr) * nq:(2 * pr + 1) * nq],
                                   ot[:, (2 * pr + 1) * nq:(2 * pr + 2) * nq]], axis=0)
            col = (g * Q_PER_KV + 2 * pr) * HEAD_DIM
            o_ref[bl * nq:(bl + 1) * nq, col:col + pair_w] = two.T

    order = [(bl, g) for bl in range(n_blk) for g in range(N_KV_HEADS)]
    return ([functools.partial(score, bl, g) for bl, g in order],
            [functools.partial(attend, bl, g) for bl, g in order])


def _staggered(score, attend, ahead):
    def piece(k):
        def run():
            if k + ahead < len(score):
                score[k + ahead]()
            attend[k]()
        return run

    def first():
        for f in score[:ahead]:
            f()
    return first, [piece(k) for k in range(len(attend))]


def _attn_sample_kernel(sink_ref, q_ref, knt_ref, vnt_ref, ckt_ref, cvt_ref, o_ref, kw_ref, vw_ref, *, n_sub, tn, unroll):
    pair_w = 2 * HEAD_DIM
    n_pairs = N_HEADS // 2
    rows = n_pairs * tn
    kv = N_KV_HEADS * HEAD_DIM
    old = WINDOW - tn
    assert tn & (tn - 1) == 0, "row -> token index uses a power-of-two mask"
    low = lax.broadcasted_iota(jnp.int32, (tn, pair_w), 1) < HEAD_DIM
    tq = lax.broadcasted_iota(jnp.int32, (2 * rows, 2 * WINDOW), 0) & (tn - 1)
    col = lax.broadcasted_iota(jnp.int32, (2 * rows, 2 * WINDOW), 1)
    valid = ((col < WINDOW) & (col >= tq)) | ((col >= WINDOW + old) & (col - WINDOW - old <= tq))
    is_new = lax.broadcasted_iota(jnp.int32, (kv, WINDOW), 1) >= old
    zeros = jnp.zeros((tn, pair_w), F32)
    nt = (((1,), (1,)), ((), ()))
    assert n_sub * tn == WINDOW

    sink = jnp.concatenate([jnp.full((tn, 1), sink_ref[2 * pr + half], F32)
                            for half in range(2) for pr in range(n_pairs)], axis=0)

    def scores(s):
        r0 = pl.multiple_of(s * tn, tn)
        q = q_ref[pl.ds(r0, tn), :]
        kt = ckt_ref[s]
        vt = cvt_ref[s]
        shift = (old - r0) & (WINDOW - 1)
        knt = jnp.where(is_new, pltpu.roll(knt_ref[...], shift, axis=1), 0.0)
        vnt = jnp.where(is_new, pltpu.roll(vnt_ref[...], shift, axis=1), 0.0)
        kw_ref[s] = jnp.where(is_new, knt, pltpu.roll(kt, old, axis=1))
        vw_ref[s] = jnp.where(is_new, vnt, pltpu.roll(vt, old, axis=1))
        k_all = jnp.concatenate([kt, knt], axis=1).astype(BF16)
        v_all = jnp.concatenate([vt, vnt], axis=1).astype(BF16)
        blocks = []
        for half in range(2):
            for g in range(N_KV_HEADS):
                for pp in range(Q_PER_KV // 2):
                    piece = q[:, (2 * g + pp) * pair_w:(2 * g + pp + 1) * pair_w]
                    if half != g % 2:
                        piece = pltpu.roll(piece, HEAD_DIM, axis=1)
                    piece = jnp.where(low if g % 2 == 0 else ~low, piece, 0.0)
                    blocks.append(jnp.concatenate([piece, zeros] if g // 2 == 0 else [zeros, piece], axis=1))
        qh = jnp.concatenate(blocks, axis=0).astype(BF16)
        return r0, v_all, jnp.where(valid, jnp.dot(qh, k_all, preferred_element_type=F32), NEG_BIG)

    def attend(r0, v_all, sc):
        m = jnp.maximum(jnp.max(sc, axis=-1, keepdims=True), sink)
        p = jnp.exp(sc - m)
        rden = 1.0 / (jnp.sum(p, axis=-1, keepdims=True) + jnp.exp(sink - m))
        o = lax.dot_general((p * rden).astype(BF16), v_all, nt, preferred_element_type=F32)
        for pr in range(n_pairs):
            g = pr // (Q_PER_KV // 2)
            sel = []
            for half in range(2):
                blk = o[half * rows + pr * tn:half * rows + (pr + 1) * tn, (g // 2) * pair_w:(g // 2 + 1) * pair_w]
                sel.append(blk if half == g % 2 else pltpu.roll(blk, HEAD_DIM, axis=1))
            o_ref[pl.ds(r0, tn), pr * pair_w:(pr + 1) * pair_w] = jnp.where(low, sel[0], sel[1])

    def some_sequences(i, carry):
        staged = [scores(i * unroll + u) for u in range(unroll)]
        for st in staged:
            attend(*st)
        return carry

    lax.fori_loop(0, n_sub // unroll, some_sequences, 0)


def _attn_sample(sinks, q, knt, vnt, cache_kt, cache_vt, n_seq, tn, n_sub, unroll):
    d = q.shape[1]
    kv = knt.shape[0]
    rows = lambda i: (i, 0)
    seqs = lambda i: (i, 0, 0)
    win = pl.BlockSpec((n_sub, kv, WINDOW), seqs)
    new = pl.BlockSpec((kv, n_sub * tn), lambda i: (0, i))
    return pl.pallas_call(
        functools.partial(_attn_sample_kernel, n_sub=n_sub, tn=tn, unroll=unroll),
        grid=(n_seq // n_sub,),
        in_specs=[pl.BlockSpec(memory_space=pltpu.SMEM),
                  pl.BlockSpec((n_sub * tn, d), rows),
                  new, new, win, win],
        out_specs=[pl.BlockSpec((n_sub * tn, d), rows), win, win],
        out_shape=[jax.ShapeDtypeStruct(q.shape, F32),
                   jax.ShapeDtypeStruct(cache_kt.shape, F32), jax.ShapeDtypeStruct(cache_vt.shape, F32)],
        compiler_params=pltpu.CompilerParams(
            dimension_semantics=("parallel",), vmem_limit_bytes=VMEM_LIMIT),
        name="attn_sample",
    )(sinks, q, knt, vnt, cache_kt, cache_vt)


def _ssm_prep_kernel(lr_ref, li_ref, ldt_ref, br_ref, bi_ref, ar_ref, ai_ref, bbr_ref, bbi_ref):
    lr = lr_ref[...]
    li = li_ref[...]
    dt = jnp.exp(ldt_ref[...])
    mag = jnp.exp(lr * dt)
    ang = li * dt
    abar_re = mag * jnp.cos(ang)
    abar_im = mag * jnp.sin(ang)
    den = lr * lr + li * li
    nr = abar_re - 1.0
    fr = (nr * lr + abar_im * li) / den
    fi = (abar_im * lr - nr * li) / den
    ar_ref[...] = abar_re
    ai_ref[...] = abar_im
    br = br_ref[...]
    bi = bi_ref[...]
    bbr_ref[...] = fr[:, None, :] * br - fi[:, None, :] * bi
    bbi_ref[...] = fr[:, None, :] * bi + fi[:, None, :] * br


def _ssm_prep(lam_re, lam_im, log_dt, b_re_t, b_im_t):
    g, n = lam_re.shape
    return pl.pallas_call(
        _ssm_prep_kernel,
        out_shape=[jax.ShapeDtypeStruct((g, n), F32), jax.ShapeDtypeStruct((g, n), F32),
                   jax.ShapeDtypeStruct(b_re_t.shape, F32), jax.ShapeDtypeStruct(b_im_t.shape, F32)],
        name="ssm_prep",
    )(lam_re, lam_im, log_dt.reshape(g, 1), b_re_t, b_im_t)


def _ssm_kernel(u_ref, h0_ref, a_ref, d_ref, bd_ref, cd_ref, y_ref, hT_ref, xh_scr, u_scr, y_scr,
                *, n_seq, tt, seq_lanes, scan_unroll):
    cw = xh_scr.shape[1] // SSM_CHUNKS
    hw = cw // 2
    uc = u_scr.shape[2]
    ssm_w = SSM_CHUNKS * uc

    @pl.when(pl.program_id(0) == 0)
    def _():
        hT_ref[...] = h0_ref[...]

    for j in range(SSM_CHUNKS):
        for b in range(n_seq):
            if seq_lanes:
                u_b = u_ref[:, b * ssm_w + j * uc:b * ssm_w + (j + 1) * uc]
            else:
                u_b = u_ref[b * tt:(b + 1) * tt, j * uc:(j + 1) * uc]
            u_scr[j, pl.ds(b, tt, stride=n_seq), :] = u_b

    def project_in(j):
        xh_scr[:, j * cw:(j + 1) * cw] = jnp.dot(u_scr[j].astype(BF16), bd_ref[j], preferred_element_type=F32)

    def scan(j):
        re = slice(j * cw, j * cw + hw)
        im = slice(j * cw + hw, (j + 1) * cw)
        ar = a_ref[:, re]
        ai = a_ref[:, im]

        def seq_group(s, carry):
            s8 = pl.multiple_of(s * 8, 8)

            def step(t, h):
                hr, hi = h
                r0 = pl.multiple_of(t * n_seq + s8, 8)
                nr = ar * hr - ai * hi + xh_scr[pl.ds(r0, 8), re]
                ni = ar * hi + ai * hr + xh_scr[pl.ds(r0, 8), im]
                xh_scr[pl.ds(r0, 8), re] = nr
                xh_scr[pl.ds(r0, 8), im] = ni
                return nr, ni

            hr, hi = lax.fori_loop(0, tt, step, (hT_ref[pl.ds(s8, 8), re], hT_ref[pl.ds(s8, 8), im]),
                                   unroll=scan_unroll)
            hT_ref[pl.ds(s8, 8), re] = hr
            hT_ref[pl.ds(s8, 8), im] = hi
            return carry

        lax.fori_loop(0, n_seq // 8, seq_group, 0)

    def project_out(j):
        y = jnp.dot(xh_scr[:, j * cw:(j + 1) * cw].astype(BF16), cd_ref[j], preferred_element_type=F32)
        cs = slice(j * uc, (j + 1) * uc)
        y_scr[j] = jax.nn.gelu(y + d_ref[:, cs] * u_scr[j])
        for b in range(n_seq):
            y_b = y_scr[j, pl.ds(b, tt, stride=n_seq), :].astype(y_ref.dtype)
            if seq_lanes:
                y_ref[:, b * ssm_w + j * uc:b * ssm_w + (j + 1) * uc] = y_b
            else:
                y_ref[b * tt:(b + 1) * tt, cs] = y_b

    for stage in range(SSM_CHUNKS + 2):
        if stage < SSM_CHUNKS:
            project_in(stage)
        if 1 <= stage <= SSM_CHUNKS:
            scan(stage - 1)
        if stage >= 2:
            project_out(stage - 2)


def _ssm(u, h0, a8, d_skip, bd, cd, n_seq, tt, seq_lanes, scan_unroll):
    ssm_w = d_skip.shape[1]
    tile = n_seq * tt
    n_state = h0.shape[1]
    block = (tt, n_seq * ssm_w) if seq_lanes else (tile, ssm_w)
    assert seq_lanes or u.shape[0] == tile, "row order (seq, t) cannot be tiled over time"
    slab = pltpu.VMEM((SSM_CHUNKS, tile, ssm_w // SSM_CHUNKS), F32)
    return pl.pallas_call(
        functools.partial(_ssm_kernel, n_seq=n_seq, tt=tt, seq_lanes=seq_lanes, scan_unroll=scan_unroll),
        grid=(u.shape[0] // block[0],),
        in_specs=[pl.BlockSpec(block, lambda i: (i, 0)),
                  _const_spec(h0.shape), _const_spec(a8.shape), _const_spec(d_skip.shape),
                  _const_spec(bd.shape), _const_spec(cd.shape)],
        out_specs=[pl.BlockSpec(block, lambda i: (i, 0)),
                   pl.BlockSpec(h0.shape, lambda i: (0, 0))],
        out_shape=[jax.ShapeDtypeStruct(u.shape, BF16), jax.ShapeDtypeStruct(h0.shape, F32)],
        scratch_shapes=[pltpu.VMEM((tile, n_state), F32), slab, slab],
        compiler_params=pltpu.CompilerParams(
            dimension_semantics=("arbitrary",), vmem_limit_bytes=VMEM_LIMIT),
        name="ssm",
    )(u, h0, a8, d_skip, bd, cd)


def _out_kernel(*refs, n_sub, tiles_per_seq):
    if tiles_per_seq:
        (sink_ref, x1_ref, ga_ref, gs_ref, y_ref, q0_ref, k0_ref, v0_ref, qn_ref, kp_ref, kn_ref, vp_ref, vn_ref,
         glua_ref, glub_ref, wo_ref, nb_ref, wg_ref, wu_ref, wd_ref, nf_ref, o_ref,
         h_scr, act_scr, at_ref, st_scr) = refs
        step = pl.program_id(0)

        @pl.when(step == 0)
        def _():
            score, attend = _attention_pieces(sink_ref, q0_ref, k0_ref.at[pl.ds(0, WINDOW)], k0_ref,
                                              v0_ref.at[:, pl.ds(0, WINDOW)], v0_ref, st_scr, at_ref, False)
            first, rest = _staggered(score, attend, ATTN_AHEAD)
            first()
            for f in rest:
                f()

        nxt = jnp.minimum(step + 1, pl.num_programs(0) - 1)
        score, attend = _attention_pieces(sink_ref, qn_ref, kp_ref, kn_ref, vp_ref, vn_ref, st_scr, at_ref,
                                          nxt % tiles_per_seq != 0)
        first, fillers = _staggered(score, attend, ATTN_AHEAD)
    else:
        (x1_ref, at_ref, ga_ref, gs_ref, y_ref, glua_ref, glub_ref, wo_ref,
         nb_ref, wg_ref, wu_ref, wd_ref, nf_ref, o_ref, h_scr, act_scr) = refs
        first, fillers = (lambda: None), ()
    rows = _row_slices(x1_ref.shape[0], n_sub)
    ssm = [jnp.dot(y_ref[r, :], glua_ref[...], preferred_element_type=F32) * jax.nn.sigmoid(
        jnp.dot(y_ref[r, :], glub_ref[...], preferred_element_type=F32)) for r in rows]
    merged = [(jax.nn.sigmoid(ga_ref[r, :]) * at_ref[r, :] + jax.nn.sigmoid(gs_ref[r, :]) * s).astype(BF16)
              for r, s in zip(rows, ssm)]
    x2 = [x1_ref[r, :] + jnp.dot(m, wo_ref[...], preferred_element_type=F32) for r, m in zip(rows, merged)]
    first()
    for r, x in zip(rows, x2):
        h_scr[r, :] = _rms(x, nb_ref[...]).astype(BF16)
    for r, x, f in zip(rows, x2, _swiglu(h_scr, wg_ref, wu_ref, wd_ref, act_scr, rows, fillers)):
        o_ref[r, :] = _rms(x + 0.5 * f, nf_ref[...])


def _out(grid, tm, n_sub, std, ymap, x1, attn, ga, gs, y_tb, glua, glub, wo, nb, wg, wu, wd, nf):
    d = nb.shape[1]
    d_ff = wg.shape[1]
    kv = N_KV_HEADS * HEAD_DIM
    shp, spec = std(d)
    weights = [glua, glub, wo, nb, wg, wu, wd, nf]
    scratch = [pltpu.VMEM((tm, d), BF16), pltpu.VMEM((tm, d_ff), BF16)]
    fused = isinstance(attn, tuple)
    if fused:
        sinks, qst, k2, vt, tiles_per_seq = attn
        (n_tiles,) = grid
        blk = tm // WINDOW
        nxt = lambda s: jnp.minimum(s + 1, n_tiles - 1)
        before = lambda s: jnp.maximum(nxt(s) * blk - 1, 0)
        once = dict(pipeline_mode=pl.Buffered(1))
        acts = [sinks, x1, ga, gs, y_tb, qst, k2, vt, qst, k2, k2, vt, vt]
        act_specs = [pl.BlockSpec(memory_space=pltpu.SMEM), spec, spec, spec, pl.BlockSpec((tm, glua.shape[0]), ymap),
                     pl.BlockSpec((tm * N_HEADS, 2 * HEAD_DIM), lambda s: (0, 0), **once),
                     pl.BlockSpec((tm, 2 * kv), lambda s: (0, 0), **once),
                     pl.BlockSpec((kv, tm), lambda s: (0, 0), **once),
                     pl.BlockSpec((tm * N_HEADS, 2 * HEAD_DIM), lambda s: (nxt(s), 0)),
                     pl.BlockSpec((WINDOW, 2 * kv), lambda s: (before(s), 0)),
                     pl.BlockSpec((tm, 2 * kv), lambda s: (nxt(s), 0)),
                     pl.BlockSpec((kv, WINDOW), lambda s: (0, before(s))),
                     pl.BlockSpec((kv, tm), lambda s: (0, nxt(s)))]
        scratch += [pltpu.VMEM((tm, d), F32), pltpu.VMEM((ATTN_SLOTS, 2 * WINDOW, Q_PER_KV * WINDOW), F32)]
    else:
        tiles_per_seq = 0
        acts = [x1, attn, ga, gs, y_tb]
        act_specs = [spec, spec, spec, spec, pl.BlockSpec((tm, glua.shape[0]), ymap)]
    return pl.pallas_call(
        functools.partial(_out_kernel, n_sub=n_sub, tiles_per_seq=tiles_per_seq),
        grid=grid,
        in_specs=act_specs + [_const_spec(w.shape) for w in weights],
        out_specs=spec,
        out_shape=jax.ShapeDtypeStruct(shp, F32),
        scratch_shapes=scratch,
        compiler_params=pltpu.CompilerParams(
            dimension_semantics=("arbitrary",) * len(grid), vmem_limit_bytes=VMEM_LIMIT),
        name="out",
    )(*acts, *weights)


def _transpose_bf16(w):
    def body(w_ref, o_ref):
        o_ref[...] = w_ref[...].T.astype(BF16)
    return pl.pallas_call(body, out_shape=jax.ShapeDtypeStruct(w.shape[::-1], BF16), name="transpose_bf16")(w)


def _block_diag(w, chunks):
    g, a, b = w.shape
    gl = g // chunks
    w = w.reshape(chunks, gl, a, b)
    eye = jnp.eye(gl, dtype=w.dtype)
    return (w[:, :, :, None, :] * eye[None, :, None, :, None]).reshape(chunks, gl * a, gl * b)


def _state_to_lanes(re, im):
    s = re.shape[0]
    re = re.reshape(s, SSM_CHUNKS, -1)
    im = im.reshape(s, SSM_CHUNKS, -1)
    return jnp.stack([re, im], axis=2).reshape(s, -1)


def _lanes_to_state(h, groups):
    s = h.shape[0]
    h = h.reshape(s, SSM_CHUNKS, 2, -1)
    return h[:, :, 0].reshape(s, groups, -1), h[:, :, 1].reshape(s, groups, -1)


def kernel(x_prompt, x_sample, cache_k_win, cache_v_win, state_ssm_re, state_ssm_im, ffn_a_norm, ffn_a_gate, ffn_a_up, ffn_a_down, mix_norm, w_in, attn_sinks, ssm_lambda_re, ssm_lambda_im, ssm_log_dt, ssm_b_re, ssm_b_im, ssm_c_re, ssm_c_im, ssm_d, glu_a, glu_b, w_out, ffn_b_norm, ffn_b_gate, ffn_b_up, ffn_b_down, final_norm):
    depth = ffn_a_norm.shape[0]
    assert depth == 1, "single-layer trunk"
    n_p, seq, d = x_prompt.shape
    n_s, dec, _ = x_sample.shape
    ssm_w = ssm_d.shape[1]
    groups = ssm_lambda_re.shape[1]
    kvw = N_KV_HEADS * HEAD_DIM
    assert cache_k_win.shape[2] == WINDOW and seq % WINDOW == 0 and n_p % 8 == 0 and n_s % 8 == 0

    l = 0
    na, nm, nb = (w[l].reshape(1, d) for w in (ffn_a_norm, mix_norm, ffn_b_norm))
    nf = final_norm.reshape(1, d)
    wga, wua, wda, win = (w[l].astype(BF16) for w in (ffn_a_gate, ffn_a_up, ffn_a_down, w_in))
    wgb, wub, wdb = (w[l].astype(BF16) for w in (ffn_b_gate, ffn_b_up, ffn_b_down))
    glua, glub, wo = (w[l].astype(BF16) for w in (glu_a, glu_b, w_out))
    sinks = attn_sinks[l]
    d_skip = ssm_d[l].reshape(1, ssm_w)

    abar_re, abar_im, bb_re_t, bb_im_t = _ssm_prep(
        ssm_lambda_re[l], ssm_lambda_im[l], ssm_log_dt[l],
        jnp.swapaxes(ssm_b_re[l], 1, 2), jnp.swapaxes(ssm_b_im[l], 1, 2))
    bd = jnp.concatenate([_block_diag(bb_re_t, SSM_CHUNKS), _block_diag(bb_im_t, SSM_CHUNKS)],
                         axis=2).astype(BF16)
    c_re_t = jnp.swapaxes(ssm_c_re[l], 1, 2)
    c_im_t = jnp.swapaxes(ssm_c_im[l], 1, 2)
    cd = jnp.concatenate([_block_diag(c_re_t, SSM_CHUNKS), -_block_diag(c_im_t, SSM_CHUNKS)],
                         axis=1).astype(BF16)
    a8 = jnp.broadcast_to(_state_to_lanes(abar_re[None], abar_im[None]), (8, 2 * groups * SSM_STATE))

    def run_group(x, n_seq, tiling, out_tiling, u_shape, attn_w, stacked, attn_fn, h0, tt):
        grid, tm, std, umap, n_sub = tiling
        x1, q, k, v, u_tb, ga, gs, *more = _ffn_in(x, grid, tm, n_sub, std, umap, u_shape,
                                                    na, wga, wua, wda, nm, win, attn_w, stacked)
        attn = attn_fn(q, k, v, *more)
        y_tb, h_t = _ssm(u_tb, h0, a8, d_skip, bd, cd, n_seq, tt,
                         seq_lanes=u_shape[1] != ssm_w, scan_unroll=tt)
        grid, tm, std, umap, n_sub = out_tiling
        y = _out(grid, tm, n_sub, std, umap, x1, attn, ga, gs, y_tb, glua, glub, wo, nb, wgb, wub, wdb, nf)
        return y, k, v, h_t

    def tiling_p(tm):
        nt = seq // tm

        def std(width):
            return (n_p * seq, width), pl.BlockSpec((tm, width), lambda b, i: (b * nt + i, 0))
        return (n_p, nt), tm, std, lambda b, i: (i, b), tm // 256

    tm_p = 512
    nt = seq // tm_p

    def attn_p(qst, k, v, k2, vt):
        return sinks, qst, k2, vt, nt

    w_kv = w_in[l][:, d:d + 2 * kvw]
    wk2 = jnp.broadcast_to(w_kv[:, :kvw].reshape(d, N_KV_HEADS, 1, HEAD_DIM),
                           (d, N_KV_HEADS, 2, HEAD_DIM)).reshape(d, 2 * kvw).astype(BF16)
    w_kvt = _transpose_bf16(w_kv)
    wkt = w_kvt[:kvw]
    wvt = w_kvt[kvw:]

    h0_p = jnp.zeros((n_p, 2 * groups * SSM_STATE), F32)
    def std_flat(width):
        return (n_p * seq, width), pl.BlockSpec((tm_p, width), lambda s: (s, 0))

    out_tiling_p = ((n_p * nt,), tm_p, std_flat, lambda s: (s % nt, s // nt), tm_p // 256)
    y_p, k_p, v_p, h_p = run_group(
        x_prompt.reshape(n_p * seq, d), n_p, tiling_p(tm_p), out_tiling_p,
        (seq, n_p * ssm_w), (wk2, wvt, lambda b, i: (b * nt + i, 0)), True, attn_p, h0_p, 1024 // n_p)

    tm_s = 512

    def std_s(width):
        return (n_s * dec, width), pl.BlockSpec((tm_s, width), lambda i, j: (i, 0))

    def window_t(c):
        return jnp.transpose(c, (0, 2, 3, 1)).reshape(c.shape[0], kvw, WINDOW)

    def window(ct):
        return jnp.transpose(ct.reshape(ct.shape[0], N_KV_HEADS, HEAD_DIM, WINDOW), (0, 3, 1, 2))[None]

    new_windows = []

    def attn_s(q, kt, vt):
        o, kw, vw = _attn_sample(sinks, q, kt, vt, window_t(cache_k_win[l]), window_t(cache_v_win[l]),
                                 n_s, dec, WINDOW // dec, 4)
        new_windows.extend([kw, vw])
        return o

    h0_s = _state_to_lanes(state_ssm_re[l], state_ssm_im[l])
    tiling_s = ((n_s * dec // tm_s, 1), tm_s, std_s, lambda i, j: (i, 0), tm_s // 256)
    y_s, _, _, h_s = run_group(
        x_sample.reshape(n_s * dec, d), n_s, tiling_s, tiling_s,
        (n_s * dec, ssm_w), (wkt, wvt, lambda i, j: (i, 0)), False, attn_s, h0_s, dec)

    sp_re, sp_im = _lanes_to_state(h_p, groups)
    ss_re, ss_im = _lanes_to_state(h_s, groups)
    return (y_p.reshape(n_p, seq, d), y_s.reshape(n_s, dec, d),
            window(k_p), window(v_p), window(new_windows[0]), window(new_windows[1]),
            sp_re[None], sp_im[None], ss_re[None], ss_im[None])
```

```python
import functools

import jax
import jax.numpy as jnp
from jax import lax
from jax.experimental import pallas as pl
from jax.experimental.pallas import tpu as pltpu

F32 = jnp.float32
BF16 = jnp.bfloat16

N_HEADS = 16
N_KV_HEADS = 4
HEAD_DIM = 64
Q_PER_KV = N_HEADS // N_KV_HEADS
WINDOW = 128
SSM_STATE = 64
RMS_EPS = 1e-6
NEG_BIG = -1e30
LOG2_E = 1.4426950408889634

MXU_COLS = 256
WINDOW_COLS = N_KV_HEADS * HEAD_DIM
ATTN_AHEAD = 2
ATTN_SLOTS = 4
SSM_CHUNKS = 4
VMEM_LIMIT = 62 * 1024 * 1024


def _rms(x, g):
    return x * lax.rsqrt(jnp.mean(x * x, axis=-1, keepdims=True) + RMS_EPS) * g


def _row_slices(tm, n_sub):
    rs = tm // n_sub
    return [slice(r * rs, (r + 1) * rs) for r in range(n_sub)]


def _swiglu(h_scr, wg_ref, wu_ref, wd_ref, act_scr, rows=(slice(None),), fillers=()):
    d_ff = wg_ref.shape[1]
    fillers = list(fillers)
    for c in range(d_ff // MXU_COLS):
        sl = slice(c * MXU_COLS, (c + 1) * MXU_COLS)
        for r in rows:
            g = jnp.dot(h_scr[r, :], wg_ref[:, sl], preferred_element_type=F32)
            u = jnp.dot(h_scr[r, :], wu_ref[:, sl], preferred_element_type=F32)
            act_scr[r, sl] = (jax.nn.silu(g) * u).astype(BF16)
            if fillers:
                fillers.pop(0)()
    for f in fillers:
        f()
    return [jnp.dot(act_scr[r, :], wd_ref[...], preferred_element_type=F32) for r in rows]


def _ffn_in_kernel(*refs, stacked, n_sub):
    if stacked:
        (x_ref, na_ref, wg_ref, wu_ref, wd_ref, nm_ref, win_ref, wvt_ref,
         x1_ref, q_ref, k_ref, v_ref, u_ref, ga_ref, gs_ref, kb_ref, vt_ref, h_scr, act_scr) = refs
        assert WINDOW_COLS == k_ref.shape[0]
    else:
        (x_ref, na_ref, wg_ref, wu_ref, wd_ref, nm_ref, win_ref, wkt_ref, wvt_ref,
         x1_ref, q_ref, k_ref, v_ref, u_ref, ga_ref, gs_ref, h_scr, act_scr) = refs
    tm, d = x_ref.shape
    rows = _row_slices(tm, n_sub)
    rs = tm // n_sub
    for r in rows:
        h_scr[r, :] = _rms(x_ref[r, :], na_ref[...]).astype(BF16)
    x1 = [x_ref[r, :] + 0.5 * f for r, f in zip(rows, _swiglu(h_scr, wg_ref, wu_ref, wd_ref, act_scr, rows))]
    for r, x in zip(rows, x1):
        x1_ref[r, :] = x
        h_scr[r, :] = _rms(x, nm_ref[...]).astype(BF16)

    def proj(w_ref, off, c, r):
        return jnp.dot(h_scr[r, :], w_ref[:, off + c * MXU_COLS: off + (c + 1) * MXU_COLS],
                       preferred_element_type=F32)

    q_scale = HEAD_DIM ** -0.5 * (LOG2_E if stacked else 1.0)
    pair_w = 2 * HEAD_DIM
    if stacked:
        assert Q_PER_KV * HEAD_DIM == MXU_COLS
        lane = lax.broadcasted_iota(jnp.int32, (rs, pair_w), 1)
        for g in range(N_KV_HEADS):
            keep = (lane < HEAD_DIM) if g % 2 == 0 else (lane >= HEAD_DIM)
            for ri, r in enumerate(rows):
                res = proj(win_ref, 0, g, r) * q_scale
                for rr in range(Q_PER_KV):
                    pair = res[:, (rr // 2) * pair_w:(rr // 2 + 1) * pair_w]
                    if rr % 2 != g % 2:
                        pair = pltpu.roll(pair, HEAD_DIM, axis=1)
                    piece = jnp.where(keep, pair, 0.0).astype(BF16)
                    for bl in range(rs // WINDOW):
                        row = (((ri * (rs // WINDOW) + bl) * N_KV_HEADS + g) * Q_PER_KV + rr) * WINDOW
                        q_ref[row:row + WINDOW, :] = piece[bl * WINDOW:(bl + 1) * WINDOW, :]
    else:
        for c in range(d // MXU_COLS):
            for r in rows:
                q_ref[r, c * MXU_COLS:(c + 1) * MXU_COLS] = (proj(win_ref, 0, c, r) * q_scale).astype(q_ref.dtype)
        for ref, wt_ref in ((k_ref, wkt_ref), (v_ref, wvt_ref)):
            for r in rows:
                ref[:, r] = lax.dot_general(wt_ref[...], h_scr[r, :], (((1,), (1,)), ((), ())),
                                            preferred_element_type=F32)
    off = d + 2 * WINDOW_COLS
    for ref in (u_ref, ga_ref, gs_ref):
        width = ref.shape[1]
        for c in range(width // MXU_COLS):
            for r in rows:
                ref[r, c * MXU_COLS:(c + 1) * MXU_COLS] = proj(win_ref, off, c, r).astype(ref.dtype)
        off += width
    if stacked:
        for r in rows:
            kb_ref[r, :] = proj(win_ref, d, 0, r).astype(kb_ref.dtype)
        for r in rows:
            vt_ref[:, r] = lax.dot_general(wvt_ref[...], h_scr[r, :], (((1,), (1,)), ((), ())),
                                           preferred_element_type=F32).astype(vt_ref.dtype)

        @pl.when(pl.program_id(1) == pl.num_programs(1) - 1)
        def _():
            tail = h_scr[tm - WINDOW:, :]
            for ref, off in ((k_ref, d), (v_ref, d + WINDOW_COLS)):
                ref[...] = jnp.dot(tail, win_ref[:, off:off + WINDOW_COLS], preferred_element_type=F32).T


def _const_spec(shape):
    nd = len(shape)
    return pl.BlockSpec(shape, lambda *_: (0,) * nd, pipeline_mode=pl.Buffered(1))


def _ffn_in(x2d, grid, tm, n_sub, std, umap, u_shape, na, wg, wu, wd, nm, win, attn_w, stacked):
    d = na.shape[1]
    d_ff = wg.shape[1]
    kv = N_KV_HEADS * HEAD_DIM
    ssm_w = win.shape[1] - 3 * d - 2 * kv
    n_tok = x2d.size // d
    extra_w, tile_map = attn_w
    t_spec = pl.BlockSpec((kv, tm), lambda *g: tile_map(*g)[::-1])
    out_shape, out_specs = [], []

    def add(shape_spec, dtype):
        out_shape.append(jax.ShapeDtypeStruct(shape_spec[0], dtype))
        out_specs.append(shape_spec[1])

    add(std(d), F32)
    if stacked:
        add(((n_tok * N_HEADS, 2 * HEAD_DIM), pl.BlockSpec((tm * N_HEADS, 2 * HEAD_DIM), tile_map)), BF16)
        for _ in range(2):
            add(((grid[0], kv, WINDOW), pl.BlockSpec((None, kv, WINDOW), lambda b, i: (b, 0, 0))), F32)
    else:
        add(std(d), F32)
        add(((kv, n_tok), t_spec), F32)
        add(((kv, n_tok), t_spec), F32)
    add((u_shape, pl.BlockSpec((tm, ssm_w), umap)), F32)
    add(std(d), F32)
    add(std(d), F32)
    weights = [na, wg, wu, wd, nm, win, *extra_w]
    if stacked:
        add(std(kv), BF16)
        add(((kv, n_tok), t_spec), BF16)
    return pl.pallas_call(
        functools.partial(_ffn_in_kernel, stacked=stacked, n_sub=n_sub),
        grid=grid,
        in_specs=[std(d)[1]] + [_const_spec(w.shape) for w in weights],
        out_specs=out_specs,
        out_shape=out_shape,
        scratch_shapes=[pltpu.VMEM((tm, d), BF16), pltpu.VMEM((tm, d_ff), BF16)],
        compiler_params=pltpu.CompilerParams(
            dimension_semantics=("parallel", "arbitrary"), vmem_limit_bytes=VMEM_LIMIT),
        name="ffn_in",
    )(x2d, *weights)


def _attention_pieces(sink_ref, q_ref, kp_ref, kc_ref, vp_ref, vc_ref, st_scr, o_ref, ctx_ok):
    nq = WINDOW
    nk = 2 * WINDOW
    lanes = Q_PER_KV * nq
    pair_w = 2 * HEAD_DIM
    n_blk = kc_ref.shape[0] // WINDOW
    n_slots = st_scr.shape[0]
    masks = {}

    def valid(bl):
        if bl not in masks:
            jk = lax.broadcasted_iota(jnp.int32, (nk, lanes), 0)
            iq = lax.broadcasted_iota(jnp.int32, (nk, lanes), 1) & (nq - 1)
            ok = True if bl > 0 else ctx_ok
            lo = iq if ok is True else jnp.maximum(iq, jnp.where(ok, 0, WINDOW))
            masks[bl] = (jk >= lo) & (jk <= iq + WINDOW)
        return masks[bl]

    def keys(bl, g):
        cols = slice((g // 2) * pair_w, (g // 2 + 1) * pair_w)
        if bl == 0:
            return jnp.concatenate([kp_ref[:, cols], kc_ref[0:WINDOW, cols]], axis=0)
        return kc_ref[(bl - 1) * WINDOW:(bl + 1) * WINDOW, cols]

    def vals_t(bl, g):
        rows = slice(g * HEAD_DIM, (g + 1) * HEAD_DIM)
        if bl == 0:
            return jnp.concatenate([vp_ref[rows, :], vc_ref[rows, 0:WINDOW]], axis=1)
        return vc_ref[rows, (bl - 1) * WINDOW:(bl + 1) * WINDOW]

    def score(bl, g):
        k = bl * N_KV_HEADS + g
        st = lax.dot_general(keys(bl, g), q_ref[k * lanes:(k + 1) * lanes, :], (((1,), (1,)), ((), ())),
                             preferred_element_type=F32)
        st_scr[k % n_slots] = jnp.where(valid(bl), st, NEG_BIG)

    def attend(bl, g):
        st = st_scr[(bl * N_KV_HEADS + g) % n_slots]
        sink = jnp.concatenate(
            [jnp.full((1, nq), sink_ref[g * Q_PER_KV + r] * LOG2_E, F32) for r in range(Q_PER_KV)], axis=1)
        m = jnp.maximum(jnp.max(st, axis=0, keepdims=True), sink)
        p = jnp.exp2(st - m)
        denom = jnp.sum(p, axis=0, keepdims=True) + jnp.exp2(sink - m)
        ot = jnp.dot(vals_t(bl, g), p.astype(BF16), preferred_element_type=F32) * (1.0 / denom)
        for pr in range(Q_PER_KV // 2):
            two = jnp.concatenate([ot[:, (2 * pr) * nq:(2 * pr + 1) * nq],
                                   ot[:, (2 * pr + 1) * nq:(2 * pr + 2) * nq]], axis=0)
            col = (g * Q_PER_KV + 2 * pr) * HEAD_DIM
            o_ref[bl * nq:(bl + 1) * nq, col:col + pair_w] = two.T

    order = [(bl, g) for bl in range(n_blk) for g in range(N_KV_HEADS)]
    return ([functools.partial(score, bl, g) for bl, g in order],
            [functools.partial(attend, bl, g) for bl, g in order])


def _staggered(score, attend, ahead):
    def piece(k):
        def run():
            if k + ahead < len(score):
                score[k + ahead]()
            attend[k]()
        return run

    def first():
        for f in score[:ahead]:
            f()
    return first, [piece(k) for k in range(len(attend))]


def _attn_sample_kernel(sink_ref, q_ref, knt_ref, vnt_ref, ckt_ref, cvt_ref, o_ref, kw_ref, vw_ref, *, n_sub, tn, unroll):
    pair_w = 2 * HEAD_DIM
    n_pairs = N_HEADS // 2
    rows = n_pairs * tn
    kv = N_KV_HEADS * HEAD_DIM
    old = WINDOW - tn
    assert tn & (tn - 1) == 0, "row -> token index uses a power-of-two mask"
    low = lax.broadcasted_iota(jnp.int32, (tn, pair_w), 1) < HEAD_DIM
    tq = lax.broadcasted_iota(jnp.int32, (2 * rows, 2 * WINDOW), 0) & (tn - 1)
    col = lax.broadcasted_iota(jnp.int32, (2 * rows, 2 * WINDOW), 1)
    valid = ((col < WINDOW) & (col >= tq)) | ((col >= WINDOW + old) & (col - WINDOW - old <= tq))
    is_new = lax.broadcasted_iota(jnp.int32, (kv, WINDOW), 1) >= old
    zeros = jnp.zeros((tn, pair_w), F32)
    nt = (((1,), (1,)), ((), ()))
    assert n_sub * tn == WINDOW

    sink = jnp.concatenate([jnp.full((tn, 1), sink_ref[2 * pr + half], F32)
                            for half in range(2) for pr in range(n_pairs)], axis=0)

    def scores(s):
        r0 = pl.multiple_of(s * tn, tn)
        q = q_ref[pl.ds(r0, tn), :]
        kt = ckt_ref[s]
        vt = cvt_ref[s]
        shift = (old - r0) & (WINDOW - 1)
        knt = jnp.where(is_new, pltpu.roll(knt_ref[...], shift, axis=1), 0.0)
        vnt = jnp.where(is_new, pltpu.roll(vnt_ref[...], shift, axis=1), 0.0)
        kw_ref[s] = jnp.where(is_new, knt, pltpu.roll(kt, old, axis=1))
        vw_ref[s] = jnp.where(is_new, vnt, pltpu.roll(vt, old, axis=1))
        k_all = jnp.concatenate([kt, knt], axis=1).astype(BF16)
        v_all = jnp.concatenate([vt, vnt], axis=1).astype(BF16)
        blocks = []
        for half in range(2):
            for g in range(N_KV_HEADS):
                for pp in range(Q_PER_KV // 2):
                    piece = q[:, (2 * g + pp) * pair_w:(2 * g + pp + 1) * pair_w]
                    if half != g % 2:
                        piece = pltpu.roll(piece, HEAD_DIM, axis=1)
                    piece = jnp.where(low if g % 2 == 0 else ~low, piece, 0.0)
                    blocks.append(jnp.concatenate([piece, zeros] if g // 2 == 0 else [zeros, piece], axis=1))
        qh = jnp.concatenate(blocks, axis=0).astype(BF16)
        return r0, v_all, jnp.where(valid, jnp.dot(qh, k_all, preferred_element_type=F32), NEG_BIG)

    def attend(r0, v_all, sc):
        m = jnp.maximum(jnp.max(sc, axis=-1, keepdims=True), sink)
        p = jnp.exp(sc - m)
        rden = 1.0 / (jnp.sum(p, axis=-1, keepdims=True) + jnp.exp(sink - m))
        o = lax.dot_general((p * rden).astype(BF16), v_all, nt, preferred_element_type=F32)
        for pr in range(n_pairs):
            g = pr // (Q_PER_KV // 2)
            sel = []
            for half in range(2):
                blk = o[half * rows + pr * tn:half * rows + (pr + 1) * tn, (g // 2) * pair_w:(g // 2 + 1) * pair_w]
                sel.append(blk if half == g % 2 else pltpu.roll(blk, HEAD_DIM, axis=1))
            o_ref[pl.ds(r0, tn), pr * pair_w:(pr + 1) * pair_w] = jnp.where(low, sel[0], sel[1])

    def some_sequences(i, carry):
        staged = [scores(i * unroll + u) for u in range(unroll)]
        for st in staged:
            attend(*st)
        return carry

    lax.fori_loop(0, n_sub // unroll, some_sequences, 0)


def _attn_sample(sinks, q, knt, vnt, cache_kt, cache_vt, n_seq, tn, n_sub, unroll):
    d = q.shape[1]
    kv = knt.shape[0]
    rows = lambda i: (i, 0)
    seqs = lambda i: (i, 0, 0)
    win = pl.BlockSpec((n_sub, kv, WINDOW), seqs)
    new = pl.BlockSpec((kv, n_sub * tn), lambda i: (0, i))
    return pl.pallas_call(
        functools.partial(_attn_sample_kernel, n_sub=n_sub, tn=tn, unroll=unroll),
        grid=(n_seq // n_sub,),
        in_specs=[pl.BlockSpec(memory_space=pltpu.SMEM),
                  pl.BlockSpec((n_sub * tn, d), rows),
                  new, new, win, win],
        out_specs=[pl.BlockSpec((n_sub * tn, d), rows), win, win],
        out_shape=[jax.ShapeDtypeStruct(q.shape, F32),
                   jax.ShapeDtypeStruct(cache_kt.shape, F32), jax.ShapeDtypeStruct(cache_vt.shape, F32)],
        compiler_params=pltpu.CompilerParams(
            dimension_semantics=("parallel",), vmem_limit_bytes=VMEM_LIMIT),
        name="attn_sample",
    )(sinks, q, knt, vnt, cache_kt, cache_vt)


def _ssm_prep_kernel(lr_ref, li_ref, ldt_ref, br_ref, bi_ref, ar_ref, ai_ref, bbr_ref, bbi_ref):
    lr = lr_ref[...]
    li = li_ref[...]
    dt = jnp.exp(ldt_ref[...])
    mag = jnp.exp(lr * dt)
    ang = li * dt
    abar_re = mag * jnp.cos(ang)
    abar_im = mag * jnp.sin(ang)
    den = lr * lr + li * li
    nr = abar_re - 1.0
    fr = (nr * lr + abar_im * li) / den
    fi = (abar_im * lr - nr * li) / den
    ar_ref[...] = abar_re
    ai_ref[...] = abar_im
    br = br_ref[...]
    bi = bi_ref[...]
    bbr_ref[...] = fr[:, None, :] * br - fi[:, None, :] * bi
    bbi_ref[...] = fr[:, None, :] * bi + fi[:, None, :] * br


def _ssm_prep(lam_re, lam_im, log_dt, b_re_t, b_im_t):
    g, n = lam_re.shape
    return pl.pallas_call(
        _ssm_prep_kernel,
        out_shape=[jax.ShapeDtypeStruct((g, n), F32), jax.ShapeDtypeStruct((g, n), F32),
                   jax.ShapeDtypeStruct(b_re_t.shape, F32), jax.ShapeDtypeStruct(b_im_t.shape, F32)],
        name="ssm_prep",
    )(lam_re, lam_im, log_dt.reshape(g, 1), b_re_t, b_im_t)


def _ssm_kernel(u_ref, h0_ref, a_ref, d_ref, bd_ref, cd_ref, y_ref, hT_ref, xh_scr, u_scr, y_scr,
                *, n_seq, tt, seq_lanes, scan_unroll):
    cw = xh_scr.shape[1] // SSM_CHUNKS
    hw = cw // 2
    uc = u_scr.shape[2]
    ssm_w = SSM_CHUNKS * uc

    @pl.when(pl.program_id(0) == 0)
    def _():
        hT_ref[...] = h0_ref[...]

    for j in range(SSM_CHUNKS):
        for b in range(n_seq):
            if seq_lanes:
                u_b = u_ref[:, b * ssm_w + j * uc:b * ssm_w + (j + 1) * uc]
            else:
                u_b = u_ref[b * tt:(b + 1) * tt, j * uc:(j + 1) * uc]
            u_scr[j, pl.ds(b, tt, stride=n_seq), :] = u_b

    def project_in(j):
        xh_scr[:, j * cw:(j + 1) * cw] = jnp.dot(u_scr[j].astype(BF16), bd_ref[j], preferred_element_type=F32)

    def scan(j):
        re = slice(j * cw, j * cw + hw)
        im = slice(j * cw + hw, (j + 1) * cw)
        ar = a_ref[:, re]
        ai = a_ref[:, im]

        def seq_group(s, carry):
            s8 = pl.multiple_of(s * 8, 8)

            def step(t, h):
                hr, hi = h
                r0 = pl.multiple_of(t * n_seq + s8, 8)
                nr = ar * hr - ai * hi + xh_scr[pl.ds(r0, 8), re]
                ni = ar * hi + ai * hr + xh_scr[pl.ds(r0, 8), im]
                xh_scr[pl.ds(r0, 8), re] = nr
                xh_scr[pl.ds(r0, 8), im] = ni
                return nr, ni

            hr, hi = lax.fori_loop(0, tt, step, (hT_ref[pl.ds(s8, 8), re], hT_ref[pl.ds(s8, 8), im]),
                                   unroll=scan_unroll)
            hT_ref[pl.ds(s8, 8), re] = hr
            hT_ref[pl.ds(s8, 8), im] = hi
            return carry

        lax.fori_loop(0, n_seq // 8, seq_group, 0)

    def project_out(j):
        y = jnp.dot(xh_scr[:, j * cw:(j + 1) * cw].astype(BF16), cd_ref[j], preferred_element_type=F32)
        cs = slice(j * uc, (j + 1) * uc)
        y_scr[j] = jax.nn.gelu(y + d_ref[:, cs] * u_scr[j])
        for b in range(n_seq):
            y_b = y_scr[j, pl.ds(b, tt, stride=n_seq), :].astype(y_ref.dtype)
            if seq_lanes:
                y_ref[:, b * ssm_w + j * uc:b * ssm_w + (j + 1) * uc] = y_b
            else:
                y_ref[b * tt:(b + 1) * tt, cs] = y_b

    for stage in range(SSM_CHUNKS + 2):
        if stage < SSM_CHUNKS:
            project_in(stage)
        if 1 <= stage <= SSM_CHUNKS:
            scan(stage - 1)
        if stage >= 2:
            project_out(stage - 2)


def _ssm(u, h0, a8, d_skip, bd, cd, n_seq, tt, seq_lanes, scan_unroll):
    ssm_w = d_skip.shape[1]
    tile = n_seq * tt
    n_state = h0.shape[1]
    block = (tt, n_seq * ssm_w) if seq_lanes else (tile, ssm_w)
    assert seq_lanes or u.shape[0] == tile, "row order (seq, t) cannot be tiled over time"
    slab = pltpu.VMEM((SSM_CHUNKS, tile, ssm_w // SSM_CHUNKS), F32)
    return pl.pallas_call(
        functools.partial(_ssm_kernel, n_seq=n_seq, tt=tt, seq_lanes=seq_lanes, scan_unroll=scan_unroll),
        grid=(u.shape[0] // block[0],),
        in_specs=[pl.BlockSpec(block, lambda i: (i, 0)),
                  _const_spec(h0.shape), _const_spec(a8.shape), _const_spec(d_skip.shape),
                  _const_spec(bd.shape), _const_spec(cd.shape)],
        out_specs=[pl.BlockSpec(block, lambda i: (i, 0)),
                   pl.BlockSpec(h0.shape, lambda i: (0, 0))],
        out_shape=[jax.ShapeDtypeStruct(u.shape, BF16), jax.ShapeDtypeStruct(h0.shape, F32)],
        scratch_shapes=[pltpu.VMEM((tile, n_state), F32), slab, slab],
        compiler_params=pltpu.CompilerParams(
            dimension_semantics=("arbitrary",), vmem_limit_bytes=VMEM_LIMIT),
        name="ssm",
    )(u, h0, a8, d_skip, bd, cd)


def _out_kernel(*refs, n_sub, tiles_per_seq):
    if tiles_per_seq:
        (sink_ref, x1_ref, ga_ref, gs_ref, y_ref, q0_ref, k0_ref, v0_ref, qn_ref, kp_ref, kn_ref, vp_ref, vn_ref,
         glua_ref, glub_ref, wo_ref, nb_ref, wg_ref, wu_ref, wd_ref, nf_ref, o_ref,
         h_scr, act_scr, at_ref, st_scr) = refs
        step = pl.program_id(0)

        @pl.when(step == 0)
        def _():
            score, attend = _attention_pieces(sink_ref, q0_ref, k0_ref.at[pl.ds(0, WINDOW)], k0_ref,
                                              v0_ref.at[:, pl.ds(0, WINDOW)], v0_ref, st_scr, at_ref, False)
            first, rest = _staggered(score, attend, ATTN_AHEAD)
            first()
            for f in rest:
                f()

        nxt = jnp.minimum(step + 1, pl.num_programs(0) - 1)
        score, attend = _attention_pieces(sink_ref, qn_ref, kp_ref, kn_ref, vp_ref, vn_ref, st_scr, at_ref,
                                          nxt % tiles_per_seq != 0)
        first, fillers = _staggered(score, attend, ATTN_AHEAD)
    else:
        (x1_ref, at_ref, ga_ref, gs_ref, y_ref, glua_ref, glub_ref, wo_ref,
         nb_ref, wg_ref, wu_ref, wd_ref, nf_ref, o_ref, h_scr, act_scr) = refs
        first, fillers = (lambda: None), ()
    rows = _row_slices(x1_ref.shape[0], n_sub)
    ssm = [jnp.dot(y_ref[r, :], glua_ref[...], preferred_element_type=F32) * jax.nn.sigmoid(
        jnp.dot(y_ref[r, :], glub_ref[...], preferred_element_type=F32)) for r in rows]
    merged = [(jax.nn.sigmoid(ga_ref[r, :]) * at_ref[r, :] + jax.nn.sigmoid(gs_ref[r, :]) * s).astype(BF16)
              for r, s in zip(rows, ssm)]
    x2 = [x1_ref[r, :] + jnp.dot(m, wo_ref[...], preferred_element_type=F32) for r, m in zip(rows, merged)]
    first()
    for r, x in zip(rows, x2):
        h_scr[r, :] = _rms(x, nb_ref[...]).astype(BF16)
    for r, x, f in zip(rows, x2, _swiglu(h_scr, wg_ref, wu_ref, wd_ref, act_scr, rows, fillers)):
        o_ref[r, :] = _rms(x + 0.5 * f, nf_ref[...])


def _out(grid, tm, n_sub, std, ymap, x1, attn, ga, gs, y_tb, glua, glub, wo, nb, wg, wu, wd, nf):
    d = nb.shape[1]
    d_ff = wg.shape[1]
    kv = N_KV_HEADS * HEAD_DIM
    shp, spec = std(d)
    weights = [glua, glub, wo, nb, wg, wu, wd, nf]
    scratch = [pltpu.VMEM((tm, d), BF16), pltpu.VMEM((tm, d_ff), BF16)]
    fused = isinstance(attn, tuple)
    if fused:
        sinks, qst, kb, vt, tiles_per_seq = attn
        (n_tiles,) = grid
        blk = tm // WINDOW
        nxt = lambda s: jnp.minimum(s + 1, n_tiles - 1)
        before = lambda s: jnp.maximum(nxt(s) * blk - 1, 0)
        once = dict(pipeline_mode=pl.Buffered(1))
        acts = [sinks, x1, ga, gs, y_tb, qst, kb, vt, qst, kb, kb, vt, vt]
        act_specs = [pl.BlockSpec(memory_space=pltpu.SMEM), spec, spec, spec, pl.BlockSpec((tm, glua.shape[0]), ymap),
                     pl.BlockSpec((tm * N_HEADS, 2 * HEAD_DIM), lambda s: (0, 0), **once),
                     pl.BlockSpec((tm, kv), lambda s: (0, 0), **once),
                     pl.BlockSpec((kv, tm), lambda s: (0, 0), **once),
                     pl.BlockSpec((tm * N_HEADS, 2 * HEAD_DIM), lambda s: (nxt(s), 0)),
                     pl.BlockSpec((WINDOW, kv), lambda s: (before(s), 0)),
                     pl.BlockSpec((tm, kv), lambda s: (nxt(s), 0)),
                     pl.BlockSpec((kv, WINDOW), lambda s: (0, before(s))),
                     pl.BlockSpec((kv, tm), lambda s: (0, nxt(s)))]
        scratch += [pltpu.VMEM((tm, d), F32), pltpu.VMEM((ATTN_SLOTS, 2 * WINDOW, Q_PER_KV * WINDOW), F32)]
    else:
        tiles_per_seq = 0
        acts = [x1, attn, ga, gs, y_tb]
        act_specs = [spec, spec, spec, spec, pl.BlockSpec((tm, glua.shape[0]), ymap)]
    return pl.pallas_call(
        functools.partial(_out_kernel, n_sub=n_sub, tiles_per_seq=tiles_per_seq),
        grid=grid,
        in_specs=act_specs + [_const_spec(w.shape) for w in weights],
        out_specs=spec,
        out_shape=jax.ShapeDtypeStruct(shp, F32),
        scratch_shapes=scratch,
        compiler_params=pltpu.CompilerParams(
            dimension_semantics=("arbitrary",) * len(grid), vmem_limit_bytes=VMEM_LIMIT),
        name="out",
    )(*acts, *weights)


def _transpose_bf16(w):
    def body(w_ref, o_ref):
        o_ref[...] = w_ref[...].T.astype(BF16)
    return pl.pallas_call(body, out_shape=jax.ShapeDtypeStruct(w.shape[::-1], BF16), name="transpose_bf16")(w)


def _block_diag(w, chunks):
    g, a, b = w.shape
    gl = g // chunks
    w = w.reshape(chunks, gl, a, b)
    eye = jnp.eye(gl, dtype=w.dtype)
    return (w[:, :, :, None, :] * eye[None, :, None, :, None]).reshape(chunks, gl * a, gl * b)


def _state_to_lanes(re, im):
    s = re.shape[0]
    re = re.reshape(s, SSM_CHUNKS, -1)
    im = im.reshape(s, SSM_CHUNKS, -1)
    return jnp.stack([re, im], axis=2).reshape(s, -1)


def _lanes_to_state(h, groups):
    s = h.shape[0]
    h = h.reshape(s, SSM_CHUNKS, 2, -1)
    return h[:, :, 0].reshape(s, groups, -1), h[:, :, 1].reshape(s, groups, -1)


def kernel(x_prompt, x_sample, cache_k_win, cache_v_win, state_ssm_re, state_ssm_im, ffn_a_norm, ffn_a_gate, ffn_a_up, ffn_a_down, mix_norm, w_in, attn_sinks, ssm_lambda_re, ssm_lambda_im, ssm_log_dt, ssm_b_re, ssm_b_im, ssm_c_re, ssm_c_im, ssm_d, glu_a, glu_b, w_out, ffn_b_norm, ffn_b_gate, ffn_b_up, ffn_b_down, final_norm):
    depth = ffn_a_norm.shape[0]
    assert depth == 1, "single-layer trunk"
    n_p, seq, d = x_prompt.shape
    n_s, dec, _ = x_sample.shape
    ssm_w = ssm_d.shape[1]
    groups = ssm_lambda_re.shape[1]
    kvw = N_KV_HEADS * HEAD_DIM
    assert cache_k_win.shape[2] == WINDOW and seq % WINDOW == 0 and n_p % 8 == 0 and n_s % 8 == 0

    l = 0
    na, nm, nb = (w[l].reshape(1, d) for w in (ffn_a_norm, mix_norm, ffn_b_norm))
    nf = final_norm.reshape(1, d)
    wga, wua, wda, win = (w[l].astype(BF16) for w in (ffn_a_gate, ffn_a_up, ffn_a_down, w_in))
    wgb, wub, wdb = (w[l].astype(BF16) for w in (ffn_b_gate, ffn_b_up, ffn_b_down))
    glua, glub, wo = (w[l].astype(BF16) for w in (glu_a, glu_b, w_out))
    sinks = attn_sinks[l]
    d_skip = ssm_d[l].reshape(1, ssm_w)

    abar_re, abar_im, bb_re_t, bb_im_t = _ssm_prep(
        ssm_lambda_re[l], ssm_lambda_im[l], ssm_log_dt[l],
        jnp.swapaxes(ssm_b_re[l], 1, 2), jnp.swapaxes(ssm_b_im[l], 1, 2))
    bd = jnp.concatenate([_block_diag(bb_re_t, SSM_CHUNKS), _block_diag(bb_im_t, SSM_CHUNKS)],
                         axis=2).astype(BF16)
    c_re_t = jnp.swapaxes(ssm_c_re[l], 1, 2)
    c_im_t = jnp.swapaxes(ssm_c_im[l], 1, 2)
    cd = jnp.concatenate([_block_diag(c_re_t, SSM_CHUNKS), -_block_diag(c_im_t, SSM_CHUNKS)],
                         axis=1).astype(BF16)
    a8 = jnp.broadcast_to(_state_to_lanes(abar_re[None], abar_im[None]), (8, 2 * groups * SSM_STATE))

    def run_group(x, n_seq, tiling, out_tiling, u_shape, attn_w, stacked, attn_fn, h0, tt):
        grid, tm, std, umap, n_sub = tiling
        x1, q, k, v, u_tb, ga, gs, *more = _ffn_in(x, grid, tm, n_sub, std, umap, u_shape,
                                                    na, wga, wua, wda, nm, win, attn_w, stacked)
        attn = attn_fn(q, k, v, *more)
        y_tb, h_t = _ssm(u_tb, h0, a8, d_skip, bd, cd, n_seq, tt,
                         seq_lanes=u_shape[1] != ssm_w, scan_unroll=tt)
        grid, tm, std, umap, n_sub = out_tiling
        y = _out(grid, tm, n_sub, std, umap, x1, attn, ga, gs, y_tb, glua, glub, wo, nb, wgb, wub, wdb, nf)
        return y, k, v, h_t

    def tiling_p(tm):
        nt = seq // tm

        def std(width):
            return (n_p * seq, width), pl.BlockSpec((tm, width), lambda b, i: (b * nt + i, 0))
        return (n_p, nt), tm, std, lambda b, i: (i, b), tm // 256

    tm_p = 512
    nt = seq // tm_p

    def attn_p(qst, k, v, kb, vt):
        return sinks, qst, kb, vt, nt

    w_kvt = _transpose_bf16(w_in[l][:, d:d + 2 * kvw])
    wkt = w_kvt[:kvw]
    wvt = w_kvt[kvw:]

    h0_p = jnp.zeros((n_p, 2 * groups * SSM_STATE), F32)
    def std_flat(width):
        return (n_p * seq, width), pl.BlockSpec((tm_p, width), lambda s: (s, 0))

    out_tiling_p = ((n_p * nt,), tm_p, std_flat, lambda s: (s % nt, s // nt), tm_p // 256)
    y_p, k_p, v_p, h_p = run_group(
        x_prompt.reshape(n_p * seq, d), n_p, tiling_p(tm_p), out_tiling_p,
        (seq, n_p * ssm_w), ([wvt], lambda b, i: (b * nt + i, 0)), True, attn_p, h0_p, 1024 // n_p)

    tm_s = 512

    def std_s(width):
        return (n_s * dec, width), pl.BlockSpec((tm_s, width), lambda i, j: (i, 0))

    def window_t(c):
        return jnp.transpose(c, (0, 2, 3, 1)).reshape(c.shape[0], kvw, WINDOW)

    def window(ct):
        return jnp.transpose(ct.reshape(ct.shape[0], N_KV_HEADS, HEAD_DIM, WINDOW), (0, 3, 1, 2))[None]

    new_windows = []

    def attn_s(q, kt, vt):
        o, kw, vw = _attn_sample(sinks, q, kt, vt, window_t(cache_k_win[l]), window_t(cache_v_win[l]),
                                 n_s, dec, WINDOW // dec, 4)
        new_windows.extend([kw, vw])
        return o

    h0_s = _state_to_lanes(state_ssm_re[l], state_ssm_im[l])
    tiling_s = ((n_s * dec // tm_s, 1), tm_s, std_s, lambda i, j: (i, 0), tm_s // 256)
    y_s, _, _, h_s = run_group(
        x_sample.reshape(n_s * dec, d), n_s, tiling_s, tiling_s,
        (n_s * dec, ssm_w), ([wkt, wvt], lambda i, j: (i, 0)), False, attn_s, h0_s, dec)

    sp_re, sp_im = _lanes_to_state(h_p, groups)
    ss_re, ss_im = _lanes_to_state(h_s, groups)
    return (y_p.reshape(n_p, seq, d), y_s.reshape(n_s, dec, d),
            window(k_p), window(v_p), window(new_windows[0]), window(new_windows[1]),
            sp_re[None], sp_im[None], ss_re[None], ss_im[None])
```

```python
import functools

import jax
import jax.numpy as jnp
from jax import lax
from jax.experimental import pallas as pl
from jax.experimental.pallas import tpu as pltpu

F32 = jnp.float32
BF16 = jnp.bfloat16

N_HEADS = 16
N_KV_HEADS = 4
HEAD_DIM = 64
Q_PER_KV = N_HEADS // N_KV_HEADS
WINDOW = 128
SSM_STATE = 64
RMS_EPS = 1e-6
NEG_BIG = -1e30
LOG2_E = 1.4426950408889634

MXU_COLS = 256
ROW_TILE = 512
SUB_TILE = 256
SSM_TILE_ROWS = 1024
ATTN_SAMPLE_UNROLL = 2
WINDOW_COLS = N_KV_HEADS * HEAD_DIM
ATTN_AHEAD = 2
ATTN_SLOTS = 4
SSM_CHUNKS = 4
VMEM_LIMIT = 62 * 1024 * 1024


def _rms(x, g):
    return x * lax.rsqrt(jnp.mean(x * x, axis=-1, keepdims=True) + RMS_EPS) * g


def _row_slices(tm, n_sub):
    rs = tm // n_sub
    return [slice(r * rs, (r + 1) * rs) for r in range(n_sub)]


def _swiglu(h_scr, wg_ref, wu_ref, wd_ref, act_scr, rows=(slice(None),), fillers=()):
    d_ff = wg_ref.shape[1]
    fillers = list(fillers)
    for c in range(d_ff // MXU_COLS):
        sl = slice(c * MXU_COLS, (c + 1) * MXU_COLS)
        for r in rows:
            g = jnp.dot(h_scr[r, :], wg_ref[:, sl], preferred_element_type=F32)
            u = jnp.dot(h_scr[r, :], wu_ref[:, sl], preferred_element_type=F32)
            act_scr[r, sl] = (jax.nn.silu(g) * u).astype(BF16)
            if fillers:
                fillers.pop(0)()
    for f in fillers:
        f()
    return [jnp.dot(act_scr[r, :], wd_ref[...], preferred_element_type=F32) for r in rows]


def _ffn_in_kernel(*refs, stacked, n_sub):
    if stacked:
        (x_ref, na_ref, wg_ref, wu_ref, wd_ref, nm_ref, win_ref, wvt_ref,
         x1_ref, q_ref, k_ref, v_ref, u_ref, ga_ref, gs_ref, kb_ref, vt_ref, h_scr, act_scr) = refs
        assert WINDOW_COLS == k_ref.shape[0]
    else:
        (x_ref, na_ref, wg_ref, wu_ref, wd_ref, nm_ref, win_ref, wkt_ref, wvt_ref,
         x1_ref, q_ref, k_ref, v_ref, u_ref, ga_ref, gs_ref, h_scr, act_scr) = refs
    tm, d = x_ref.shape
    rows = _row_slices(tm, n_sub)
    rs = tm // n_sub
    for r in rows:
        h_scr[r, :] = _rms(x_ref[r, :], na_ref[...]).astype(BF16)
    x1 = [x_ref[r, :] + 0.5 * f for r, f in zip(rows, _swiglu(h_scr, wg_ref, wu_ref, wd_ref, act_scr, rows))]
    for r, x in zip(rows, x1):
        x1_ref[r, :] = x
        h_scr[r, :] = _rms(x, nm_ref[...]).astype(BF16)

    def proj(w_ref, off, c, r):
        return jnp.dot(h_scr[r, :], w_ref[:, off + c * MXU_COLS: off + (c + 1) * MXU_COLS],
                       preferred_element_type=F32)

    q_scale = HEAD_DIM ** -0.5 * (LOG2_E if stacked else 1.0)
    pair_w = 2 * HEAD_DIM
    if stacked:
        assert Q_PER_KV * HEAD_DIM == MXU_COLS
        lane = lax.broadcasted_iota(jnp.int32, (rs, pair_w), 1)
        for g in range(N_KV_HEADS):
            keep = (lane < HEAD_DIM) if g % 2 == 0 else (lane >= HEAD_DIM)
            for ri, r in enumerate(rows):
                res = proj(win_ref, 0, g, r) * q_scale
                for rr in range(Q_PER_KV):
                    pair = res[:, (rr // 2) * pair_w:(rr // 2 + 1) * pair_w]
                    if rr % 2 != g % 2:
                        pair = pltpu.roll(pair, HEAD_DIM, axis=1)
                    piece = jnp.where(keep, pair, 0.0).astype(BF16)
                    for bl in range(rs // WINDOW):
                        row = (((ri * (rs // WINDOW) + bl) * N_KV_HEADS + g) * Q_PER_KV + rr) * WINDOW
                        q_ref[row:row + WINDOW, :] = piece[bl * WINDOW:(bl + 1) * WINDOW, :]
    else:
        for c in range(d // MXU_COLS):
            for r in rows:
                q_ref[r, c * MXU_COLS:(c + 1) * MXU_COLS] = (proj(win_ref, 0, c, r) * q_scale).astype(q_ref.dtype)
        for ref, wt_ref in ((k_ref, wkt_ref), (v_ref, wvt_ref)):
            for r in rows:
                ref[:, r] = lax.dot_general(wt_ref[...], h_scr[r, :], (((1,), (1,)), ((), ())),
                                            preferred_element_type=F32)
    off = d + 2 * WINDOW_COLS
    for ref in (u_ref, ga_ref, gs_ref):
        width = ref.shape[1]
        for c in range(width // MXU_COLS):
            for r in rows:
                ref[r, c * MXU_COLS:(c + 1) * MXU_COLS] = proj(win_ref, off, c, r).astype(ref.dtype)
        off += width
    if stacked:
        for r in rows:
            kb_ref[r, :] = proj(win_ref, d, 0, r).astype(kb_ref.dtype)
        for r in rows:
            vt_ref[:, r] = lax.dot_general(wvt_ref[...], h_scr[r, :], (((1,), (1,)), ((), ())),
                                           preferred_element_type=F32).astype(vt_ref.dtype)

        @pl.when(pl.program_id(1) == pl.num_programs(1) - 1)
        def _():
            tail = h_scr[tm - WINDOW:, :]
            for ref, off in ((k_ref, d), (v_ref, d + WINDOW_COLS)):
                ref[...] = jnp.dot(tail, win_ref[:, off:off + WINDOW_COLS], preferred_element_type=F32).T


def _const_spec(shape):
    nd = len(shape)
    return pl.BlockSpec(shape, lambda *_: (0,) * nd, pipeline_mode=pl.Buffered(1))


def _ffn_in(x2d, grid, tm, n_sub, std, umap, u_shape, na, wg, wu, wd, nm, win, attn_w, stacked):
    d = na.shape[1]
    d_ff = wg.shape[1]
    kv = N_KV_HEADS * HEAD_DIM
    ssm_w = win.shape[1] - 3 * d - 2 * kv
    n_tok = x2d.size // d
    extra_w, tile_map = attn_w
    t_spec = pl.BlockSpec((kv, tm), lambda *g: tile_map(*g)[::-1])
    out_shape, out_specs = [], []

    def add(shape_spec, dtype):
        out_shape.append(jax.ShapeDtypeStruct(shape_spec[0], dtype))
        out_specs.append(shape_spec[1])

    add(std(d), F32)
    if stacked:
        add(((n_tok * N_HEADS, 2 * HEAD_DIM), pl.BlockSpec((tm * N_HEADS, 2 * HEAD_DIM), tile_map)), BF16)
        for _ in range(2):
            add(((grid[0], kv, WINDOW), pl.BlockSpec((None, kv, WINDOW), lambda b, i: (b, 0, 0))), F32)
    else:
        add(std(d), F32)
        add(((kv, n_tok), t_spec), F32)
        add(((kv, n_tok), t_spec), F32)
    add((u_shape, pl.BlockSpec((tm, ssm_w), umap)), F32)
    add(std(d), F32)
    add(std(d), F32)
    weights = [na, wg, wu, wd, nm, win, *extra_w]
    if stacked:
        add(std(kv), BF16)
        add(((kv, n_tok), t_spec), BF16)
    return pl.pallas_call(
        functools.partial(_ffn_in_kernel, stacked=stacked, n_sub=n_sub),
        grid=grid,
        in_specs=[std(d)[1]] + [_const_spec(w.shape) for w in weights],
        out_specs=out_specs,
        out_shape=out_shape,
        scratch_shapes=[pltpu.VMEM((tm, d), BF16), pltpu.VMEM((tm, d_ff), BF16)],
        compiler_params=pltpu.CompilerParams(
            dimension_semantics=("parallel", "arbitrary"), vmem_limit_bytes=VMEM_LIMIT),
        name="ffn_in",
    )(x2d, *weights)


def _attention_pieces(sink_ref, q_ref, kp_ref, kc_ref, vp_ref, vc_ref, st_scr, o_ref, ctx_ok):
    nq = WINDOW
    nk = 2 * WINDOW
    lanes = Q_PER_KV * nq
    pair_w = 2 * HEAD_DIM
    n_blk = kc_ref.shape[0] // WINDOW
    n_slots = st_scr.shape[0]
    masks = {}

    def valid(bl):
        if bl not in masks:
            jk = lax.broadcasted_iota(jnp.int32, (nk, lanes), 0)
            iq = lax.broadcasted_iota(jnp.int32, (nk, lanes), 1) & (nq - 1)
            ok = True if bl > 0 else ctx_ok
            lo = iq if ok is True else jnp.maximum(iq, jnp.where(ok, 0, WINDOW))
            masks[bl] = (jk >= lo) & (jk <= iq + WINDOW)
        return masks[bl]

    def keys(bl, g):
        cols = slice((g // 2) * pair_w, (g // 2 + 1) * pair_w)
        if bl == 0:
            return jnp.concatenate([kp_ref[:, cols], kc_ref[0:WINDOW, cols]], axis=0)
        return kc_ref[(bl - 1) * WINDOW:(bl + 1) * WINDOW, cols]

    def vals_t(bl, g):
        rows = slice(g * HEAD_DIM, (g + 1) * HEAD_DIM)
        if bl == 0:
            return jnp.concatenate([vp_ref[rows, :], vc_ref[rows, 0:WINDOW]], axis=1)
        return vc_ref[rows, (bl - 1) * WINDOW:(bl + 1) * WINDOW]

    def score(bl, g):
        k = bl * N_KV_HEADS + g
        st = lax.dot_general(keys(bl, g), q_ref[k * lanes:(k + 1) * lanes, :], (((1,), (1,)), ((), ())),
                             preferred_element_type=F32)
        st_scr[k % n_slots] = jnp.where(valid(bl), st, NEG_BIG)

    def attend(bl, g):
        st = st_scr[(bl * N_KV_HEADS + g) % n_slots]
        sink = jnp.concatenate(
            [jnp.full((1, nq), sink_ref[g * Q_PER_KV + r] * LOG2_E, F32) for r in range(Q_PER_KV)], axis=1)
        m = jnp.maximum(jnp.max(st, axis=0, keepdims=True), sink)
        p = jnp.exp2(st - m)
        denom = jnp.sum(p, axis=0, keepdims=True) + jnp.exp2(sink - m)
        ot = jnp.dot(vals_t(bl, g), p.astype(BF16), preferred_element_type=F32) * (1.0 / denom)
        for pr in range(Q_PER_KV // 2):
            two = jnp.concatenate([ot[:, (2 * pr) * nq:(2 * pr + 1) * nq],
                                   ot[:, (2 * pr + 1) * nq:(2 * pr + 2) * nq]], axis=0)
            col = (g * Q_PER_KV + 2 * pr) * HEAD_DIM
            o_ref[bl * nq:(bl + 1) * nq, col:col + pair_w] = two.T

    order = [(bl, g) for bl in range(n_blk) for g in range(N_KV_HEADS)]
    return ([functools.partial(score, bl, g) for bl, g in order],
            [functools.partial(attend, bl, g) for bl, g in order])


def _staggered(score, attend, ahead):
    def piece(k):
        def run():
            if k + ahead < len(score):
                score[k + ahead]()
            attend[k]()
        return run

    def first():
        for f in score[:ahead]:
            f()
    return first, [piece(k) for k in range(len(attend))]


def _attn_sample_kernel(sink_ref, q_ref, knt_ref, vnt_ref, ckt_ref, cvt_ref, o_ref, kw_ref, vw_ref, *, n_sub, tn, unroll):
    pair_w = 2 * HEAD_DIM
    n_pairs = N_HEADS // 2
    rows = n_pairs * tn
    kv = N_KV_HEADS * HEAD_DIM
    old = WINDOW - tn
    assert tn & (tn - 1) == 0, "row -> token index uses a power-of-two mask"
    low = lax.broadcasted_iota(jnp.int32, (tn, pair_w), 1) < HEAD_DIM
    tq = lax.broadcasted_iota(jnp.int32, (2 * rows, 2 * WINDOW), 0) & (tn - 1)
    col = lax.broadcasted_iota(jnp.int32, (2 * rows, 2 * WINDOW), 1)
    valid = ((col < WINDOW) & (col >= tq)) | ((col >= WINDOW + old) & (col - WINDOW - old <= tq))
    is_new = lax.broadcasted_iota(jnp.int32, (kv, WINDOW), 1) >= old
    zeros = jnp.zeros((tn, pair_w), F32)
    nt = (((1,), (1,)), ((), ()))
    assert n_sub * tn == WINDOW

    sink = jnp.concatenate([jnp.full((tn, 1), sink_ref[2 * pr + half], F32)
                            for half in range(2) for pr in range(n_pairs)], axis=0)

    def scores(s):
        r0 = pl.multiple_of(s * tn, tn)
        q = q_ref[pl.ds(r0, tn), :]
        kt = ckt_ref[s]
        vt = cvt_ref[s]
        shift = (old - r0) & (WINDOW - 1)
        knt = jnp.where(is_new, pltpu.roll(knt_ref[...], shift, axis=1), 0.0)
        vnt = jnp.where(is_new, pltpu.roll(vnt_ref[...], shift, axis=1), 0.0)
        kw_ref[s] = jnp.where(is_new, knt, pltpu.roll(kt, old, axis=1))
        vw_ref[s] = jnp.where(is_new, vnt, pltpu.roll(vt, old, axis=1))
        k_all = jnp.concatenate([kt, knt], axis=1).astype(BF16)
        v_all = jnp.concatenate([vt, vnt], axis=1).astype(BF16)
        blocks = []
        for half in range(2):
            for g in range(N_KV_HEADS):
                for pp in range(Q_PER_KV // 2):
                    piece = q[:, (2 * g + pp) * pair_w:(2 * g + pp + 1) * pair_w]
                    if half != g % 2:
                        piece = pltpu.roll(piece, HEAD_DIM, axis=1)
                    piece = jnp.where(low if g % 2 == 0 else ~low, piece, 0.0)
                    blocks.append(jnp.concatenate([piece, zeros] if g // 2 == 0 else [zeros, piece], axis=1))
        qh = jnp.concatenate(blocks, axis=0).astype(BF16)
        return r0, v_all, jnp.where(valid, jnp.dot(qh, k_all, preferred_element_type=F32), NEG_BIG)

    def attend(r0, v_all, sc):
        m = jnp.maximum(jnp.max(sc, axis=-1, keepdims=True), sink)
        p = jnp.exp(sc - m)
        rden = 1.0 / (jnp.sum(p, axis=-1, keepdims=True) + jnp.exp(sink - m))
        o = lax.dot_general((p * rden).astype(BF16), v_all, nt, preferred_element_type=F32)
        for pr in range(n_pairs):
            g = pr // (Q_PER_KV // 2)
            sel = []
            for half in range(2):
                blk = o[half * rows + pr * tn:half * rows + (pr + 1) * tn, (g // 2) * pair_w:(g // 2 + 1) * pair_w]
                sel.append(blk if half == g % 2 else pltpu.roll(blk, HEAD_DIM, axis=1))
            o_ref[pl.ds(r0, tn), pr * pair_w:(pr + 1) * pair_w] = jnp.where(low, sel[0], sel[1])

    def some_sequences(i, carry):
        staged = [scores(i * unroll + u) for u in range(unroll)]
        for st in staged:
            attend(*st)
        return carry

    lax.fori_loop(0, n_sub // unroll, some_sequences, 0)


def _attn_sample(sinks, q, knt, vnt, cache_kt, cache_vt, n_seq, tn, n_sub, unroll):
    d = q.shape[1]
    kv = knt.shape[0]
    rows = lambda i: (i, 0)
    seqs = lambda i: (i, 0, 0)
    win = pl.BlockSpec((n_sub, kv, WINDOW), seqs)
    new = pl.BlockSpec((kv, n_sub * tn), lambda i: (0, i))
    return pl.pallas_call(
        functools.partial(_attn_sample_kernel, n_sub=n_sub, tn=tn, unroll=unroll),
        grid=(n_seq // n_sub,),
        in_specs=[pl.BlockSpec(memory_space=pltpu.SMEM),
                  pl.BlockSpec((n_sub * tn, d), rows),
                  new, new, win, win],
        out_specs=[pl.BlockSpec((n_sub * tn, d), rows), win, win],
        out_shape=[jax.ShapeDtypeStruct(q.shape, F32),
                   jax.ShapeDtypeStruct(cache_kt.shape, F32), jax.ShapeDtypeStruct(cache_vt.shape, F32)],
        compiler_params=pltpu.CompilerParams(
            dimension_semantics=("parallel",), vmem_limit_bytes=VMEM_LIMIT),
        name="attn_sample",
    )(sinks, q, knt, vnt, cache_kt, cache_vt)


def _ssm_prep_kernel(lr_ref, li_ref, ldt_ref, br_ref, bi_ref, ar_ref, ai_ref, bbr_ref, bbi_ref):
    lr = lr_ref[...]
    li = li_ref[...]
    dt = jnp.exp(ldt_ref[...])
    mag = jnp.exp(lr * dt)
    ang = li * dt
    abar_re = mag * jnp.cos(ang)
    abar_im = mag * jnp.sin(ang)
    den = lr * lr + li * li
    nr = abar_re - 1.0
    fr = (nr * lr + abar_im * li) / den
    fi = (abar_im * lr - nr * li) / den
    ar_ref[...] = abar_re
    ai_ref[...] = abar_im
    br = br_ref[...]
    bi = bi_ref[...]
    bbr_ref[...] = fr[:, None, :] * br - fi[:, None, :] * bi
    bbi_ref[...] = fr[:, None, :] * bi + fi[:, None, :] * br


def _ssm_prep(lam_re, lam_im, log_dt, b_re_t, b_im_t):
    g, n = lam_re.shape
    return pl.pallas_call(
        _ssm_prep_kernel,
        out_shape=[jax.ShapeDtypeStruct((g, n), F32), jax.ShapeDtypeStruct((g, n), F32),
                   jax.ShapeDtypeStruct(b_re_t.shape, F32), jax.ShapeDtypeStruct(b_im_t.shape, F32)],
        name="ssm_prep",
    )(lam_re, lam_im, log_dt.reshape(g, 1), b_re_t, b_im_t)


def _ssm_kernel(u_ref, h0_ref, a_ref, d_ref, bd_ref, cd_ref, y_ref, hT_ref, xh_scr, u_scr, y_scr,
                *, n_seq, tt, seq_lanes, scan_unroll):
    cw = xh_scr.shape[1] // SSM_CHUNKS
    hw = cw // 2
    uc = u_scr.shape[2]
    ssm_w = SSM_CHUNKS * uc

    @pl.when(pl.program_id(0) == 0)
    def _():
        hT_ref[...] = h0_ref[...]

    for j in range(SSM_CHUNKS):
        for b in range(n_seq):
            if seq_lanes:
                u_b = u_ref[:, b * ssm_w + j * uc:b * ssm_w + (j + 1) * uc]
            else:
                u_b = u_ref[b * tt:(b + 1) * tt, j * uc:(j + 1) * uc]
            u_scr[j, pl.ds(b, tt, stride=n_seq), :] = u_b

    def project_in(j):
        xh_scr[:, j * cw:(j + 1) * cw] = jnp.dot(u_scr[j].astype(BF16), bd_ref[j], preferred_element_type=F32)

    def scan(j):
        re = slice(j * cw, j * cw + hw)
        im = slice(j * cw + hw, (j + 1) * cw)
        ar = a_ref[:, re]
        ai = a_ref[:, im]

        def seq_group(s, carry):
            s8 = pl.multiple_of(s * 8, 8)

            def step(t, h):
                hr, hi = h
                r0 = pl.multiple_of(t * n_seq + s8, 8)
                nr = ar * hr - ai * hi + xh_scr[pl.ds(r0, 8), re]
                ni = ar * hi + ai * hr + xh_scr[pl.ds(r0, 8), im]
                xh_scr[pl.ds(r0, 8), re] = nr
                xh_scr[pl.ds(r0, 8), im] = ni
                return nr, ni

            hr, hi = lax.fori_loop(0, tt, step, (hT_ref[pl.ds(s8, 8), re], hT_ref[pl.ds(s8, 8), im]),
                                   unroll=scan_unroll)
            hT_ref[pl.ds(s8, 8), re] = hr
            hT_ref[pl.ds(s8, 8), im] = hi
            return carry

        lax.fori_loop(0, n_seq // 8, seq_group, 0)

    def project_out(j):
        y = jnp.dot(xh_scr[:, j * cw:(j + 1) * cw].astype(BF16), cd_ref[j], preferred_element_type=F32)
        cs = slice(j * uc, (j + 1) * uc)
        y_scr[j] = jax.nn.gelu(y + d_ref[:, cs] * u_scr[j])
        for b in range(n_seq):
            y_b = y_scr[j, pl.ds(b, tt, stride=n_seq), :].astype(y_ref.dtype)
            if seq_lanes:
                y_ref[:, b * ssm_w + j * uc:b * ssm_w + (j + 1) * uc] = y_b
            else:
                y_ref[b * tt:(b + 1) * tt, cs] = y_b

    for stage in range(SSM_CHUNKS + 2):
        if stage < SSM_CHUNKS:
            project_in(stage)
        if 1 <= stage <= SSM_CHUNKS:
            scan(stage - 1)
        if stage >= 2:
            project_out(stage - 2)


def _ssm(u, h0, a8, d_skip, bd, cd, n_seq, tt, seq_lanes, scan_unroll):
    ssm_w = d_skip.shape[1]
    tile = n_seq * tt
    n_state = h0.shape[1]
    block = (tt, n_seq * ssm_w) if seq_lanes else (tile, ssm_w)
    assert seq_lanes or u.shape[0] == tile, "row order (seq, t) cannot be tiled over time"
    slab = pltpu.VMEM((SSM_CHUNKS, tile, ssm_w // SSM_CHUNKS), F32)
    return pl.pallas_call(
        functools.partial(_ssm_kernel, n_seq=n_seq, tt=tt, seq_lanes=seq_lanes, scan_unroll=scan_unroll),
        grid=(u.shape[0] // block[0],),
        in_specs=[pl.BlockSpec(block, lambda i: (i, 0)),
                  _const_spec(h0.shape), _const_spec(a8.shape), _const_spec(d_skip.shape),
                  _const_spec(bd.shape), _const_spec(cd.shape)],
        out_specs=[pl.BlockSpec(block, lambda i: (i, 0)),
                   pl.BlockSpec(h0.shape, lambda i: (0, 0))],
        out_shape=[jax.ShapeDtypeStruct(u.shape, BF16), jax.ShapeDtypeStruct(h0.shape, F32)],
        scratch_shapes=[pltpu.VMEM((tile, n_state), F32), slab, slab],
        compiler_params=pltpu.CompilerParams(
            dimension_semantics=("arbitrary",), vmem_limit_bytes=VMEM_LIMIT),
        name="ssm",
    )(u, h0, a8, d_skip, bd, cd)


def _out_kernel(*refs, n_sub, tiles_per_seq):
    if tiles_per_seq:
        (sink_ref, x1_ref, ga_ref, gs_ref, y_ref, q0_ref, k0_ref, v0_ref, qn_ref, kp_ref, kn_ref, vp_ref, vn_ref,
         glua_ref, glub_ref, wo_ref, nb_ref, wg_ref, wu_ref, wd_ref, nf_ref, o_ref,
         h_scr, act_scr, at_ref, st_scr) = refs
        step = pl.program_id(0)

        @pl.when(step == 0)
        def _():
            score, attend = _attention_pieces(sink_ref, q0_ref, k0_ref.at[pl.ds(0, WINDOW)], k0_ref,
                                              v0_ref.at[:, pl.ds(0, WINDOW)], v0_ref, st_scr, at_ref, False)
            first, rest = _staggered(score, attend, ATTN_AHEAD)
            first()
            for f in rest:
                f()

        nxt = jnp.minimum(step + 1, pl.num_programs(0) - 1)
        score, attend = _attention_pieces(sink_ref, qn_ref, kp_ref, kn_ref, vp_ref, vn_ref, st_scr, at_ref,
                                          nxt % tiles_per_seq != 0)
        first, fillers = _staggered(score, attend, ATTN_AHEAD)
    else:
        (x1_ref, at_ref, ga_ref, gs_ref, y_ref, glua_ref, glub_ref, wo_ref,
         nb_ref, wg_ref, wu_ref, wd_ref, nf_ref, o_ref, h_scr, act_scr) = refs
        first, fillers = (lambda: None), ()
    rows = _row_slices(x1_ref.shape[0], n_sub)
    ssm = [jnp.dot(y_ref[r, :], glua_ref[...], preferred_element_type=F32) * jax.nn.sigmoid(
        jnp.dot(y_ref[r, :], glub_ref[...], preferred_element_type=F32)) for r in rows]
    merged = [(jax.nn.sigmoid(ga_ref[r, :]) * at_ref[r, :] + jax.nn.sigmoid(gs_ref[r, :]) * s).astype(BF16)
              for r, s in zip(rows, ssm)]
    x2 = [x1_ref[r, :] + jnp.dot(m, wo_ref[...], preferred_element_type=F32) for r, m in zip(rows, merged)]
    first()
    for r, x in zip(rows, x2):
        h_scr[r, :] = _rms(x, nb_ref[...]).astype(BF16)
    for r, x, f in zip(rows, x2, _swiglu(h_scr, wg_ref, wu_ref, wd_ref, act_scr, rows, fillers)):
        o_ref[r, :] = _rms(x + 0.5 * f, nf_ref[...])


def _out(grid, tm, n_sub, std, ymap, x1, attn, ga, gs, y_tb, glua, glub, wo, nb, wg, wu, wd, nf):
    d = nb.shape[1]
    d_ff = wg.shape[1]
    kv = N_KV_HEADS * HEAD_DIM
    shp, spec = std(d)
    weights = [glua, glub, wo, nb, wg, wu, wd, nf]
    scratch = [pltpu.VMEM((tm, d), BF16), pltpu.VMEM((tm, d_ff), BF16)]
    fused = isinstance(attn, tuple)
    if fused:
        sinks, qst, kb, vt, tiles_per_seq = attn
        (n_tiles,) = grid
        blk = tm // WINDOW
        nxt = lambda s: jnp.minimum(s + 1, n_tiles - 1)
        before = lambda s: jnp.maximum(nxt(s) * blk - 1, 0)
        once = dict(pipeline_mode=pl.Buffered(1))
        acts = [sinks, x1, ga, gs, y_tb, qst, kb, vt, qst, kb, kb, vt, vt]
        act_specs = [pl.BlockSpec(memory_space=pltpu.SMEM), spec, spec, spec, pl.BlockSpec((tm, glua.shape[0]), ymap),
                     pl.BlockSpec((tm * N_HEADS, 2 * HEAD_DIM), lambda s: (0, 0), **once),
                     pl.BlockSpec((tm, kv), lambda s: (0, 0), **once),
                     pl.BlockSpec((kv, tm), lambda s: (0, 0), **once),
                     pl.BlockSpec((tm * N_HEADS, 2 * HEAD_DIM), lambda s: (nxt(s), 0)),
                     pl.BlockSpec((WINDOW, kv), lambda s: (before(s), 0)),
                     pl.BlockSpec((tm, kv), lambda s: (nxt(s), 0)),
                     pl.BlockSpec((kv, WINDOW), lambda s: (0, before(s))),
                     pl.BlockSpec((kv, tm), lambda s: (0, nxt(s)))]
        scratch += [pltpu.VMEM((tm, d), F32), pltpu.VMEM((ATTN_SLOTS, 2 * WINDOW, Q_PER_KV * WINDOW), F32)]
    else:
        tiles_per_seq = 0
        acts = [x1, attn, ga, gs, y_tb]
        act_specs = [spec, spec, spec, spec, pl.BlockSpec((tm, glua.shape[0]), ymap)]
    return pl.pallas_call(
        functools.partial(_out_kernel, n_sub=n_sub, tiles_per_seq=tiles_per_seq),
        grid=grid,
        in_specs=act_specs + [_const_spec(w.shape) for w in weights],
        out_specs=spec,
        out_shape=jax.ShapeDtypeStruct(shp, F32),
        scratch_shapes=scratch,
        compiler_params=pltpu.CompilerParams(
            dimension_semantics=("arbitrary",) * len(grid), vmem_limit_bytes=VMEM_LIMIT),
        name="out",
    )(*acts, *weights)


def _transpose_bf16(w):
    def body(w_ref, o_ref):
        o_ref[...] = w_ref[...].T.astype(BF16)
    return pl.pallas_call(body, out_shape=jax.ShapeDtypeStruct(w.shape[::-1], BF16), name="transpose_bf16")(w)


def _block_diag(w, chunks):
    g, a, b = w.shape
    gl = g // chunks
    w = w.reshape(chunks, gl, a, b)
    eye = jnp.eye(gl, dtype=w.dtype)
    return (w[:, :, :, None, :] * eye[None, :, None, :, None]).reshape(chunks, gl * a, gl * b)


def _state_to_lanes(re, im):
    s = re.shape[0]
    re = re.reshape(s, SSM_CHUNKS, -1)
    im = im.reshape(s, SSM_CHUNKS, -1)
    return jnp.stack([re, im], axis=2).reshape(s, -1)


def _lanes_to_state(h, groups):
    s = h.shape[0]
    h = h.reshape(s, SSM_CHUNKS, 2, -1)
    return h[:, :, 0].reshape(s, groups, -1), h[:, :, 1].reshape(s, groups, -1)


def kernel(x_prompt, x_sample, cache_k_win, cache_v_win, state_ssm_re, state_ssm_im, ffn_a_norm, ffn_a_gate, ffn_a_up, ffn_a_down, mix_norm, w_in, attn_sinks, ssm_lambda_re, ssm_lambda_im, ssm_log_dt, ssm_b_re, ssm_b_im, ssm_c_re, ssm_c_im, ssm_d, glu_a, glu_b, w_out, ffn_b_norm, ffn_b_gate, ffn_b_up, ffn_b_down, final_norm):
    depth = ffn_a_norm.shape[0]
    assert depth == 1, "single-layer trunk"
    n_p, seq, d = x_prompt.shape
    n_s, dec, _ = x_sample.shape
    ssm_w = ssm_d.shape[1]
    groups = ssm_lambda_re.shape[1]
    kvw = N_KV_HEADS * HEAD_DIM
    assert cache_k_win.shape[2] == WINDOW and seq % WINDOW == 0 and n_p % 8 == 0 and n_s % 8 == 0

    l = 0
    na, nm, nb = (w[l].reshape(1, d) for w in (ffn_a_norm, mix_norm, ffn_b_norm))
    nf = final_norm.reshape(1, d)
    wga, wua, wda, win = (w[l].astype(BF16) for w in (ffn_a_gate, ffn_a_up, ffn_a_down, w_in))
    wgb, wub, wdb = (w[l].astype(BF16) for w in (ffn_b_gate, ffn_b_up, ffn_b_down))
    glua, glub, wo = (w[l].astype(BF16) for w in (glu_a, glu_b, w_out))
    sinks = attn_sinks[l]
    d_skip = ssm_d[l].reshape(1, ssm_w)

    abar_re, abar_im, bb_re_t, bb_im_t = _ssm_prep(
        ssm_lambda_re[l], ssm_lambda_im[l], ssm_log_dt[l],
        jnp.swapaxes(ssm_b_re[l], 1, 2), jnp.swapaxes(ssm_b_im[l], 1, 2))
    bd = jnp.concatenate([_block_diag(bb_re_t, SSM_CHUNKS), _block_diag(bb_im_t, SSM_CHUNKS)],
                         axis=2).astype(BF16)
    c_re_t = jnp.swapaxes(ssm_c_re[l], 1, 2)
    c_im_t = jnp.swapaxes(ssm_c_im[l], 1, 2)
    cd = jnp.concatenate([_block_diag(c_re_t, SSM_CHUNKS), -_block_diag(c_im_t, SSM_CHUNKS)],
                         axis=1).astype(BF16)
    a8 = jnp.broadcast_to(_state_to_lanes(abar_re[None], abar_im[None]), (8, 2 * groups * SSM_STATE))

    def run_group(x, n_seq, tiling, out_tiling, u_shape, attn_w, stacked, attn_fn, h0, tt):
        grid, tm, std, umap, n_sub = tiling
        x1, q, k, v, u_tb, ga, gs, *more = _ffn_in(x, grid, tm, n_sub, std, umap, u_shape,
                                                    na, wga, wua, wda, nm, win, attn_w, stacked)
        attn = attn_fn(q, k, v, *more)
        y_tb, h_t = _ssm(u_tb, h0, a8, d_skip, bd, cd, n_seq, tt,
                         seq_lanes=u_shape[1] != ssm_w, scan_unroll=tt)
        grid, tm, std, umap, n_sub = out_tiling
        y = _out(grid, tm, n_sub, std, umap, x1, attn, ga, gs, y_tb, glua, glub, wo, nb, wgb, wub, wdb, nf)
        return y, k, v, h_t

    def tiling_p(tm):
        nt = seq // tm

        def std(width):
            return (n_p * seq, width), pl.BlockSpec((tm, width), lambda b, i: (b * nt + i, 0))
        return (n_p, nt), tm, std, lambda b, i: (i, b), tm // SUB_TILE

    tm_p = ROW_TILE
    nt = seq // tm_p

    def attn_p(qst, k, v, kb, vt):
        return sinks, qst, kb, vt, nt

    w_kvt = _transpose_bf16(w_in[l][:, d:d + 2 * kvw])
    wkt = w_kvt[:kvw]
    wvt = w_kvt[kvw:]

    h0_p = jnp.zeros((n_p, 2 * groups * SSM_STATE), F32)
    def std_flat(width):
        return (n_p * seq, width), pl.BlockSpec((tm_p, width), lambda s: (s, 0))

    out_tiling_p = ((n_p * nt,), tm_p, std_flat, lambda s: (s % nt, s // nt), tm_p // SUB_TILE)
    y_p, k_p, v_p, h_p = run_group(
        x_prompt.reshape(n_p * seq, d), n_p, tiling_p(tm_p), out_tiling_p,
        (seq, n_p * ssm_w), ([wvt], lambda b, i: (b * nt + i, 0)), True, attn_p, h0_p, SSM_TILE_ROWS // n_p)

    tm_s = ROW_TILE

    def std_s(width):
        return (n_s * dec, width), pl.BlockSpec((tm_s, width), lambda i, j: (i, 0))

    def window_t(c):
        return jnp.transpose(c, (0, 2, 3, 1)).reshape(c.shape[0], kvw, WINDOW)

    def window(ct):
        return jnp.transpose(ct.reshape(ct.shape[0], N_KV_HEADS, HEAD_DIM, WINDOW), (0, 3, 1, 2))[None]

    new_windows = []

    def attn_s(q, kt, vt):
        o, kw, vw = _attn_sample(sinks, q, kt, vt, window_t(cache_k_win[l]), window_t(cache_v_win[l]),
                                 n_s, dec, WINDOW // dec, ATTN_SAMPLE_UNROLL)
        new_windows.extend([kw, vw])
        return o

    h0_s = _state_to_lanes(state_ssm_re[l], state_ssm_im[l])
    tiling_s = ((n_s * dec // tm_s, 1), tm_s, std_s, lambda i, j: (i, 0), tm_s // SUB_TILE)
    y_s, _, _, h_s = run_group(
        x_sample.reshape(n_s * dec, d), n_s, tiling_s, tiling_s,
        (n_s * dec, ssm_w), ([wkt, wvt], lambda i, j: (i, 0)), False, attn_s, h0_s, dec)

    sp_re, sp_im = _lanes_to_state(h_p, groups)
    ss_re, ss_im = _lanes_to_state(h_s, groups)
    return (y_p.reshape(n_p, seq, d), y_s.reshape(n_s, dec, d),
            window(k_p), window(v_p), window(new_windows[0]), window(new_windows[1]),
            sp_re[None], sp_im[None], ss_re[None], ss_im[None])
```

```python
import functools

import jax
import jax.numpy as jnp
from jax import lax
from jax.experimental import pallas as pl
from jax.experimental.pallas import tpu as pltpu

F32 = jnp.float32
BF16 = jnp.bfloat16

N_HEADS = 16
N_KV_HEADS = 4
HEAD_DIM = 64
Q_PER_KV = N_HEADS // N_KV_HEADS
WINDOW = 128
SSM_STATE = 64
RMS_EPS = 1e-6
NEG_BIG = -1e30
LOG2_E = 1.4426950408889634

MXU_COLS = 256
ROW_TILE = 512
SUB_TILE = 256
SSM_TILE_ROWS = 1024
ATTN_SAMPLE_UNROLL = 2
WINDOW_COLS = N_KV_HEADS * HEAD_DIM
ATTN_AHEAD = 2
ATTN_SLOTS = 4
SSM_CHUNKS = 4
VMEM_LIMIT = 62 * 1024 * 1024


def _rms(x, g):
    return x * lax.rsqrt(jnp.mean(x * x, axis=-1, keepdims=True) + RMS_EPS) * g


def _row_slices(tm, n_sub):
    rs = tm // n_sub
    return [slice(r * rs, (r + 1) * rs) for r in range(n_sub)]


def _swiglu(h_scr, wg_ref, wu_ref, wd_ref, act_scr, rows=(slice(None),), fillers=()):
    d_ff = wg_ref.shape[1]
    fillers = list(fillers)
    for c in range(d_ff // MXU_COLS):
        sl = slice(c * MXU_COLS, (c + 1) * MXU_COLS)
        for r in rows:
            g = jnp.dot(h_scr[r, :], wg_ref[:, sl], preferred_element_type=F32)
            u = jnp.dot(h_scr[r, :], wu_ref[:, sl], preferred_element_type=F32)
            act_scr[r, sl] = (jax.nn.silu(g) * u).astype(BF16)
            if fillers:
                fillers.pop(0)()
    for f in fillers:
        f()
    return [jnp.dot(act_scr[r, :], wd_ref[...], preferred_element_type=F32) for r in rows]


def _ffn_in_kernel(*refs, stacked, n_sub):
    if stacked:
        (x_ref, na_ref, wg_ref, wu_ref, wd_ref, nm_ref, win_ref, wvt_ref,
         x1_ref, q_ref, k_ref, v_ref, u_ref, ga_ref, gs_ref, kb_ref, vt_ref, h_scr, act_scr) = refs
        assert WINDOW_COLS == k_ref.shape[0]
    else:
        (x_ref, na_ref, wg_ref, wu_ref, wd_ref, nm_ref, win_ref, wkt_ref, wvt_ref,
         x1_ref, q_ref, k_ref, v_ref, u_ref, ga_ref, gs_ref, h_scr, act_scr) = refs
    tm, d = x_ref.shape
    rows = _row_slices(tm, n_sub)
    rs = tm // n_sub
    for r in rows:
        h_scr[r, :] = _rms(x_ref[r, :], na_ref[...]).astype(BF16)
    x1 = [x_ref[r, :] + 0.5 * f for r, f in zip(rows, _swiglu(h_scr, wg_ref, wu_ref, wd_ref, act_scr, rows))]
    for r, x in zip(rows, x1):
        x1_ref[r, :] = x
        h_scr[r, :] = _rms(x, nm_ref[...]).astype(BF16)

    def proj(w_ref, off, c, r):
        return jnp.dot(h_scr[r, :], w_ref[:, off + c * MXU_COLS: off + (c + 1) * MXU_COLS],
                       preferred_element_type=F32)

    q_scale = HEAD_DIM ** -0.5 * (LOG2_E if stacked else 1.0)
    pair_w = 2 * HEAD_DIM
    if stacked:
        assert Q_PER_KV * HEAD_DIM == MXU_COLS
        lane = lax.broadcasted_iota(jnp.int32, (rs, pair_w), 1)
        for g in range(N_KV_HEADS):
            keep = (lane < HEAD_DIM) if g % 2 == 0 else (lane >= HEAD_DIM)
            for ri, r in enumerate(rows):
                res = proj(win_ref, 0, g, r) * q_scale
                for rr in range(Q_PER_KV):
                    pair = res[:, (rr // 2) * pair_w:(rr // 2 + 1) * pair_w]
                    if rr % 2 != g % 2:
                        pair = pltpu.roll(pair, HEAD_DIM, axis=1)
                    piece = jnp.where(keep, pair, 0.0).astype(BF16)
                    for bl in range(rs // WINDOW):
                        row = (((ri * (rs // WINDOW) + bl) * N_KV_HEADS + g) * Q_PER_KV + rr) * WINDOW
                        q_ref[row:row + WINDOW, :] = piece[bl * WINDOW:(bl + 1) * WINDOW, :]
    else:
        for c in range(d // MXU_COLS):
            for r in rows:
                q_ref[r, c * MXU_COLS:(c + 1) * MXU_COLS] = (proj(win_ref, 0, c, r) * q_scale).astype(q_ref.dtype)
        for ref, wt_ref in ((k_ref, wkt_ref), (v_ref, wvt_ref)):
            for r in rows:
                ref[:, r] = lax.dot_general(wt_ref[...], h_scr[r, :], (((1,), (1,)), ((), ())),
                                            preferred_element_type=F32)
    off = d + 2 * WINDOW_COLS
    for ref in (u_ref, ga_ref, gs_ref):
        width = ref.shape[1]
        for c in range(width // MXU_COLS):
            for r in rows:
                ref[r, c * MXU_COLS:(c + 1) * MXU_COLS] = proj(win_ref, off, c, r).astype(ref.dtype)
        off += width
    if stacked:
        for r in rows:
            kb_ref[r, :] = proj(win_ref, d, 0, r).astype(kb_ref.dtype)
        for r in rows:
            vt_ref[:, r] = lax.dot_general(wvt_ref[...], h_scr[r, :], (((1,), (1,)), ((), ())),
                                           preferred_element_type=F32).astype(vt_ref.dtype)

        @pl.when(pl.program_id(1) == pl.num_programs(1) - 1)
        def _():
            tail = h_scr[tm - WINDOW:, :]
            for ref, off in ((k_ref, d), (v_ref, d + WINDOW_COLS)):
                ref[...] = jnp.dot(tail, win_ref[:, off:off + WINDOW_COLS], preferred_element_type=F32).T


def _const_spec(shape):
    nd = len(shape)
    return pl.BlockSpec(shape, lambda *_: (0,) * nd, pipeline_mode=pl.Buffered(1))


def _ffn_in(x2d, grid, tm, n_sub, std, umap, u_shape, na, wg, wu, wd, nm, win, attn_w, stacked):
    d = na.shape[1]
    d_ff = wg.shape[1]
    kv = N_KV_HEADS * HEAD_DIM
    ssm_w = win.shape[1] - 3 * d - 2 * kv
    n_tok = x2d.size // d
    extra_w, tile_map = attn_w
    t_spec = pl.BlockSpec((kv, tm), lambda *g: tile_map(*g)[::-1])
    out_shape, out_specs = [], []

    def add(shape_spec, dtype):
        out_shape.append(jax.ShapeDtypeStruct(shape_spec[0], dtype))
        out_specs.append(shape_spec[1])

    add(std(d), F32)
    if stacked:
        add(((n_tok * N_HEADS, 2 * HEAD_DIM), pl.BlockSpec((tm * N_HEADS, 2 * HEAD_DIM), tile_map)), BF16)
        for _ in range(2):
            add(((grid[0], kv, WINDOW), pl.BlockSpec((None, kv, WINDOW), lambda b, i: (b, 0, 0))), F32)
    else:
        add(std(d), F32)
        add(((kv, n_tok), t_spec), F32)
        add(((kv, n_tok), t_spec), F32)
    add((u_shape, pl.BlockSpec((tm, ssm_w), umap)), F32)
    add(std(d), BF16)
    add(std(d), BF16)
    weights = [na, wg, wu, wd, nm, win, *extra_w]
    if stacked:
        add(std(kv), BF16)
        add(((kv, n_tok), t_spec), BF16)
    return pl.pallas_call(
        functools.partial(_ffn_in_kernel, stacked=stacked, n_sub=n_sub),
        grid=grid,
        in_specs=[std(d)[1]] + [_const_spec(w.shape) for w in weights],
        out_specs=out_specs,
        out_shape=out_shape,
        scratch_shapes=[pltpu.VMEM((tm, d), BF16), pltpu.VMEM((tm, d_ff), BF16)],
        compiler_params=pltpu.CompilerParams(
            dimension_semantics=("parallel", "arbitrary"), vmem_limit_bytes=VMEM_LIMIT),
        name="ffn_in",
    )(x2d, *weights)


def _attention_pieces(sink_ref, q_ref, kp_ref, kc_ref, vp_ref, vc_ref, st_scr, o_ref, ctx_ok):
    nq = WINDOW
    nk = 2 * WINDOW
    lanes = Q_PER_KV * nq
    pair_w = 2 * HEAD_DIM
    n_blk = kc_ref.shape[0] // WINDOW
    n_slots = st_scr.shape[0]
    masks = {}

    def valid(bl):
        if bl not in masks:
            jk = lax.broadcasted_iota(jnp.int32, (nk, lanes), 0)
            iq = lax.broadcasted_iota(jnp.int32, (nk, lanes), 1) & (nq - 1)
            ok = True if bl > 0 else ctx_ok
            lo = iq if ok is True else jnp.maximum(iq, jnp.where(ok, 0, WINDOW))
            masks[bl] = (jk >= lo) & (jk <= iq + WINDOW)
        return masks[bl]

    def keys(bl, g):
        cols = slice((g // 2) * pair_w, (g // 2 + 1) * pair_w)
        if bl == 0:
            return jnp.concatenate([kp_ref[:, cols], kc_ref[0:WINDOW, cols]], axis=0)
        return kc_ref[(bl - 1) * WINDOW:(bl + 1) * WINDOW, cols]

    def vals_t(bl, g):
        rows = slice(g * HEAD_DIM, (g + 1) * HEAD_DIM)
        if bl == 0:
            return jnp.concatenate([vp_ref[rows, :], vc_ref[rows, 0:WINDOW]], axis=1)
        return vc_ref[rows, (bl - 1) * WINDOW:(bl + 1) * WINDOW]

    def score(bl, g):
        k = bl * N_KV_HEADS + g
        st = lax.dot_general(keys(bl, g), q_ref[k * lanes:(k + 1) * lanes, :], (((1,), (1,)), ((), ())),
                             preferred_element_type=F32)
        st_scr[k % n_slots] = jnp.where(valid(bl), st, NEG_BIG)

    def attend(bl, g):
        st = st_scr[(bl * N_KV_HEADS + g) % n_slots]
        sink = jnp.concatenate(
            [jnp.full((1, nq), sink_ref[g * Q_PER_KV + r] * LOG2_E, F32) for r in range(Q_PER_KV)], axis=1)
        m = jnp.maximum(jnp.max(st, axis=0, keepdims=True), sink)
        p = jnp.exp2(st - m)
        denom = jnp.sum(p, axis=0, keepdims=True) + jnp.exp2(sink - m)
        ot = jnp.dot(vals_t(bl, g), p.astype(BF16), preferred_element_type=F32) * (1.0 / denom)
        for pr in range(Q_PER_KV // 2):
            two = jnp.concatenate([ot[:, (2 * pr) * nq:(2 * pr + 1) * nq],
                                   ot[:, (2 * pr + 1) * nq:(2 * pr + 2) * nq]], axis=0)
            col = (g * Q_PER_KV + 2 * pr) * HEAD_DIM
            o_ref[bl * nq:(bl + 1) * nq, col:col + pair_w] = two.T

    order = [(bl, g) for bl in range(n_blk) for g in range(N_KV_HEADS)]
    return ([functools.partial(score, bl, g) for bl, g in order],
            [functools.partial(attend, bl, g) for bl, g in order])


def _staggered(score, attend, ahead):
    def piece(k):
        def run():
            if k + ahead < len(score):
                score[k + ahead]()
            attend[k]()
        return run

    def first():
        for f in score[:ahead]:
            f()
    return first, [piece(k) for k in range(len(attend))]


def _attn_sample_kernel(sink_ref, q_ref, knt_ref, vnt_ref, ckt_ref, cvt_ref, o_ref, kw_ref, vw_ref, *, n_sub, tn, unroll):
    pair_w = 2 * HEAD_DIM
    n_pairs = N_HEADS // 2
    rows = n_pairs * tn
    kv = N_KV_HEADS * HEAD_DIM
    old = WINDOW - tn
    assert tn & (tn - 1) == 0, "row -> token index uses a power-of-two mask"
    low = lax.broadcasted_iota(jnp.int32, (tn, pair_w), 1) < HEAD_DIM
    tq = lax.broadcasted_iota(jnp.int32, (2 * rows, 2 * WINDOW), 0) & (tn - 1)
    col = lax.broadcasted_iota(jnp.int32, (2 * rows, 2 * WINDOW), 1)
    valid = ((col < WINDOW) & (col >= tq)) | ((col >= WINDOW + old) & (col - WINDOW - old <= tq))
    is_new = lax.broadcasted_iota(jnp.int32, (kv, WINDOW), 1) >= old
    zeros = jnp.zeros((tn, pair_w), F32)
    nt = (((1,), (1,)), ((), ()))
    assert n_sub * tn == WINDOW

    sink = jnp.concatenate([jnp.full((tn, 1), sink_ref[2 * pr + half], F32)
                            for half in range(2) for pr in range(n_pairs)], axis=0)

    def scores(s):
        r0 = pl.multiple_of(s * tn, tn)
        q = q_ref[pl.ds(r0, tn), :]
        kt = ckt_ref[s]
        vt = cvt_ref[s]
        shift = (old - r0) & (WINDOW - 1)
        knt = jnp.where(is_new, pltpu.roll(knt_ref[...], shift, axis=1), 0.0)
        vnt = jnp.where(is_new, pltpu.roll(vnt_ref[...], shift, axis=1), 0.0)
        kw_ref[s] = jnp.where(is_new, knt, pltpu.roll(kt, old, axis=1))
        vw_ref[s] = jnp.where(is_new, vnt, pltpu.roll(vt, old, axis=1))
        k_all = jnp.concatenate([kt, knt], axis=1).astype(BF16)
        v_all = jnp.concatenate([vt, vnt], axis=1).astype(BF16)
        blocks = []
        for half in range(2):
            for g in range(N_KV_HEADS):
                for pp in range(Q_PER_KV // 2):
                    piece = q[:, (2 * g + pp) * pair_w:(2 * g + pp + 1) * pair_w]
                    if half != g % 2:
                        piece = pltpu.roll(piece, HEAD_DIM, axis=1)
                    piece = jnp.where(low if g % 2 == 0 else ~low, piece, 0.0)
                    blocks.append(jnp.concatenate([piece, zeros] if g // 2 == 0 else [zeros, piece], axis=1))
        qh = jnp.concatenate(blocks, axis=0).astype(BF16)
        return r0, v_all, jnp.where(valid, jnp.dot(qh, k_all, preferred_element_type=F32), NEG_BIG)

    def attend(r0, v_all, sc):
        m = jnp.maximum(jnp.max(sc, axis=-1, keepdims=True), sink)
        p = jnp.exp(sc - m)
        rden = 1.0 / (jnp.sum(p, axis=-1, keepdims=True) + jnp.exp(sink - m))
        o = lax.dot_general((p * rden).astype(BF16), v_all, nt, preferred_element_type=F32)
        for pr in range(n_pairs):
            g = pr // (Q_PER_KV // 2)
            sel = []
            for half in range(2):
                blk = o[half * rows + pr * tn:half * rows + (pr + 1) * tn, (g // 2) * pair_w:(g // 2 + 1) * pair_w]
                sel.append(blk if half == g % 2 else pltpu.roll(blk, HEAD_DIM, axis=1))
            o_ref[pl.ds(r0, tn), pr * pair_w:(pr + 1) * pair_w] = jnp.where(low, sel[0], sel[1])

    def some_sequences(i, carry):
        staged = [scores(i * unroll + u) for u in range(unroll)]
        for st in staged:
            attend(*st)
        return carry

    lax.fori_loop(0, n_sub // unroll, some_sequences, 0)


def _attn_sample(sinks, q, knt, vnt, cache_kt, cache_vt, n_seq, tn, n_sub, unroll):
    d = q.shape[1]
    kv = knt.shape[0]
    rows = lambda i: (i, 0)
    seqs = lambda i: (i, 0, 0)
    win = pl.BlockSpec((n_sub, kv, WINDOW), seqs)
    new = pl.BlockSpec((kv, n_sub * tn), lambda i: (0, i))
    return pl.pallas_call(
        functools.partial(_attn_sample_kernel, n_sub=n_sub, tn=tn, unroll=unroll),
        grid=(n_seq // n_sub,),
        in_specs=[pl.BlockSpec(memory_space=pltpu.SMEM),
                  pl.BlockSpec((n_sub * tn, d), rows),
                  new, new, win, win],
        out_specs=[pl.BlockSpec((n_sub * tn, d), rows), win, win],
        out_shape=[jax.ShapeDtypeStruct(q.shape, F32),
                   jax.ShapeDtypeStruct(cache_kt.shape, F32), jax.ShapeDtypeStruct(cache_vt.shape, F32)],
        compiler_params=pltpu.CompilerParams(
            dimension_semantics=("parallel",), vmem_limit_bytes=VMEM_LIMIT),
        name="attn_sample",
    )(sinks, q, knt, vnt, cache_kt, cache_vt)


def _ssm_prep_kernel(lr_ref, li_ref, ldt_ref, br_ref, bi_ref, ar_ref, ai_ref, bbr_ref, bbi_ref):
    lr = lr_ref[...]
    li = li_ref[...]
    dt = jnp.exp(ldt_ref[...])
    mag = jnp.exp(lr * dt)
    ang = li * dt
    abar_re = mag * jnp.cos(ang)
    abar_im = mag * jnp.sin(ang)
    den = lr * lr + li * li
    nr = abar_re - 1.0
    fr = (nr * lr + abar_im * li) / den
    fi = (abar_im * lr - nr * li) / den
    ar_ref[...] = abar_re
    ai_ref[...] = abar_im
    br = br_ref[...]
    bi = bi_ref[...]
    bbr_ref[...] = fr[:, None, :] * br - fi[:, None, :] * bi
    bbi_ref[...] = fr[:, None, :] * bi + fi[:, None, :] * br


def _ssm_prep(lam_re, lam_im, log_dt, b_re_t, b_im_t):
    g, n = lam_re.shape
    return pl.pallas_call(
        _ssm_prep_kernel,
        out_shape=[jax.ShapeDtypeStruct((g, n), F32), jax.ShapeDtypeStruct((g, n), F32),
                   jax.ShapeDtypeStruct(b_re_t.shape, F32), jax.ShapeDtypeStruct(b_im_t.shape, F32)],
        name="ssm_prep",
    )(lam_re, lam_im, log_dt.reshape(g, 1), b_re_t, b_im_t)


def _ssm_kernel(u_ref, h0_ref, a_ref, d_ref, bd_ref, cd_ref, y_ref, hT_ref, xh_scr, u_scr, y_scr,
                *, n_seq, tt, seq_lanes, scan_unroll):
    cw = xh_scr.shape[1] // SSM_CHUNKS
    hw = cw // 2
    uc = u_scr.shape[2]
    ssm_w = SSM_CHUNKS * uc

    @pl.when(pl.program_id(0) == 0)
    def _():
        hT_ref[...] = h0_ref[...]

    for j in range(SSM_CHUNKS):
        for b in range(n_seq):
            if seq_lanes:
                u_b = u_ref[:, b * ssm_w + j * uc:b * ssm_w + (j + 1) * uc]
            else:
                u_b = u_ref[b * tt:(b + 1) * tt, j * uc:(j + 1) * uc]
            u_scr[j, pl.ds(b, tt, stride=n_seq), :] = u_b

    def project_in(j):
        xh_scr[:, j * cw:(j + 1) * cw] = jnp.dot(u_scr[j].astype(BF16), bd_ref[j], preferred_element_type=F32)

    def scan(j):
        re = slice(j * cw, j * cw + hw)
        im = slice(j * cw + hw, (j + 1) * cw)
        ar = a_ref[:, re]
        ai = a_ref[:, im]

        def seq_group(s, carry):
            s8 = pl.multiple_of(s * 8, 8)

            def step(t, h):
                hr, hi = h
                r0 = pl.multiple_of(t * n_seq + s8, 8)
                nr = ar * hr - ai * hi + xh_scr[pl.ds(r0, 8), re]
                ni = ar * hi + ai * hr + xh_scr[pl.ds(r0, 8), im]
                xh_scr[pl.ds(r0, 8), re] = nr
                xh_scr[pl.ds(r0, 8), im] = ni
                return nr, ni

            hr, hi = lax.fori_loop(0, tt, step, (hT_ref[pl.ds(s8, 8), re], hT_ref[pl.ds(s8, 8), im]),
                                   unroll=scan_unroll)
            hT_ref[pl.ds(s8, 8), re] = hr
            hT_ref[pl.ds(s8, 8), im] = hi
            return carry

        lax.fori_loop(0, n_seq // 8, seq_group, 0)

    def project_out(j):
        y = jnp.dot(xh_scr[:, j * cw:(j + 1) * cw].astype(BF16), cd_ref[j], preferred_element_type=F32)
        cs = slice(j * uc, (j + 1) * uc)
        y_scr[j] = jax.nn.gelu(y + d_ref[:, cs] * u_scr[j])
        for b in range(n_seq):
            y_b = y_scr[j, pl.ds(b, tt, stride=n_seq), :].astype(y_ref.dtype)
            if seq_lanes:
                y_ref[:, b * ssm_w + j * uc:b * ssm_w + (j + 1) * uc] = y_b
            else:
                y_ref[b * tt:(b + 1) * tt, cs] = y_b

    for stage in range(SSM_CHUNKS + 2):
        if stage < SSM_CHUNKS:
            project_in(stage)
        if 1 <= stage <= SSM_CHUNKS:
            scan(stage - 1)
        if stage >= 2:
            project_out(stage - 2)


def _ssm(u, h0, a8, d_skip, bd, cd, n_seq, tt, seq_lanes, scan_unroll):
    ssm_w = d_skip.shape[1]
    tile = n_seq * tt
    n_state = h0.shape[1]
    block = (tt, n_seq * ssm_w) if seq_lanes else (tile, ssm_w)
    assert seq_lanes or u.shape[0] == tile, "row order (seq, t) cannot be tiled over time"
    slab = pltpu.VMEM((SSM_CHUNKS, tile, ssm_w // SSM_CHUNKS), F32)
    return pl.pallas_call(
        functools.partial(_ssm_kernel, n_seq=n_seq, tt=tt, seq_lanes=seq_lanes, scan_unroll=scan_unroll),
        grid=(u.shape[0] // block[0],),
        in_specs=[pl.BlockSpec(block, lambda i: (i, 0)),
                  _const_spec(h0.shape), _const_spec(a8.shape), _const_spec(d_skip.shape),
                  _const_spec(bd.shape), _const_spec(cd.shape)],
        out_specs=[pl.BlockSpec(block, lambda i: (i, 0)),
                   pl.BlockSpec(h0.shape, lambda i: (0, 0))],
        out_shape=[jax.ShapeDtypeStruct(u.shape, BF16), jax.ShapeDtypeStruct(h0.shape, F32)],
        scratch_shapes=[pltpu.VMEM((tile, n_state), F32), slab, slab],
        compiler_params=pltpu.CompilerParams(
            dimension_semantics=("arbitrary",), vmem_limit_bytes=VMEM_LIMIT),
        name="ssm",
    )(u, h0, a8, d_skip, bd, cd)


def _out_kernel(*refs, n_sub, tiles_per_seq):
    if tiles_per_seq:
        (sink_ref, x1_ref, ga_ref, gs_ref, y_ref, q0_ref, k0_ref, v0_ref, qn_ref, kp_ref, kn_ref, vp_ref, vn_ref,
         glua_ref, glub_ref, wo_ref, nb_ref, wg_ref, wu_ref, wd_ref, nf_ref, o_ref,
         h_scr, act_scr, at_ref, st_scr) = refs
        step = pl.program_id(0)

        @pl.when(step == 0)
        def _():
            score, attend = _attention_pieces(sink_ref, q0_ref, k0_ref.at[pl.ds(0, WINDOW)], k0_ref,
                                              v0_ref.at[:, pl.ds(0, WINDOW)], v0_ref, st_scr, at_ref, False)
            first, rest = _staggered(score, attend, ATTN_AHEAD)
            first()
            for f in rest:
                f()

        nxt = jnp.minimum(step + 1, pl.num_programs(0) - 1)
        score, attend = _attention_pieces(sink_ref, qn_ref, kp_ref, kn_ref, vp_ref, vn_ref, st_scr, at_ref,
                                          nxt % tiles_per_seq != 0)
        first, fillers = _staggered(score, attend, ATTN_AHEAD)
    else:
        (x1_ref, at_ref, ga_ref, gs_ref, y_ref, glua_ref, glub_ref, wo_ref,
         nb_ref, wg_ref, wu_ref, wd_ref, nf_ref, o_ref, h_scr, act_scr) = refs
        first, fillers = (lambda: None), ()
    rows = _row_slices(x1_ref.shape[0], n_sub)
    ssm = [jnp.dot(y_ref[r, :], glua_ref[...], preferred_element_type=F32) * jax.nn.sigmoid(
        jnp.dot(y_ref[r, :], glub_ref[...], preferred_element_type=F32)) for r in rows]
    merged = [(jax.nn.sigmoid(ga_ref[r, :].astype(F32)) * at_ref[r, :]
               + jax.nn.sigmoid(gs_ref[r, :].astype(F32)) * s).astype(BF16)
              for r, s in zip(rows, ssm)]
    x2 = [x1_ref[r, :] + jnp.dot(m, wo_ref[...], preferred_element_type=F32) for r, m in zip(rows, merged)]
    first()
    for r, x in zip(rows, x2):
        h_scr[r, :] = _rms(x, nb_ref[...]).astype(BF16)
    for r, x, f in zip(rows, x2, _swiglu(h_scr, wg_ref, wu_ref, wd_ref, act_scr, rows, fillers)):
        o_ref[r, :] = _rms(x + 0.5 * f, nf_ref[...])


def _out(grid, tm, n_sub, std, ymap, x1, attn, ga, gs, y_tb, glua, glub, wo, nb, wg, wu, wd, nf):
    d = nb.shape[1]
    d_ff = wg.shape[1]
    kv = N_KV_HEADS * HEAD_DIM
    shp, spec = std(d)
    weights = [glua, glub, wo, nb, wg, wu, wd, nf]
    scratch = [pltpu.VMEM((tm, d), BF16), pltpu.VMEM((tm, d_ff), BF16)]
    fused = isinstance(attn, tuple)
    if fused:
        sinks, qst, kb, vt, tiles_per_seq = attn
        (n_tiles,) = grid
        blk = tm // WINDOW
        nxt = lambda s: jnp.minimum(s + 1, n_tiles - 1)
        before = lambda s: jnp.maximum(nxt(s) * blk - 1, 0)
        once = dict(pipeline_mode=pl.Buffered(1))
        acts = [sinks, x1, ga, gs, y_tb, qst, kb, vt, qst, kb, kb, vt, vt]
        act_specs = [pl.BlockSpec(memory_space=pltpu.SMEM), spec, spec, spec, pl.BlockSpec((tm, glua.shape[0]), ymap),
                     pl.BlockSpec((tm * N_HEADS, 2 * HEAD_DIM), lambda s: (0, 0), **once),
                     pl.BlockSpec((tm, kv), lambda s: (0, 0), **once),
                     pl.BlockSpec((kv, tm), lambda s: (0, 0), **once),
                     pl.BlockSpec((tm * N_HEADS, 2 * HEAD_DIM), lambda s: (nxt(s), 0)),
                     pl.BlockSpec((WINDOW, kv), lambda s: (before(s), 0)),
                     pl.BlockSpec((tm, kv), lambda s: (nxt(s), 0)),
                     pl.BlockSpec((kv, WINDOW), lambda s: (0, before(s))),
                     pl.BlockSpec((kv, tm), lambda s: (0, nxt(s)))]
        scratch += [pltpu.VMEM((tm, d), F32), pltpu.VMEM((ATTN_SLOTS, 2 * WINDOW, Q_PER_KV * WINDOW), F32)]
    else:
        tiles_per_seq = 0
        acts = [x1, attn, ga, gs, y_tb]
        act_specs = [spec, spec, spec, spec, pl.BlockSpec((tm, glua.shape[0]), ymap)]
    return pl.pallas_call(
        functools.partial(_out_kernel, n_sub=n_sub, tiles_per_seq=tiles_per_seq),
        grid=grid,
        in_specs=act_specs + [_const_spec(w.shape) for w in weights],
        out_specs=spec,
        out_shape=jax.ShapeDtypeStruct(shp, F32),
        scratch_shapes=scratch,
        compiler_params=pltpu.CompilerParams(
            dimension_semantics=("arbitrary",) * len(grid), vmem_limit_bytes=VMEM_LIMIT),
        name="out",
    )(*acts, *weights)


def _transpose_bf16(w):
    def body(w_ref, o_ref):
        o_ref[...] = w_ref[...].T.astype(BF16)
    return pl.pallas_call(body, out_shape=jax.ShapeDtypeStruct(w.shape[::-1], BF16), name="transpose_bf16")(w)


def _block_diag(w, chunks):
    g, a, b = w.shape
    gl = g // chunks
    w = w.reshape(chunks, gl, a, b)
    eye = jnp.eye(gl, dtype=w.dtype)
    return (w[:, :, :, None, :] * eye[None, :, None, :, None]).reshape(chunks, gl * a, gl * b)


def _state_to_lanes(re, im):
    s = re.shape[0]
    re = re.reshape(s, SSM_CHUNKS, -1)
    im = im.reshape(s, SSM_CHUNKS, -1)
    return jnp.stack([re, im], axis=2).reshape(s, -1)


def _lanes_to_state(h, groups):
    s = h.shape[0]
    h = h.reshape(s, SSM_CHUNKS, 2, -1)
    return h[:, :, 0].reshape(s, groups, -1), h[:, :, 1].reshape(s, groups, -1)


def kernel(x_prompt, x_sample, cache_k_win, cache_v_win, state_ssm_re, state_ssm_im, ffn_a_norm, ffn_a_gate, ffn_a_up, ffn_a_down, mix_norm, w_in, attn_sinks, ssm_lambda_re, ssm_lambda_im, ssm_log_dt, ssm_b_re, ssm_b_im, ssm_c_re, ssm_c_im, ssm_d, glu_a, glu_b, w_out, ffn_b_norm, ffn_b_gate, ffn_b_up, ffn_b_down, final_norm):
    depth = ffn_a_norm.shape[0]
    assert depth == 1, "single-layer trunk"
    n_p, seq, d = x_prompt.shape
    n_s, dec, _ = x_sample.shape
    ssm_w = ssm_d.shape[1]
    groups = ssm_lambda_re.shape[1]
    kvw = N_KV_HEADS * HEAD_DIM
    assert cache_k_win.shape[2] == WINDOW and seq % WINDOW == 0 and n_p % 8 == 0 and n_s % 8 == 0

    l = 0
    na, nm, nb = (w[l].reshape(1, d) for w in (ffn_a_norm, mix_norm, ffn_b_norm))
    nf = final_norm.reshape(1, d)
    wga, wua, wda, win = (w[l].astype(BF16) for w in (ffn_a_gate, ffn_a_up, ffn_a_down, w_in))
    wgb, wub, wdb = (w[l].astype(BF16) for w in (ffn_b_gate, ffn_b_up, ffn_b_down))
    glua, glub, wo = (w[l].astype(BF16) for w in (glu_a, glu_b, w_out))
    sinks = attn_sinks[l]
    d_skip = ssm_d[l].reshape(1, ssm_w)

    abar_re, abar_im, bb_re_t, bb_im_t = _ssm_prep(
        ssm_lambda_re[l], ssm_lambda_im[l], ssm_log_dt[l],
        jnp.swapaxes(ssm_b_re[l], 1, 2), jnp.swapaxes(ssm_b_im[l], 1, 2))
    bd = jnp.concatenate([_block_diag(bb_re_t, SSM_CHUNKS), _block_diag(bb_im_t, SSM_CHUNKS)],
                         axis=2).astype(BF16)
    c_re_t = jnp.swapaxes(ssm_c_re[l], 1, 2)
    c_im_t = jnp.swapaxes(ssm_c_im[l], 1, 2)
    cd = jnp.concatenate([_block_diag(c_re_t, SSM_CHUNKS), -_block_diag(c_im_t, SSM_CHUNKS)],
                         axis=1).astype(BF16)
    a8 = jnp.broadcast_to(_state_to_lanes(abar_re[None], abar_im[None]), (8, 2 * groups * SSM_STATE))

    def run_group(x, n_seq, tiling, out_tiling, u_shape, attn_w, stacked, attn_fn, h0, tt):
        grid, tm, std, umap, n_sub = tiling
        x1, q, k, v, u_tb, ga, gs, *more = _ffn_in(x, grid, tm, n_sub, std, umap, u_shape,
                                                    na, wga, wua, wda, nm, win, attn_w, stacked)
        attn = attn_fn(q, k, v, *more)
        y_tb, h_t = _ssm(u_tb, h0, a8, d_skip, bd, cd, n_seq, tt,
                         seq_lanes=u_shape[1] != ssm_w, scan_unroll=tt)
        grid, tm, std, umap, n_sub = out_tiling
        y = _out(grid, tm, n_sub, std, umap, x1, attn, ga, gs, y_tb, glua, glub, wo, nb, wgb, wub, wdb, nf)
        return y, k, v, h_t

    def tiling_p(tm):
        nt = seq // tm

        def std(width):
            return (n_p * seq, width), pl.BlockSpec((tm, width), lambda b, i: (b * nt + i, 0))
        return (n_p, nt), tm, std, lambda b, i: (i, b), tm // SUB_TILE

    tm_p = ROW_TILE
    nt = seq // tm_p

    def attn_p(qst, k, v, kb, vt):
        return sinks, qst, kb, vt, nt

    w_kvt = _transpose_bf16(w_in[l][:, d:d + 2 * kvw])
    wkt = w_kvt[:kvw]
    wvt = w_kvt[kvw:]

    h0_p = jnp.zeros((n_p, 2 * groups * SSM_STATE), F32)
    def std_flat(width):
        return (n_p * seq, width), pl.BlockSpec((tm_p, width), lambda s: (s, 0))

    out_tiling_p = ((n_p * nt,), tm_p, std_flat, lambda s: (s % nt, s // nt), tm_p // SUB_TILE)
    y_p, k_p, v_p, h_p = run_group(
        x_prompt.reshape(n_p * seq, d), n_p, tiling_p(tm_p), out_tiling_p,
        (seq, n_p * ssm_w), ([wvt], lambda b, i: (b * nt + i, 0)), True, attn_p, h0_p, SSM_TILE_ROWS // n_p)

    tm_s = ROW_TILE

    def std_s(width):
        return (n_s * dec, width), pl.BlockSpec((tm_s, width), lambda i, j: (i, 0))

    def window_t(c):
        return jnp.transpose(c, (0, 2, 3, 1)).reshape(c.shape[0], kvw, WINDOW)

    def window(ct):
        return jnp.transpose(ct.reshape(ct.shape[0], N_KV_HEADS, HEAD_DIM, WINDOW), (0, 3, 1, 2))[None]

    new_windows = []

    def attn_s(q, kt, vt):
        o, kw, vw = _attn_sample(sinks, q, kt, vt, window_t(cache_k_win[l]), window_t(cache_v_win[l]),
                                 n_s, dec, WINDOW // dec, ATTN_SAMPLE_UNROLL)
        new_windows.extend([kw, vw])
        return o

    h0_s = _state_to_lanes(state_ssm_re[l], state_ssm_im[l])
    tiling_s = ((n_s * dec // tm_s, 1), tm_s, std_s, lambda i, j: (i, 0), tm_s // SUB_TILE)
    y_s, _, _, h_s = run_group(
        x_sample.reshape(n_s * dec, d), n_s, tiling_s, tiling_s,
        (n_s * dec, ssm_w), ([wkt, wvt], lambda i, j: (i, 0)), False, attn_s, h0_s, dec)

    sp_re, sp_im = _lanes_to_state(h_p, groups)
    ss_re, ss_im = _lanes_to_state(h_s, groups)
    return (y_p.reshape(n_p, seq, d), y_s.reshape(n_s, dec, d),
            window(k_p), window(v_p), window(new_windows[0]), window(new_windows[1]),
            sp_re[None], sp_im[None], ss_re[None], ss_im[None])
```

```python
import functools

import jax
import jax.numpy as jnp
from jax import lax
from jax.experimental import pallas as pl
from jax.experimental.pallas import tpu as pltpu

F32 = jnp.float32
BF16 = jnp.bfloat16

N_HEADS = 16
N_KV_HEADS = 4
HEAD_DIM = 64
Q_PER_KV = N_HEADS // N_KV_HEADS
WINDOW = 128
SSM_STATE = 64
RMS_EPS = 1e-6
NEG_BIG = -1e30
LOG2_E = 1.4426950408889634

MXU_COLS = 256
WINDOW_COLS = N_KV_HEADS * HEAD_DIM
ROW_TILE = 512
SUB_TILE = 256
SSM_TILE_ROWS = 1024
SSM_CHUNKS = 4
ATTN_AHEAD = 2
ATTN_SLOTS = 4
ATTN_SAMPLE_UNROLL = 2
VMEM_LIMIT = 62 * 1024 * 1024


def _rms(x, g):
    return x * lax.rsqrt(jnp.mean(x * x, axis=-1, keepdims=True) + RMS_EPS) * g


def _row_slices(tm, n_sub):
    rs = tm // n_sub
    return [slice(r * rs, (r + 1) * rs) for r in range(n_sub)]


def _swiglu(h_scr, wg_ref, wu_ref, wd_ref, act_scr, rows=(slice(None),), fillers=()):
    d_ff = wg_ref.shape[1]
    fillers = list(fillers)
    for c in range(d_ff // MXU_COLS):
        sl = slice(c * MXU_COLS, (c + 1) * MXU_COLS)
        for r in rows:
            g = jnp.dot(h_scr[r, :], wg_ref[:, sl], preferred_element_type=F32)
            u = jnp.dot(h_scr[r, :], wu_ref[:, sl], preferred_element_type=F32)
            act_scr[r, sl] = (jax.nn.silu(g) * u).astype(BF16)
            if fillers:
                fillers.pop(0)()
    for f in fillers:
        f()
    return [jnp.dot(act_scr[r, :], wd_ref[...], preferred_element_type=F32) for r in rows]


def _ffn_in_kernel(*refs, stacked, n_sub):
    if stacked:
        (x_ref, na_ref, wg_ref, wu_ref, wd_ref, nm_ref, win_ref, wvt_ref,
         x1_ref, q_ref, k_ref, v_ref, u_ref, ga_ref, gs_ref, kb_ref, vt_ref, h_scr, act_scr) = refs
        assert WINDOW_COLS == k_ref.shape[0]
    else:
        (x_ref, na_ref, wg_ref, wu_ref, wd_ref, nm_ref, win_ref, wkt_ref, wvt_ref,
         x1_ref, q_ref, k_ref, v_ref, u_ref, ga_ref, gs_ref, h_scr, act_scr) = refs
    tm, d = x_ref.shape
    rows = _row_slices(tm, n_sub)
    rs = tm // n_sub
    for r in rows:
        h_scr[r, :] = _rms(x_ref[r, :], na_ref[...]).astype(BF16)
    x1 = [x_ref[r, :] + 0.5 * f for r, f in zip(rows, _swiglu(h_scr, wg_ref, wu_ref, wd_ref, act_scr, rows))]
    for r, x in zip(rows, x1):
        x1_ref[r, :] = x
        h_scr[r, :] = _rms(x, nm_ref[...]).astype(BF16)

    def proj(w_ref, off, c, r):
        return jnp.dot(h_scr[r, :], w_ref[:, off + c * MXU_COLS: off + (c + 1) * MXU_COLS],
                       preferred_element_type=F32)

    q_scale = HEAD_DIM ** -0.5 * (LOG2_E if stacked else 1.0)
    pair_w = 2 * HEAD_DIM
    if stacked:
        assert Q_PER_KV * HEAD_DIM == MXU_COLS
        lane = lax.broadcasted_iota(jnp.int32, (rs, pair_w), 1)
        for g in range(N_KV_HEADS):
            keep = (lane < HEAD_DIM) if g % 2 == 0 else (lane >= HEAD_DIM)
            for ri, r in enumerate(rows):
                res = proj(win_ref, 0, g, r) * q_scale
                for rr in range(Q_PER_KV):
                    pair = res[:, (rr // 2) * pair_w:(rr // 2 + 1) * pair_w]
                    if rr % 2 != g % 2:
                        pair = pltpu.roll(pair, HEAD_DIM, axis=1)
                    piece = jnp.where(keep, pair, 0.0).astype(BF16)
                    for bl in range(rs // WINDOW):
                        row = (((ri * (rs // WINDOW) + bl) * N_KV_HEADS + g) * Q_PER_KV + rr) * WINDOW
                        q_ref[row:row + WINDOW, :] = piece[bl * WINDOW:(bl + 1) * WINDOW, :]
    else:
        for c in range(d // MXU_COLS):
            for r in rows:
                q_ref[r, c * MXU_COLS:(c + 1) * MXU_COLS] = (proj(win_ref, 0, c, r) * q_scale).astype(q_ref.dtype)
        for ref, wt_ref in ((k_ref, wkt_ref), (v_ref, wvt_ref)):
            for r in rows:
                ref[:, r] = lax.dot_general(wt_ref[...], h_scr[r, :], (((1,), (1,)), ((), ())),
                                            preferred_element_type=F32)
    off = d + 2 * WINDOW_COLS
    for ref in (u_ref, ga_ref, gs_ref):
        width = ref.shape[1]
        for c in range(width // MXU_COLS):
            for r in rows:
                ref[r, c * MXU_COLS:(c + 1) * MXU_COLS] = proj(win_ref, off, c, r).astype(ref.dtype)
        off += width
    if stacked:
        for r in rows:
            kb_ref[r, :] = proj(win_ref, d, 0, r).astype(kb_ref.dtype)
        for r in rows:
            vt_ref[:, r] = lax.dot_general(wvt_ref[...], h_scr[r, :], (((1,), (1,)), ((), ())),
                                           preferred_element_type=F32).astype(vt_ref.dtype)

        @pl.when(pl.program_id(1) == pl.num_programs(1) - 1)
        def _():
            tail = h_scr[tm - WINDOW:, :]
            for ref, off in ((k_ref, d), (v_ref, d + WINDOW_COLS)):
                ref[...] = jnp.dot(tail, win_ref[:, off:off + WINDOW_COLS], preferred_element_type=F32).T


def _const_spec(shape):
    nd = len(shape)
    return pl.BlockSpec(shape, lambda *_: (0,) * nd, pipeline_mode=pl.Buffered(1))


def _ffn_in(x2d, grid, tm, n_sub, std, umap, u_shape, na, wg, wu, wd, nm, win, attn_w, stacked):
    d = na.shape[1]
    d_ff = wg.shape[1]
    kv = N_KV_HEADS * HEAD_DIM
    ssm_w = win.shape[1] - 3 * d - 2 * kv
    n_tok = x2d.size // d
    extra_w, tile_map = attn_w
    t_spec = pl.BlockSpec((kv, tm), lambda *g: tile_map(*g)[::-1])
    out_shape, out_specs = [], []

    def add(shape_spec, dtype):
        out_shape.append(jax.ShapeDtypeStruct(shape_spec[0], dtype))
        out_specs.append(shape_spec[1])

    add(std(d), F32)
    if stacked:
        add(((n_tok * N_HEADS, 2 * HEAD_DIM), pl.BlockSpec((tm * N_HEADS, 2 * HEAD_DIM), tile_map)), BF16)
        for _ in range(2):
            add(((grid[0], kv, WINDOW), pl.BlockSpec((None, kv, WINDOW), lambda b, i: (b, 0, 0))), F32)
    else:
        add(std(d), F32)
        add(((kv, n_tok), t_spec), F32)
        add(((kv, n_tok), t_spec), F32)
    add((u_shape, pl.BlockSpec((tm, ssm_w), umap)), F32)
    add(std(d), F32)
    add(std(d), F32)
    weights = [na, wg, wu, wd, nm, win, *extra_w]
    if stacked:
        add(std(kv), BF16)
        add(((kv, n_tok), t_spec), BF16)
    return pl.pallas_call(
        functools.partial(_ffn_in_kernel, stacked=stacked, n_sub=n_sub),
        grid=grid,
        in_specs=[std(d)[1]] + [_const_spec(w.shape) for w in weights],
        out_specs=out_specs,
        out_shape=out_shape,
        scratch_shapes=[pltpu.VMEM((tm, d), BF16), pltpu.VMEM((tm, d_ff), BF16)],
        compiler_params=pltpu.CompilerParams(
            dimension_semantics=("parallel", "arbitrary"), vmem_limit_bytes=VMEM_LIMIT),
        name="ffn_in",
    )(x2d, *weights)


def _attention_pieces(sink_ref, q_ref, kp_ref, kc_ref, vp_ref, vc_ref, st_scr, o_ref, ctx_ok):
    nq = WINDOW
    nk = 2 * WINDOW
    lanes = Q_PER_KV * nq
    pair_w = 2 * HEAD_DIM
    n_blk = kc_ref.shape[0] // WINDOW
    n_slots = st_scr.shape[0]
    masks = {}

    def valid(bl):
        if bl not in masks:
            jk = lax.broadcasted_iota(jnp.int32, (nk, lanes), 0)
            iq = lax.broadcasted_iota(jnp.int32, (nk, lanes), 1) & (nq - 1)
            ok = True if bl > 0 else ctx_ok
            lo = iq if ok is True else jnp.maximum(iq, jnp.where(ok, 0, WINDOW))
            masks[bl] = (jk >= lo) & (jk <= iq + WINDOW)
        return masks[bl]

    def keys(bl, g):
        cols = slice((g // 2) * pair_w, (g // 2 + 1) * pair_w)
        if bl == 0:
            return jnp.concatenate([kp_ref[:, cols], kc_ref[0:WINDOW, cols]], axis=0)
        return kc_ref[(bl - 1) * WINDOW:(bl + 1) * WINDOW, cols]

    def vals_t(bl, g):
        rows = slice(g * HEAD_DIM, (g + 1) * HEAD_DIM)
        if bl == 0:
            return jnp.concatenate([vp_ref[rows, :], vc_ref[rows, 0:WINDOW]], axis=1)
        return vc_ref[rows, (bl - 1) * WINDOW:(bl + 1) * WINDOW]

    def score(bl, g):
        k = bl * N_KV_HEADS + g
        st = lax.dot_general(keys(bl, g), q_ref[k * lanes:(k + 1) * lanes, :], (((1,), (1,)), ((), ())),
                             preferred_element_type=F32)
        st_scr[k % n_slots] = jnp.where(valid(bl), st, NEG_BIG)

    def attend(bl, g):
        st = st_scr[(bl * N_KV_HEADS + g) % n_slots]
        sink = jnp.concatenate(
            [jnp.full((1, nq), sink_ref[g * Q_PER_KV + r] * LOG2_E, F32) for r in range(Q_PER_KV)], axis=1)
        m = jnp.maximum(jnp.max(st, axis=0, keepdims=True), sink)
        p = jnp.exp2(st - m)
        denom = jnp.sum(p, axis=0, keepdims=True) + jnp.exp2(sink - m)
        ot = jnp.dot(vals_t(bl, g), p.astype(BF16), preferred_element_type=F32) * (1.0 / denom)
        for pr in range(Q_PER_KV // 2):
            two = jnp.concatenate([ot[:, (2 * pr) * nq:(2 * pr + 1) * nq],
                                   ot[:, (2 * pr + 1) * nq:(2 * pr + 2) * nq]], axis=0)
            col = (g * Q_PER_KV + 2 * pr) * HEAD_DIM
            o_ref[bl * nq:(bl + 1) * nq, col:col + pair_w] = two.T

    order = [(bl, g) for bl in range(n_blk) for g in range(N_KV_HEADS)]
    return ([functools.partial(score, bl, g) for bl, g in order],
            [functools.partial(attend, bl, g) for bl, g in order])


def _staggered(score, attend, ahead):
    def piece(k):
        def run():
            if k + ahead < len(score):
                score[k + ahead]()
            attend[k]()
        return run

    def first():
        for f in score[:ahead]:
            f()
    return first, [piece(k) for k in range(len(attend))]


def _attn_sample_kernel(sink_ref, q_ref, knt_ref, vnt_ref, ckt_ref, cvt_ref, o_ref, kw_ref, vw_ref, *, n_sub, tn, unroll):
    pair_w = 2 * HEAD_DIM
    n_pairs = N_HEADS // 2
    rows = n_pairs * tn
    kv = N_KV_HEADS * HEAD_DIM
    old = WINDOW - tn
    assert tn & (tn - 1) == 0, "row -> token index uses a power-of-two mask"
    low = lax.broadcasted_iota(jnp.int32, (tn, pair_w), 1) < HEAD_DIM
    tq = lax.broadcasted_iota(jnp.int32, (2 * rows, 2 * WINDOW), 0) & (tn - 1)
    col = lax.broadcasted_iota(jnp.int32, (2 * rows, 2 * WINDOW), 1)
    valid = ((col < WINDOW) & (col >= tq)) | ((col >= WINDOW + old) & (col - WINDOW - old <= tq))
    is_new = lax.broadcasted_iota(jnp.int32, (kv, WINDOW), 1) >= old
    zeros = jnp.zeros((tn, pair_w), F32)
    nt = (((1,), (1,)), ((), ()))
    assert n_sub * tn == WINDOW

    sink = jnp.concatenate([jnp.full((tn, 1), sink_ref[2 * pr + half], F32)
                            for half in range(2) for pr in range(n_pairs)], axis=0)

    def scores(s):
        r0 = pl.multiple_of(s * tn, tn)
        q = q_ref[pl.ds(r0, tn), :]
        kt = ckt_ref[s]
        vt = cvt_ref[s]
        shift = (old - r0) & (WINDOW - 1)
        knt = jnp.where(is_new, pltpu.roll(knt_ref[...], shift, axis=1), 0.0)
        vnt = jnp.where(is_new, pltpu.roll(vnt_ref[...], shift, axis=1), 0.0)
        kw_ref[s] = jnp.where(is_new, knt, pltpu.roll(kt, old, axis=1))
        vw_ref[s] = jnp.where(is_new, vnt, pltpu.roll(vt, old, axis=1))
        k_all = jnp.concatenate([kt, knt], axis=1).astype(BF16)
        v_all = jnp.concatenate([vt, vnt], axis=1).astype(BF16)
        blocks = []
        for half in range(2):
            for g in range(N_KV_HEADS):
                for pp in range(Q_PER_KV // 2):
                    piece = q[:, (2 * g + pp) * pair_w:(2 * g + pp + 1) * pair_w]
                    if half != g % 2:
                        piece = pltpu.roll(piece, HEAD_DIM, axis=1)
                    piece = jnp.where(low if g % 2 == 0 else ~low, piece, 0.0)
                    blocks.append(jnp.concatenate([piece, zeros] if g // 2 == 0 else [zeros, piece], axis=1))
        qh = jnp.concatenate(blocks, axis=0).astype(BF16)
        return r0, v_all, jnp.where(valid, jnp.dot(qh, k_all, preferred_element_type=F32), NEG_BIG)

    def attend(r0, v_all, sc):
        m = jnp.maximum(jnp.max(sc, axis=-1, keepdims=True), sink)
        p = jnp.exp(sc - m)
        rden = 1.0 / (jnp.sum(p, axis=-1, keepdims=True) + jnp.exp(sink - m))
        o = lax.dot_general((p * rden).astype(BF16), v_all, nt, preferred_element_type=F32)
        for pr in range(n_pairs):
            g = pr // (Q_PER_KV // 2)
            sel = []
            for half in range(2):
                blk = o[half * rows + pr * tn:half * rows + (pr + 1) * tn, (g // 2) * pair_w:(g // 2 + 1) * pair_w]
                sel.append(blk if half == g % 2 else pltpu.roll(blk, HEAD_DIM, axis=1))
            o_ref[pl.ds(r0, tn), pr * pair_w:(pr + 1) * pair_w] = jnp.where(low, sel[0], sel[1])

    def some_sequences(i, carry):
        staged = [scores(i * unroll + u) for u in range(unroll)]
        for st in staged:
            attend(*st)
        return carry

    lax.fori_loop(0, n_sub // unroll, some_sequences, 0)


def _attn_sample(sinks, q, knt, vnt, cache_kt, cache_vt, n_seq, tn, n_sub, unroll):
    d = q.shape[1]
    kv = knt.shape[0]
    rows = lambda i: (i, 0)
    seqs = lambda i: (i, 0, 0)
    win = pl.BlockSpec((n_sub, kv, WINDOW), seqs)
    new = pl.BlockSpec((kv, n_sub * tn), lambda i: (0, i))
    return pl.pallas_call(
        functools.partial(_attn_sample_kernel, n_sub=n_sub, tn=tn, unroll=unroll),
        grid=(n_seq // n_sub,),
        in_specs=[pl.BlockSpec(memory_space=pltpu.SMEM),
                  pl.BlockSpec((n_sub * tn, d), rows),
                  new, new, win, win],
        out_specs=[pl.BlockSpec((n_sub * tn, d), rows), win, win],
        out_shape=[jax.ShapeDtypeStruct(q.shape, F32),
                   jax.ShapeDtypeStruct(cache_kt.shape, F32), jax.ShapeDtypeStruct(cache_vt.shape, F32)],
        compiler_params=pltpu.CompilerParams(
            dimension_semantics=("parallel",), vmem_limit_bytes=VMEM_LIMIT),
        name="attn_sample",
    )(sinks, q, knt, vnt, cache_kt, cache_vt)


def _ssm_prep_kernel(lr_ref, li_ref, ldt_ref, br_ref, bi_ref, ar_ref, ai_ref, bbr_ref, bbi_ref):
    lr = lr_ref[...]
    li = li_ref[...]
    dt = jnp.exp(ldt_ref[...])
    mag = jnp.exp(lr * dt)
    ang = li * dt
    abar_re = mag * jnp.cos(ang)
    abar_im = mag * jnp.sin(ang)
    den = lr * lr + li * li
    nr = abar_re - 1.0
    fr = (nr * lr + abar_im * li) / den
    fi = (abar_im * lr - nr * li) / den
    ar_ref[...] = abar_re
    ai_ref[...] = abar_im
    br = br_ref[...]
    bi = bi_ref[...]
    bbr_ref[...] = fr[:, None, :] * br - fi[:, None, :] * bi
    bbi_ref[...] = fr[:, None, :] * bi + fi[:, None, :] * br


def _ssm_prep(lam_re, lam_im, log_dt, b_re_t, b_im_t):
    g, n = lam_re.shape
    return pl.pallas_call(
        _ssm_prep_kernel,
        out_shape=[jax.ShapeDtypeStruct((g, n), F32), jax.ShapeDtypeStruct((g, n), F32),
                   jax.ShapeDtypeStruct(b_re_t.shape, F32), jax.ShapeDtypeStruct(b_im_t.shape, F32)],
        name="ssm_prep",
    )(lam_re, lam_im, log_dt.reshape(g, 1), b_re_t, b_im_t)


def _ssm_kernel(u_ref, h0_ref, a_ref, d_ref, bd_ref, cd_ref, y_ref, hT_ref, xh_scr, u_scr, y_scr,
                *, n_seq, tt, seq_lanes, scan_unroll):
    cw = xh_scr.shape[1] // SSM_CHUNKS
    hw = cw // 2
    uc = u_scr.shape[2]
    ssm_w = SSM_CHUNKS * uc

    @pl.when(pl.program_id(0) == 0)
    def _():
        hT_ref[...] = h0_ref[...]

    for j in range(SSM_CHUNKS):
        for b in range(n_seq):
            if seq_lanes:
                u_b = u_ref[:, b * ssm_w + j * uc:b * ssm_w + (j + 1) * uc]
            else:
                u_b = u_ref[b * tt:(b + 1) * tt, j * uc:(j + 1) * uc]
            u_scr[j, pl.ds(b, tt, stride=n_seq), :] = u_b

    def project_in(j):
        xh_scr[:, j * cw:(j + 1) * cw] = jnp.dot(u_scr[j].astype(BF16), bd_ref[j], preferred_element_type=F32)

    def scan(j):
        re = slice(j * cw, j * cw + hw)
        im = slice(j * cw + hw, (j + 1) * cw)
        ar = a_ref[:, re]
        ai = a_ref[:, im]

        def seq_group(s, carry):
            s8 = pl.multiple_of(s * 8, 8)

            def step(t, h):
                hr, hi = h
                r0 = pl.multiple_of(t * n_seq + s8, 8)
                nr = ar * hr - ai * hi + xh_scr[pl.ds(r0, 8), re]
                ni = ar * hi + ai * hr + xh_scr[pl.ds(r0, 8), im]
                xh_scr[pl.ds(r0, 8), re] = nr
                xh_scr[pl.ds(r0, 8), im] = ni
                return nr, ni

            hr, hi = lax.fori_loop(0, tt, step, (hT_ref[pl.ds(s8, 8), re], hT_ref[pl.ds(s8, 8), im]),
                                   unroll=scan_unroll)
            hT_ref[pl.ds(s8, 8), re] = hr
            hT_ref[pl.ds(s8, 8), im] = hi
            return carry

        lax.fori_loop(0, n_seq // 8, seq_group, 0)

    def project_out(j):
        y = jnp.dot(xh_scr[:, j * cw:(j + 1) * cw].astype(BF16), cd_ref[j], preferred_element_type=F32)
        cs = slice(j * uc, (j + 1) * uc)
        y_scr[j] = jax.nn.gelu(y + d_ref[:, cs] * u_scr[j])
        for b in range(n_seq):
            y_b = y_scr[j, pl.ds(b, tt, stride=n_seq), :].astype(y_ref.dtype)
            if seq_lanes:
                y_ref[:, b * ssm_w + j * uc:b * ssm_w + (j + 1) * uc] = y_b
            else:
                y_ref[b * tt:(b + 1) * tt, cs] = y_b

    for stage in range(SSM_CHUNKS + 2):
        if stage < SSM_CHUNKS:
            project_in(stage)
        if 1 <= stage <= SSM_CHUNKS:
            scan(stage - 1)
        if stage >= 2:
            project_out(stage - 2)


def _ssm(u, h0, a8, d_skip, bd, cd, n_seq, tt, seq_lanes, scan_unroll):
    ssm_w = d_skip.shape[1]
    tile = n_seq * tt
    n_state = h0.shape[1]
    block = (tt, n_seq * ssm_w) if seq_lanes else (tile, ssm_w)
    assert seq_lanes or u.shape[0] == tile, "row order (seq, t) cannot be tiled over time"
    slab = pltpu.VMEM((SSM_CHUNKS, tile, ssm_w // SSM_CHUNKS), F32)
    return pl.pallas_call(
        functools.partial(_ssm_kernel, n_seq=n_seq, tt=tt, seq_lanes=seq_lanes, scan_unroll=scan_unroll),
        grid=(u.shape[0] // block[0],),
        in_specs=[pl.BlockSpec(block, lambda i: (i, 0)),
                  _const_spec(h0.shape), _const_spec(a8.shape), _const_spec(d_skip.shape),
                  _const_spec(bd.shape), _const_spec(cd.shape)],
        out_specs=[pl.BlockSpec(block, lambda i: (i, 0)),
                   pl.BlockSpec(h0.shape, lambda i: (0, 0))],
        out_shape=[jax.ShapeDtypeStruct(u.shape, BF16), jax.ShapeDtypeStruct(h0.shape, F32)],
        scratch_shapes=[pltpu.VMEM((tile, n_state), F32), slab, slab],
        compiler_params=pltpu.CompilerParams(
            dimension_semantics=("arbitrary",), vmem_limit_bytes=VMEM_LIMIT),
        name="ssm",
    )(u, h0, a8, d_skip, bd, cd)


def _out_kernel(*refs, n_sub, tiles_per_seq):
    if tiles_per_seq:
        (sink_ref, x1_ref, ga_ref, gs_ref, y_ref, q0_ref, k0_ref, v0_ref, qn_ref, kp_ref, kn_ref, vp_ref, vn_ref,
         glua_ref, glub_ref, wo_ref, nb_ref, wg_ref, wu_ref, wd_ref, nf_ref, o_ref,
         h_scr, act_scr, at_ref, st_scr) = refs
        step = pl.program_id(0)

        @pl.when(step == 0)
        def _():
            score, attend = _attention_pieces(sink_ref, q0_ref, k0_ref.at[pl.ds(0, WINDOW)], k0_ref,
                                              v0_ref.at[:, pl.ds(0, WINDOW)], v0_ref, st_scr, at_ref, False)
            first, rest = _staggered(score, attend, ATTN_AHEAD)
            first()
            for f in rest:
                f()

        nxt = jnp.minimum(step + 1, pl.num_programs(0) - 1)
        score, attend = _attention_pieces(sink_ref, qn_ref, kp_ref, kn_ref, vp_ref, vn_ref, st_scr, at_ref,
                                          nxt % tiles_per_seq != 0)
        first, fillers = _staggered(score, attend, ATTN_AHEAD)
    else:
        (x1_ref, at_ref, ga_ref, gs_ref, y_ref, glua_ref, glub_ref, wo_ref,
         nb_ref, wg_ref, wu_ref, wd_ref, nf_ref, o_ref, h_scr, act_scr) = refs
        first, fillers = (lambda: None), ()
    rows = _row_slices(x1_ref.shape[0], n_sub)
    ssm = [jnp.dot(y_ref[r, :], glua_ref[...], preferred_element_type=F32) * jax.nn.sigmoid(
        jnp.dot(y_ref[r, :], glub_ref[...], preferred_element_type=F32)) for r in rows]
    merged = [(jax.nn.sigmoid(ga_ref[r, :]) * at_ref[r, :] + jax.nn.sigmoid(gs_ref[r, :]) * s).astype(BF16)
              for r, s in zip(rows, ssm)]
    x2 = [x1_ref[r, :] + jnp.dot(m, wo_ref[...], preferred_element_type=F32) for r, m in zip(rows, merged)]
    first()
    for r, x in zip(rows, x2):
        h_scr[r, :] = _rms(x, nb_ref[...]).astype(BF16)
    for r, x, f in zip(rows, x2, _swiglu(h_scr, wg_ref, wu_ref, wd_ref, act_scr, rows, fillers)):
        o_ref[r, :] = _rms(x + 0.5 * f, nf_ref[...])


def _out(grid, tm, n_sub, std, ymap, x1, attn, ga, gs, y_tb, glua, glub, wo, nb, wg, wu, wd, nf):
    d = nb.shape[1]
    d_ff = wg.shape[1]
    kv = N_KV_HEADS * HEAD_DIM
    shp, spec = std(d)
    weights = [glua, glub, wo, nb, wg, wu, wd, nf]
    scratch = [pltpu.VMEM((tm, d), BF16), pltpu.VMEM((tm, d_ff), BF16)]
    fused = isinstance(attn, tuple)
    if fused:
        sinks, qst, kb, vt, tiles_per_seq = attn
        (n_tiles,) = grid
        blk = tm // WINDOW
        nxt = lambda s: jnp.minimum(s + 1, n_tiles - 1)
        before = lambda s: jnp.maximum(nxt(s) * blk - 1, 0)
        once = dict(pipeline_mode=pl.Buffered(1))
        acts = [sinks, x1, ga, gs, y_tb, qst, kb, vt, qst, kb, kb, vt, vt]
        act_specs = [pl.BlockSpec(memory_space=pltpu.SMEM), spec, spec, spec, pl.BlockSpec((tm, glua.shape[0]), ymap),
                     pl.BlockSpec((tm * N_HEADS, 2 * HEAD_DIM), lambda s: (0, 0), **once),
                     pl.BlockSpec((tm, kv), lambda s: (0, 0), **once),
                     pl.BlockSpec((kv, tm), lambda s: (0, 0), **once),
                     pl.BlockSpec((tm * N_HEADS, 2 * HEAD_DIM), lambda s: (nxt(s), 0)),
                     pl.BlockSpec((WINDOW, kv), lambda s: (before(s), 0)),
                     pl.BlockSpec((tm, kv), lambda s: (nxt(s), 0)),
                     pl.BlockSpec((kv, WINDOW), lambda s: (0, before(s))),
                     pl.BlockSpec((kv, tm), lambda s: (0, nxt(s)))]
        scratch += [pltpu.VMEM((tm, d), F32), pltpu.VMEM((ATTN_SLOTS, 2 * WINDOW, Q_PER_KV * WINDOW), F32)]
    else:
        tiles_per_seq = 0
        acts = [x1, attn, ga, gs, y_tb]
        act_specs = [spec, spec, spec, spec, pl.BlockSpec((tm, glua.shape[0]), ymap)]
    return pl.pallas_call(
        functools.partial(_out_kernel, n_sub=n_sub, tiles_per_seq=tiles_per_seq),
        grid=grid,
        in_specs=act_specs + [_const_spec(w.shape) for w in weights],
        out_specs=spec,
        out_shape=jax.ShapeDtypeStruct(shp, F32),
        scratch_shapes=scratch,
        compiler_params=pltpu.CompilerParams(
            dimension_semantics=("arbitrary",) * len(grid), vmem_limit_bytes=VMEM_LIMIT),
        name="out",
    )(*acts, *weights)


def _transpose_bf16(w):
    def body(w_ref, o_ref):
        o_ref[...] = w_ref[...].T.astype(BF16)
    return pl.pallas_call(body, out_shape=jax.ShapeDtypeStruct(w.shape[::-1], BF16), name="transpose_bf16")(w)


def _block_diag(w, chunks):
    g, a, b = w.shape
    gl = g // chunks
    w = w.reshape(chunks, gl, a, b)
    eye = jnp.eye(gl, dtype=w.dtype)
    return (w[:, :, :, None, :] * eye[None, :, None, :, None]).reshape(chunks, gl * a, gl * b)


def _state_to_lanes(re, im):
    s = re.shape[0]
    re = re.reshape(s, SSM_CHUNKS, -1)
    im = im.reshape(s, SSM_CHUNKS, -1)
    return jnp.stack([re, im], axis=2).reshape(s, -1)


def _lanes_to_state(h, groups):
    s = h.shape[0]
    h = h.reshape(s, SSM_CHUNKS, 2, -1)
    return h[:, :, 0].reshape(s, groups, -1), h[:, :, 1].reshape(s, groups, -1)


def kernel(x_prompt, x_sample, cache_k_win, cache_v_win, state_ssm_re, state_ssm_im, ffn_a_norm, ffn_a_gate, ffn_a_up, ffn_a_down, mix_norm, w_in, attn_sinks, ssm_lambda_re, ssm_lambda_im, ssm_log_dt, ssm_b_re, ssm_b_im, ssm_c_re, ssm_c_im, ssm_d, glu_a, glu_b, w_out, ffn_b_norm, ffn_b_gate, ffn_b_up, ffn_b_down, final_norm):
    depth = ffn_a_norm.shape[0]
    assert depth == 1, "single-layer trunk"
    n_p, seq, d = x_prompt.shape
    n_s, dec, _ = x_sample.shape
    ssm_w = ssm_d.shape[1]
    groups = ssm_lambda_re.shape[1]
    kvw = N_KV_HEADS * HEAD_DIM
    assert cache_k_win.shape[2] == WINDOW and seq % WINDOW == 0 and n_p % 8 == 0 and n_s % 8 == 0

    l = 0
    na, nm, nb = (w[l].reshape(1, d) for w in (ffn_a_norm, mix_norm, ffn_b_norm))
    nf = final_norm.reshape(1, d)
    wga, wua, wda, win = (w[l].astype(BF16) for w in (ffn_a_gate, ffn_a_up, ffn_a_down, w_in))
    wgb, wub, wdb = (w[l].astype(BF16) for w in (ffn_b_gate, ffn_b_up, ffn_b_down))
    glua, glub, wo = (w[l].astype(BF16) for w in (glu_a, glu_b, w_out))
    sinks = attn_sinks[l]
    d_skip = ssm_d[l].reshape(1, ssm_w)

    abar_re, abar_im, bb_re_t, bb_im_t = _ssm_prep(
        ssm_lambda_re[l], ssm_lambda_im[l], ssm_log_dt[l],
        jnp.swapaxes(ssm_b_re[l], 1, 2), jnp.swapaxes(ssm_b_im[l], 1, 2))
    bd = jnp.concatenate([_block_diag(bb_re_t, SSM_CHUNKS), _block_diag(bb_im_t, SSM_CHUNKS)],
                         axis=2).astype(BF16)
    c_re_t = jnp.swapaxes(ssm_c_re[l], 1, 2)
    c_im_t = jnp.swapaxes(ssm_c_im[l], 1, 2)
    cd = jnp.concatenate([_block_diag(c_re_t, SSM_CHUNKS), -_block_diag(c_im_t, SSM_CHUNKS)],
                         axis=1).astype(BF16)
    a8 = jnp.broadcast_to(_state_to_lanes(abar_re[None], abar_im[None]), (8, 2 * groups * SSM_STATE))

    def run_group(x, n_seq, tiling, out_tiling, u_shape, attn_w, stacked, attn_fn, h0, tt):
        grid, tm, std, umap, n_sub = tiling
        x1, q, k, v, u_tb, ga, gs, *more = _ffn_in(x, grid, tm, n_sub, std, umap, u_shape,
                                                    na, wga, wua, wda, nm, win, attn_w, stacked)
        attn = attn_fn(q, k, v, *more)
        y_tb, h_t = _ssm(u_tb, h0, a8, d_skip, bd, cd, n_seq, tt,
                         seq_lanes=u_shape[1] != ssm_w, scan_unroll=tt)
        grid, tm, std, umap, n_sub = out_tiling
        y = _out(grid, tm, n_sub, std, umap, x1, attn, ga, gs, y_tb, glua, glub, wo, nb, wgb, wub, wdb, nf)
        return y, k, v, h_t

    def tiling_p(tm):
        nt = seq // tm

        def std(width):
            return (n_p * seq, width), pl.BlockSpec((tm, width), lambda b, i: (b * nt + i, 0))
        return (n_p, nt), tm, std, lambda b, i: (i, b), tm // SUB_TILE

    tm_p = ROW_TILE
    nt = seq // tm_p

    def attn_p(qst, k, v, kb, vt):
        return sinks, qst, kb, vt, nt

    w_kvt = _transpose_bf16(w_in[l][:, d:d + 2 * kvw])
    wkt = w_kvt[:kvw]
    wvt = w_kvt[kvw:]

    h0_p = jnp.zeros((n_p, 2 * groups * SSM_STATE), F32)
    def std_flat(width):
        return (n_p * seq, width), pl.BlockSpec((tm_p, width), lambda s: (s, 0))

    out_tiling_p = ((n_p * nt,), tm_p, std_flat, lambda s: (s % nt, s // nt), tm_p // SUB_TILE)
    y_p, k_p, v_p, h_p = run_group(
        x_prompt.reshape(n_p * seq, d), n_p, tiling_p(tm_p), out_tiling_p,
        (seq, n_p * ssm_w), ([wvt], lambda b, i: (b * nt + i, 0)), True, attn_p, h0_p, SSM_TILE_ROWS // n_p)

    tm_s = ROW_TILE

    def std_s(width):
        return (n_s * dec, width), pl.BlockSpec((tm_s, width), lambda i, j: (i, 0))

    def window_t(c):
        return jnp.transpose(c, (0, 2, 3, 1)).reshape(c.shape[0], kvw, WINDOW)

    def window(ct):
        return jnp.transpose(ct.reshape(ct.shape[0], N_KV_HEADS, HEAD_DIM, WINDOW), (0, 3, 1, 2))[None]

    new_windows = []

    def attn_s(q, kt, vt):
        o, kw, vw = _attn_sample(sinks, q, kt, vt, window_t(cache_k_win[l]), window_t(cache_v_win[l]),
                                 n_s, dec, WINDOW // dec, ATTN_SAMPLE_UNROLL)
        new_windows.extend([kw, vw])
        return o

    h0_s = _state_to_lanes(state_ssm_re[l], state_ssm_im[l])
    tiling_s = ((n_s * dec // tm_s, 1), tm_s, std_s, lambda i, j: (i, 0), tm_s // SUB_TILE)
    y_s, _, _, h_s = run_group(
        x_sample.reshape(n_s * dec, d), n_s, tiling_s, tiling_s,
        (n_s * dec, ssm_w), ([wkt, wvt], lambda i, j: (i, 0)), False, attn_s, h0_s, dec)

    sp_re, sp_im = _lanes_to_state(h_p, groups)
    ss_re, ss_im = _lanes_to_state(h_s, groups)
    return (y_p.reshape(n_p, seq, d), y_s.reshape(n_s, dec, d),
            window(k_p), window(v_p), window(new_windows[0]), window(new_windows[1]),
            sp_re[None], sp_im[None], ss_re[None], ss_im[None])
```

```python
import functools

import jax
import jax.numpy as jnp
from jax import lax
from jax.experimental import pallas as pl
from jax.experimental.pallas import tpu as pltpu

F32 = jnp.float32
BF16 = jnp.bfloat16

N_HEADS = 16
N_KV_HEADS = 4
HEAD_DIM = 64
Q_PER_KV = N_HEADS // N_KV_HEADS
WINDOW = 128
SSM_STATE = 64
RMS_EPS = 1e-6
NEG_BIG = -1e30
LOG2_E = 1.4426950408889634

MXU_COLS = 256
WINDOW_COLS = N_KV_HEADS * HEAD_DIM
ROW_TILE = 512
SUB_TILE = 256
SSM_TILE_ROWS = 1024
SSM_CHUNKS = 4
ATTN_AHEAD = 2
ATTN_SLOTS = 4
ATTN_SAMPLE_UNROLL = 2
VMEM_LIMIT = 62 * 1024 * 1024


def _rms(x, g):
    return x * lax.rsqrt(jnp.mean(x * x, axis=-1, keepdims=True) + RMS_EPS) * g


def _row_slices(tm, n_sub):
    rs = tm // n_sub
    return [slice(r * rs, (r + 1) * rs) for r in range(n_sub)]


def _swiglu(h_scr, wg_ref, wu_ref, wd_ref, act_scr, rows=(slice(None),), fillers=()):
    d_ff = wg_ref.shape[1]
    fillers = list(fillers)
    for c in range(d_ff // MXU_COLS):
        sl = slice(c * MXU_COLS, (c + 1) * MXU_COLS)
        for r in rows:
            g = jnp.dot(h_scr[r, :], wg_ref[:, sl], preferred_element_type=F32)
            u = jnp.dot(h_scr[r, :], wu_ref[:, sl], preferred_element_type=F32)
            act_scr[r, sl] = (jax.nn.silu(g) * u).astype(BF16)
            if fillers:
                fillers.pop(0)()
    for f in fillers:
        f()
    return [jnp.dot(act_scr[r, :], wd_ref[...], preferred_element_type=F32) for r in rows]


def _ffn_in_kernel(*refs, stacked, n_sub):
    if stacked:
        (x_ref, na_ref, wg_ref, wu_ref, wd_ref, nm_ref, win_ref, wvt_ref,
         x1_ref, q_ref, k_ref, v_ref, u_ref, ga_ref, gs_ref, kb_ref, vt_ref, h_scr, act_scr) = refs
        assert WINDOW_COLS == k_ref.shape[0]
    else:
        (x_ref, na_ref, wg_ref, wu_ref, wd_ref, nm_ref, win_ref, wkt_ref, wvt_ref,
         x1_ref, q_ref, k_ref, v_ref, u_ref, ga_ref, gs_ref, h_scr, act_scr) = refs
    tm, d = x_ref.shape
    rows = _row_slices(tm, n_sub)
    rs = tm // n_sub
    for r in rows:
        h_scr[r, :] = _rms(x_ref[r, :], na_ref[...]).astype(BF16)
    x1 = [x_ref[r, :] + 0.5 * f for r, f in zip(rows, _swiglu(h_scr, wg_ref, wu_ref, wd_ref, act_scr, rows))]
    for r, x in zip(rows, x1):
        x1_ref[r, :] = x
        h_scr[r, :] = _rms(x, nm_ref[...]).astype(BF16)

    def proj(w_ref, off, c, r):
        return jnp.dot(h_scr[r, :], w_ref[:, off + c * MXU_COLS: off + (c + 1) * MXU_COLS],
                       preferred_element_type=F32)

    q_scale = HEAD_DIM ** -0.5 * (LOG2_E if stacked else 1.0)
    pair_w = 2 * HEAD_DIM
    if stacked:
        assert Q_PER_KV * HEAD_DIM == MXU_COLS
        lane = lax.broadcasted_iota(jnp.int32, (rs, pair_w), 1)
        for g in range(N_KV_HEADS):
            keep = (lane < HEAD_DIM) if g % 2 == 0 else (lane >= HEAD_DIM)
            for ri, r in enumerate(rows):
                res = proj(win_ref, 0, g, r) * q_scale
                for rr in range(Q_PER_KV):
                    pair = res[:, (rr // 2) * pair_w:(rr // 2 + 1) * pair_w]
                    if rr % 2 != g % 2:
                        pair = pltpu.roll(pair, HEAD_DIM, axis=1)
                    piece = jnp.where(keep, pair, 0.0).astype(BF16)
                    for bl in range(rs // WINDOW):
                        row = (((ri * (rs // WINDOW) + bl) * N_KV_HEADS + g) * Q_PER_KV + rr) * WINDOW
                        q_ref[row:row + WINDOW, :] = piece[bl * WINDOW:(bl + 1) * WINDOW, :]
    else:
        for c in range(d // MXU_COLS):
            for r in rows:
                q_ref[r, c * MXU_COLS:(c + 1) * MXU_COLS] = (proj(win_ref, 0, c, r) * q_scale).astype(q_ref.dtype)
        for ref, wt_ref in ((k_ref, wkt_ref), (v_ref, wvt_ref)):
            for r in rows:
                ref[:, r] = lax.dot_general(wt_ref[...], h_scr[r, :], (((1,), (1,)), ((), ())),
                                            preferred_element_type=F32)
    off = d + 2 * WINDOW_COLS
    for ref in (u_ref, ga_ref, gs_ref):
        width = ref.shape[1]
        for c in range(width // MXU_COLS):
            for r in rows:
                ref[r, c * MXU_COLS:(c + 1) * MXU_COLS] = proj(win_ref, off, c, r).astype(ref.dtype)
        off += width
    if stacked:
        for r in rows:
            kb_ref[r, :] = proj(win_ref, d, 0, r).astype(kb_ref.dtype)
        for r in rows:
            vt_ref[:, r] = lax.dot_general(wvt_ref[...], h_scr[r, :], (((1,), (1,)), ((), ())),
                                           preferred_element_type=F32).astype(vt_ref.dtype)

        @pl.when(pl.program_id(1) == pl.num_programs(1) - 1)
        def _():
            tail = h_scr[tm - WINDOW:, :]
            for ref, off in ((k_ref, d), (v_ref, d + WINDOW_COLS)):
                ref[...] = jnp.dot(tail, win_ref[:, off:off + WINDOW_COLS], preferred_element_type=F32).T


def _const_spec(shape):
    nd = len(shape)
    return pl.BlockSpec(shape, lambda *_: (0,) * nd, pipeline_mode=pl.Buffered(1))


def _ffn_in(x2d, grid, tm, n_sub, std, umap, u_shape, na, wg, wu, wd, nm, win, attn_w, stacked):
    d = na.shape[1]
    d_ff = wg.shape[1]
    kv = N_KV_HEADS * HEAD_DIM
    ssm_w = win.shape[1] - 3 * d - 2 * kv
    n_tok = x2d.size // d
    extra_w, tile_map = attn_w
    t_spec = pl.BlockSpec((kv, tm), lambda *g: tile_map(*g)[::-1])
    out_shape, out_specs = [], []

    def add(shape_spec, dtype):
        out_shape.append(jax.ShapeDtypeStruct(shape_spec[0], dtype))
        out_specs.append(shape_spec[1])

    add(std(d), F32)
    if stacked:
        add(((n_tok * N_HEADS, 2 * HEAD_DIM), pl.BlockSpec((tm * N_HEADS, 2 * HEAD_DIM), tile_map)), BF16)
        for _ in range(2):
            add(((grid[0], kv, WINDOW), pl.BlockSpec((None, kv, WINDOW), lambda b, i: (b, 0, 0))), F32)
    else:
        add(std(d), F32)
        add(((kv, n_tok), t_spec), F32)
        add(((kv, n_tok), t_spec), F32)
    add((u_shape, pl.BlockSpec((tm, ssm_w), umap)), F32)
    add(std(d), F32)
    add(std(d), F32)
    weights = [na, wg, wu, wd, nm, win, *extra_w]
    if stacked:
        add(std(kv), BF16)
        add(((kv, n_tok), t_spec), BF16)
    return pl.pallas_call(
        functools.partial(_ffn_in_kernel, stacked=stacked, n_sub=n_sub),
        grid=grid,
        in_specs=[std(d)[1]] + [_const_spec(w.shape) for w in weights],
        out_specs=out_specs,
        out_shape=out_shape,
        scratch_shapes=[pltpu.VMEM((tm, d), BF16), pltpu.VMEM((tm, d_ff), BF16)],
        compiler_params=pltpu.CompilerParams(
            dimension_semantics=("parallel", "arbitrary"), vmem_limit_bytes=VMEM_LIMIT),
        name="ffn_in",
    )(x2d, *weights)


def _attention_pieces(sink_ref, q_ref, kp_ref, kc_ref, vp_ref, vc_ref, st_scr, o_ref, ctx_ok):
    nq = WINDOW
    nk = 2 * WINDOW
    lanes = Q_PER_KV * nq
    pair_w = 2 * HEAD_DIM
    n_blk = kc_ref.shape[0] // WINDOW
    n_slots = st_scr.shape[0]
    masks = {}

    def valid(bl):
        if bl not in masks:
            jk = lax.broadcasted_iota(jnp.int32, (nk, lanes), 0)
            iq = lax.broadcasted_iota(jnp.int32, (nk, lanes), 1) & (nq - 1)
            ok = True if bl > 0 else ctx_ok
            lo = iq if ok is True else jnp.maximum(iq, jnp.where(ok, 0, WINDOW))
            masks[bl] = (jk >= lo) & (jk <= iq + WINDOW)
        return masks[bl]

    def keys(bl, g):
        cols = slice((g // 2) * pair_w, (g // 2 + 1) * pair_w)
        if bl == 0:
            return jnp.concatenate([kp_ref[:, cols], kc_ref[0:WINDOW, cols]], axis=0)
        return kc_ref[(bl - 1) * WINDOW:(bl + 1) * WINDOW, cols]

    def vals_t(bl, g):
        rows = slice(g * HEAD_DIM, (g + 1) * HEAD_DIM)
        if bl == 0:
            return jnp.concatenate([vp_ref[rows, :], vc_ref[rows, 0:WINDOW]], axis=1)
        return vc_ref[rows, (bl - 1) * WINDOW:(bl + 1) * WINDOW]

    def score(bl, g):
        k = bl * N_KV_HEADS + g
        st = lax.dot_general(keys(bl, g), q_ref[k * lanes:(k + 1) * lanes, :], (((1,), (1,)), ((), ())),
                             preferred_element_type=F32)
        st_scr[k % n_slots] = jnp.where(valid(bl), st, NEG_BIG)

    def attend(bl, g):
        st = st_scr[(bl * N_KV_HEADS + g) % n_slots]
        sink = jnp.concatenate(
            [jnp.full((1, nq), sink_ref[g * Q_PER_KV + r] * LOG2_E, F32) for r in range(Q_PER_KV)], axis=1)
        m = jnp.maximum(jnp.max(st, axis=0, keepdims=True), sink)
        p = jnp.exp2(st - m)
        denom = jnp.sum(p, axis=0, keepdims=True) + jnp.exp2(sink - m)
        ot = jnp.dot(vals_t(bl, g), p.astype(BF16), preferred_element_type=F32) * (1.0 / denom)
        for pr in range(Q_PER_KV // 2):
            two = jnp.concatenate([ot[:, (2 * pr) * nq:(2 * pr + 1) * nq],
                                   ot[:, (2 * pr + 1) * nq:(2 * pr + 2) * nq]], axis=0)
            col = (g * Q_PER_KV + 2 * pr) * HEAD_DIM
            o_ref[bl * nq:(bl + 1) * nq, col:col + pair_w] = two.T

    order = [(bl, g) for bl in range(n_blk) for g in range(N_KV_HEADS)]
    return ([functools.partial(score, bl, g) for bl, g in order],
            [functools.partial(attend, bl, g) for bl, g in order])


def _staggered(score, attend, ahead):
    def piece(k):
        def run():
            if k + ahead < len(score):
                score[k + ahead]()
            attend[k]()
        return run

    def first():
        for f in score[:ahead]:
            f()
    return first, [piece(k) for k in range(len(attend))]


def _attn_sample_kernel(sink_ref, q_ref, knt_ref, vnt_ref, ckt_ref, cvt_ref, o_ref, kw_ref, vw_ref, *, n_sub, tn, unroll):
    pair_w = 2 * HEAD_DIM
    n_pairs = N_HEADS // 2
    rows = n_pairs * tn
    kv = N_KV_HEADS * HEAD_DIM
    old = WINDOW - tn
    assert tn & (tn - 1) == 0, "row -> token index uses a power-of-two mask"
    low = lax.broadcasted_iota(jnp.int32, (tn, pair_w), 1) < HEAD_DIM
    tq = lax.broadcasted_iota(jnp.int32, (2 * rows, 2 * WINDOW), 0) & (tn - 1)
    col = lax.broadcasted_iota(jnp.int32, (2 * rows, 2 * WINDOW), 1)
    valid = ((col < WINDOW) & (col >= tq)) | ((col >= WINDOW + old) & (col - WINDOW - old <= tq))
    is_new = lax.broadcasted_iota(jnp.int32, (kv, WINDOW), 1) >= old
    zeros = jnp.zeros((tn, pair_w), F32)
    nt = (((1,), (1,)), ((), ()))
    assert n_sub * tn == WINDOW

    sink = jnp.concatenate([jnp.full((tn, 1), sink_ref[2 * pr + half], F32)
                            for half in range(2) for pr in range(n_pairs)], axis=0)

    def scores(s):
        r0 = pl.multiple_of(s * tn, tn)
        q = q_ref[pl.ds(r0, tn), :]
        kt = ckt_ref[s]
        vt = cvt_ref[s]
        shift = (old - r0) & (WINDOW - 1)
        knt = jnp.where(is_new, pltpu.roll(knt_ref[...], shift, axis=1), 0.0)
        vnt = jnp.where(is_new, pltpu.roll(vnt_ref[...], shift, axis=1), 0.0)
        kw_ref[s] = jnp.where(is_new, knt, pltpu.roll(kt, old, axis=1))
        vw_ref[s] = jnp.where(is_new, vnt, pltpu.roll(vt, old, axis=1))
        k_all = jnp.concatenate([kt, knt], axis=1).astype(BF16)
        v_all = jnp.concatenate([vt, vnt], axis=1).astype(BF16)
        blocks = []
        for half in range(2):
            for g in range(N_KV_HEADS):
                for pp in range(Q_PER_KV // 2):
                    piece = q[:, (2 * g + pp) * pair_w:(2 * g + pp + 1) * pair_w]
                    if half != g % 2:
                        piece = pltpu.roll(piece, HEAD_DIM, axis=1)
                    piece = jnp.where(low if g % 2 == 0 else ~low, piece, 0.0)
                    blocks.append(jnp.concatenate([piece, zeros] if g // 2 == 0 else [zeros, piece], axis=1))
        qh = jnp.concatenate(blocks, axis=0).astype(BF16)
        return r0, v_all, jnp.where(valid, jnp.dot(qh, k_all, preferred_element_type=F32), NEG_BIG)

    def attend(r0, v_all, sc):
        m = jnp.maximum(jnp.max(sc, axis=-1, keepdims=True), sink)
        p = jnp.exp(sc - m)
        rden = 1.0 / (jnp.sum(p, axis=-1, keepdims=True) + jnp.exp(sink - m))
        o = lax.dot_general((p * rden).astype(BF16), v_all, nt, preferred_element_type=F32)
        for pr in range(n_pairs):
            g = pr // (Q_PER_KV // 2)
            sel = []
            for half in range(2):
                blk = o[half * rows + pr * tn:half * rows + (pr + 1) * tn, (g // 2) * pair_w:(g // 2 + 1) * pair_w]
                sel.append(blk if half == g % 2 else pltpu.roll(blk, HEAD_DIM, axis=1))
            o_ref[pl.ds(r0, tn), pr * pair_w:(pr + 1) * pair_w] = jnp.where(low, sel[0], sel[1])

    def some_sequences(i, carry):
        staged = [scores(i * unroll + u) for u in range(unroll)]
        for st in staged:
            attend(*st)
        return carry

    lax.fori_loop(0, n_sub // unroll, some_sequences, 0)


def _attn_sample(sinks, q, knt, vnt, cache_kt, cache_vt, n_seq, tn, n_sub, unroll):
    d = q.shape[1]
    kv = knt.shape[0]
    rows = lambda i: (i, 0)
    seqs = lambda i: (i, 0, 0)
    win = pl.BlockSpec((n_sub, kv, WINDOW), seqs)
    new = pl.BlockSpec((kv, n_sub * tn), lambda i: (0, i))
    return pl.pallas_call(
        functools.partial(_attn_sample_kernel, n_sub=n_sub, tn=tn, unroll=unroll),
        grid=(n_seq // n_sub,),
        in_specs=[pl.BlockSpec(memory_space=pltpu.SMEM),
                  pl.BlockSpec((n_sub * tn, d), rows),
                  new, new, win, win],
        out_specs=[pl.BlockSpec((n_sub * tn, d), rows), win, win],
        out_shape=[jax.ShapeDtypeStruct(q.shape, F32),
                   jax.ShapeDtypeStruct(cache_kt.shape, F32), jax.ShapeDtypeStruct(cache_vt.shape, F32)],
        compiler_params=pltpu.CompilerParams(
            dimension_semantics=("parallel",), vmem_limit_bytes=VMEM_LIMIT),
        name="attn_sample",
    )(sinks, q, knt, vnt, cache_kt, cache_vt)


def _ssm_prep_kernel(lr_ref, li_ref, ldt_ref, br_ref, bi_ref, cr_ref, ci_ref, ar_ref, ai_ref, bd_ref, cd_ref):
    lr = lr_ref[...]
    li = li_ref[...]
    dt = jnp.exp(ldt_ref[...])
    mag = jnp.exp(lr * dt)
    ang = li * dt
    abar_re = mag * jnp.cos(ang)
    abar_im = mag * jnp.sin(ang)
    den = lr * lr + li * li
    nr = abar_re - 1.0
    fr = (nr * lr + abar_im * li) / den
    fi = (abar_im * lr - nr * li) / den
    ar_ref[...] = abar_re
    ai_ref[...] = abar_im
    br = br_ref[...]
    bi = bi_ref[...]
    bb_re = fr[:, None, :] * br - fi[:, None, :] * bi
    bb_im = fr[:, None, :] * bi + fi[:, None, :] * br
    gl = lr.shape[0] // SSM_CHUNKS

    def diag(x, j):
        a, b = x.shape[1:]
        blk = x[j * gl:(j + 1) * gl].reshape(gl * a, b).astype(BF16)
        spread = (lax.broadcasted_iota(jnp.int32, (b, gl * b), 1) % b
                  == lax.broadcasted_iota(jnp.int32, (b, gl * b), 0)).astype(BF16)
        tiled = jnp.dot(blk, spread, preferred_element_type=F32)
        on_diag = (lax.broadcasted_iota(jnp.int32, tiled.shape, 0) // a
                   == lax.broadcasted_iota(jnp.int32, tiled.shape, 1) // b)
        return jnp.where(on_diag, tiled, 0.0)

    cr = cr_ref[...]
    ci = ci_ref[...]
    for j in range(SSM_CHUNKS):
        bd_ref[j] = jnp.concatenate([diag(bb_re, j), diag(bb_im, j)], axis=1).astype(BF16)
        cd_ref[j] = jnp.concatenate([diag(cr, j), -diag(ci, j)], axis=0).astype(BF16)


def _ssm_prep(lam_re, lam_im, log_dt, b_re_t, b_im_t, c_re_t, c_im_t):
    g, n = lam_re.shape
    c = b_re_t.shape[1]
    gl = g // SSM_CHUNKS
    return pl.pallas_call(
        _ssm_prep_kernel,
        out_shape=[jax.ShapeDtypeStruct((g, n), F32), jax.ShapeDtypeStruct((g, n), F32),
                   jax.ShapeDtypeStruct((SSM_CHUNKS, gl * c, 2 * gl * n), BF16),
                   jax.ShapeDtypeStruct((SSM_CHUNKS, 2 * gl * n, gl * c), BF16)],
        name="ssm_prep",
    )(lam_re, lam_im, log_dt.reshape(g, 1), b_re_t, b_im_t, c_re_t, c_im_t)


def _ssm_kernel(u_ref, h0_ref, a_ref, d_ref, bd_ref, cd_ref, y_ref, hT_ref, xh_scr, u_scr, y_scr,
                *, n_seq, tt, seq_lanes, scan_unroll):
    cw = xh_scr.shape[1] // SSM_CHUNKS
    hw = cw // 2
    uc = u_scr.shape[2]
    ssm_w = SSM_CHUNKS * uc

    @pl.when(pl.program_id(0) == 0)
    def _():
        hT_ref[...] = h0_ref[...]

    for j in range(SSM_CHUNKS):
        for b in range(n_seq):
            if seq_lanes:
                u_b = u_ref[:, b * ssm_w + j * uc:b * ssm_w + (j + 1) * uc]
            else:
                u_b = u_ref[b * tt:(b + 1) * tt, j * uc:(j + 1) * uc]
            u_scr[j, pl.ds(b, tt, stride=n_seq), :] = u_b

    def project_in(j):
        xh_scr[:, j * cw:(j + 1) * cw] = jnp.dot(u_scr[j].astype(BF16), bd_ref[j], preferred_element_type=F32)

    def scan(j):
        re = slice(j * cw, j * cw + hw)
        im = slice(j * cw + hw, (j + 1) * cw)
        ar = a_ref[:, re]
        ai = a_ref[:, im]

        def seq_group(s, carry):
            s8 = pl.multiple_of(s * 8, 8)

            def step(t, h):
                hr, hi = h
                r0 = pl.multiple_of(t * n_seq + s8, 8)
                nr = ar * hr - ai * hi + xh_scr[pl.ds(r0, 8), re]
                ni = ar * hi + ai * hr + xh_scr[pl.ds(r0, 8), im]
                xh_scr[pl.ds(r0, 8), re] = nr
                xh_scr[pl.ds(r0, 8), im] = ni
                return nr, ni

            hr, hi = lax.fori_loop(0, tt, step, (hT_ref[pl.ds(s8, 8), re], hT_ref[pl.ds(s8, 8), im]),
                                   unroll=scan_unroll)
            hT_ref[pl.ds(s8, 8), re] = hr
            hT_ref[pl.ds(s8, 8), im] = hi
            return carry

        lax.fori_loop(0, n_seq // 8, seq_group, 0)

    def project_out(j):
        y = jnp.dot(xh_scr[:, j * cw:(j + 1) * cw].astype(BF16), cd_ref[j], preferred_element_type=F32)
        cs = slice(j * uc, (j + 1) * uc)
        y_scr[j] = jax.nn.gelu(y + d_ref[:, cs] * u_scr[j])
        for b in range(n_seq):
            y_b = y_scr[j, pl.ds(b, tt, stride=n_seq), :].astype(y_ref.dtype)
            if seq_lanes:
                y_ref[:, b * ssm_w + j * uc:b * ssm_w + (j + 1) * uc] = y_b
            else:
                y_ref[b * tt:(b + 1) * tt, cs] = y_b

    for stage in range(SSM_CHUNKS + 2):
        if stage < SSM_CHUNKS:
            project_in(stage)
        if 1 <= stage <= SSM_CHUNKS:
            scan(stage - 1)
        if stage >= 2:
            project_out(stage - 2)


def _ssm(u, h0, a8, d_skip, bd, cd, n_seq, tt, seq_lanes, scan_unroll):
    ssm_w = d_skip.shape[1]
    tile = n_seq * tt
    n_state = h0.shape[1]
    block = (tt, n_seq * ssm_w) if seq_lanes else (tile, ssm_w)
    assert seq_lanes or u.shape[0] == tile, "row order (seq, t) cannot be tiled over time"
    slab = pltpu.VMEM((SSM_CHUNKS, tile, ssm_w // SSM_CHUNKS), F32)
    return pl.pallas_call(
        functools.partial(_ssm_kernel, n_seq=n_seq, tt=tt, seq_lanes=seq_lanes, scan_unroll=scan_unroll),
        grid=(u.shape[0] // block[0],),
        in_specs=[pl.BlockSpec(block, lambda i: (i, 0)),
                  _const_spec(h0.shape), _const_spec(a8.shape), _const_spec(d_skip.shape),
                  _const_spec(bd.shape), _const_spec(cd.shape)],
        out_specs=[pl.BlockSpec(block, lambda i: (i, 0)),
                   pl.BlockSpec(h0.shape, lambda i: (0, 0))],
        out_shape=[jax.ShapeDtypeStruct(u.shape, BF16), jax.ShapeDtypeStruct(h0.shape, F32)],
        scratch_shapes=[pltpu.VMEM((tile, n_state), F32), slab, slab],
        compiler_params=pltpu.CompilerParams(
            dimension_semantics=("arbitrary",), vmem_limit_bytes=VMEM_LIMIT),
        name="ssm",
    )(u, h0, a8, d_skip, bd, cd)


def _out_kernel(*refs, n_sub, tiles_per_seq):
    if tiles_per_seq:
        (sink_ref, x1_ref, ga_ref, gs_ref, y_ref, q0_ref, k0_ref, v0_ref, qn_ref, kp_ref, kn_ref, vp_ref, vn_ref,
         glua_ref, glub_ref, wo_ref, nb_ref, wg_ref, wu_ref, wd_ref, nf_ref, o_ref,
         h_scr, act_scr, at_ref, st_scr) = refs
        step = pl.program_id(0)

        @pl.when(step == 0)
        def _():
            score, attend = _attention_pieces(sink_ref, q0_ref, k0_ref.at[pl.ds(0, WINDOW)], k0_ref,
                                              v0_ref.at[:, pl.ds(0, WINDOW)], v0_ref, st_scr, at_ref, False)
            first, rest = _staggered(score, attend, ATTN_AHEAD)
            first()
            for f in rest:
                f()

        nxt = jnp.minimum(step + 1, pl.num_programs(0) - 1)
        score, attend = _attention_pieces(sink_ref, qn_ref, kp_ref, kn_ref, vp_ref, vn_ref, st_scr, at_ref,
                                          nxt % tiles_per_seq != 0)
        first, fillers = _staggered(score, attend, ATTN_AHEAD)
    else:
        (x1_ref, at_ref, ga_ref, gs_ref, y_ref, glua_ref, glub_ref, wo_ref,
         nb_ref, wg_ref, wu_ref, wd_ref, nf_ref, o_ref, h_scr, act_scr) = refs
        first, fillers = (lambda: None), ()
    rows = _row_slices(x1_ref.shape[0], n_sub)
    ssm = [jnp.dot(y_ref[r, :], glua_ref[...], preferred_element_type=F32) * jax.nn.sigmoid(
        jnp.dot(y_ref[r, :], glub_ref[...], preferred_element_type=F32)) for r in rows]
    merged = [(jax.nn.sigmoid(ga_ref[r, :]) * at_ref[r, :] + jax.nn.sigmoid(gs_ref[r, :]) * s).astype(BF16)
              for r, s in zip(rows, ssm)]
    x2 = [x1_ref[r, :] + jnp.dot(m, wo_ref[...], preferred_element_type=F32) for r, m in zip(rows, merged)]
    first()
    for r, x in zip(rows, x2):
        h_scr[r, :] = _rms(x, nb_ref[...]).astype(BF16)
    for r, x, f in zip(rows, x2, _swiglu(h_scr, wg_ref, wu_ref, wd_ref, act_scr, rows, fillers)):
        o_ref[r, :] = _rms(x + 0.5 * f, nf_ref[...])


def _out(grid, tm, n_sub, std, ymap, x1, attn, ga, gs, y_tb, glua, glub, wo, nb, wg, wu, wd, nf):
    d = nb.shape[1]
    d_ff = wg.shape[1]
    kv = N_KV_HEADS * HEAD_DIM
    shp, spec = std(d)
    weights = [glua, glub, wo, nb, wg, wu, wd, nf]
    scratch = [pltpu.VMEM((tm, d), BF16), pltpu.VMEM((tm, d_ff), BF16)]
    fused = isinstance(attn, tuple)
    if fused:
        sinks, qst, kb, vt, tiles_per_seq = attn
        (n_tiles,) = grid
        blk = tm // WINDOW
        nxt = lambda s: jnp.minimum(s + 1, n_tiles - 1)
        before = lambda s: jnp.maximum(nxt(s) * blk - 1, 0)
        once = dict(pipeline_mode=pl.Buffered(1))
        acts = [sinks, x1, ga, gs, y_tb, qst, kb, vt, qst, kb, kb, vt, vt]
        act_specs = [pl.BlockSpec(memory_space=pltpu.SMEM), spec, spec, spec, pl.BlockSpec((tm, glua.shape[0]), ymap),
                     pl.BlockSpec((tm * N_HEADS, 2 * HEAD_DIM), lambda s: (0, 0), **once),
                     pl.BlockSpec((tm, kv), lambda s: (0, 0), **once),
                     pl.BlockSpec((kv, tm), lambda s: (0, 0), **once),
                     pl.BlockSpec((tm * N_HEADS, 2 * HEAD_DIM), lambda s: (nxt(s), 0)),
                     pl.BlockSpec((WINDOW, kv), lambda s: (before(s), 0)),
                     pl.BlockSpec((tm, kv), lambda s: (nxt(s), 0)),
                     pl.BlockSpec((kv, WINDOW), lambda s: (0, before(s))),
                     pl.BlockSpec((kv, tm), lambda s: (0, nxt(s)))]
        scratch += [pltpu.VMEM((tm, d), F32), pltpu.VMEM((ATTN_SLOTS, 2 * WINDOW, Q_PER_KV * WINDOW), F32)]
    else:
        tiles_per_seq = 0
        acts = [x1, attn, ga, gs, y_tb]
        act_specs = [spec, spec, spec, spec, pl.BlockSpec((tm, glua.shape[0]), ymap)]
    return pl.pallas_call(
        functools.partial(_out_kernel, n_sub=n_sub, tiles_per_seq=tiles_per_seq),
        grid=grid,
        in_specs=act_specs + [_const_spec(w.shape) for w in weights],
        out_specs=spec,
        out_shape=jax.ShapeDtypeStruct(shp, F32),
        scratch_shapes=scratch,
        compiler_params=pltpu.CompilerParams(
            dimension_semantics=("arbitrary",) * len(grid), vmem_limit_bytes=VMEM_LIMIT),
        name="out",
    )(*acts, *weights)


def _transpose_bf16(w):
    def body(w_ref, o_ref):
        o_ref[...] = w_ref[...].T.astype(BF16)
    return pl.pallas_call(body, out_shape=jax.ShapeDtypeStruct(w.shape[::-1], BF16), name="transpose_bf16")(w)


def _state_to_lanes(re, im):
    s = re.shape[0]
    re = re.reshape(s, SSM_CHUNKS, -1)
    im = im.reshape(s, SSM_CHUNKS, -1)
    return jnp.stack([re, im], axis=2).reshape(s, -1)


def _lanes_to_state(h, groups):
    s = h.shape[0]
    h = h.reshape(s, SSM_CHUNKS, 2, -1)
    return h[:, :, 0].reshape(s, groups, -1), h[:, :, 1].reshape(s, groups, -1)


def kernel(x_prompt, x_sample, cache_k_win, cache_v_win, state_ssm_re, state_ssm_im, ffn_a_norm, ffn_a_gate, ffn_a_up, ffn_a_down, mix_norm, w_in, attn_sinks, ssm_lambda_re, ssm_lambda_im, ssm_log_dt, ssm_b_re, ssm_b_im, ssm_c_re, ssm_c_im, ssm_d, glu_a, glu_b, w_out, ffn_b_norm, ffn_b_gate, ffn_b_up, ffn_b_down, final_norm):
    depth = ffn_a_norm.shape[0]
    assert depth == 1, "single-layer trunk"
    n_p, seq, d = x_prompt.shape
    n_s, dec, _ = x_sample.shape
    ssm_w = ssm_d.shape[1]
    groups = ssm_lambda_re.shape[1]
    kvw = N_KV_HEADS * HEAD_DIM
    assert cache_k_win.shape[2] == WINDOW and seq % WINDOW == 0 and n_p % 8 == 0 and n_s % 8 == 0

    l = 0
    na, nm, nb = (w[l].reshape(1, d) for w in (ffn_a_norm, mix_norm, ffn_b_norm))
    nf = final_norm.reshape(1, d)
    wga, wua, wda, win = (w[l].astype(BF16) for w in (ffn_a_gate, ffn_a_up, ffn_a_down, w_in))
    wgb, wub, wdb = (w[l].astype(BF16) for w in (ffn_b_gate, ffn_b_up, ffn_b_down))
    glua, glub, wo = (w[l].astype(BF16) for w in (glu_a, glu_b, w_out))
    sinks = attn_sinks[l]
    d_skip = ssm_d[l].reshape(1, ssm_w)

    abar_re, abar_im, bd, cd = _ssm_prep(
        ssm_lambda_re[l], ssm_lambda_im[l], ssm_log_dt[l],
        jnp.swapaxes(ssm_b_re[l], 1, 2), jnp.swapaxes(ssm_b_im[l], 1, 2),
        jnp.swapaxes(ssm_c_re[l], 1, 2), jnp.swapaxes(ssm_c_im[l], 1, 2))
    a8 = jnp.broadcast_to(_state_to_lanes(abar_re[None], abar_im[None]), (8, 2 * groups * SSM_STATE))

    def run_group(x, n_seq, tiling, out_tiling, u_shape, attn_w, stacked, attn_fn, h0, tt):
        grid, tm, std, umap, n_sub = tiling
        x1, q, k, v, u_tb, ga, gs, *more = _ffn_in(x, grid, tm, n_sub, std, umap, u_shape,
                                                    na, wga, wua, wda, nm, win, attn_w, stacked)
        attn = attn_fn(q, k, v, *more)
        y_tb, h_t = _ssm(u_tb, h0, a8, d_skip, bd, cd, n_seq, tt,
                         seq_lanes=u_shape[1] != ssm_w, scan_unroll=tt)
        grid, tm, std, umap, n_sub = out_tiling
        y = _out(grid, tm, n_sub, std, umap, x1, attn, ga, gs, y_tb, glua, glub, wo, nb, wgb, wub, wdb, nf)
        return y, k, v, h_t

    def tiling_p(tm):
        nt = seq // tm

        def std(width):
            return (n_p * seq, width), pl.BlockSpec((tm, width), lambda b, i: (b * nt + i, 0))
        return (n_p, nt), tm, std, lambda b, i: (i, b), tm // SUB_TILE

    tm_p = ROW_TILE
    nt = seq // tm_p

    def attn_p(qst, k, v, kb, vt):
        return sinks, qst, kb, vt, nt

    w_kvt = _transpose_bf16(w_in[l][:, d:d + 2 * kvw])
    wkt = w_kvt[:kvw]
    wvt = w_kvt[kvw:]

    h0_p = jnp.zeros((n_p, 2 * groups * SSM_STATE), F32)
    def std_flat(width):
        return (n_p * seq, width), pl.BlockSpec((tm_p, width), lambda s: (s, 0))

    out_tiling_p = ((n_p * nt,), tm_p, std_flat, lambda s: (s % nt, s // nt), tm_p // SUB_TILE)
    y_p, k_p, v_p, h_p = run_group(
        x_prompt.reshape(n_p * seq, d), n_p, tiling_p(tm_p), out_tiling_p,
        (seq, n_p * ssm_w), ([wvt], lambda b, i: (b * nt + i, 0)), True, attn_p, h0_p, SSM_TILE_ROWS // n_p)

    tm_s = ROW_TILE

    def std_s(width):
        return (n_s * dec, width), pl.BlockSpec((tm_s, width), lambda i, j: (i, 0))

    def window_t(c):
        return jnp.transpose(c, (0, 2, 3, 1)).reshape(c.shape[0], kvw, WINDOW)

    def window(ct):
        return jnp.transpose(ct.reshape(ct.shape[0], N_KV_HEADS, HEAD_DIM, WINDOW), (0, 3, 1, 2))[None]

    new_windows = []

    def attn_s(q, kt, vt):
        o, kw, vw = _attn_sample(sinks, q, kt, vt, window_t(cache_k_win[l]), window_t(cache_v_win[l]),
                                 n_s, dec, WINDOW // dec, ATTN_SAMPLE_UNROLL)
        new_windows.extend([kw, vw])
        return o

    h0_s = _state_to_lanes(state_ssm_re[l], state_ssm_im[l])
    tiling_s = ((n_s * dec // tm_s, 1), tm_s, std_s, lambda i, j: (i, 0), tm_s // SUB_TILE)
    y_s, _, _, h_s = run_group(
        x_sample.reshape(n_s * dec, d), n_s, tiling_s, tiling_s,
        (n_s * dec, ssm_w), ([wkt, wvt], lambda i, j: (i, 0)), False, attn_s, h0_s, dec)

    sp_re, sp_im = _lanes_to_state(h_p, groups)
    ss_re, ss_im = _lanes_to_state(h_s, groups)
    return (y_p.reshape(n_p, seq, d), y_s.reshape(n_s, dec, d),
            window(k_p), window(v_p), window(new_windows[0]), window(new_windows[1]),
            sp_re[None], sp_im[None], ss_re[None], ss_im[None])
```

```python
import functools

import jax
import jax.numpy as jnp
from jax import lax
from jax.experimental import pallas as pl
from jax.experimental.pallas import tpu as pltpu

F32 = jnp.float32
BF16 = jnp.bfloat16

N_HEADS = 16
N_KV_HEADS = 4
HEAD_DIM = 64
Q_PER_KV = N_HEADS // N_KV_HEADS
WINDOW = 128
SSM_STATE = 64
RMS_EPS = 1e-6
NEG_BIG = -1e30
LOG2_E = 1.4426950408889634

MXU_COLS = 256
WINDOW_COLS = N_KV_HEADS * HEAD_DIM
ROW_TILE = 512
SUB_TILE = 256
SSM_TILE_ROWS = 1024
SSM_CHUNKS = 4
ATTN_AHEAD = 2
ATTN_SLOTS = 4
ATTN_SAMPLE_UNROLL = 2
VMEM_LIMIT = 62 * 1024 * 1024


def _rms(x, g):
    return x * lax.rsqrt(jnp.mean(x * x, axis=-1, keepdims=True) + RMS_EPS) * g


def _row_slices(tm, n_sub):
    rs = tm // n_sub
    return [slice(r * rs, (r + 1) * rs) for r in range(n_sub)]


def _swiglu(h_scr, wg_ref, wu_ref, wd_ref, act_scr, rows=(slice(None),), fillers=()):
    d_ff = wg_ref.shape[1]
    fillers = list(fillers)
    for c in range(d_ff // MXU_COLS):
        sl = slice(c * MXU_COLS, (c + 1) * MXU_COLS)
        for r in rows:
            g = jnp.dot(h_scr[r, :], wg_ref[:, sl], preferred_element_type=F32)
            u = jnp.dot(h_scr[r, :], wu_ref[:, sl], preferred_element_type=F32)
            act_scr[r, sl] = (jax.nn.silu(g) * u).astype(BF16)
            if fillers:
                fillers.pop(0)()
    for f in fillers:
        f()
    return [jnp.dot(act_scr[r, :], wd_ref[...], preferred_element_type=F32) for r in rows]


def _ffn_in_kernel(*refs, stacked, n_sub):
    if stacked:
        (x_ref, na_ref, wg_ref, wu_ref, wd_ref, nm_ref, win_ref, wvt_ref,
         x1_ref, q_ref, k_ref, v_ref, u_ref, ga_ref, gs_ref, kb_ref, vt_ref, h_scr, act_scr) = refs
        assert WINDOW_COLS == k_ref.shape[0]
    else:
        (x_ref, na_ref, wg_ref, wu_ref, wd_ref, nm_ref, win_ref, wkt_ref, wvt_ref,
         x1_ref, q_ref, k_ref, v_ref, u_ref, ga_ref, gs_ref, h_scr, act_scr) = refs
    tm, d = x_ref.shape
    rows = _row_slices(tm, n_sub)
    rs = tm // n_sub
    for r in rows:
        h_scr[r, :] = _rms(x_ref[r, :], na_ref[...]).astype(BF16)
    x1 = [x_ref[r, :] + 0.5 * f for r, f in zip(rows, _swiglu(h_scr, wg_ref, wu_ref, wd_ref, act_scr, rows))]
    for r, x in zip(rows, x1):
        x1_ref[r, :] = x
        h_scr[r, :] = _rms(x, nm_ref[...]).astype(BF16)

    def proj(w_ref, off, c, r):
        return jnp.dot(h_scr[r, :], w_ref[:, off + c * MXU_COLS: off + (c + 1) * MXU_COLS],
                       preferred_element_type=F32)

    q_scale = HEAD_DIM ** -0.5 * (LOG2_E if stacked else 1.0)
    pair_w = 2 * HEAD_DIM
    if stacked:
        assert Q_PER_KV * HEAD_DIM == MXU_COLS
        lane = lax.broadcasted_iota(jnp.int32, (rs, pair_w), 1)
        for g in range(N_KV_HEADS):
            keep = (lane < HEAD_DIM) if g % 2 == 0 else (lane >= HEAD_DIM)
            for ri, r in enumerate(rows):
                res = proj(win_ref, 0, g, r) * q_scale
                for rr in range(Q_PER_KV):
                    pair = res[:, (rr // 2) * pair_w:(rr // 2 + 1) * pair_w]
                    if rr % 2 != g % 2:
                        pair = pltpu.roll(pair, HEAD_DIM, axis=1)
                    piece = jnp.where(keep, pair, 0.0).astype(BF16)
                    for bl in range(rs // WINDOW):
                        row = (((ri * (rs // WINDOW) + bl) * N_KV_HEADS + g) * Q_PER_KV + rr) * WINDOW
                        q_ref[row:row + WINDOW, :] = piece[bl * WINDOW:(bl + 1) * WINDOW, :]
    else:
        for c in range(d // MXU_COLS):
            for r in rows:
                q_ref[r, c * MXU_COLS:(c + 1) * MXU_COLS] = (proj(win_ref, 0, c, r) * q_scale).astype(q_ref.dtype)
        for ref, wt_ref in ((k_ref, wkt_ref), (v_ref, wvt_ref)):
            for r in rows:
                ref[:, r] = lax.dot_general(wt_ref[...], h_scr[r, :], (((1,), (1,)), ((), ())),
                                            preferred_element_type=F32)
    off = d + 2 * WINDOW_COLS
    for ref in (u_ref, ga_ref, gs_ref):
        width = ref.shape[1]
        for c in range(width // MXU_COLS):
            for r in rows:
                ref[r, c * MXU_COLS:(c + 1) * MXU_COLS] = proj(win_ref, off, c, r).astype(ref.dtype)
        off += width
    if stacked:
        for r in rows:
            kb_ref[r, :] = proj(win_ref, d, 0, r).astype(kb_ref.dtype)
        for r in rows:
            vt_ref[:, r] = lax.dot_general(wvt_ref[...], h_scr[r, :], (((1,), (1,)), ((), ())),
                                           preferred_element_type=F32).astype(vt_ref.dtype)

        @pl.when(pl.program_id(1) == pl.num_programs(1) - 1)
        def _():
            tail = h_scr[tm - WINDOW:, :]
            for ref, off in ((k_ref, d), (v_ref, d + WINDOW_COLS)):
                ref[...] = jnp.dot(tail, win_ref[:, off:off + WINDOW_COLS], preferred_element_type=F32).T


def _const_spec(shape):
    nd = len(shape)
    return pl.BlockSpec(shape, lambda *_: (0,) * nd, pipeline_mode=pl.Buffered(1))


def _ffn_in(x2d, grid, tm, n_sub, std, umap, u_shape, na, wg, wu, wd, nm, win, attn_w, stacked):
    d = na.shape[1]
    d_ff = wg.shape[1]
    kv = N_KV_HEADS * HEAD_DIM
    ssm_w = win.shape[1] - 3 * d - 2 * kv
    n_tok = x2d.size // d
    extra_w, tile_map = attn_w
    t_spec = pl.BlockSpec((kv, tm), lambda *g: tile_map(*g)[::-1])
    out_shape, out_specs = [], []

    def add(shape_spec, dtype):
        out_shape.append(jax.ShapeDtypeStruct(shape_spec[0], dtype))
        out_specs.append(shape_spec[1])

    add(std(d), F32)
    if stacked:
        add(((n_tok * N_HEADS, 2 * HEAD_DIM), pl.BlockSpec((tm * N_HEADS, 2 * HEAD_DIM), tile_map)), BF16)
        for _ in range(2):
            add(((grid[0], kv, WINDOW), pl.BlockSpec((None, kv, WINDOW), lambda b, i: (b, 0, 0))), F32)
    else:
        add(std(d), F32)
        add(((kv, n_tok), t_spec), F32)
        add(((kv, n_tok), t_spec), F32)
    add((u_shape, pl.BlockSpec((tm, ssm_w), umap)), F32)
    add(std(d), F32)
    add(std(d), F32)
    weights = [na, wg, wu, wd, nm, win, *extra_w]
    if stacked:
        add(std(kv), BF16)
        add(((kv, n_tok), t_spec), BF16)
    return pl.pallas_call(
        functools.partial(_ffn_in_kernel, stacked=stacked, n_sub=n_sub),
        grid=grid,
        in_specs=[std(d)[1]] + [_const_spec(w.shape) for w in weights],
        out_specs=out_specs,
        out_shape=out_shape,
        scratch_shapes=[pltpu.VMEM((tm, d), BF16), pltpu.VMEM((tm, d_ff), BF16)],
        compiler_params=pltpu.CompilerParams(
            dimension_semantics=("parallel", "arbitrary"), vmem_limit_bytes=VMEM_LIMIT),
        name="ffn_in",
    )(x2d, *weights)


def _attention_pieces(sink_ref, q_ref, kp_ref, kc_ref, vp_ref, vc_ref, st_scr, o_ref, ctx_ok):
    nq = WINDOW
    nk = 2 * WINDOW
    lanes = Q_PER_KV * nq
    pair_w = 2 * HEAD_DIM
    n_blk = kc_ref.shape[0] // WINDOW
    n_slots = st_scr.shape[0]
    masks = {}

    def valid(bl):
        if bl not in masks:
            jk = lax.broadcasted_iota(jnp.int32, (nk, lanes), 0)
            iq = lax.broadcasted_iota(jnp.int32, (nk, lanes), 1) & (nq - 1)
            ok = True if bl > 0 else ctx_ok
            lo = iq if ok is True else jnp.maximum(iq, jnp.where(ok, 0, WINDOW))
            masks[bl] = (jk >= lo) & (jk <= iq + WINDOW)
        return masks[bl]

    def keys(bl, g):
        cols = slice((g // 2) * pair_w, (g // 2 + 1) * pair_w)
        if bl == 0:
            return jnp.concatenate([kp_ref[:, cols], kc_ref[0:WINDOW, cols]], axis=0)
        return kc_ref[(bl - 1) * WINDOW:(bl + 1) * WINDOW, cols]

    def vals_t(bl, g):
        rows = slice(g * HEAD_DIM, (g + 1) * HEAD_DIM)
        if bl == 0:
            return jnp.concatenate([vp_ref[rows, :], vc_ref[rows, 0:WINDOW]], axis=1)
        return vc_ref[rows, (bl - 1) * WINDOW:(bl + 1) * WINDOW]

    def score(bl, g):
        k = bl * N_KV_HEADS + g
        st = lax.dot_general(keys(bl, g), q_ref[k * lanes:(k + 1) * lanes, :], (((1,), (1,)), ((), ())),
                             preferred_element_type=F32)
        st_scr[k % n_slots] = jnp.where(valid(bl), st, NEG_BIG)

    def attend(bl, g):
        st = st_scr[(bl * N_KV_HEADS + g) % n_slots]
        sink = jnp.concatenate(
            [jnp.full((1, nq), sink_ref[g * Q_PER_KV + r] * LOG2_E, F32) for r in range(Q_PER_KV)], axis=1)
        m = jnp.maximum(jnp.max(st, axis=0, keepdims=True), sink)
        p = jnp.exp2(st - m)
        denom = jnp.sum(p, axis=0, keepdims=True) + jnp.exp2(sink - m)
        ot = jnp.dot(vals_t(bl, g), p.astype(BF16), preferred_element_type=F32) * (1.0 / denom)
        for pr in range(Q_PER_KV // 2):
            two = jnp.concatenate([ot[:, (2 * pr) * nq:(2 * pr + 1) * nq],
                                   ot[:, (2 * pr + 1) * nq:(2 * pr + 2) * nq]], axis=0)
            col = (g * Q_PER_KV + 2 * pr) * HEAD_DIM
            o_ref[bl * nq:(bl + 1) * nq, col:col + pair_w] = two.T

    order = [(bl, g) for bl in range(n_blk) for g in range(N_KV_HEADS)]
    return ([functools.partial(score, bl, g) for bl, g in order],
            [functools.partial(attend, bl, g) for bl, g in order])


def _staggered(score, attend, ahead):
    def piece(k):
        def run():
            if k + ahead < len(score):
                score[k + ahead]()
            attend[k]()
        return run

    def first():
        for f in score[:ahead]:
            f()
    return first, [piece(k) for k in range(len(attend))]


def _attn_sample_kernel(sink_ref, q_ref, knt_ref, vnt_ref, ckt_ref, cvt_ref, o_ref, kw_ref, vw_ref, *, n_sub, tn, unroll):
    pair_w = 2 * HEAD_DIM
    n_pairs = N_HEADS // 2
    rows = n_pairs * tn
    kv = N_KV_HEADS * HEAD_DIM
    old = WINDOW - tn
    assert tn & (tn - 1) == 0, "row -> token index uses a power-of-two mask"
    low = lax.broadcasted_iota(jnp.int32, (tn, pair_w), 1) < HEAD_DIM
    tq = lax.broadcasted_iota(jnp.int32, (2 * rows, 2 * WINDOW), 0) & (tn - 1)
    col = lax.broadcasted_iota(jnp.int32, (2 * rows, 2 * WINDOW), 1)
    valid = ((col < WINDOW) & (col >= tq)) | ((col >= WINDOW + old) & (col - WINDOW - old <= tq))
    is_new = lax.broadcasted_iota(jnp.int32, (kv, WINDOW), 1) >= old
    zeros = jnp.zeros((tn, pair_w), F32)
    nt = (((1,), (1,)), ((), ()))
    assert n_sub * tn == WINDOW

    sink = jnp.concatenate([jnp.full((tn, 1), sink_ref[2 * pr + half], F32)
                            for half in range(2) for pr in range(n_pairs)], axis=0)

    def scores(s):
        r0 = pl.multiple_of(s * tn, tn)
        q = q_ref[pl.ds(r0, tn), :]
        kt = ckt_ref[s]
        vt = cvt_ref[s]
        shift = (old - r0) & (WINDOW - 1)
        knt = jnp.where(is_new, pltpu.roll(knt_ref[...], shift, axis=1), 0.0)
        vnt = jnp.where(is_new, pltpu.roll(vnt_ref[...], shift, axis=1), 0.0)
        kw_ref[s] = jnp.where(is_new, knt, pltpu.roll(kt, old, axis=1))
        vw_ref[s] = jnp.where(is_new, vnt, pltpu.roll(vt, old, axis=1))
        k_all = jnp.concatenate([kt, knt], axis=1).astype(BF16)
        v_all = jnp.concatenate([vt, vnt], axis=1).astype(BF16)
        blocks = []
        for half in range(2):
            for g in range(N_KV_HEADS):
                for pp in range(Q_PER_KV // 2):
                    piece = q[:, (2 * g + pp) * pair_w:(2 * g + pp + 1) * pair_w]
                    if half != g % 2:
                        piece = pltpu.roll(piece, HEAD_DIM, axis=1)
                    piece = jnp.where(low if g % 2 == 0 else ~low, piece, 0.0)
                    blocks.append(jnp.concatenate([piece, zeros] if g // 2 == 0 else [zeros, piece], axis=1))
        qh = jnp.concatenate(blocks, axis=0).astype(BF16)
        return r0, v_all, jnp.where(valid, jnp.dot(qh, k_all, preferred_element_type=F32), NEG_BIG)

    def attend(r0, v_all, sc):
        m = jnp.maximum(jnp.max(sc, axis=-1, keepdims=True), sink)
        p = jnp.exp(sc - m)
        rden = 1.0 / (jnp.sum(p, axis=-1, keepdims=True) + jnp.exp(sink - m))
        o = lax.dot_general((p * rden).astype(BF16), v_all, nt, preferred_element_type=F32)
        for pr in range(n_pairs):
            g = pr // (Q_PER_KV // 2)
            sel = []
            for half in range(2):
                blk = o[half * rows + pr * tn:half * rows + (pr + 1) * tn, (g // 2) * pair_w:(g // 2 + 1) * pair_w]
                sel.append(blk if half == g % 2 else pltpu.roll(blk, HEAD_DIM, axis=1))
            o_ref[pl.ds(r0, tn), pr * pair_w:(pr + 1) * pair_w] = jnp.where(low, sel[0], sel[1])

    def some_sequences(i, carry):
        staged = [scores(i * unroll + u) for u in range(unroll)]
        for st in staged:
            attend(*st)
        return carry

    lax.fori_loop(0, n_sub // unroll, some_sequences, 0)


def _attn_sample(sinks, q, knt, vnt, cache_kt, cache_vt, n_seq, tn, n_sub, unroll):
    d = q.shape[1]
    kv = knt.shape[0]
    rows = lambda i: (i, 0)
    seqs = lambda i: (i, 0, 0)
    win = pl.BlockSpec((n_sub, kv, WINDOW), seqs)
    new = pl.BlockSpec((kv, n_sub * tn), lambda i: (0, i))
    return pl.pallas_call(
        functools.partial(_attn_sample_kernel, n_sub=n_sub, tn=tn, unroll=unroll),
        grid=(n_seq // n_sub,),
        in_specs=[pl.BlockSpec(memory_space=pltpu.SMEM),
                  pl.BlockSpec((n_sub * tn, d), rows),
                  new, new, win, win],
        out_specs=[pl.BlockSpec((n_sub * tn, d), rows), win, win],
        out_shape=[jax.ShapeDtypeStruct(q.shape, F32),
                   jax.ShapeDtypeStruct(cache_kt.shape, F32), jax.ShapeDtypeStruct(cache_vt.shape, F32)],
        compiler_params=pltpu.CompilerParams(
            dimension_semantics=("parallel",), vmem_limit_bytes=VMEM_LIMIT),
        name="attn_sample",
    )(sinks, q, knt, vnt, cache_kt, cache_vt)


def _ssm_prep_kernel(lr_ref, li_ref, ldt_ref, br_ref, bi_ref, cr_ref, ci_ref, ar_ref, ai_ref, bd_ref, cd_ref):
    lr = lr_ref[...]
    li = li_ref[...]
    dt = jnp.exp(ldt_ref[...])
    mag = jnp.exp(lr * dt)
    ang = li * dt
    abar_re = mag * jnp.cos(ang)
    abar_im = mag * jnp.sin(ang)
    den = lr * lr + li * li
    nr = abar_re - 1.0
    fr = (nr * lr + abar_im * li) / den
    fi = (abar_im * lr - nr * li) / den
    ar_ref[...] = abar_re
    ai_ref[...] = abar_im
    br = br_ref[...]
    bi = bi_ref[...]
    bb_re = fr[:, None, :] * br - fi[:, None, :] * bi
    bb_im = fr[:, None, :] * bi + fi[:, None, :] * br
    gl = lr.shape[0] // SSM_CHUNKS

    def diag(x, j):
        a, b = x.shape[1:]
        blk = x[j * gl:(j + 1) * gl].reshape(gl * a, b).astype(BF16)
        spread = (lax.broadcasted_iota(jnp.int32, (b, gl * b), 1) % b
                  == lax.broadcasted_iota(jnp.int32, (b, gl * b), 0)).astype(BF16)
        tiled = jnp.dot(blk, spread, preferred_element_type=F32)
        on_diag = (lax.broadcasted_iota(jnp.int32, tiled.shape, 0) // a
                   == lax.broadcasted_iota(jnp.int32, tiled.shape, 1) // b)
        return jnp.where(on_diag, tiled, 0.0)

    cr = cr_ref[...]
    ci = ci_ref[...]
    for j in range(SSM_CHUNKS):
        bd_ref[j] = jnp.concatenate([diag(bb_re, j), diag(bb_im, j)], axis=1).astype(BF16)
        cd_ref[j] = jnp.concatenate([diag(cr, j), -diag(ci, j)], axis=0).astype(BF16)


def _ssm_prep(lam_re, lam_im, log_dt, b_re_t, b_im_t, c_re_t, c_im_t):
    g, n = lam_re.shape
    c = b_re_t.shape[1]
    gl = g // SSM_CHUNKS
    return pl.pallas_call(
        _ssm_prep_kernel,
        out_shape=[jax.ShapeDtypeStruct((g, n), F32), jax.ShapeDtypeStruct((g, n), F32),
                   jax.ShapeDtypeStruct((SSM_CHUNKS, gl * c, 2 * gl * n), BF16),
                   jax.ShapeDtypeStruct((SSM_CHUNKS, 2 * gl * n, gl * c), BF16)],
        name="ssm_prep",
    )(lam_re, lam_im, log_dt.reshape(g, 1), b_re_t, b_im_t, c_re_t, c_im_t)


def _ssm_kernel(u_ref, h0_ref, a_ref, d_ref, bd_ref, cd_ref, y_ref, hT_ref, xh_scr, u_scr, y_scr,
                *, n_seq, tt, seq_lanes, scan_unroll):
    cw = xh_scr.shape[1] // SSM_CHUNKS
    hw = cw // 2
    uc = u_scr.shape[2]
    ssm_w = SSM_CHUNKS * uc

    @pl.when(pl.program_id(0) == 0)
    def _():
        hT_ref[...] = h0_ref[...]

    for j in range(SSM_CHUNKS):
        for b in range(n_seq):
            if seq_lanes:
                u_b = u_ref[:, b * ssm_w + j * uc:b * ssm_w + (j + 1) * uc]
            else:
                u_b = u_ref[b * tt:(b + 1) * tt, j * uc:(j + 1) * uc]
            u_scr[j, pl.ds(b, tt, stride=n_seq), :] = u_b

    def project_in(j):
        xh_scr[:, j * cw:(j + 1) * cw] = jnp.dot(u_scr[j].astype(BF16), bd_ref[j], preferred_element_type=F32)

    def scan(j):
        re = slice(j * cw, j * cw + hw)
        im = slice(j * cw + hw, (j + 1) * cw)
        ar = a_ref[:, re]
        ai = a_ref[:, im]

        def seq_group(s, carry):
            s8 = pl.multiple_of(s * 8, 8)

            def step(t, h):
                hr, hi = h
                r0 = pl.multiple_of(t * n_seq + s8, 8)
                nr = ar * hr - ai * hi + xh_scr[pl.ds(r0, 8), re]
                ni = ar * hi + ai * hr + xh_scr[pl.ds(r0, 8), im]
                xh_scr[pl.ds(r0, 8), re] = nr
                xh_scr[pl.ds(r0, 8), im] = ni
                return nr, ni

            hr, hi = lax.fori_loop(0, tt, step, (hT_ref[pl.ds(s8, 8), re], hT_ref[pl.ds(s8, 8), im]),
                                   unroll=scan_unroll)
            hT_ref[pl.ds(s8, 8), re] = hr
            hT_ref[pl.ds(s8, 8), im] = hi
            return carry

        lax.fori_loop(0, n_seq // 8, seq_group, 0)

    def project_out(j):
        y = jnp.dot(xh_scr[:, j * cw:(j + 1) * cw].astype(BF16), cd_ref[j], preferred_element_type=F32)
        cs = slice(j * uc, (j + 1) * uc)
        y_scr[j] = jax.nn.gelu(y + d_ref[:, cs] * u_scr[j])
        for b in range(n_seq):
            y_b = y_scr[j, pl.ds(b, tt, stride=n_seq), :].astype(y_ref.dtype)
            if seq_lanes:
                y_ref[:, b * ssm_w + j * uc:b * ssm_w + (j + 1) * uc] = y_b
            else:
                y_ref[b * tt:(b + 1) * tt, cs] = y_b

    for stage in range(SSM_CHUNKS + 2):
        if stage < SSM_CHUNKS:
            project_in(stage)
        if 1 <= stage <= SSM_CHUNKS:
            scan(stage - 1)
        if stage >= 2:
            project_out(stage - 2)


def _ssm(u, h0, a8, d_skip, bd, cd, n_seq, tt, seq_lanes, scan_unroll):
    ssm_w = d_skip.shape[1]
    tile = n_seq * tt
    n_state = h0.shape[1]
    block = (tt, n_seq * ssm_w) if seq_lanes else (tile, ssm_w)
    assert seq_lanes or u.shape[0] == tile, "row order (seq, t) cannot be tiled over time"
    slab = pltpu.VMEM((SSM_CHUNKS, tile, ssm_w // SSM_CHUNKS), F32)
    return pl.pallas_call(
        functools.partial(_ssm_kernel, n_seq=n_seq, tt=tt, seq_lanes=seq_lanes, scan_unroll=scan_unroll),
        grid=(u.shape[0] // block[0],),
        in_specs=[pl.BlockSpec(block, lambda i: (i, 0)),
                  _const_spec(h0.shape), _const_spec(a8.shape), _const_spec(d_skip.shape),
                  _const_spec(bd.shape), _const_spec(cd.shape)],
        out_specs=[pl.BlockSpec(block, lambda i: (i, 0)),
                   pl.BlockSpec(h0.shape, lambda i: (0, 0))],
        out_shape=[jax.ShapeDtypeStruct(u.shape, BF16), jax.ShapeDtypeStruct(h0.shape, F32)],
        scratch_shapes=[pltpu.VMEM((tile, n_state), F32), slab, slab],
        compiler_params=pltpu.CompilerParams(
            dimension_semantics=("arbitrary",), vmem_limit_bytes=VMEM_LIMIT),
        name="ssm",
    )(u, h0, a8, d_skip, bd, cd)


def _out_kernel(*refs, n_sub, tiles_per_seq):
    if tiles_per_seq:
        (sink_ref, x1_ref, ga_ref, gs_ref, y_ref, q0_ref, k0_ref, v0_ref, qn_ref, kp_ref, kn_ref, vp_ref, vn_ref,
         glua_ref, glub_ref, wo_ref, nb_ref, wg_ref, wu_ref, wd_ref, nf_ref, o_ref,
         h_scr, act_scr, at_ref, st_scr) = refs
        step = pl.program_id(0)

        @pl.when(step == 0)
        def _():
            score, attend = _attention_pieces(sink_ref, q0_ref, k0_ref.at[pl.ds(0, WINDOW)], k0_ref,
                                              v0_ref.at[:, pl.ds(0, WINDOW)], v0_ref, st_scr, at_ref, False)
            first, rest = _staggered(score, attend, ATTN_AHEAD)
            first()
            for f in rest:
                f()

        nxt = jnp.minimum(step + 1, pl.num_programs(0) - 1)
        score, attend = _attention_pieces(sink_ref, qn_ref, kp_ref, kn_ref, vp_ref, vn_ref, st_scr, at_ref,
                                          nxt % tiles_per_seq != 0)
        first, fillers = _staggered(score, attend, ATTN_AHEAD)
    else:
        (x1_ref, at_ref, ga_ref, gs_ref, y_ref, glua_ref, glub_ref, wo_ref,
         nb_ref, wg_ref, wu_ref, wd_ref, nf_ref, o_ref, h_scr, act_scr) = refs
        first, fillers = (lambda: None), ()
    rows = _row_slices(x1_ref.shape[0], n_sub)
    ssm = [jnp.dot(y_ref[r, :], glua_ref[...], preferred_element_type=F32) * jax.nn.sigmoid(
        jnp.dot(y_ref[r, :], glub_ref[...], preferred_element_type=F32)) for r in rows]
    merged = [(jax.nn.sigmoid(ga_ref[r, :]) * at_ref[r, :] + jax.nn.sigmoid(gs_ref[r, :]) * s).astype(BF16)
              for r, s in zip(rows, ssm)]
    x2 = [x1_ref[r, :] + jnp.dot(m, wo_ref[...], preferred_element_type=F32) for r, m in zip(rows, merged)]
    first()
    for r, x in zip(rows, x2):
        h_scr[r, :] = _rms(x, nb_ref[...]).astype(BF16)
    for r, x, f in zip(rows, x2, _swiglu(h_scr, wg_ref, wu_ref, wd_ref, act_scr, rows, fillers)):
        o_ref[r, :] = _rms(x + 0.5 * f, nf_ref[...])


def _out(grid, tm, n_sub, std, ymap, x1, attn, ga, gs, y_tb, glua, glub, wo, nb, wg, wu, wd, nf):
    d = nb.shape[1]
    d_ff = wg.shape[1]
    kv = N_KV_HEADS * HEAD_DIM
    shp, spec = std(d)
    weights = [glua, glub, wo, nb, wg, wu, wd, nf]
    scratch = [pltpu.VMEM((tm, d), BF16), pltpu.VMEM((tm, d_ff), BF16)]
    fused = isinstance(attn, tuple)
    if fused:
        sinks, qst, kb, vt, tiles_per_seq = attn
        (n_tiles,) = grid
        blk = tm // WINDOW
        nxt = lambda s: jnp.minimum(s + 1, n_tiles - 1)
        before = lambda s: jnp.maximum(nxt(s) * blk - 1, 0)
        once = dict(pipeline_mode=pl.Buffered(1))
        acts = [sinks, x1, ga, gs, y_tb, qst, kb, vt, qst, kb, kb, vt, vt]
        act_specs = [pl.BlockSpec(memory_space=pltpu.SMEM), spec, spec, spec, pl.BlockSpec((tm, glua.shape[0]), ymap),
                     pl.BlockSpec((tm * N_HEADS, 2 * HEAD_DIM), lambda s: (0, 0), **once),
                     pl.BlockSpec((tm, kv), lambda s: (0, 0), **once),
                     pl.BlockSpec((kv, tm), lambda s: (0, 0), **once),
                     pl.BlockSpec((tm * N_HEADS, 2 * HEAD_DIM), lambda s: (nxt(s), 0)),
                     pl.BlockSpec((WINDOW, kv), lambda s: (before(s), 0)),
                     pl.BlockSpec((tm, kv), lambda s: (nxt(s), 0)),
                     pl.BlockSpec((kv, WINDOW), lambda s: (0, before(s))),
                     pl.BlockSpec((kv, tm), lambda s: (0, nxt(s)))]
        scratch += [pltpu.VMEM((tm, d), F32), pltpu.VMEM((ATTN_SLOTS, 2 * WINDOW, Q_PER_KV * WINDOW), F32)]
    else:
        tiles_per_seq = 0
        acts = [x1, attn, ga, gs, y_tb]
        act_specs = [spec, spec, spec, spec, pl.BlockSpec((tm, glua.shape[0]), ymap)]
    return pl.pallas_call(
        functools.partial(_out_kernel, n_sub=n_sub, tiles_per_seq=tiles_per_seq),
        grid=grid,
        in_specs=act_specs + [_const_spec(w.shape) for w in weights],
        out_specs=spec,
        out_shape=jax.ShapeDtypeStruct(shp, F32),
        scratch_shapes=scratch,
        compiler_params=pltpu.CompilerParams(
            dimension_semantics=("arbitrary",) * len(grid), vmem_limit_bytes=VMEM_LIMIT),
        name="out",
    )(*acts, *weights)


def _transpose_bf16(w, col0, width):
    assert col0 % width == 0 and width % 2 == 0
    half = width // 2

    def body(w_ref, lo_ref, hi_ref):
        t = w_ref[...].T.astype(BF16)
        lo_ref[...] = t[:half]
        hi_ref[...] = t[half:]
    rows = w.shape[0]
    return pl.pallas_call(
        body,
        grid=(1,),
        in_specs=[pl.BlockSpec((rows, width), lambda i: (0, col0 // width))],
        out_specs=[pl.BlockSpec((half, rows), lambda i: (0, 0))] * 2,
        out_shape=[jax.ShapeDtypeStruct((half, rows), BF16)] * 2,
        name="transpose_bf16",
    )(w)


def _state_to_lanes(re, im):
    s = re.shape[0]
    re = re.reshape(s, SSM_CHUNKS, -1)
    im = im.reshape(s, SSM_CHUNKS, -1)
    return jnp.stack([re, im], axis=2).reshape(s, -1)


def _lanes_to_state(h, groups):
    s = h.shape[0]
    h = h.reshape(s, SSM_CHUNKS, 2, -1)
    return h[:, :, 0].reshape(s, groups, -1), h[:, :, 1].reshape(s, groups, -1)


def kernel(x_prompt, x_sample, cache_k_win, cache_v_win, state_ssm_re, state_ssm_im, ffn_a_norm, ffn_a_gate, ffn_a_up, ffn_a_down, mix_norm, w_in, attn_sinks, ssm_lambda_re, ssm_lambda_im, ssm_log_dt, ssm_b_re, ssm_b_im, ssm_c_re, ssm_c_im, ssm_d, glu_a, glu_b, w_out, ffn_b_norm, ffn_b_gate, ffn_b_up, ffn_b_down, final_norm):
    depth = ffn_a_norm.shape[0]
    assert depth == 1, "single-layer trunk"
    n_p, seq, d = x_prompt.shape
    n_s, dec, _ = x_sample.shape
    ssm_w = ssm_d.shape[1]
    groups = ssm_lambda_re.shape[1]
    kvw = N_KV_HEADS * HEAD_DIM
    assert cache_k_win.shape[2] == WINDOW and seq % WINDOW == 0 and n_p % 8 == 0 and n_s % 8 == 0

    l = 0
    na, nm, nb = (w[l].reshape(1, d) for w in (ffn_a_norm, mix_norm, ffn_b_norm))
    nf = final_norm.reshape(1, d)
    wga, wua, wda, win = (w[l].astype(BF16) for w in (ffn_a_gate, ffn_a_up, ffn_a_down, w_in))
    wgb, wub, wdb = (w[l].astype(BF16) for w in (ffn_b_gate, ffn_b_up, ffn_b_down))
    glua, glub, wo = (w[l].astype(BF16) for w in (glu_a, glu_b, w_out))
    sinks = attn_sinks[l]
    d_skip = ssm_d[l].reshape(1, ssm_w)

    abar_re, abar_im, bd, cd = _ssm_prep(
        ssm_lambda_re[l], ssm_lambda_im[l], ssm_log_dt[l],
        jnp.swapaxes(ssm_b_re[l], 1, 2), jnp.swapaxes(ssm_b_im[l], 1, 2),
        jnp.swapaxes(ssm_c_re[l], 1, 2), jnp.swapaxes(ssm_c_im[l], 1, 2))
    a8 = jnp.broadcast_to(_state_to_lanes(abar_re[None], abar_im[None]), (8, 2 * groups * SSM_STATE))

    def run_group(x, n_seq, tiling, out_tiling, u_shape, attn_w, stacked, attn_fn, h0, tt):
        grid, tm, std, umap, n_sub = tiling
        x1, q, k, v, u_tb, ga, gs, *more = _ffn_in(x, grid, tm, n_sub, std, umap, u_shape,
                                                    na, wga, wua, wda, nm, win, attn_w, stacked)
        attn = attn_fn(q, k, v, *more)
        y_tb, h_t = _ssm(u_tb, h0, a8, d_skip, bd, cd, n_seq, tt,
                         seq_lanes=u_shape[1] != ssm_w, scan_unroll=tt)
        grid, tm, std, umap, n_sub = out_tiling
        y = _out(grid, tm, n_sub, std, umap, x1, attn, ga, gs, y_tb, glua, glub, wo, nb, wgb, wub, wdb, nf)
        return y, k, v, h_t

    def tiling_p(tm):
        nt = seq // tm

        def std(width):
            return (n_p * seq, width), pl.BlockSpec((tm, width), lambda b, i: (b * nt + i, 0))
        return (n_p, nt), tm, std, lambda b, i: (i, b), tm // SUB_TILE

    tm_p = ROW_TILE
    nt = seq // tm_p

    def attn_p(qst, k, v, kb, vt):
        return sinks, qst, kb, vt, nt

    wkt, wvt = _transpose_bf16(w_in[l], d, 2 * kvw)

    h0_p = jnp.zeros((n_p, 2 * groups * SSM_STATE), F32)
    def std_flat(width):
        return (n_p * seq, width), pl.BlockSpec((tm_p, width), lambda s: (s, 0))

    out_tiling_p = ((n_p * nt,), tm_p, std_flat, lambda s: (s % nt, s // nt), tm_p // SUB_TILE)
    y_p, k_p, v_p, h_p = run_group(
        x_prompt.reshape(n_p * seq, d), n_p, tiling_p(tm_p), out_tiling_p,
        (seq, n_p * ssm_w), ([wvt], lambda b, i: (b * nt + i, 0)), True, attn_p, h0_p, SSM_TILE_ROWS // n_p)

    tm_s = ROW_TILE

    def std_s(width):
        return (n_s * dec, width), pl.BlockSpec((tm_s, width), lambda i, j: (i, 0))

    def window_t(c):
        return jnp.transpose(c, (0, 2, 3, 1)).reshape(c.shape[0], kvw, WINDOW)

    def window(ct):
        return jnp.transpose(ct.reshape(ct.shape[0], N_KV_HEADS, HEAD_DIM, WINDOW), (0, 3, 1, 2))[None]

    new_windows = []

    def attn_s(q, kt, vt):
        o, kw, vw = _attn_sample(sinks, q, kt, vt, window_t(cache_k_win[l]), window_t(cache_v_win[l]),
                                 n_s, dec, WINDOW // dec, ATTN_SAMPLE_UNROLL)
        new_windows.extend([kw, vw])
        return o

    h0_s = _state_to_lanes(state_ssm_re[l], state_ssm_im[l])
    tiling_s = ((n_s * dec // tm_s, 1), tm_s, std_s, lambda i, j: (i, 0), tm_s // SUB_TILE)
    y_s, _, _, h_s = run_group(
        x_sample.reshape(n_s * dec, d), n_s, tiling_s, tiling_s,
        (n_s * dec, ssm_w), ([wkt, wvt], lambda i, j: (i, 0)), False, attn_s, h0_s, dec)

    sp_re, sp_im = _lanes_to_state(h_p, groups)
    ss_re, ss_im = _lanes_to_state(h_s, groups)
    return (y_p.reshape(n_p, seq, d), y_s.reshape(n_s, dec, d),
            window(k_p), window(v_p), window(new_windows[0]), window(new_windows[1]),
            sp_re[None], sp_im[None], ss_re[None], ss_im[None])
```

```python
import functools

import jax
import jax.numpy as jnp
from jax import lax
from jax.experimental import pallas as pl
from jax.experimental.pallas import tpu as pltpu

F32 = jnp.float32
BF16 = jnp.bfloat16

N_HEADS = 16
N_KV_HEADS = 4
HEAD_DIM = 64
Q_PER_KV = N_HEADS // N_KV_HEADS
WINDOW = 128
SSM_STATE = 64
RMS_EPS = 1e-6
NEG_BIG = -1e30
LOG2_E = 1.4426950408889634

MXU_COLS = 256
WINDOW_COLS = N_KV_HEADS * HEAD_DIM
ROW_TILE = 512
SUB_TILE = 256
SSM_TILE_ROWS = 1024
SSM_CHUNKS = 4
ATTN_AHEAD = 2
ATTN_SLOTS = 4
ATTN_SAMPLE_UNROLL = 2
VMEM_LIMIT = 62 * 1024 * 1024


def _rms(x, g):
    return x * lax.rsqrt(jnp.mean(x * x, axis=-1, keepdims=True) + RMS_EPS) * g


def _row_slices(tm, n_sub):
    rs = tm // n_sub
    return [slice(r * rs, (r + 1) * rs) for r in range(n_sub)]


def _swiglu(h_scr, wg_ref, wu_ref, wd_ref, act_scr, rows=(slice(None),), fillers=()):
    d_ff = wg_ref.shape[1]
    fillers = list(fillers)
    for c in range(d_ff // MXU_COLS):
        sl = slice(c * MXU_COLS, (c + 1) * MXU_COLS)
        for r in rows:
            g = jnp.dot(h_scr[r, :], wg_ref[:, sl], preferred_element_type=F32)
            u = jnp.dot(h_scr[r, :], wu_ref[:, sl], preferred_element_type=F32)
            act_scr[r, sl] = (jax.nn.silu(g) * u).astype(BF16)
            if fillers:
                fillers.pop(0)()
    for f in fillers:
        f()
    return [jnp.dot(act_scr[r, :], wd_ref[...], preferred_element_type=F32) for r in rows]


def _ffn_in_kernel(*refs, stacked, n_sub):
    if stacked:
        (x_ref, na_ref, wg_ref, wu_ref, wd_ref, nm_ref, win_ref, wvt_ref,
         x1_ref, q_ref, k_ref, v_ref, u_ref, ga_ref, gs_ref, kb_ref, vt_ref, h_scr, act_scr) = refs
        assert WINDOW_COLS == k_ref.shape[0]
    else:
        (x_ref, na_ref, wg_ref, wu_ref, wd_ref, nm_ref, win_ref, wkt_ref, wvt_ref,
         x1_ref, q_ref, k_ref, v_ref, u_ref, ga_ref, gs_ref, h_scr, act_scr) = refs
    tm, d = x_ref.shape
    rows = _row_slices(tm, n_sub)
    rs = tm // n_sub
    for r in rows:
        h_scr[r, :] = _rms(x_ref[r, :], na_ref[...]).astype(BF16)
    x1 = [x_ref[r, :] + 0.5 * f for r, f in zip(rows, _swiglu(h_scr, wg_ref, wu_ref, wd_ref, act_scr, rows))]
    for r, x in zip(rows, x1):
        x1_ref[r, :] = x
        h_scr[r, :] = _rms(x, nm_ref[...]).astype(BF16)

    def proj(w_ref, off, c, r):
        return jnp.dot(h_scr[r, :], w_ref[:, off + c * MXU_COLS: off + (c + 1) * MXU_COLS],
                       preferred_element_type=F32)

    q_scale = HEAD_DIM ** -0.5 * (LOG2_E if stacked else 1.0)
    pair_w = 2 * HEAD_DIM
    if stacked:
        assert Q_PER_KV * HEAD_DIM == MXU_COLS
        lane = lax.broadcasted_iota(jnp.int32, (rs, pair_w), 1)
        for g in range(N_KV_HEADS):
            keep = (lane < HEAD_DIM) if g % 2 == 0 else (lane >= HEAD_DIM)
            for ri, r in enumerate(rows):
                res = proj(win_ref, 0, g, r) * q_scale
                for rr in range(Q_PER_KV):
                    pair = res[:, (rr // 2) * pair_w:(rr // 2 + 1) * pair_w]
                    if rr % 2 != g % 2:
                        pair = pltpu.roll(pair, HEAD_DIM, axis=1)
                    piece = jnp.where(keep, pair, 0.0).astype(BF16)
                    for bl in range(rs // WINDOW):
                        row = (((ri * (rs // WINDOW) + bl) * N_KV_HEADS + g) * Q_PER_KV + rr) * WINDOW
                        q_ref[row:row + WINDOW, :] = piece[bl * WINDOW:(bl + 1) * WINDOW, :]
    else:
        for c in range(d // MXU_COLS):
            for r in rows:
                q_ref[r, c * MXU_COLS:(c + 1) * MXU_COLS] = (proj(win_ref, 0, c, r) * q_scale).astype(q_ref.dtype)
        for ref, wt_ref in ((k_ref, wkt_ref), (v_ref, wvt_ref)):
            for r in rows:
                ref[:, r] = lax.dot_general(wt_ref[...], h_scr[r, :], (((1,), (1,)), ((), ())),
                                            preferred_element_type=F32)
    off = d + 2 * WINDOW_COLS
    for ref in (u_ref, ga_ref, gs_ref):
        width = ref.shape[1]
        for c in range(width // MXU_COLS):
            for r in rows:
                ref[r, c * MXU_COLS:(c + 1) * MXU_COLS] = proj(win_ref, off, c, r).astype(ref.dtype)
        off += width
    if stacked:
        for r in rows:
            kb_ref[r, :] = proj(win_ref, d, 0, r).astype(kb_ref.dtype)
        for r in rows:
            vt_ref[:, r] = lax.dot_general(wvt_ref[...], h_scr[r, :], (((1,), (1,)), ((), ())),
                                           preferred_element_type=F32).astype(vt_ref.dtype)

        @pl.when(pl.program_id(1) == pl.num_programs(1) - 1)
        def _():
            tail = h_scr[tm - WINDOW:, :]
            for ref, off in ((k_ref, d), (v_ref, d + WINDOW_COLS)):
                ref[...] = jnp.dot(tail, win_ref[:, off:off + WINDOW_COLS], preferred_element_type=F32).T


def _const_spec(shape):
    nd = len(shape)
    return pl.BlockSpec(shape, lambda *_: (0,) * nd, pipeline_mode=pl.Buffered(1))


def _ffn_in(x2d, grid, tm, n_sub, std, umap, u_shape, na, wg, wu, wd, nm, win, attn_w, stacked):
    d = na.shape[1]
    d_ff = wg.shape[1]
    kv = N_KV_HEADS * HEAD_DIM
    ssm_w = win.shape[1] - 3 * d - 2 * kv
    n_tok = x2d.size // d
    extra_w, tile_map = attn_w
    t_spec = pl.BlockSpec((kv, tm), lambda *g: tile_map(*g)[::-1])
    out_shape, out_specs = [], []

    def add(shape_spec, dtype):
        out_shape.append(jax.ShapeDtypeStruct(shape_spec[0], dtype))
        out_specs.append(shape_spec[1])

    add(std(d), F32)
    if stacked:
        add(((n_tok * N_HEADS, 2 * HEAD_DIM), pl.BlockSpec((tm * N_HEADS, 2 * HEAD_DIM), tile_map)), BF16)
        for _ in range(2):
            add(((grid[0], kv, WINDOW), pl.BlockSpec((None, kv, WINDOW), lambda b, i: (b, 0, 0))), F32)
    else:
        add(std(d), F32)
        add(((kv, n_tok), t_spec), F32)
        add(((kv, n_tok), t_spec), F32)
    add((u_shape, pl.BlockSpec((tm, ssm_w), umap)), F32)
    add(std(d), F32)
    add(std(d), F32)
    weights = [na, wg, wu, wd, nm, win, *extra_w]
    if stacked:
        add(std(kv), BF16)
        add(((kv, n_tok), t_spec), BF16)
    return pl.pallas_call(
        functools.partial(_ffn_in_kernel, stacked=stacked, n_sub=n_sub),
        grid=grid,
        in_specs=[std(d)[1]] + [_const_spec(w.shape) for w in weights],
        out_specs=out_specs,
        out_shape=out_shape,
        scratch_shapes=[pltpu.VMEM((tm, d), BF16), pltpu.VMEM((tm, d_ff), BF16)],
        compiler_params=pltpu.CompilerParams(
            dimension_semantics=("parallel", "arbitrary"), vmem_limit_bytes=VMEM_LIMIT),
        name="ffn_in",
    )(x2d, *weights)


def _attention_pieces(sink_ref, q_ref, kp_ref, kc_ref, vp_ref, vc_ref, st_scr, o_ref, ctx_ok):
    nq = WINDOW
    nk = 2 * WINDOW
    lanes = Q_PER_KV * nq
    pair_w = 2 * HEAD_DIM
    n_blk = kc_ref.shape[0] // WINDOW
    n_slots = st_scr.shape[0]
    masks = {}

    def valid(bl):
        if bl not in masks:
            jk = lax.broadcasted_iota(jnp.int32, (nk, lanes), 0)
            iq = lax.broadcasted_iota(jnp.int32, (nk, lanes), 1) & (nq - 1)
            ok = True if bl > 0 else ctx_ok
            lo = iq if ok is True else jnp.maximum(iq, jnp.where(ok, 0, WINDOW))
            masks[bl] = (jk >= lo) & (jk <= iq + WINDOW)
        return masks[bl]

    def keys(bl, g):
        cols = slice((g // 2) * pair_w, (g // 2 + 1) * pair_w)
        if bl == 0:
            return jnp.concatenate([kp_ref[:, cols], kc_ref[0:WINDOW, cols]], axis=0)
        return kc_ref[(bl - 1) * WINDOW:(bl + 1) * WINDOW, cols]

    def vals_t(bl, g):
        rows = slice(g * HEAD_DIM, (g + 1) * HEAD_DIM)
        if bl == 0:
            return jnp.concatenate([vp_ref[rows, :], vc_ref[rows, 0:WINDOW]], axis=1)
        return vc_ref[rows, (bl - 1) * WINDOW:(bl + 1) * WINDOW]

    def score(bl, g):
        k = bl * N_KV_HEADS + g
        st = lax.dot_general(keys(bl, g), q_ref[k * lanes:(k + 1) * lanes, :], (((1,), (1,)), ((), ())),
                             preferred_element_type=F32)
        st_scr[k % n_slots] = jnp.where(valid(bl), st, NEG_BIG)

    def attend(bl, g):
        st = st_scr[(bl * N_KV_HEADS + g) % n_slots]
        sink = jnp.concatenate(
            [jnp.full((1, nq), sink_ref[g * Q_PER_KV + r] * LOG2_E, F32) for r in range(Q_PER_KV)], axis=1)
        m = jnp.maximum(jnp.max(st, axis=0, keepdims=True), sink)
        p = jnp.exp2(st - m)
        denom = jnp.sum(p, axis=0, keepdims=True) + jnp.exp2(sink - m)
        ot = jnp.dot(vals_t(bl, g), p.astype(BF16), preferred_element_type=F32) * (1.0 / denom)
        for pr in range(Q_PER_KV // 2):
            two = jnp.concatenate([ot[:, (2 * pr) * nq:(2 * pr + 1) * nq],
                                   ot[:, (2 * pr + 1) * nq:(2 * pr + 2) * nq]], axis=0)
            col = (g * Q_PER_KV + 2 * pr) * HEAD_DIM
            o_ref[bl * nq:(bl + 1) * nq, col:col + pair_w] = two.T

    order = [(bl, g) for bl in range(n_blk) for g in range(N_KV_HEADS)]
    return ([functools.partial(score, bl, g) for bl, g in order],
            [functools.partial(attend, bl, g) for bl, g in order])


def _staggered(score, attend, ahead):
    def piece(k):
        def run():
            if k + ahead < len(score):
                score[k + ahead]()
            attend[k]()
        return run

    def first():
        for f in score[:ahead]:
            f()
    return first, [piece(k) for k in range(len(attend))]


def _attn_sample_kernel(sink_ref, q_ref, knt_ref, vnt_ref, ckt_ref, cvt_ref, o_ref, kw_ref, vw_ref, *, n_sub, tn, unroll):
    pair_w = 2 * HEAD_DIM
    n_pairs = N_HEADS // 2
    rows = n_pairs * tn
    kv = N_KV_HEADS * HEAD_DIM
    old = WINDOW - tn
    assert tn & (tn - 1) == 0, "row -> token index uses a power-of-two mask"
    low = lax.broadcasted_iota(jnp.int32, (tn, pair_w), 1) < HEAD_DIM
    tq = lax.broadcasted_iota(jnp.int32, (2 * rows, 2 * WINDOW), 0) & (tn - 1)
    col = lax.broadcasted_iota(jnp.int32, (2 * rows, 2 * WINDOW), 1)
    valid = ((col < WINDOW) & (col >= tq)) | ((col >= WINDOW + old) & (col - WINDOW - old <= tq))
    is_new = lax.broadcasted_iota(jnp.int32, (kv, WINDOW), 1) >= old
    zeros = jnp.zeros((tn, pair_w), F32)
    nt = (((1,), (1,)), ((), ()))
    assert n_sub * tn == WINDOW

    sink = jnp.concatenate([jnp.full((tn, 1), sink_ref[2 * pr + half], F32)
                            for half in range(2) for pr in range(n_pairs)], axis=0)

    def scores(s):
        r0 = pl.multiple_of(s * tn, tn)
        q = q_ref[pl.ds(r0, tn), :]
        kt = ckt_ref[s]
        vt = cvt_ref[s]
        shift = (old - r0) & (WINDOW - 1)
        knt = jnp.where(is_new, pltpu.roll(knt_ref[...], shift, axis=1), 0.0)
        vnt = jnp.where(is_new, pltpu.roll(vnt_ref[...], shift, axis=1), 0.0)
        kw_ref[s] = jnp.where(is_new, knt, pltpu.roll(kt, old, axis=1))
        vw_ref[s] = jnp.where(is_new, vnt, pltpu.roll(vt, old, axis=1))
        k_all = jnp.concatenate([kt, knt], axis=1).astype(BF16)
        v_all = jnp.concatenate([vt, vnt], axis=1).astype(BF16)
        blocks = []
        for half in range(2):
            for g in range(N_KV_HEADS):
                for pp in range(Q_PER_KV // 2):
                    piece = q[:, (2 * g + pp) * pair_w:(2 * g + pp + 1) * pair_w]
                    if half != g % 2:
                        piece = pltpu.roll(piece, HEAD_DIM, axis=1)
                    piece = jnp.where(low if g % 2 == 0 else ~low, piece, 0.0)
                    blocks.append(jnp.concatenate([piece, zeros] if g // 2 == 0 else [zeros, piece], axis=1))
        qh = jnp.concatenate(blocks, axis=0).astype(BF16)
        return r0, v_all, jnp.where(valid, jnp.dot(qh, k_all, preferred_element_type=F32), NEG_BIG)

    def attend(r0, v_all, sc):
        m = jnp.maximum(jnp.max(sc, axis=-1, keepdims=True), sink)
        p = jnp.exp(sc - m)
        rden = 1.0 / (jnp.sum(p, axis=-1, keepdims=True) + jnp.exp(sink - m))
        o = lax.dot_general((p * rden).astype(BF16), v_all, nt, preferred_element_type=F32)
        for pr in range(n_pairs):
            g = pr // (Q_PER_KV // 2)
            sel = []
            for half in range(2):
                blk = o[half * rows + pr * tn:half * rows + (pr + 1) * tn, (g // 2) * pair_w:(g // 2 + 1) * pair_w]
                sel.append(blk if half == g % 2 else pltpu.roll(blk, HEAD_DIM, axis=1))
            o_ref[pl.ds(r0, tn), pr * pair_w:(pr + 1) * pair_w] = jnp.where(low, sel[0], sel[1])

    def some_sequences(i, carry):
        staged = [scores(i * unroll + u) for u in range(unroll)]
        for st in staged:
            attend(*st)
        return carry

    lax.fori_loop(0, n_sub // unroll, some_sequences, 0)


def _attn_sample(sinks, q, knt, vnt, cache_kt, cache_vt, n_seq, tn, n_sub, unroll):
    d = q.shape[1]
    kv = knt.shape[0]
    rows = lambda i: (i, 0)
    seqs = lambda i: (i, 0, 0)
    win = pl.BlockSpec((n_sub, kv, WINDOW), seqs)
    new = pl.BlockSpec((kv, n_sub * tn), lambda i: (0, i))
    return pl.pallas_call(
        functools.partial(_attn_sample_kernel, n_sub=n_sub, tn=tn, unroll=unroll),
        grid=(n_seq // n_sub,),
        in_specs=[pl.BlockSpec(memory_space=pltpu.SMEM),
                  pl.BlockSpec((n_sub * tn, d), rows),
                  new, new, win, win],
        out_specs=[pl.BlockSpec((n_sub * tn, d), rows), win, win],
        out_shape=[jax.ShapeDtypeStruct(q.shape, F32),
                   jax.ShapeDtypeStruct(cache_kt.shape, F32), jax.ShapeDtypeStruct(cache_vt.shape, F32)],
        compiler_params=pltpu.CompilerParams(
            dimension_semantics=("parallel",), vmem_limit_bytes=VMEM_LIMIT),
        name="attn_sample",
    )(sinks, q, knt, vnt, cache_kt, cache_vt)


def _ssm_prep_kernel(lr_ref, li_ref, ldt_ref, br_ref, bi_ref, ar_ref, ai_ref, bbr_ref, bbi_ref):
    lr = lr_ref[...]
    li = li_ref[...]
    dt = jnp.exp(ldt_ref[...])
    mag = jnp.exp(lr * dt)
    ang = li * dt
    abar_re = mag * jnp.cos(ang)
    abar_im = mag * jnp.sin(ang)
    den = lr * lr + li * li
    nr = abar_re - 1.0
    fr = (nr * lr + abar_im * li) / den
    fi = (abar_im * lr - nr * li) / den
    ar_ref[...] = abar_re
    ai_ref[...] = abar_im
    br = br_ref[...]
    bi = bi_ref[...]
    bbr_ref[...] = fr[:, None, :] * br - fi[:, None, :] * bi
    bbi_ref[...] = fr[:, None, :] * bi + fi[:, None, :] * br


def _ssm_prep(lam_re, lam_im, log_dt, b_re_t, b_im_t):
    g, n = lam_re.shape
    return pl.pallas_call(
        _ssm_prep_kernel,
        out_shape=[jax.ShapeDtypeStruct((g, n), F32), jax.ShapeDtypeStruct((g, n), F32),
                   jax.ShapeDtypeStruct(b_re_t.shape, F32), jax.ShapeDtypeStruct(b_im_t.shape, F32)],
        name="ssm_prep",
    )(lam_re, lam_im, log_dt.reshape(g, 1), b_re_t, b_im_t)


def _ssm_kernel(u_ref, h0_ref, a_ref, d_ref, bd_ref, cd_ref, y_ref, hT_ref, xh_scr, u_scr, y_scr,
                *, n_seq, tt, seq_lanes, scan_unroll):
    cw = xh_scr.shape[1] // SSM_CHUNKS
    hw = cw // 2
    uc = u_scr.shape[2]
    ssm_w = SSM_CHUNKS * uc

    @pl.when(pl.program_id(0) == 0)
    def _():
        hT_ref[...] = h0_ref[...]

    for j in range(SSM_CHUNKS):
        for b in range(n_seq):
            if seq_lanes:
                u_b = u_ref[:, b * ssm_w + j * uc:b * ssm_w + (j + 1) * uc]
            else:
                u_b = u_ref[b * tt:(b + 1) * tt, j * uc:(j + 1) * uc]
            u_scr[j, pl.ds(b, tt, stride=n_seq), :] = u_b

    def project_in(j):
        xh_scr[:, j * cw:(j + 1) * cw] = jnp.dot(u_scr[j].astype(BF16), bd_ref[j], preferred_element_type=F32)

    def scan(j):
        re = slice(j * cw, j * cw + hw)
        im = slice(j * cw + hw, (j + 1) * cw)
        ar = a_ref[:, re]
        ai = a_ref[:, im]

        def seq_group(s, carry):
            s8 = pl.multiple_of(s * 8, 8)

            def step(t, h):
                hr, hi = h
                r0 = pl.multiple_of(t * n_seq + s8, 8)
                nr = ar * hr - ai * hi + xh_scr[pl.ds(r0, 8), re]
                ni = ar * hi + ai * hr + xh_scr[pl.ds(r0, 8), im]
                xh_scr[pl.ds(r0, 8), re] = nr
                xh_scr[pl.ds(r0, 8), im] = ni
                return nr, ni

            hr, hi = lax.fori_loop(0, tt, step, (hT_ref[pl.ds(s8, 8), re], hT_ref[pl.ds(s8, 8), im]),
                                   unroll=scan_unroll)
            hT_ref[pl.ds(s8, 8), re] = hr
            hT_ref[pl.ds(s8, 8), im] = hi
            return carry

        lax.fori_loop(0, n_seq // 8, seq_group, 0)

    def project_out(j):
        y = jnp.dot(xh_scr[:, j * cw:(j + 1) * cw].astype(BF16), cd_ref[j], preferred_element_type=F32)
        cs = slice(j * uc, (j + 1) * uc)
        y_scr[j] = jax.nn.gelu(y + d_ref[:, cs] * u_scr[j])
        for b in range(n_seq):
            y_b = y_scr[j, pl.ds(b, tt, stride=n_seq), :].astype(y_ref.dtype)
            if seq_lanes:
                y_ref[:, b * ssm_w + j * uc:b * ssm_w + (j + 1) * uc] = y_b
            else:
                y_ref[b * tt:(b + 1) * tt, cs] = y_b

    for stage in range(SSM_CHUNKS + 2):
        if stage < SSM_CHUNKS:
            project_in(stage)
        if 1 <= stage <= SSM_CHUNKS:
            scan(stage - 1)
        if stage >= 2:
            project_out(stage - 2)


def _ssm(u, h0, a8, d_skip, bd, cd, n_seq, tt, seq_lanes, scan_unroll):
    ssm_w = d_skip.shape[1]
    tile = n_seq * tt
    n_state = h0.shape[1]
    block = (tt, n_seq * ssm_w) if seq_lanes else (tile, ssm_w)
    assert seq_lanes or u.shape[0] == tile, "row order (seq, t) cannot be tiled over time"
    slab = pltpu.VMEM((SSM_CHUNKS, tile, ssm_w // SSM_CHUNKS), F32)
    return pl.pallas_call(
        functools.partial(_ssm_kernel, n_seq=n_seq, tt=tt, seq_lanes=seq_lanes, scan_unroll=scan_unroll),
        grid=(u.shape[0] // block[0],),
        in_specs=[pl.BlockSpec(block, lambda i: (i, 0)),
                  _const_spec(h0.shape), _const_spec(a8.shape), _const_spec(d_skip.shape),
                  _const_spec(bd.shape), _const_spec(cd.shape)],
        out_specs=[pl.BlockSpec(block, lambda i: (i, 0)),
                   pl.BlockSpec(h0.shape, lambda i: (0, 0))],
        out_shape=[jax.ShapeDtypeStruct(u.shape, BF16), jax.ShapeDtypeStruct(h0.shape, F32)],
        scratch_shapes=[pltpu.VMEM((tile, n_state), F32), slab, slab],
        compiler_params=pltpu.CompilerParams(
            dimension_semantics=("arbitrary",), vmem_limit_bytes=VMEM_LIMIT),
        name="ssm",
    )(u, h0, a8, d_skip, bd, cd)


def _out_kernel(*refs, n_sub, tiles_per_seq):
    if tiles_per_seq:
        (sink_ref, x1_ref, ga_ref, gs_ref, y_ref, q0_ref, k0_ref, v0_ref, qn_ref, kp_ref, kn_ref, vp_ref, vn_ref,
         glua_ref, glub_ref, wo_ref, nb_ref, wg_ref, wu_ref, wd_ref, nf_ref, o_ref,
         h_scr, act_scr, at_ref, st_scr) = refs
        step = pl.program_id(0)

        @pl.when(step == 0)
        def _():
            score, attend = _attention_pieces(sink_ref, q0_ref, k0_ref.at[pl.ds(0, WINDOW)], k0_ref,
                                              v0_ref.at[:, pl.ds(0, WINDOW)], v0_ref, st_scr, at_ref, False)
            first, rest = _staggered(score, attend, ATTN_AHEAD)
            first()
            for f in rest:
                f()

        nxt = jnp.minimum(step + 1, pl.num_programs(0) - 1)
        score, attend = _attention_pieces(sink_ref, qn_ref, kp_ref, kn_ref, vp_ref, vn_ref, st_scr, at_ref,
                                          nxt % tiles_per_seq != 0)
        first, fillers = _staggered(score, attend, ATTN_AHEAD)
    else:
        (x1_ref, at_ref, ga_ref, gs_ref, y_ref, glua_ref, glub_ref, wo_ref,
         nb_ref, wg_ref, wu_ref, wd_ref, nf_ref, o_ref, h_scr, act_scr) = refs
        first, fillers = (lambda: None), ()
    rows = _row_slices(x1_ref.shape[0], n_sub)
    ssm = [jnp.dot(y_ref[r, :], glua_ref[...], preferred_element_type=F32) * jax.nn.sigmoid(
        jnp.dot(y_ref[r, :], glub_ref[...], preferred_element_type=F32)) for r in rows]
    merged = [(jax.nn.sigmoid(ga_ref[r, :]) * at_ref[r, :] + jax.nn.sigmoid(gs_ref[r, :]) * s).astype(BF16)
              for r, s in zip(rows, ssm)]
    x2 = [x1_ref[r, :] + jnp.dot(m, wo_ref[...], preferred_element_type=F32) for r, m in zip(rows, merged)]
    first()
    for r, x in zip(rows, x2):
        h_scr[r, :] = _rms(x, nb_ref[...]).astype(BF16)
    for r, x, f in zip(rows, x2, _swiglu(h_scr, wg_ref, wu_ref, wd_ref, act_scr, rows, fillers)):
        o_ref[r, :] = _rms(x + 0.5 * f, nf_ref[...])


def _out(grid, tm, n_sub, std, ymap, x1, attn, ga, gs, y_tb, glua, glub, wo, nb, wg, wu, wd, nf):
    d = nb.shape[1]
    d_ff = wg.shape[1]
    kv = N_KV_HEADS * HEAD_DIM
    shp, spec = std(d)
    weights = [glua, glub, wo, nb, wg, wu, wd, nf]
    scratch = [pltpu.VMEM((tm, d), BF16), pltpu.VMEM((tm, d_ff), BF16)]
    fused = isinstance(attn, tuple)
    if fused:
        sinks, qst, kb, vt, tiles_per_seq = attn
        (n_tiles,) = grid
        blk = tm // WINDOW
        nxt = lambda s: jnp.minimum(s + 1, n_tiles - 1)
        before = lambda s: jnp.maximum(nxt(s) * blk - 1, 0)
        once = dict(pipeline_mode=pl.Buffered(1))
        acts = [sinks, x1, ga, gs, y_tb, qst, kb, vt, qst, kb, kb, vt, vt]
        act_specs = [pl.BlockSpec(memory_space=pltpu.SMEM), spec, spec, spec, pl.BlockSpec((tm, glua.shape[0]), ymap),
                     pl.BlockSpec((tm * N_HEADS, 2 * HEAD_DIM), lambda s: (0, 0), **once),
                     pl.BlockSpec((tm, kv), lambda s: (0, 0), **once),
                     pl.BlockSpec((kv, tm), lambda s: (0, 0), **once),
                     pl.BlockSpec((tm * N_HEADS, 2 * HEAD_DIM), lambda s: (nxt(s), 0)),
                     pl.BlockSpec((WINDOW, kv), lambda s: (before(s), 0)),
                     pl.BlockSpec((tm, kv), lambda s: (nxt(s), 0)),
                     pl.BlockSpec((kv, WINDOW), lambda s: (0, before(s))),
                     pl.BlockSpec((kv, tm), lambda s: (0, nxt(s)))]
        scratch += [pltpu.VMEM((tm, d), F32), pltpu.VMEM((ATTN_SLOTS, 2 * WINDOW, Q_PER_KV * WINDOW), F32)]
    else:
        tiles_per_seq = 0
        acts = [x1, attn, ga, gs, y_tb]
        act_specs = [spec, spec, spec, spec, pl.BlockSpec((tm, glua.shape[0]), ymap)]
    return pl.pallas_call(
        functools.partial(_out_kernel, n_sub=n_sub, tiles_per_seq=tiles_per_seq),
        grid=grid,
        in_specs=act_specs + [_const_spec(w.shape) for w in weights],
        out_specs=spec,
        out_shape=jax.ShapeDtypeStruct(shp, F32),
        scratch_shapes=scratch,
        compiler_params=pltpu.CompilerParams(
            dimension_semantics=("arbitrary",) * len(grid), vmem_limit_bytes=VMEM_LIMIT),
        name="out",
    )(*acts, *weights)


def _transpose_bf16(w, col0, width):
    assert col0 % width == 0 and width % 2 == 0
    half = width // 2

    def body(w_ref, lo_ref, hi_ref):
        t = w_ref[...].T.astype(BF16)
        lo_ref[...] = t[:half]
        hi_ref[...] = t[half:]
    rows = w.shape[0]
    return pl.pallas_call(
        body,
        grid=(1,),
        in_specs=[pl.BlockSpec((rows, width), lambda i: (0, col0 // width))],
        out_specs=[pl.BlockSpec((half, rows), lambda i: (0, 0))] * 2,
        out_shape=[jax.ShapeDtypeStruct((half, rows), BF16)] * 2,
        name="transpose_bf16",
    )(w)


def _block_diag(w, chunks):
    g, a, b = w.shape
    gl = g // chunks
    w = w.reshape(chunks, gl, a, b)
    eye = jnp.eye(gl, dtype=w.dtype)
    return (w[:, :, :, None, :] * eye[None, :, None, :, None]).reshape(chunks, gl * a, gl * b)


def _state_to_lanes(re, im):
    s = re.shape[0]
    re = re.reshape(s, SSM_CHUNKS, -1)
    im = im.reshape(s, SSM_CHUNKS, -1)
    return jnp.stack([re, im], axis=2).reshape(s, -1)


def _lanes_to_state(h, groups):
    s = h.shape[0]
    h = h.reshape(s, SSM_CHUNKS, 2, -1)
    return h[:, :, 0].reshape(s, groups, -1), h[:, :, 1].reshape(s, groups, -1)


def kernel(x_prompt, x_sample, cache_k_win, cache_v_win, state_ssm_re, state_ssm_im, ffn_a_norm, ffn_a_gate, ffn_a_up, ffn_a_down, mix_norm, w_in, attn_sinks, ssm_lambda_re, ssm_lambda_im, ssm_log_dt, ssm_b_re, ssm_b_im, ssm_c_re, ssm_c_im, ssm_d, glu_a, glu_b, w_out, ffn_b_norm, ffn_b_gate, ffn_b_up, ffn_b_down, final_norm):
    depth = ffn_a_norm.shape[0]
    assert depth == 1, "single-layer trunk"
    n_p, seq, d = x_prompt.shape
    n_s, dec, _ = x_sample.shape
    ssm_w = ssm_d.shape[1]
    groups = ssm_lambda_re.shape[1]
    kvw = N_KV_HEADS * HEAD_DIM
    assert cache_k_win.shape[2] == WINDOW and seq % WINDOW == 0 and n_p % 8 == 0 and n_s % 8 == 0

    l = 0
    na, nm, nb = (w[l].reshape(1, d) for w in (ffn_a_norm, mix_norm, ffn_b_norm))
    nf = final_norm.reshape(1, d)
    wga, wua, wda, win = (w[l].astype(BF16) for w in (ffn_a_gate, ffn_a_up, ffn_a_down, w_in))
    wgb, wub, wdb = (w[l].astype(BF16) for w in (ffn_b_gate, ffn_b_up, ffn_b_down))
    glua, glub, wo = (w[l].astype(BF16) for w in (glu_a, glu_b, w_out))
    sinks = attn_sinks[l]
    d_skip = ssm_d[l].reshape(1, ssm_w)

    abar_re, abar_im, bb_re_t, bb_im_t = _ssm_prep(
        ssm_lambda_re[l], ssm_lambda_im[l], ssm_log_dt[l],
        jnp.swapaxes(ssm_b_re[l], 1, 2), jnp.swapaxes(ssm_b_im[l], 1, 2))
    bd = jnp.concatenate([_block_diag(bb_re_t, SSM_CHUNKS), _block_diag(bb_im_t, SSM_CHUNKS)],
                         axis=2).astype(BF16)
    c_re_t = jnp.swapaxes(ssm_c_re[l], 1, 2)
    c_im_t = jnp.swapaxes(ssm_c_im[l], 1, 2)
    cd = jnp.concatenate([_block_diag(c_re_t, SSM_CHUNKS), -_block_diag(c_im_t, SSM_CHUNKS)],
                         axis=1).astype(BF16)
    a8 = jnp.broadcast_to(_state_to_lanes(abar_re[None], abar_im[None]), (8, 2 * groups * SSM_STATE))

    def run_group(x, n_seq, tiling, out_tiling, u_shape, attn_w, stacked, attn_fn, h0, tt):
        grid, tm, std, umap, n_sub = tiling
        x1, q, k, v, u_tb, ga, gs, *more = _ffn_in(x, grid, tm, n_sub, std, umap, u_shape,
                                                    na, wga, wua, wda, nm, win, attn_w, stacked)
        attn = attn_fn(q, k, v, *more)
        y_tb, h_t = _ssm(u_tb, h0, a8, d_skip, bd, cd, n_seq, tt,
                         seq_lanes=u_shape[1] != ssm_w, scan_unroll=tt)
        grid, tm, std, umap, n_sub = out_tiling
        y = _out(grid, tm, n_sub, std, umap, x1, attn, ga, gs, y_tb, glua, glub, wo, nb, wgb, wub, wdb, nf)
        return y, k, v, h_t

    def tiling_p(tm):
        nt = seq // tm

        def std(width):
            return (n_p * seq, width), pl.BlockSpec((tm, width), lambda b, i: (b * nt + i, 0))
        return (n_p, nt), tm, std, lambda b, i: (i, b), tm // SUB_TILE

    tm_p = ROW_TILE
    nt = seq // tm_p

    def attn_p(qst, k, v, kb, vt):
        return sinks, qst, kb, vt, nt

    wkt, wvt = _transpose_bf16(w_in[l], d, 2 * kvw)

    h0_p = jnp.zeros((n_p, 2 * groups * SSM_STATE), F32)
    def std_flat(width):
        return (n_p * seq, width), pl.BlockSpec((tm_p, width), lambda s: (s, 0))

    out_tiling_p = ((n_p * nt,), tm_p, std_flat, lambda s: (s % nt, s // nt), tm_p // SUB_TILE)
    y_p, k_p, v_p, h_p = run_group(
        x_prompt.reshape(n_p * seq, d), n_p, tiling_p(tm_p), out_tiling_p,
        (seq, n_p * ssm_w), ([wvt], lambda b, i: (b * nt + i, 0)), True, attn_p, h0_p, SSM_TILE_ROWS // n_p)

    tm_s = ROW_TILE

    def std_s(width):
        return (n_s * dec, width), pl.BlockSpec((tm_s, width), lambda i, j: (i, 0))

    def window_t(c):
        return jnp.transpose(c, (0, 2, 3, 1)).reshape(c.shape[0], kvw, WINDOW)

    def window(ct):
        return jnp.transpose(ct.reshape(ct.shape[0], N_KV_HEADS, HEAD_DIM, WINDOW), (0, 3, 1, 2))[None]

    new_windows = []

    def attn_s(q, kt, vt):
        o, kw, vw = _attn_sample(sinks, q, kt, vt, window_t(cache_k_win[l]), window_t(cache_v_win[l]),
                                 n_s, dec, WINDOW // dec, ATTN_SAMPLE_UNROLL)
        new_windows.extend([kw, vw])
        return o

    h0_s = _state_to_lanes(state_ssm_re[l], state_ssm_im[l])
    tiling_s = ((n_s * dec // tm_s, 1), tm_s, std_s, lambda i, j: (i, 0), tm_s // SUB_TILE)
    y_s, _, _, h_s = run_group(
        x_sample.reshape(n_s * dec, d), n_s, tiling_s, tiling_s,
        (n_s * dec, ssm_w), ([wkt, wvt], lambda i, j: (i, 0)), False, attn_s, h0_s, dec)

    sp_re, sp_im = _lanes_to_state(h_p, groups)
    ss_re, ss_im = _lanes_to_state(h_s, groups)
    return (y_p.reshape(n_p, seq, d), y_s.reshape(n_s, dec, d),
            window(k_p), window(v_p), window(new_windows[0]), window(new_windows[1]),
            sp_re[None], sp_im[None], ss_re[None], ss_im[None])
```

```python
import functools

import jax
import jax.numpy as jnp
from jax import lax
from jax.experimental import pallas as pl
from jax.experimental.pallas import tpu as pltpu

F32 = jnp.float32
BF16 = jnp.bfloat16

N_HEADS = 16
N_KV_HEADS = 4
HEAD_DIM = 64
Q_PER_KV = N_HEADS // N_KV_HEADS
WINDOW = 128
SSM_STATE = 64
RMS_EPS = 1e-6
NEG_BIG = -1e30
LOG2_E = 1.4426950408889634

MXU_COLS = 256
WINDOW_COLS = N_KV_HEADS * HEAD_DIM
ROW_TILE = 512
SUB_TILE = 256
SSM_TILE_ROWS = 1024
SSM_CHUNKS = 4
ATTN_AHEAD = 2
ATTN_SLOTS = 4
ATTN_SAMPLE_UNROLL = 2
VMEM_LIMIT = 62 * 1024 * 1024


def _rms(x, g):
    return x * lax.rsqrt(jnp.mean(x * x, axis=-1, keepdims=True) + RMS_EPS) * g


def _row_slices(tm, n_sub):
    rs = tm // n_sub
    return [slice(r * rs, (r + 1) * rs) for r in range(n_sub)]


def _swiglu(h_scr, wg_ref, wu_ref, wd_ref, act_scr, rows=(slice(None),), fillers=()):
    d_ff = wg_ref.shape[1]
    fillers = list(fillers)
    for c in range(d_ff // MXU_COLS):
        sl = slice(c * MXU_COLS, (c + 1) * MXU_COLS)
        for r in rows:
            g = jnp.dot(h_scr[r, :], wg_ref[:, sl], preferred_element_type=F32)
            u = jnp.dot(h_scr[r, :], wu_ref[:, sl], preferred_element_type=F32)
            act_scr[r, sl] = (jax.nn.silu(g) * u).astype(BF16)
            if fillers:
                fillers.pop(0)()
    for f in fillers:
        f()
    return [jnp.dot(act_scr[r, :], wd_ref[...], preferred_element_type=F32) for r in rows]


def _ffn_in_kernel(*refs, stacked, n_sub):
    if stacked:
        (x_ref, na_ref, wg_ref, wu_ref, wd_ref, nm_ref, win_ref, wvt_ref,
         x1_ref, q_ref, k_ref, v_ref, u_ref, ga_ref, gs_ref, kb_ref, vt_ref, h_scr, act_scr) = refs
        assert WINDOW_COLS == k_ref.shape[0]
    else:
        (x_ref, na_ref, wg_ref, wu_ref, wd_ref, nm_ref, win_ref, wkt_ref, wvt_ref,
         x1_ref, q_ref, k_ref, v_ref, u_ref, ga_ref, gs_ref, h_scr, act_scr) = refs
    tm, d = x_ref.shape
    rows = _row_slices(tm, n_sub)
    rs = tm // n_sub
    for r in rows:
        h_scr[r, :] = _rms(x_ref[r, :], na_ref[...]).astype(BF16)
    x1 = [x_ref[r, :] + 0.5 * f for r, f in zip(rows, _swiglu(h_scr, wg_ref, wu_ref, wd_ref, act_scr, rows))]
    for r, x in zip(rows, x1):
        x1_ref[r, :] = x
        h_scr[r, :] = _rms(x, nm_ref[...]).astype(BF16)

    def proj(w_ref, off, c, r):
        return jnp.dot(h_scr[r, :], w_ref[:, off + c * MXU_COLS: off + (c + 1) * MXU_COLS],
                       preferred_element_type=F32)

    q_scale = HEAD_DIM ** -0.5 * (LOG2_E if stacked else 1.0)
    pair_w = 2 * HEAD_DIM
    if stacked:
        assert Q_PER_KV * HEAD_DIM == MXU_COLS
        lane = lax.broadcasted_iota(jnp.int32, (rs, pair_w), 1)
        for g in range(N_KV_HEADS):
            keep = (lane < HEAD_DIM) if g % 2 == 0 else (lane >= HEAD_DIM)
            for ri, r in enumerate(rows):
                res = proj(win_ref, 0, g, r) * q_scale
                for rr in range(Q_PER_KV):
                    pair = res[:, (rr // 2) * pair_w:(rr // 2 + 1) * pair_w]
                    if rr % 2 != g % 2:
                        pair = pltpu.roll(pair, HEAD_DIM, axis=1)
                    piece = jnp.where(keep, pair, 0.0).astype(BF16)
                    for bl in range(rs // WINDOW):
                        row = (((ri * (rs // WINDOW) + bl) * N_KV_HEADS + g) * Q_PER_KV + rr) * WINDOW
                        q_ref[row:row + WINDOW, :] = piece[bl * WINDOW:(bl + 1) * WINDOW, :]
    else:
        for c in range(d // MXU_COLS):
            for r in rows:
                q_ref[r, c * MXU_COLS:(c + 1) * MXU_COLS] = (proj(win_ref, 0, c, r) * q_scale).astype(q_ref.dtype)
        for ref, wt_ref in ((k_ref, wkt_ref), (v_ref, wvt_ref)):
            for r in rows:
                ref[:, r] = lax.dot_general(wt_ref[...], h_scr[r, :], (((1,), (1,)), ((), ())),
                                            preferred_element_type=F32)
    off = d + 2 * WINDOW_COLS
    for ref in (u_ref, ga_ref, gs_ref):
        width = ref.shape[1]
        for c in range(width // MXU_COLS):
            for r in rows:
                ref[r, c * MXU_COLS:(c + 1) * MXU_COLS] = proj(win_ref, off, c, r).astype(ref.dtype)
        off += width
    if stacked:
        for r in rows:
            kb_ref[r, :] = proj(win_ref, d, 0, r).astype(kb_ref.dtype)
        for r in rows:
            vt_ref[:, r] = lax.dot_general(wvt_ref[...], h_scr[r, :], (((1,), (1,)), ((), ())),
                                           preferred_element_type=F32).astype(vt_ref.dtype)

        @pl.when(pl.program_id(1) == pl.num_programs(1) - 1)
        def _():
            tail = h_scr[tm - WINDOW:, :]
            for ref, off in ((k_ref, d), (v_ref, d + WINDOW_COLS)):
                ref[...] = jnp.dot(tail, win_ref[:, off:off + WINDOW_COLS], preferred_element_type=F32).T


def _const_spec(shape):
    nd = len(shape)
    return pl.BlockSpec(shape, lambda *_: (0,) * nd, pipeline_mode=pl.Buffered(1))


def _ffn_in(x2d, grid, tm, n_sub, std, umap, u_shape, na, wg, wu, wd, nm, win, attn_w, stacked):
    d = na.shape[1]
    d_ff = wg.shape[1]
    kv = N_KV_HEADS * HEAD_DIM
    ssm_w = win.shape[1] - 3 * d - 2 * kv
    n_tok = x2d.size // d
    extra_w, tile_map = attn_w
    t_spec = pl.BlockSpec((kv, tm), lambda *g: tile_map(*g)[::-1])
    out_shape, out_specs = [], []

    def add(shape_spec, dtype):
        out_shape.append(jax.ShapeDtypeStruct(shape_spec[0], dtype))
        out_specs.append(shape_spec[1])

    add(std(d), F32)
    if stacked:
        add(((n_tok * N_HEADS, 2 * HEAD_DIM), pl.BlockSpec((tm * N_HEADS, 2 * HEAD_DIM), tile_map)), BF16)
        for _ in range(2):
            add(((grid[0], kv, WINDOW), pl.BlockSpec((None, kv, WINDOW), lambda b, i: (b, 0, 0))), F32)
    else:
        add(std(d), F32)
        add(((kv, n_tok), t_spec), F32)
        add(((kv, n_tok), t_spec), F32)
    add((u_shape, pl.BlockSpec((tm, ssm_w), umap)), F32)
    add(std(d), F32)
    add(std(d), F32)
    weights = [na, wg, wu, wd, nm, win, *extra_w]
    if stacked:
        add(std(kv), BF16)
        add(((kv, n_tok), t_spec), BF16)
    return pl.pallas_call(
        functools.partial(_ffn_in_kernel, stacked=stacked, n_sub=n_sub),
        grid=grid,
        in_specs=[std(d)[1]] + [_const_spec(w.shape) for w in weights],
        out_specs=out_specs,
        out_shape=out_shape,
        scratch_shapes=[pltpu.VMEM((tm, d), BF16), pltpu.VMEM((tm, d_ff), BF16)],
        compiler_params=pltpu.CompilerParams(
            dimension_semantics=("parallel", "arbitrary"), vmem_limit_bytes=VMEM_LIMIT),
        name="ffn_in",
    )(x2d, *weights)


def _attention_pieces(sink_ref, q_ref, kp_ref, kc_ref, vp_ref, vc_ref, st_scr, o_ref, ctx_ok):
    nq = WINDOW
    nk = 2 * WINDOW
    lanes = Q_PER_KV * nq
    pair_w = 2 * HEAD_DIM
    n_blk = kc_ref.shape[0] // WINDOW
    n_slots = st_scr.shape[0]
    masks = {}

    def valid(bl):
        if bl not in masks:
            jk = lax.broadcasted_iota(jnp.int32, (nk, lanes), 0)
            iq = lax.broadcasted_iota(jnp.int32, (nk, lanes), 1) & (nq - 1)
            ok = True if bl > 0 else ctx_ok
            lo = iq if ok is True else jnp.maximum(iq, jnp.where(ok, 0, WINDOW))
            masks[bl] = (jk >= lo) & (jk <= iq + WINDOW)
        return masks[bl]

    def keys(bl, g):
        cols = slice((g // 2) * pair_w, (g // 2 + 1) * pair_w)
        if bl == 0:
            return jnp.concatenate([kp_ref[:, cols], kc_ref[0:WINDOW, cols]], axis=0)
        return kc_ref[(bl - 1) * WINDOW:(bl + 1) * WINDOW, cols]

    def vals_t(bl, g):
        rows = slice(g * HEAD_DIM, (g + 1) * HEAD_DIM)
        if bl == 0:
            return jnp.concatenate([vp_ref[rows, :], vc_ref[rows, 0:WINDOW]], axis=1)
        return vc_ref[rows, (bl - 1) * WINDOW:(bl + 1) * WINDOW]

    def score(bl, g):
        k = bl * N_KV_HEADS + g
        st = lax.dot_general(keys(bl, g), q_ref[k * lanes:(k + 1) * lanes, :], (((1,), (1,)), ((), ())),
                             preferred_element_type=F32)
        st_scr[k % n_slots] = jnp.where(valid(bl), st, NEG_BIG)

    def attend(bl, g):
        st = st_scr[(bl * N_KV_HEADS + g) % n_slots]
        sink = jnp.concatenate(
            [jnp.full((1, nq), sink_ref[g * Q_PER_KV + r] * LOG2_E, F32) for r in range(Q_PER_KV)], axis=1)
        m = jnp.maximum(jnp.max(st, axis=0, keepdims=True), sink)
        p = jnp.exp2(st - m)
        denom = jnp.sum(p, axis=0, keepdims=True) + jnp.exp2(sink - m)
        ot = jnp.dot(vals_t(bl, g), p.astype(BF16), preferred_element_type=F32) * (1.0 / denom)
        for pr in range(Q_PER_KV // 2):
            two = jnp.concatenate([ot[:, (2 * pr) * nq:(2 * pr + 1) * nq],
                                   ot[:, (2 * pr + 1) * nq:(2 * pr + 2) * nq]], axis=0)
            col = (g * Q_PER_KV + 2 * pr) * HEAD_DIM
            o_ref[bl * nq:(bl + 1) * nq, col:col + pair_w] = two.T

    order = [(bl, g) for bl in range(n_blk) for g in range(N_KV_HEADS)]
    return ([functools.partial(score, bl, g) for bl, g in order],
            [functools.partial(attend, bl, g) for bl, g in order])


def _staggered(score, attend, ahead):
    def piece(k):
        def run():
            if k + ahead < len(score):
                score[k + ahead]()
            attend[k]()
        return run

    def first():
        for f in score[:ahead]:
            f()
    return first, [piece(k) for k in range(len(attend))]


def _attn_sample_kernel(sink_ref, q_ref, knt_ref, vnt_ref, ckt_ref, cvt_ref, o_ref, kw_ref, vw_ref, *, n_sub, tn, unroll):
    pair_w = 2 * HEAD_DIM
    n_pairs = N_HEADS // 2
    rows = n_pairs * tn
    kv = N_KV_HEADS * HEAD_DIM
    old = WINDOW - tn
    assert tn & (tn - 1) == 0, "row -> token index uses a power-of-two mask"
    low = lax.broadcasted_iota(jnp.int32, (tn, pair_w), 1) < HEAD_DIM
    tq = lax.broadcasted_iota(jnp.int32, (2 * rows, 2 * WINDOW), 0) & (tn - 1)
    col = lax.broadcasted_iota(jnp.int32, (2 * rows, 2 * WINDOW), 1)
    valid = ((col < WINDOW) & (col >= tq)) | ((col >= WINDOW + old) & (col - WINDOW - old <= tq))
    is_new = lax.broadcasted_iota(jnp.int32, (kv, WINDOW), 1) >= old
    zeros = jnp.zeros((tn, pair_w), F32)
    nt = (((1,), (1,)), ((), ()))
    assert n_sub * tn == WINDOW

    sink = jnp.concatenate([jnp.full((tn, 1), sink_ref[2 * pr + half], F32)
                            for half in range(2) for pr in range(n_pairs)], axis=0)

    def scores(s):
        r0 = pl.multiple_of(s * tn, tn)
        q = q_ref[pl.ds(r0, tn), :]
        kt = ckt_ref[s]
        vt = cvt_ref[s]
        shift = (old - r0) & (WINDOW - 1)
        knt = jnp.where(is_new, pltpu.roll(knt_ref[...], shift, axis=1), 0.0)
        vnt = jnp.where(is_new, pltpu.roll(vnt_ref[...], shift, axis=1), 0.0)
        kw_ref[s] = jnp.where(is_new, knt, pltpu.roll(kt, old, axis=1))
        vw_ref[s] = jnp.where(is_new, vnt, pltpu.roll(vt, old, axis=1))
        k_all = jnp.concatenate([kt, knt], axis=1).astype(BF16)
        v_all = jnp.concatenate([vt, vnt], axis=1).astype(BF16)
        blocks = []
        for half in range(2):
            for g in range(N_KV_HEADS):
                for pp in range(Q_PER_KV // 2):
                    piece = q[:, (2 * g + pp) * pair_w:(2 * g + pp + 1) * pair_w]
                    if half != g % 2:
                        piece = pltpu.roll(piece, HEAD_DIM, axis=1)
                    piece = jnp.where(low if g % 2 == 0 else ~low, piece, 0.0)
                    blocks.append(jnp.concatenate([piece, zeros] if g // 2 == 0 else [zeros, piece], axis=1))
        qh = jnp.concatenate(blocks, axis=0).astype(BF16)
        return r0, v_all, jnp.where(valid, jnp.dot(qh, k_all, preferred_element_type=F32), NEG_BIG)

    def attend(r0, v_all, sc):
        m = jnp.maximum(jnp.max(sc, axis=-1, keepdims=True), sink)
        p = jnp.exp(sc - m)
        rden = 1.0 / (jnp.sum(p, axis=-1, keepdims=True) + jnp.exp(sink - m))
        o = lax.dot_general((p * rden).astype(BF16), v_all, nt, preferred_element_type=F32)
        for pr in range(n_pairs):
            g = pr // (Q_PER_KV // 2)
            sel = []
            for half in range(2):
                blk = o[half * rows + pr * tn:half * rows + (pr + 1) * tn, (g // 2) * pair_w:(g // 2 + 1) * pair_w]
                sel.append(blk if half == g % 2 else pltpu.roll(blk, HEAD_DIM, axis=1))
            o_ref[pl.ds(r0, tn), pr * pair_w:(pr + 1) * pair_w] = jnp.where(low, sel[0], sel[1])

    def some_sequences(i, carry):
        staged = [scores(i * unroll + u) for u in range(unroll)]
        for st in staged:
            attend(*st)
        return carry

    lax.fori_loop(0, n_sub // unroll, some_sequences, 0)


def _attn_sample(sinks, q, knt, vnt, cache_kt, cache_vt, n_seq, tn, n_sub, unroll):
    d = q.shape[1]
    kv = knt.shape[0]
    rows = lambda i: (i, 0)
    seqs = lambda i: (i, 0, 0)
    win = pl.BlockSpec((n_sub, kv, WINDOW), seqs)
    new = pl.BlockSpec((kv, n_sub * tn), lambda i: (0, i))
    return pl.pallas_call(
        functools.partial(_attn_sample_kernel, n_sub=n_sub, tn=tn, unroll=unroll),
        grid=(n_seq // n_sub,),
        in_specs=[pl.BlockSpec(memory_space=pltpu.SMEM),
                  pl.BlockSpec((n_sub * tn, d), rows),
                  new, new, win, win],
        out_specs=[pl.BlockSpec((n_sub * tn, d), rows), win, win],
        out_shape=[jax.ShapeDtypeStruct(q.shape, F32),
                   jax.ShapeDtypeStruct(cache_kt.shape, F32), jax.ShapeDtypeStruct(cache_vt.shape, F32)],
        compiler_params=pltpu.CompilerParams(
            dimension_semantics=("parallel",), vmem_limit_bytes=VMEM_LIMIT),
        name="attn_sample",
    )(sinks, q, knt, vnt, cache_kt, cache_vt)


def _ssm_prep_kernel(lr_ref, li_ref, ldt_ref, br_ref, bi_ref, ar_ref, ai_ref, bbr_ref, bbi_ref):
    lr = lr_ref[...]
    li = li_ref[...]
    dt = jnp.exp(ldt_ref[...])
    mag = jnp.exp(lr * dt)
    ang = li * dt
    abar_re = mag * jnp.cos(ang)
    abar_im = mag * jnp.sin(ang)
    den = lr * lr + li * li
    nr = abar_re - 1.0
    fr = (nr * lr + abar_im * li) / den
    fi = (abar_im * lr - nr * li) / den
    ar_ref[...] = abar_re
    ai_ref[...] = abar_im
    br = br_ref[...]
    bi = bi_ref[...]
    bbr_ref[...] = fr[:, None, :] * br - fi[:, None, :] * bi
    bbi_ref[...] = fr[:, None, :] * bi + fi[:, None, :] * br


def _ssm_prep(lam_re, lam_im, log_dt, b_re_t, b_im_t):
    g, n = lam_re.shape
    return pl.pallas_call(
        _ssm_prep_kernel,
        out_shape=[jax.ShapeDtypeStruct((g, n), F32), jax.ShapeDtypeStruct((g, n), F32),
                   jax.ShapeDtypeStruct(b_re_t.shape, F32), jax.ShapeDtypeStruct(b_im_t.shape, F32)],
        name="ssm_prep",
    )(lam_re, lam_im, log_dt.reshape(g, 1), b_re_t, b_im_t)


def _ssm_kernel(u_ref, h0r_ref, h0i_ref, a_ref, d_ref, bd_ref, cd_ref, y_ref, hTr_ref, hTi_ref, xh_scr, u_scr, y_scr,
                *, n_seq, tt, seq_lanes, scan_unroll):
    cw = xh_scr.shape[1] // SSM_CHUNKS
    hw = cw // 2
    uc = u_scr.shape[2]
    ssm_w = SSM_CHUNKS * uc

    @pl.when(pl.program_id(0) == 0)
    def _():
        hTr_ref[...] = h0r_ref[...]
        hTi_ref[...] = h0i_ref[...]

    for j in range(SSM_CHUNKS):
        for b in range(n_seq):
            if seq_lanes:
                u_b = u_ref[:, b * ssm_w + j * uc:b * ssm_w + (j + 1) * uc]
            else:
                u_b = u_ref[b * tt:(b + 1) * tt, j * uc:(j + 1) * uc]
            u_scr[j, pl.ds(b, tt, stride=n_seq), :] = u_b

    def project_in(j):
        xh_scr[:, j * cw:(j + 1) * cw] = jnp.dot(u_scr[j].astype(BF16), bd_ref[j], preferred_element_type=F32)

    def scan(j):
        re = slice(j * cw, j * cw + hw)
        im = slice(j * cw + hw, (j + 1) * cw)
        hs = slice(j * hw, (j + 1) * hw)
        ar = a_ref[:, re]
        ai = a_ref[:, im]

        def seq_group(s, carry):
            s8 = pl.multiple_of(s * 8, 8)

            def step(t, h):
                hr, hi = h
                r0 = pl.multiple_of(t * n_seq + s8, 8)
                nr = ar * hr - ai * hi + xh_scr[pl.ds(r0, 8), re]
                ni = ar * hi + ai * hr + xh_scr[pl.ds(r0, 8), im]
                xh_scr[pl.ds(r0, 8), re] = nr
                xh_scr[pl.ds(r0, 8), im] = ni
                return nr, ni

            hr, hi = lax.fori_loop(0, tt, step, (hTr_ref[pl.ds(s8, 8), hs], hTi_ref[pl.ds(s8, 8), hs]),
                                   unroll=scan_unroll)
            hTr_ref[pl.ds(s8, 8), hs] = hr
            hTi_ref[pl.ds(s8, 8), hs] = hi
            return carry

        lax.fori_loop(0, n_seq // 8, seq_group, 0)

    def project_out(j):
        y = jnp.dot(xh_scr[:, j * cw:(j + 1) * cw].astype(BF16), cd_ref[j], preferred_element_type=F32)
        cs = slice(j * uc, (j + 1) * uc)
        y_scr[j] = jax.nn.gelu(y + d_ref[:, cs] * u_scr[j])
        for b in range(n_seq):
            y_b = y_scr[j, pl.ds(b, tt, stride=n_seq), :].astype(y_ref.dtype)
            if seq_lanes:
                y_ref[:, b * ssm_w + j * uc:b * ssm_w + (j + 1) * uc] = y_b
            else:
                y_ref[b * tt:(b + 1) * tt, cs] = y_b

    for stage in range(SSM_CHUNKS + 2):
        if stage < SSM_CHUNKS:
            project_in(stage)
        if 1 <= stage <= SSM_CHUNKS:
            scan(stage - 1)
        if stage >= 2:
            project_out(stage - 2)


def _ssm(u, h0, a8, d_skip, bd, cd, n_seq, tt, seq_lanes, scan_unroll):
    ssm_w = d_skip.shape[1]
    tile = n_seq * tt
    h0_re, h0_im = h0
    n_state = 2 * h0_re.shape[1]
    state_spec = pl.BlockSpec(h0_re.shape, lambda i: (0, 0))
    state_shape = jax.ShapeDtypeStruct(h0_re.shape, F32)
    block = (tt, n_seq * ssm_w) if seq_lanes else (tile, ssm_w)
    assert seq_lanes or u.shape[0] == tile, "row order (seq, t) cannot be tiled over time"
    slab = pltpu.VMEM((SSM_CHUNKS, tile, ssm_w // SSM_CHUNKS), F32)
    return pl.pallas_call(
        functools.partial(_ssm_kernel, n_seq=n_seq, tt=tt, seq_lanes=seq_lanes, scan_unroll=scan_unroll),
        grid=(u.shape[0] // block[0],),
        in_specs=[pl.BlockSpec(block, lambda i: (i, 0)),
                  _const_spec(h0_re.shape), _const_spec(h0_im.shape), _const_spec(a8.shape), _const_spec(d_skip.shape),
                  _const_spec(bd.shape), _const_spec(cd.shape)],
        out_specs=[pl.BlockSpec(block, lambda i: (i, 0)), state_spec, state_spec],
        out_shape=[jax.ShapeDtypeStruct(u.shape, BF16), state_shape, state_shape],
        scratch_shapes=[pltpu.VMEM((tile, n_state), F32), slab, slab],
        compiler_params=pltpu.CompilerParams(
            dimension_semantics=("arbitrary",), vmem_limit_bytes=VMEM_LIMIT),
        name="ssm",
    )(u, h0_re, h0_im, a8, d_skip, bd, cd)


def _out_kernel(*refs, n_sub, tiles_per_seq):
    if tiles_per_seq:
        (sink_ref, x1_ref, ga_ref, gs_ref, y_ref, q0_ref, k0_ref, v0_ref, qn_ref, kp_ref, kn_ref, vp_ref, vn_ref,
         glua_ref, glub_ref, wo_ref, nb_ref, wg_ref, wu_ref, wd_ref, nf_ref, o_ref,
         h_scr, act_scr, at_ref, st_scr) = refs
        step = pl.program_id(0)

        @pl.when(step == 0)
        def _():
            score, attend = _attention_pieces(sink_ref, q0_ref, k0_ref.at[pl.ds(0, WINDOW)], k0_ref,
                                              v0_ref.at[:, pl.ds(0, WINDOW)], v0_ref, st_scr, at_ref, False)
            first, rest = _staggered(score, attend, ATTN_AHEAD)
            first()
            for f in rest:
                f()

        nxt = jnp.minimum(step + 1, pl.num_programs(0) - 1)
        score, attend = _attention_pieces(sink_ref, qn_ref, kp_ref, kn_ref, vp_ref, vn_ref, st_scr, at_ref,
                                          nxt % tiles_per_seq != 0)
        first, fillers = _staggered(score, attend, ATTN_AHEAD)
    else:
        (x1_ref, at_ref, ga_ref, gs_ref, y_ref, glua_ref, glub_ref, wo_ref,
         nb_ref, wg_ref, wu_ref, wd_ref, nf_ref, o_ref, h_scr, act_scr) = refs
        first, fillers = (lambda: None), ()
    rows = _row_slices(x1_ref.shape[0], n_sub)
    ssm = [jnp.dot(y_ref[r, :], glua_ref[...], preferred_element_type=F32) * jax.nn.sigmoid(
        jnp.dot(y_ref[r, :], glub_ref[...], preferred_element_type=F32)) for r in rows]
    merged = [(jax.nn.sigmoid(ga_ref[r, :]) * at_ref[r, :] + jax.nn.sigmoid(gs_ref[r, :]) * s).astype(BF16)
              for r, s in zip(rows, ssm)]
    x2 = [x1_ref[r, :] + jnp.dot(m, wo_ref[...], preferred_element_type=F32) for r, m in zip(rows, merged)]
    first()
    for r, x in zip(rows, x2):
        h_scr[r, :] = _rms(x, nb_ref[...]).astype(BF16)
    for r, x, f in zip(rows, x2, _swiglu(h_scr, wg_ref, wu_ref, wd_ref, act_scr, rows, fillers)):
        o_ref[r, :] = _rms(x + 0.5 * f, nf_ref[...])


def _out(grid, tm, n_sub, std, ymap, x1, attn, ga, gs, y_tb, glua, glub, wo, nb, wg, wu, wd, nf):
    d = nb.shape[1]
    d_ff = wg.shape[1]
    kv = N_KV_HEADS * HEAD_DIM
    shp, spec = std(d)
    weights = [glua, glub, wo, nb, wg, wu, wd, nf]
    scratch = [pltpu.VMEM((tm, d), BF16), pltpu.VMEM((tm, d_ff), BF16)]
    fused = isinstance(attn, tuple)
    if fused:
        sinks, qst, kb, vt, tiles_per_seq = attn
        (n_tiles,) = grid
        blk = tm // WINDOW
        nxt = lambda s: jnp.minimum(s + 1, n_tiles - 1)
        before = lambda s: jnp.maximum(nxt(s) * blk - 1, 0)
        once = dict(pipeline_mode=pl.Buffered(1))
        acts = [sinks, x1, ga, gs, y_tb, qst, kb, vt, qst, kb, kb, vt, vt]
        act_specs = [pl.BlockSpec(memory_space=pltpu.SMEM), spec, spec, spec, pl.BlockSpec((tm, glua.shape[0]), ymap),
                     pl.BlockSpec((tm * N_HEADS, 2 * HEAD_DIM), lambda s: (0, 0), **once),
                     pl.BlockSpec((tm, kv), lambda s: (0, 0), **once),
                     pl.BlockSpec((kv, tm), lambda s: (0, 0), **once),
                     pl.BlockSpec((tm * N_HEADS, 2 * HEAD_DIM), lambda s: (nxt(s), 0)),
                     pl.BlockSpec((WINDOW, kv), lambda s: (before(s), 0)),
                     pl.BlockSpec((tm, kv), lambda s: (nxt(s), 0)),
                     pl.BlockSpec((kv, WINDOW), lambda s: (0, before(s))),
                     pl.BlockSpec((kv, tm), lambda s: (0, nxt(s)))]
        scratch += [pltpu.VMEM((tm, d), F32), pltpu.VMEM((ATTN_SLOTS, 2 * WINDOW, Q_PER_KV * WINDOW), F32)]
    else:
        tiles_per_seq = 0
        acts = [x1, attn, ga, gs, y_tb]
        act_specs = [spec, spec, spec, spec, pl.BlockSpec((tm, glua.shape[0]), ymap)]
    return pl.pallas_call(
        functools.partial(_out_kernel, n_sub=n_sub, tiles_per_seq=tiles_per_seq),
        grid=grid,
        in_specs=act_specs + [_const_spec(w.shape) for w in weights],
        out_specs=spec,
        out_shape=jax.ShapeDtypeStruct(shp, F32),
        scratch_shapes=scratch,
        compiler_params=pltpu.CompilerParams(
            dimension_semantics=("arbitrary",) * len(grid), vmem_limit_bytes=VMEM_LIMIT),
        name="out",
    )(*acts, *weights)


def _transpose_bf16(w, col0, width):
    assert col0 % width == 0 and width % 2 == 0
    half = width // 2

    def body(w_ref, lo_ref, hi_ref):
        t = w_ref[...].T.astype(BF16)
        lo_ref[...] = t[:half]
        hi_ref[...] = t[half:]
    rows = w.shape[0]
    return pl.pallas_call(
        body,
        grid=(1,),
        in_specs=[pl.BlockSpec((rows, width), lambda i: (0, col0 // width))],
        out_specs=[pl.BlockSpec((half, rows), lambda i: (0, 0))] * 2,
        out_shape=[jax.ShapeDtypeStruct((half, rows), BF16)] * 2,
        name="transpose_bf16",
    )(w)


def _block_diag(w, chunks):
    g, a, b = w.shape
    gl = g // chunks
    w = w.reshape(chunks, gl, a, b)
    eye = jnp.eye(gl, dtype=w.dtype)
    return (w[:, :, :, None, :] * eye[None, :, None, :, None]).reshape(chunks, gl * a, gl * b)


def _state_to_lanes(re, im):
    s = re.shape[0]
    re = re.reshape(s, SSM_CHUNKS, -1)
    im = im.reshape(s, SSM_CHUNKS, -1)
    return jnp.stack([re, im], axis=2).reshape(s, -1)


def kernel(x_prompt, x_sample, cache_k_win, cache_v_win, state_ssm_re, state_ssm_im, ffn_a_norm, ffn_a_gate, ffn_a_up, ffn_a_down, mix_norm, w_in, attn_sinks, ssm_lambda_re, ssm_lambda_im, ssm_log_dt, ssm_b_re, ssm_b_im, ssm_c_re, ssm_c_im, ssm_d, glu_a, glu_b, w_out, ffn_b_norm, ffn_b_gate, ffn_b_up, ffn_b_down, final_norm):
    depth = ffn_a_norm.shape[0]
    assert depth == 1, "single-layer trunk"
    n_p, seq, d = x_prompt.shape
    n_s, dec, _ = x_sample.shape
    ssm_w = ssm_d.shape[1]
    groups = ssm_lambda_re.shape[1]
    kvw = N_KV_HEADS * HEAD_DIM
    assert cache_k_win.shape[2] == WINDOW and seq % WINDOW == 0 and n_p % 8 == 0 and n_s % 8 == 0

    l = 0
    na, nm, nb = (w[l].reshape(1, d) for w in (ffn_a_norm, mix_norm, ffn_b_norm))
    nf = final_norm.reshape(1, d)
    wga, wua, wda, win = (w[l].astype(BF16) for w in (ffn_a_gate, ffn_a_up, ffn_a_down, w_in))
    wgb, wub, wdb = (w[l].astype(BF16) for w in (ffn_b_gate, ffn_b_up, ffn_b_down))
    glua, glub, wo = (w[l].astype(BF16) for w in (glu_a, glu_b, w_out))
    sinks = attn_sinks[l]
    d_skip = ssm_d[l].reshape(1, ssm_w)

    abar_re, abar_im, bb_re_t, bb_im_t = _ssm_prep(
        ssm_lambda_re[l], ssm_lambda_im[l], ssm_log_dt[l],
        jnp.swapaxes(ssm_b_re[l], 1, 2), jnp.swapaxes(ssm_b_im[l], 1, 2))
    bd = jnp.concatenate([_block_diag(bb_re_t, SSM_CHUNKS), _block_diag(bb_im_t, SSM_CHUNKS)],
                         axis=2).astype(BF16)
    c_re_t = jnp.swapaxes(ssm_c_re[l], 1, 2)
    c_im_t = jnp.swapaxes(ssm_c_im[l], 1, 2)
    cd = jnp.concatenate([_block_diag(c_re_t, SSM_CHUNKS), -_block_diag(c_im_t, SSM_CHUNKS)],
                         axis=1).astype(BF16)
    a8 = jnp.broadcast_to(_state_to_lanes(abar_re[None], abar_im[None]), (8, 2 * groups * SSM_STATE))

    def run_group(x, n_seq, tiling, out_tiling, u_shape, attn_w, stacked, attn_fn, h0, tt):
        grid, tm, std, umap, n_sub = tiling
        x1, q, k, v, u_tb, ga, gs, *more = _ffn_in(x, grid, tm, n_sub, std, umap, u_shape,
                                                    na, wga, wua, wda, nm, win, attn_w, stacked)
        attn = attn_fn(q, k, v, *more)
        y_tb, *h_t = _ssm(u_tb, h0, a8, d_skip, bd, cd, n_seq, tt,
                         seq_lanes=u_shape[1] != ssm_w, scan_unroll=tt)
        grid, tm, std, umap, n_sub = out_tiling
        y = _out(grid, tm, n_sub, std, umap, x1, attn, ga, gs, y_tb, glua, glub, wo, nb, wgb, wub, wdb, nf)
        return y, k, v, h_t

    def tiling_p(tm):
        nt = seq // tm

        def std(width):
            return (n_p * seq, width), pl.BlockSpec((tm, width), lambda b, i: (b * nt + i, 0))
        return (n_p, nt), tm, std, lambda b, i: (i, b), tm // SUB_TILE

    tm_p = ROW_TILE
    nt = seq // tm_p

    def attn_p(qst, k, v, kb, vt):
        return sinks, qst, kb, vt, nt

    wkt, wvt = _transpose_bf16(w_in[l], d, 2 * kvw)

    h0_p = (jnp.zeros((n_p, groups * SSM_STATE), F32),) * 2
    def std_flat(width):
        return (n_p * seq, width), pl.BlockSpec((tm_p, width), lambda s: (s, 0))

    out_tiling_p = ((n_p * nt,), tm_p, std_flat, lambda s: (s % nt, s // nt), tm_p // SUB_TILE)
    y_p, k_p, v_p, h_p = run_group(
        x_prompt.reshape(n_p * seq, d), n_p, tiling_p(tm_p), out_tiling_p,
        (seq, n_p * ssm_w), ([wvt], lambda b, i: (b * nt + i, 0)), True, attn_p, h0_p, SSM_TILE_ROWS // n_p)

    tm_s = ROW_TILE

    def std_s(width):
        return (n_s * dec, width), pl.BlockSpec((tm_s, width), lambda i, j: (i, 0))

    def window_t(c):
        return jnp.transpose(c, (0, 2, 3, 1)).reshape(c.shape[0], kvw, WINDOW)

    def window(ct):
        return jnp.transpose(ct.reshape(ct.shape[0], N_KV_HEADS, HEAD_DIM, WINDOW), (0, 3, 1, 2))[None]

    new_windows = []

    def attn_s(q, kt, vt):
        o, kw, vw = _attn_sample(sinks, q, kt, vt, window_t(cache_k_win[l]), window_t(cache_v_win[l]),
                                 n_s, dec, WINDOW // dec, ATTN_SAMPLE_UNROLL)
        new_windows.extend([kw, vw])
        return o

    h0_s = (state_ssm_re[l].reshape(n_s, -1), state_ssm_im[l].reshape(n_s, -1))
    tiling_s = ((n_s * dec // tm_s, 1), tm_s, std_s, lambda i, j: (i, 0), tm_s // SUB_TILE)
    y_s, _, _, h_s = run_group(
        x_sample.reshape(n_s * dec, d), n_s, tiling_s, tiling_s,
        (n_s * dec, ssm_w), ([wkt, wvt], lambda i, j: (i, 0)), False, attn_s, h0_s, dec)

    sp_re, sp_im = (h.reshape(n_p, groups, -1) for h in h_p)
    ss_re, ss_im = (h.reshape(n_s, groups, -1) for h in h_s)
    return (y_p.reshape(n_p, seq, d), y_s.reshape(n_s, dec, d),
            window(k_p), window(v_p), window(new_windows[0]), window(new_windows[1]),
            sp_re[None], sp_im[None], ss_re[None], ss_im[None])
```
